```python
import math
import jax, jax.numpy as jnp
from jax import lax
import numpy as np

D_MODEL = 1024
BATCH = 32
SEQ = 2048
DEPTH = 2

N_EVEN = (DEPTH + 1) // 2
N_ODD = DEPTH // 2
ATT_HEAD_DIM = 64
N_HEADS_FOX = 8
N_HEADS_DIL = 8
FOX_WIDTH = N_HEADS_FOX * ATT_HEAD_DIM
DIL_WIDTH = N_HEADS_DIL * ATT_HEAD_DIM
DIL_CONFIGS = ((128, 1), (512, 4), (2048, 16))
ROPE_THETA = 500000.0
ROPE_DIMS = ATT_HEAD_DIM // 4
Q_BLOCK = 128
N_HEADS_MLSTM = 8
MLSTM_HEAD_DIM = D_MODEL // N_HEADS_MLSTM
MLSTM_CHUNK = 64
CONV_WIDTH = 4
D_FF_DENSE = 2816
N_EXPERTS = 8
TOP_K = 2
D_FF_EXPERT = 3584
DEEPNORM_ALPHA = (2 * DEPTH) ** 0.25
DEEPNORM_BETA = (8 * DEPTH) ** -0.25
LN_EPS = 1e-5
EVEN_IN_WIDTH = 3 * FOX_WIDTH + N_HEADS_FOX + 3 * DIL_WIDTH
ODD_IN_WIDTH = 4 * D_MODEL + 2 * N_HEADS_MLSTM

kernel_name = 'hybrid_fox_dilated_mlstm_moe_deepnorm'


def _split(a, sizes):
    idx = np.cumsum(np.array(sizes))[:-1].tolist()
    return jnp.split(a, idx, axis=-1)


def layer_norm(x, g, b):
    xf = x.astype(jnp.float32)
    mu = jnp.mean(xf, axis=-1, keepdims=True)
    var = jnp.mean(jnp.square(xf - mu), axis=-1, keepdims=True)
    return ((xf - mu) * lax.rsqrt(var + LN_EPS) * g + b).astype(x.dtype)


def partial_rotary(x, pos):
    half = ROPE_DIMS // 2
    inv = ROPE_THETA ** (-jnp.arange(half, dtype=jnp.float32) / half)
    ang = pos.astype(jnp.float32)[:, None] * inv[None, :]
    cos = jnp.cos(ang)[None, :, None, :]
    sin = jnp.sin(ang)[None, :, None, :]
    xr = x[..., :ROPE_DIMS].astype(jnp.float32)
    x1, x2 = xr[..., :half], xr[..., half:]
    rot = jnp.concatenate([x1 * cos - x2 * sin, x2 * cos + x1 * sin], axis=-1).astype(x.dtype)
    return jnp.concatenate([rot, x[..., ROPE_DIMS:]], axis=-1)


def forgetting_attention(q, k, v, log_f):
    B, S, H, D = q.shape
    nb = S // Q_BLOCK
    scale = D ** -0.5
    c = jnp.cumsum(log_f, axis=1).transpose(0, 2, 1)
    qb = q.reshape(B, nb, Q_BLOCK, H, D).transpose(1, 0, 2, 3, 4)
    cb = c.reshape(B, H, nb, Q_BLOCK).transpose(2, 0, 1, 3)
    starts = jnp.arange(nb, dtype=jnp.int32) * Q_BLOCK
    kpos = jnp.arange(S, dtype=jnp.int32)

    def block(args):
        q_blk, c_blk, start = args
        s = jnp.einsum('bqhd,bkhd->bhqk', q_blk, k).astype(jnp.float32) * scale
        s = s + (c_blk[..., :, None] - c[..., None, :])
        qpos = start + jnp.arange(Q_BLOCK, dtype=jnp.int32)
        s = jnp.where(kpos[None, :] <= qpos[:, None], s, -jnp.inf)
        p = jax.nn.softmax(s, axis=-1).astype(v.dtype)
        return jnp.einsum('bhqk,bkhd->bqhd', p, v)

    o = lax.map(block, (qb, cb, starts))
    return o.transpose(1, 0, 2, 3, 4).reshape(B, S, H, D)


def dilated_branch(q, k, v, window, dilation):
    B, S, H, D = q.shape
    L = S // dilation
    W = window // dilation
    Qb = math.gcd(L, Q_BLOCK)
    nb = L // Qb
    scale = D ** -0.5

    def strided(t):
        return t.reshape(B, L, dilation, H, D).transpose(0, 2, 1, 3, 4)

    qs, ks, vs = strided(q), strided(k), strided(v)
    pad = ((0, 0), (0, 0), (W, 0), (0, 0), (0, 0))
    kp, vp = jnp.pad(ks, pad), jnp.pad(vs, pad)
    kidx = (jnp.arange(nb) * Qb)[:, None] + jnp.arange(Qb + W)[None, :]
    kb = kp[:, :, kidx]
    vb = vp[:, :, kidx]
    qb = qs.reshape(B, dilation, nb, Qb, H, D)
    s = jnp.einsum('brnqhd,brnkhd->brnhqk', qb, kb).astype(jnp.float32) * scale
    i = jnp.arange(Qb)[:, None]
    j = jnp.arange(Qb + W)[None, :]
    dist = i - j + W
    key_sub = (jnp.arange(nb) * Qb)[:, None, None] + j[None] - W
    valid = (dist >= 0)[None] & (dist <= W)[None] & (key_sub >= 0)
    s = jnp.where(valid[None, None, :, None], s, -jnp.inf)
    lse = jax.nn.logsumexp(s, axis=-1)
    p = jnp.exp(s - lse[..., None]).astype(v.dtype)
    o = jnp.einsum('brnhqk,brnkhd->brnqhd', p, vb)
    o = o.reshape(B, dilation, L, H, D).transpose(0, 2, 1, 3, 4).reshape(B, S, H, D)
    lse = lse.transpose(0, 1, 2, 4, 3).reshape(B, dilation, L, H).transpose(0, 2, 1, 3).reshape(B, S, H)
    return o, lse


def dilated_attention(q, k, v):
    outs, lses = [], []
    for window, dilation in DIL_CONFIGS:
        o, lse = dilated_branch(q, k, v, window, dilation)
        outs.append(o)
        lses.append(lse)
    wts = jax.nn.softmax(jnp.stack(lses, axis=0), axis=0).astype(v.dtype)
    return jnp.einsum('nbsh,nbshd->bshd', wts, jnp.stack(outs, axis=0))


def fox_dilated_mixer(x, w_in, b_forget, w_out, pos):
    B, S, _ = x.shape
    qa, ka, va, fa, qd, kd, vd = _split(x @ w_in, [FOX_WIDTH] * 3 + [N_HEADS_FOX] + [DIL_WIDTH] * 3)
    log_f = jax.nn.log_sigmoid((fa + b_forget).astype(jnp.float32))
    o_fox = forgetting_attention(qa.reshape(B, S, N_HEADS_FOX, ATT_HEAD_DIM),
                                 ka.reshape(B, S, N_HEADS_FOX, ATT_HEAD_DIM),
                                 va.reshape(B, S, N_HEADS_FOX, ATT_HEAD_DIM), log_f)
    qd = partial_rotary(qd.reshape(B, S, N_HEADS_DIL, ATT_HEAD_DIM), pos)
    kd = partial_rotary(kd.reshape(B, S, N_HEADS_DIL, ATT_HEAD_DIM), pos)
    o_dil = dilated_attention(qd, kd, vd.reshape(B, S, N_HEADS_DIL, ATT_HEAD_DIM))
    o = jnp.concatenate([o_fox.reshape(B, S, FOX_WIDTH), o_dil.reshape(B, S, DIL_WIDTH)], axis=-1)
    return o @ w_out


def causal_depthwise_conv(x, w):
    S = x.shape[1]
    K = w.shape[0]
    xp = jnp.pad(x, ((0, 0), (K - 1, 0), (0, 0)))
    y = xp[:, 0:S] * w[0]
    for i in range(1, K):
        y = y + xp[:, i:i + S] * w[i]
    return y


def mlstm_chunkwise(q, k, v, log_i, log_f):
    B, S, H, D = q.shape
    L = MLSTM_CHUNK
    nc = S // L
    f32 = jnp.float32

    def to_chunks(a):
        a = a.reshape((B, nc, L, H) + a.shape[3:])
        return jnp.moveaxis(a, (1, 3), (0, 2))

    xs = (to_chunks(q.astype(f32)), to_chunks(k.astype(f32)), to_chunks(v.astype(f32)),
          to_chunks(log_i), to_chunks(log_f))
    causal = jnp.tril(jnp.ones((L, L), dtype=bool))

    def body(carry, xc):
        C, n, m = carry
        qc, kc, vc, ic, fc = xc
        b = jnp.cumsum(fc, axis=-1)
        dlog = jnp.where(causal, b[..., :, None] - b[..., None, :] + ic[..., None, :], -jnp.inf)
        inter = b + m[..., None]
        m_t = jnp.maximum(inter, jnp.max(dlog, axis=-1))
        s = jnp.einsum('bhtd,bhsd->bhts', qc, kc) * jnp.exp(dlog - m_t[..., None])
        inter_w = jnp.exp(inter - m_t)
        num = jnp.einsum('bhts,bhsd->bhtd', s, vc) + inter_w[..., None] * jnp.einsum('bhtk,bhkv->bhtv', qc, C)
        den = jnp.sum(s, axis=-1) + inter_w * jnp.einsum('bhtk,bhk->bht', qc, n)
        h = num / jnp.maximum(jnp.abs(den), jnp.exp(-m_t))[..., None]
        b_last = b[..., -1]
        g = b_last[..., None] - b + ic
        m_new = jnp.maximum(b_last + m, jnp.max(g, axis=-1))
        w = jnp.exp(g - m_new[..., None])
        decay = jnp.exp(b_last + m - m_new)
        C = decay[..., None, None] * C + jnp.einsum('bhs,bhsk,bhsv->bhkv', w, kc, vc)
        n = decay[..., None] * n + jnp.einsum('bhs,bhsk->bhk', w, kc)
        return (C, n, m_new), h

    init = (jnp.zeros((B, H, D, D), f32), jnp.zeros((B, H, D), f32), jnp.full((B, H), -jnp.inf, f32))
    _, h = lax.scan(body, init, xs)
    return jnp.moveaxis(h, (0, 2), (1, 3)).reshape(B, S, H, D).astype(q.dtype)


def mlstm_mixer(x, w_in, b_igate, b_fgate, w_conv, norm_g, w_out):
    B, S, _ = x.shape
    qk, v, ig, fg, og = _split(x @ w_in, [2 * D_MODEL, D_MODEL, N_HEADS_MLSTM, N_HEADS_MLSTM, D_MODEL])
    qk = jax.nn.silu(causal_depthwise_conv(qk, w_conv))
    q, k = jnp.split(qk, 2, axis=-1)
    shp = (B, S, N_HEADS_MLSTM, MLSTM_HEAD_DIM)
    log_i = (ig + b_igate).astype(jnp.float32)
    log_f = jax.nn.log_sigmoid((fg + b_fgate).astype(jnp.float32))
    h = mlstm_chunkwise(q.reshape(shp), k.reshape(shp) * (MLSTM_HEAD_DIM ** -0.5), v.reshape(shp), log_i, log_f)
    hf = h.astype(jnp.float32)
    mu = jnp.mean(hf, axis=-1, keepdims=True)
    var = jnp.mean(jnp.square(hf - mu), axis=-1, keepdims=True)
    hn = ((hf - mu) * lax.rsqrt(var + LN_EPS)).reshape(B, S, D_MODEL) * norm_g
    h = hn.astype(x.dtype) * jax.nn.sigmoid(og)
    return h @ w_out


def swiglu(x, w_gate, w_up, w_down):
    return (jax.nn.silu(x @ w_gate) * (x @ w_up)) @ w_down


def moe_swiglu(x, w_router, w_gate, w_up, w_down):
    B, S, Dm = x.shape
    xt = x.reshape(B * S, Dm)
    logits = (xt @ w_router).astype(jnp.float32)
    top_v, top_i = lax.top_k(logits, TOP_K)
    wts = jax.nn.softmax(top_v, axis=-1)
    gates = jnp.einsum('tk,tke->te', wts, jax.nn.one_hot(top_i, N_EXPERTS, dtype=jnp.float32)).astype(x.dtype)
    y = jnp.zeros_like(xt)
    for e in range(N_EXPERTS):
        y = y + gates[:, e:e + 1] * swiglu(xt, w_gate[e], w_up[e], w_down[e])
    return y.reshape(B, S, Dm)


def setup_inputs(seed: int = 0) -> dict:
    key = jax.random.key(seed)
    ks = jax.random.split(key, 25)
    E, O = N_EVEN, N_ODD

    def nrm(k, shape, scale):
        return jax.random.normal(k, shape, jnp.float32) * scale

    def gain(k, shape):
        return 1.0 + 0.02 * jax.random.normal(k, shape, jnp.float32)

    return {
        'x': nrm(ks[0], (BATCH, SEQ, D_MODEL), 1.0),
        'w_in_e': nrm(ks[1], (E, D_MODEL, EVEN_IN_WIDTH), D_MODEL ** -0.5),
        'b_forget_e': jax.random.uniform(ks[2], (E, N_HEADS_FOX), dtype=jnp.float32, minval=1.0, maxval=4.0),
        'w_out_e': nrm(ks[3], (E, D_MODEL, D_MODEL), DEEPNORM_BETA * D_MODEL ** -0.5),
        'ln_mix_g_e': gain(ks[4], (E, D_MODEL)),
        'ln_mix_b_e': nrm(ks[5], (E, D_MODEL), 0.02),
        'ffn_w_gate_e': nrm(ks[6], (E, D_MODEL, D_FF_DENSE), D_MODEL ** -0.5),
        'ffn_w_up_e': nrm(ks[7], (E, D_MODEL, D_FF_DENSE), D_MODEL ** -0.5),
        'ffn_w_down_e': nrm(ks[8], (E, D_FF_DENSE, D_MODEL), DEEPNORM_BETA * D_FF_DENSE ** -0.5),
        'ln_ffn_g_e': gain(ks[9], (E, D_MODEL)),
        'ln_ffn_b_e': nrm(ks[10], (E, D_MODEL), 0.02),
        'w_in_o': nrm(ks[11], (O, D_MODEL, ODD_IN_WIDTH), D_MODEL ** -0.5),
        'b_igate_o': nrm(ks[12], (O, N_HEADS_MLSTM), 0.1),
        'b_fgate_o': jnp.linspace(3.0, 6.0, N_HEADS_MLSTM, dtype=jnp.float32)[None, :] + nrm(ks[13], (O, N_HEADS_MLSTM), 0.1),
        'w_conv_o': nrm(ks[14], (O, CONV_WIDTH, 2 * D_MODEL), CONV_WIDTH ** -0.5),
        'mlstm_norm_g_o': gain(ks[15], (O, D_MODEL)),
        'w_out_o': nrm(ks[16], (O, D_MODEL, D_MODEL), DEEPNORM_BETA * D_MODEL ** -0.5),
        'ln_mix_g_o': gain(ks[17], (O, D_MODEL)),
        'ln_mix_b_o': nrm(ks[18], (O, D_MODEL), 0.02),
        'w_router_o': nrm(ks[19], (O, D_MODEL, N_EXPERTS), D_MODEL ** -0.5),
        'moe_w_gate_o': nrm(ks[20], (O, N_EXPERTS, D_MODEL, D_FF_EXPERT), D_MODEL ** -0.5),
        'moe_w_up_o': nrm(ks[21], (O, N_EXPERTS, D_MODEL, D_FF_EXPERT), D_MODEL ** -0.5),
        'moe_w_down_o': nrm(ks[22], (O, N_EXPERTS, D_FF_EXPERT, D_MODEL), DEEPNORM_BETA * D_FF_EXPERT ** -0.5),
        'ln_ffn_g_o': gain(ks[23], (O, D_MODEL)),
        'ln_ffn_b_o': nrm(ks[24], (O, D_MODEL), 0.02),
    }


def reference(x, w_in_e, b_forget_e, w_out_e, ln_mix_g_e, ln_mix_b_e, ffn_w_gate_e, ffn_w_up_e,
              ffn_w_down_e, ln_ffn_g_e, ln_ffn_b_e, w_in_o, b_igate_o, b_fgate_o, w_conv_o,
              mlstm_norm_g_o, w_out_o, ln_mix_g_o, ln_mix_b_o, w_router_o, moe_w_gate_o,
              moe_w_up_o, moe_w_down_o, ln_ffn_g_o, ln_ffn_b_o):
    pos = jnp.arange(x.shape[1], dtype=jnp.int32)
    h = x
    for layer in range(DEPTH):
        i = layer // 2
        if layer % 2 == 0:
            mix = fox_dilated_mixer(h, w_in_e[i], b_forget_e[i], w_out_e[i], pos)
            h = layer_norm(DEEPNORM_ALPHA * h + mix, ln_mix_g_e[i], ln_mix_b_e[i])
            ffn = swiglu(h, ffn_w_gate_e[i], ffn_w_up_e[i], ffn_w_down_e[i])
            h = layer_norm(DEEPNORM_ALPHA * h + ffn, ln_ffn_g_e[i], ln_ffn_b_e[i])
        else:
            mix = mlstm_mixer(h, w_in_o[i], b_igate_o[i], b_fgate_o[i], w_conv_o[i], mlstm_norm_g_o[i], w_out_o[i])
            h = layer_norm(DEEPNORM_ALPHA * h + mix, ln_mix_g_o[i], ln_mix_b_o[i])
            ffn = moe_swiglu(h, w_router_o[i], moe_w_gate_o[i], moe_w_up_o[i], moe_w_down_o[i])
            h = layer_norm(DEEPNORM_ALPHA * h + ffn, ln_ffn_g_o[i], ln_ffn_b_o[i])
    return h
```

```python
import functools
import math

import jax
import jax.numpy as jnp
from jax import lax
from jax.experimental import pallas as pl
from jax.experimental.pallas import tpu as pltpu

F32 = jnp.float32
BF16 = jnp.bfloat16

D_MODEL = 1024
HEAD_DIM = 64
N_HEADS = 8
ATT_WIDTH = N_HEADS * HEAD_DIM
DIL_CONFIGS = ((128, 1), (512, 4), (2048, 16))
ROPE_THETA = 500000.0
ROPE_DIMS = HEAD_DIM // 4
ROPE_HALF = ROPE_DIMS // 2
MLSTM_HEAD_DIM = D_MODEL // N_HEADS
CONV_WIDTH = 4
D_FF_DENSE = 2816
N_EXPERTS = 8
D_FF_EXPERT = 3584
DEPTH = 2
DEEPNORM_ALPHA = (2 * DEPTH) ** 0.25
LN_EPS = 1e-5

LANES = 128
MXU_DIM = 256
BAND = 128
MIB = 1024 * 1024

NEG_INF = float("-inf")


def _cparams(semantics, vmem_mib):
    return pltpu.CompilerParams(dimension_semantics=semantics, vmem_limit_bytes=vmem_mib * MIB)


def _dot(a, b):
    return jnp.dot(a, b, preferred_element_type=F32)


def _dot_nt(a, b):
    return lax.dot_general(a, b, (((1,), (1,)), ((), ())), preferred_element_type=F32)


def _dot_tn(a, b):
    return lax.dot_general(a, b, (((0,), (0,)), ((), ())), preferred_element_type=F32)


def _split3(x):
    hi = x.astype(BF16)
    r = x - hi.astype(F32)
    mid = r.astype(BF16)
    lo = (r - mid.astype(F32)).astype(BF16)
    return hi, mid, lo


def _log_sigmoid(z):
    return -(jnp.maximum(-z, 0.0) + jnp.log1p(jnp.exp(-jnp.abs(z))))


def _layer_norm(z, g, b):
    mu = jnp.mean(z, axis=-1, keepdims=True)
    zc = z - mu
    var = jnp.mean(zc * zc, axis=-1, keepdims=True)
    return zc * lax.rsqrt(var + LN_EPS) * g + b


def _proj0_kernel(x_ref, w_ref, wf_ref, cos_ref, sup_ref, sdn_ref,
                  qa_ref, ka_ref, va_ref, qd_ref, kd_ref, vd_ref, ft_ref):
    xb = x_ref[...].astype(BF16)
    outs = (qa_ref, ka_ref, va_ref, qd_ref, kd_ref, vd_ref)
    scale = HEAD_DIM ** -0.5
    for j, o_ref in enumerate(outs):
        for c in range(ATT_WIDTH // LANES):
            lo = j * ATT_WIDTH + c * LANES
            r = _dot(xb, w_ref[:, lo:lo + LANES])
            if j in (3, 4):
                r = (r * cos_ref[...]
                     + pltpu.roll(r, LANES - ROPE_HALF, axis=1) * sup_ref[...]
                     + pltpu.roll(r, ROPE_HALF, axis=1) * sdn_ref[...])
            if j in (0, 3):
                r = r * scale
            o_ref[:, c * LANES:(c + 1) * LANES] = r.astype(BF16)
    ft_ref[...] = _dot_nt(wf_ref[...], xb)


def _proj0(x, w_main, wf_t, cos_t, sup_t, sdn_t, tm):
    B, S, _ = x.shape
    n_i = S // tm
    act = jax.ShapeDtypeStruct((B, S, ATT_WIDTH), BF16)
    act_spec = pl.BlockSpec((None, tm, ATT_WIDTH), lambda b, i: (b, i, 0))
    tab_spec = pl.BlockSpec((tm, LANES), lambda b, i: (i, 0))
    return pl.pallas_call(
        _proj0_kernel,
        grid=(B, n_i),
        in_specs=[
            pl.BlockSpec((None, tm, D_MODEL), lambda b, i: (b, i, 0)),
            pl.BlockSpec((D_MODEL, 6 * ATT_WIDTH), lambda b, i: (0, 0)),
            pl.BlockSpec((N_HEADS, D_MODEL), lambda b, i: (0, 0)),
            tab_spec, tab_spec, tab_spec,
        ],
        out_specs=[act_spec] * 6 + [pl.BlockSpec((None, N_HEADS, tm), lambda b, i: (b, 0, i))],
        out_shape=[act] * 6 + [jax.ShapeDtypeStruct((B, N_HEADS, S), F32)],
        compiler_params=_cparams(("parallel", "parallel"), 48),
        name="proj0",
    )(x, w_main, wf_t, cos_t, sup_t, sdn_t)


def _fox_cumsum_kernel(f_ref, bias_ref, c_ref):
    S = f_ref.shape[1]
    lf = _log_sigmoid(f_ref[...] + bias_ref[...])
    row = lax.broadcasted_iota(jnp.int32, (S, S), 0)
    col = lax.broadcasted_iota(jnp.int32, (S, S), 1)
    upper = jnp.where(row <= col, 1.0, 0.0).astype(BF16)
    hi, mid, lo = _split3(lf)
    c_ref[...] = _dot(hi, upper) + _dot(mid, upper) + _dot(lo, upper)


def _fox_cumsum(f_t, bias_col):
    R, S = f_t.shape
    return pl.pallas_call(
        _fox_cumsum_kernel,
        grid=(1,),
        in_specs=[pl.BlockSpec((R, S), lambda i: (0, 0)), pl.BlockSpec((R, 1), lambda i: (0, 0))],
        out_specs=pl.BlockSpec((R, S), lambda i: (0, 0)),
        out_shape=jax.ShapeDtypeStruct((R, S), F32),
        compiler_params=_cparams(("arbitrary",), 48),
        name="fox_cumsum",
    )(f_t, bias_col)


def _fox_kernel(q_ref, k_ref, v_ref, c_ref, o_ref, m_sc, l_sc, acc_sc, *, t):
    i = pl.program_id(1)
    j = pl.program_id(2)

    @pl.when(j == 0)
    def _():
        m_sc[...] = jnp.full(m_sc.shape, NEG_INF, F32)
        l_sc[...] = jnp.zeros(l_sc.shape, F32)
        acc_sc[...] = jnp.zeros(acc_sc.shape, F32)

    def step(masked):
        if masked:
            row = lax.broadcasted_iota(jnp.int32, (t, t), 0)
            col = lax.broadcasted_iota(jnp.int32, (t, t), 1)
            keep = col <= row
        for h in range(N_HEADS):
            sl = slice(h * HEAD_DIM, (h + 1) * HEAD_DIM)
            s = _dot_nt(q_ref[:, sl], k_ref[:, sl]) - c_ref[h:h + 1, :]
            if masked:
                s = jnp.where(keep, s, NEG_INF)
            m_prev = m_sc[h]
            m_new = jnp.maximum(m_prev, jnp.max(s, axis=-1, keepdims=True))
            alpha = jnp.exp(m_prev - m_new)
            p = jnp.exp(s - m_new)
            l_sc[h] = alpha * l_sc[h] + jnp.sum(p, axis=-1, keepdims=True)
            acc_sc[h] = alpha * acc_sc[h] + _dot(p.astype(BF16), v_ref[:, sl])
            m_sc[h] = m_new

    @pl.when(j < i)
    def _():
        step(False)

    @pl.when(j == i)
    def _():
        step(True)
        o_ref[...] = jnp.concatenate(
            [acc_sc[h] / l_sc[h] for h in range(N_HEADS)], axis=-1).astype(BF16)


def _fox_attention(q, k, v, c, t):
    B, S, _ = q.shape
    n = S // t
    q_spec = pl.BlockSpec((None, t, ATT_WIDTH), lambda b, i, j: (b, i, 0))
    kv_spec = pl.BlockSpec((None, t, ATT_WIDTH), lambda b, i, j: (b, jnp.minimum(j, i), 0))
    return pl.pallas_call(
        functools.partial(_fox_kernel, t=t),
        grid=(B, n, n),
        in_specs=[q_spec, kv_spec, kv_spec,
                  pl.BlockSpec((None, N_HEADS, t), lambda b, i, j: (b, 0, jnp.minimum(j, i)))],
        out_specs=q_spec,
        out_shape=jax.ShapeDtypeStruct((B, S, ATT_WIDTH), BF16),
        scratch_shapes=[pltpu.VMEM((N_HEADS, t, 1), F32), pltpu.VMEM((N_HEADS, t, 1), F32),
                        pltpu.VMEM((N_HEADS, t, HEAD_DIM), F32)],
        compiler_params=_cparams(("parallel", "parallel", "arbitrary"), 48),
        name="fox_attention",
    )(q, k, v, c)


def _dil_kernel(*refs, rg, sub, has_prev, first, last):
    refs = list(refs)
    q_ref, k_ref, v_ref = refs[:3]
    pos = 3
    if has_prev:
        kp_ref, vp_ref = refs[pos:pos + 2]
        pos += 2
    if not first:
        acc_in, st_in = refs[pos:pos + 2]
        pos += 2
    if last:
        o_ref = refs[pos]
    else:
        acc_out, st_out = refs[pos:pos + 2]

    blk = pl.program_id(1)
    row = lax.broadcasted_iota(jnp.int32, (BAND, BAND), 0)
    col = lax.broadcasted_iota(jnp.int32, (BAND, BAND), 1)
    keep_cur = col <= row
    keep_prev = col >= row
    keep_prev_edge = jnp.logical_and(keep_prev, blk > 0)

    for a in range(sub):
        rows = slice(a * BAND, (a + 1) * BAND)
        for rr in range(rg):
            if not first:
                st_old = st_in[rows, rr * LANES:(rr + 1) * LANES]
                acc_old = acc_in[rows, rr * ATT_WIDTH:(rr + 1) * ATT_WIDTH]
            accs, ms, ls = [], [], []
            for h in range(N_HEADS):
                lo = rr * ATT_WIDTH + h * HEAD_DIM
                sl = slice(lo, lo + HEAD_DIM)
                q = q_ref[rows, sl]
                s1 = jnp.where(keep_cur, _dot_nt(q, k_ref[rows, sl]), NEG_INF)
                m = jnp.max(s1, axis=-1, keepdims=True)
                if has_prev:
                    if a == 0:
                        k0, v0, keep0 = kp_ref[:, sl], vp_ref[:, sl], keep_prev_edge
                    else:
                        prows = slice((a - 1) * BAND, a * BAND)
                        k0, v0, keep0 = k_ref[prows, sl], v_ref[prows, sl], keep_prev
                    s0 = jnp.where(keep0, _dot_nt(q, k0), NEG_INF)
                    m = jnp.maximum(m, jnp.max(s0, axis=-1, keepdims=True))
                if not first:
                    m_old = st_old[:, h:h + 1]
                    l_old = st_old[:, N_HEADS + h:N_HEADS + h + 1]
                    m_new = jnp.maximum(m, m_old)
                    alpha = jnp.exp(m_old - m_new)
                    m = m_new
                p1 = jnp.exp(s1 - m)
                l = jnp.sum(p1, axis=-1, keepdims=True)
                acc = _dot(p1.astype(BF16), v_ref[rows, sl])
                if has_prev:
                    p0 = jnp.exp(s0 - m)
                    l = l + jnp.sum(p0, axis=-1, keepdims=True)
                    acc = acc + _dot(p0.astype(BF16), v0)
                if not first:
                    l = l + alpha * l_old
                    acc = acc + alpha * acc_old[:, h * HEAD_DIM:(h + 1) * HEAD_DIM]
                if last:
                    acc = acc / l
                accs.append(acc)
                ms.append(m)
                ls.append(l)
            out_lanes = slice(rr * ATT_WIDTH, (rr + 1) * ATT_WIDTH)
            if last:
                o_ref[rows, out_lanes] = jnp.concatenate(accs, axis=-1).astype(BF16)
            else:
                acc_out[rows, out_lanes] = jnp.concatenate(accs, axis=-1)
                pad = jnp.zeros((BAND, LANES - 2 * N_HEADS), F32)
                st_out[rows, rr * LANES:(rr + 1) * LANES] = jnp.concatenate(ms + ls + [pad], axis=-1)


def _dilated_branch(q, k, v, state, dilation, last):
    B, S, _ = q.shape
    d = dilation
    L = S // d
    first = state is None
    if d == 1:
        tqb, rg = min(4 * BAND, L), 1
    else:
        tqb, rg = BAND, min(4, d)
    sub = tqb // BAND
    n_blk = L // tqb
    n_grp = d // rg
    has_prev = L > BAND
    wq = rg * ATT_WIDTH
    wst = rg * LANES

    def view(a):
        return a.reshape(B, L, d * a.shape[-1])

    main = lambda b, i, g: (b, i, g)
    prev = lambda b, i, g: (b, jnp.maximum(i * sub - 1, 0), g)
    qkv_spec = pl.BlockSpec((None, tqb, wq), main)
    in_specs = [qkv_spec] * 3
    args = [view(q), view(k), view(v)]
    if has_prev:
        in_specs += [pl.BlockSpec((None, BAND, wq), prev)] * 2
        args += [view(k), view(v)]
    if not first:
        in_specs += [pl.BlockSpec((None, tqb, wq), main), pl.BlockSpec((None, tqb, wst), main)]
        args += [view(state[0]), view(state[1])]
    if last:
        out_specs = pl.BlockSpec((None, tqb, wq), main)
        out_shape = jax.ShapeDtypeStruct((B, L, d * ATT_WIDTH), BF16)
    else:
        out_specs = [pl.BlockSpec((None, tqb, wq), main), pl.BlockSpec((None, tqb, wst), main)]
        out_shape = [jax.ShapeDtypeStruct((B, L, d * ATT_WIDTH), F32),
                     jax.ShapeDtypeStruct((B, L, d * LANES), F32)]
    out = pl.pallas_call(
        functools.partial(_dil_kernel, rg=rg, sub=sub, has_prev=has_prev, first=first, last=last),
        grid=(B, n_blk, n_grp),
        in_specs=in_specs,
        out_specs=out_specs,
        out_shape=out_shape,
        compiler_params=_cparams(("parallel", "parallel", "parallel"), 48),
        name=f"dilated_d{d}",
    )(*args)
    if last:
        return out.reshape(B, S, ATT_WIDTH)
    return out[0].reshape(B, S, ATT_WIDTH), out[1].reshape(B, S, LANES)


def _dilated_attention(q, k, v):
    state = None
    order = sorted(DIL_CONFIGS, key=lambda wd: -wd[1])
    for n, (window, d) in enumerate(order):
        assert window // d == BAND
        state = _dilated_branch(q, k, v, state, d, last=(n == len(order) - 1))
    return state


def _outproj_ln_kernel(*refs, n_in):
    a_refs = refs[:n_in]
    w_refs = refs[n_in:2 * n_in]
    x_ref, g_ref, b_ref, o_ref = refs[2 * n_in:]
    y = _dot(a_refs[0][...], w_refs[0][...])
    for a_ref, w_ref in zip(a_refs[1:], w_refs[1:]):
        y = y + _dot(a_ref[...], w_ref[...])
    o_ref[...] = _layer_norm(DEEPNORM_ALPHA * x_ref[...] + y, g_ref[...], b_ref[...])


def _outproj_ln(acts, ws, x, g, b, tm):
    T = x.shape[0]
    n_in = len(acts)
    in_specs = [pl.BlockSpec((tm, a.shape[1]), lambda i: (i, 0)) for a in acts]
    in_specs += [pl.BlockSpec(w.shape, lambda i: (0, 0)) for w in ws]
    row = pl.BlockSpec((tm, D_MODEL), lambda i: (i, 0))
    vec = pl.BlockSpec((1, D_MODEL), lambda i: (0, 0))
    return pl.pallas_call(
        functools.partial(_outproj_ln_kernel, n_in=n_in),
        grid=(T // tm,),
        in_specs=in_specs + [row, vec, vec],
        out_specs=row,
        out_shape=jax.ShapeDtypeStruct((T, D_MODEL), F32),
        compiler_params=_cparams(("parallel",), 48),
        name="outproj_ln",
    )(*acts, *ws, x, g, b)


def _ffn_chunks(width):
    chunks, lo = [], 0
    while lo < width:
        size = min(2 * MXU_DIM, width - lo)
        chunks.append((lo, size))
        lo += size
    return chunks


def _dense_ffn_kernel(x_ref, wg_ref, wu_ref, wd_ref, g_ref, b_ref, o_ref):
    x = x_ref[...]
    xb = x.astype(BF16)
    y = None
    for lo, size in _ffn_chunks(wg_ref.shape[1]):
        gate = _dot(xb, wg_ref[:, lo:lo + size])
        up = _dot(xb, wu_ref[:, lo:lo + size])
        act = (gate * jax.nn.sigmoid(gate) * up).astype(BF16)
        part = _dot(act, wd_ref[lo:lo + size, :])
        y = part if y is None else y + part
    o_ref[...] = _layer_norm(DEEPNORM_ALPHA * x + y, g_ref[...], b_ref[...])


def _dense_ffn(x, wg, wu, wd, g, b, tm):
    T = x.shape[0]
    F = wg.shape[1]
    row = pl.BlockSpec((tm, D_MODEL), lambda i: (i, 0))
    vec = pl.BlockSpec((1, D_MODEL), lambda i: (0, 0))
    once = pl.Buffered(1)
    return pl.pallas_call(
        _dense_ffn_kernel,
        grid=(T // tm,),
        in_specs=[row,
                  pl.BlockSpec((D_MODEL, F), lambda i: (0, 0), pipeline_mode=once),
                  pl.BlockSpec((D_MODEL, F), lambda i: (0, 0), pipeline_mode=once),
                  pl.BlockSpec((F, D_MODEL), lambda i: (0, 0), pipeline_mode=once),
                  vec, vec],
        out_specs=row,
        out_shape=jax.ShapeDtypeStruct((T, D_MODEL), F32),
        compiler_params=_cparams(("parallel",), 56),
        name="dense_ffn",
    )(x, wg, wu, wd, g, b)


CONV_PAD = 8


def _proj1_kernel(x_ref, wqk_ref, wv_ref, wog_ref, wgt_ref, wconv_ref,
                  q_ref, k_ref, v_ref, og_ref, gt_ref, buf):
    i = pl.program_id(1)
    tm = x_ref.shape[0]
    xb = x_ref[...].astype(BF16)
    kscale = MLSTM_HEAD_DIM ** -0.5

    @pl.when(i == 0)
    def _():
        buf[0:CONV_PAD, :] = jnp.zeros((CONV_PAD, buf.shape[1]), F32)

    for c in range(2 * D_MODEL // LANES):
        lanes = slice(c * LANES, (c + 1) * LANES)
        buf[CONV_PAD:CONV_PAD + tm, lanes] = _dot(xb, wqk_ref[:, lanes])
        y = None
        for tap in range(CONV_WIDTH):
            off = CONV_PAD - (CONV_WIDTH - 1) + tap
            term = buf[off:off + tm, lanes] * wconv_ref[tap:tap + 1, lanes]
            y = term if y is None else y + term
        y = y * jax.nn.sigmoid(y)
        buf[0:CONV_PAD, lanes] = buf[tm:tm + CONV_PAD, lanes]
        if c < D_MODEL // LANES:
            q_ref[:, lanes] = y.astype(BF16)
        else:
            k_ref[:, c * LANES - D_MODEL:(c + 1) * LANES - D_MODEL] = (y * kscale).astype(BF16)
    for c in range(D_MODEL // LANES):
        lanes = slice(c * LANES, (c + 1) * LANES)
        v_ref[:, lanes] = _dot(xb, wv_ref[:, lanes]).astype(BF16)
        og_ref[:, lanes] = _dot(xb, wog_ref[:, lanes])
    gt_ref[...] = _dot(xb, wgt_ref[...])


def _proj1(x, wqk, wv, wog, wgt, wconv, tm):
    B, S, _ = x.shape
    row = lambda b, i: (b, i, 0)
    const = lambda b, i: (0, 0)
    act_spec = pl.BlockSpec((None, tm, D_MODEL), row)
    act = jax.ShapeDtypeStruct((B, S, D_MODEL), BF16)
    return pl.pallas_call(
        _proj1_kernel,
        grid=(B, S // tm),
        in_specs=[act_spec,
                  pl.BlockSpec((D_MODEL, 2 * D_MODEL), const),
                  pl.BlockSpec((D_MODEL, D_MODEL), const),
                  pl.BlockSpec((D_MODEL, D_MODEL), const),
                  pl.BlockSpec((D_MODEL, 2 * N_HEADS), const),
                  pl.BlockSpec((CONV_WIDTH, 2 * D_MODEL), const)],
        out_specs=[act_spec, act_spec, act_spec, act_spec,
                   pl.BlockSpec((None, tm, 2 * N_HEADS), row)],
        out_shape=[act, act, act, jax.ShapeDtypeStruct((B, S, D_MODEL), F32),
                   jax.ShapeDtypeStruct((B, S, 2 * N_HEADS), F32)],
        scratch_shapes=[pltpu.VMEM((tm + CONV_PAD, 2 * D_MODEL), F32)],
        compiler_params=_cparams(("parallel", "arbitrary"), 56),
        name="proj1",
    )(x, wqk, wv, wog, wgt, wconv)


def _mlstm_kernel(q_ref, k_ref, v_ref, og_ref, gn_ref, gt_ref, bi_row, bf_row, bi_col, bf_col,
                  ng_ref, o_ref, c_sc, n_sc, m_sc, *, L):
    ci = pl.program_id(1)

    @pl.when(ci == 0)
    def _():
        c_sc[...] = jnp.zeros(c_sc.shape, F32)
        n_sc[...] = jnp.zeros(n_sc.shape, F32)
        m_sc[...] = jnp.full(m_sc.shape, NEG_INF, F32)

    row = lax.broadcasted_iota(jnp.int32, (L, L), 0)
    col = lax.broadcasted_iota(jnp.int32, (L, L), 1)
    causal = col <= row
    lower = jnp.where(causal, 1.0, 0.0).astype(BF16)
    upper = jnp.where(row <= col, 1.0, 0.0).astype(BF16)

    gn = gn_ref[...]
    gt = gt_ref[...]
    i_col = gn[:, :N_HEADS] + bi_row[...]
    lf_col = _log_sigmoid(gn[:, N_HEADS:] + bf_row[...])
    i_row = gt[:N_HEADS, :] + bi_col[...]
    lf_row = _log_sigmoid(gt[N_HEADS:, :] + bf_col[...])
    b_col = sum(_dot(lower, part) for part in _split3(lf_col))
    b_row = sum(_dot(part, upper) for part in _split3(lf_row))

    for h in range(N_HEADS):
        lanes = slice(h * MLSTM_HEAD_DIM, (h + 1) * MLSTM_HEAD_DIM)
        q = q_ref[:, lanes]
        k = k_ref[:, lanes]
        v = v_ref[:, lanes]
        bc = b_col[:, h:h + 1]
        ic = i_col[:, h:h + 1]
        br = b_row[h:h + 1, :]
        ir = i_row[h:h + 1, :]
        m_prev = m_sc[h]
        c_prev = c_sc[h]
        n_prev = n_sc[h]

        dlog = jnp.where(causal, bc - br + ir, NEG_INF)
        inter = bc + m_prev
        m_t = jnp.maximum(inter, jnp.max(dlog, axis=-1, keepdims=True))
        s = _dot_nt(q, k) * jnp.exp(dlog - m_t)
        inter_w = jnp.exp(inter - m_t)
        num = _dot(s.astype(BF16), v) + inter_w * _dot(q, c_prev.astype(BF16))
        den = (jnp.sum(s, axis=-1, keepdims=True)
               + inter_w * jnp.sum(q.astype(F32) * n_prev, axis=-1, keepdims=True))
        hh = num / jnp.maximum(jnp.abs(den), jnp.exp(-m_t))

        b_last = bc[L - 1:L, :]
        g = b_last - bc + ic
        m_new = jnp.maximum(b_last + m_prev, jnp.max(g, axis=0, keepdims=True))
        w = jnp.exp(g - m_new)
        decay = jnp.exp(b_last + m_prev - m_new)
        kw = k.astype(F32) * w
        c_sc[h] = decay * c_prev + _dot_tn(kw.astype(BF16), v)
        n_sc[h] = decay * n_prev + jnp.sum(kw, axis=0, keepdims=True)
        m_sc[h] = m_new

        mu = jnp.mean(hh, axis=-1, keepdims=True)
        hc = hh - mu
        var = jnp.mean(hc * hc, axis=-1, keepdims=True)
        hn = hc * lax.rsqrt(var + LN_EPS) * ng_ref[:, lanes]
        o_ref[:, lanes] = (hn * jax.nn.sigmoid(og_ref[:, lanes])).astype(BF16)


def _mlstm(q, k, v, og, gates, gates_t, b_i, b_f, norm_g, L):
    B, S, _ = q.shape
    row = lambda b, c: (b, c, 0)
    const = lambda b, c: (0, 0)
    act_spec = pl.BlockSpec((None, L, D_MODEL), row)
    return pl.pallas_call(
        functools.partial(_mlstm_kernel, L=L),
        grid=(B, S // L),
        in_specs=[act_spec, act_spec, act_spec, act_spec,
                  pl.BlockSpec((None, L, 2 * N_HEADS), row),
                  pl.BlockSpec((None, 2 * N_HEADS, L), lambda b, c: (b, 0, c)),
                  pl.BlockSpec((1, N_HEADS), const), pl.BlockSpec((1, N_HEADS), const),
                  pl.BlockSpec((N_HEADS, 1), const), pl.BlockSpec((N_HEADS, 1), const),
                  pl.BlockSpec((1, D_MODEL), const)],
        out_specs=act_spec,
        out_shape=jax.ShapeDtypeStruct((B, S, D_MODEL), BF16),
        scratch_shapes=[pltpu.VMEM((N_HEADS, MLSTM_HEAD_DIM, MLSTM_HEAD_DIM), F32),
                        pltpu.VMEM((N_HEADS, 1, MLSTM_HEAD_DIM), F32),
                        pltpu.VMEM((N_HEADS, 1, 1), F32)],
        compiler_params=_cparams(("parallel", "arbitrary"), 48),
        name="mlstm",
    )(q, k, v, og, gates, gates_t, b_i[None, :], b_f[None, :], b_i[:, None], b_f[:, None],
      norm_g[None, :])


def _router_kernel(x_ref, w_ref, gate_ref):
    logits = jnp.dot(x_ref[...], w_ref[...], preferred_element_type=F32,
                     precision=lax.Precision.HIGHEST)
    idx = lax.broadcasted_iota(jnp.int32, logits.shape, 1)
    m1 = jnp.max(logits, axis=-1, keepdims=True)
    i1 = jnp.min(jnp.where(logits == m1, idx, N_EXPERTS), axis=-1, keepdims=True)
    pick1 = idx == i1
    rest = jnp.where(pick1, NEG_INF, logits)
    m2 = jnp.max(rest, axis=-1, keepdims=True)
    i2 = jnp.min(jnp.where(rest == m2, idx, N_EXPERTS), axis=-1, keepdims=True)
    pick2 = idx == i2
    e2 = jnp.exp(m2 - m1)
    w1 = 1.0 / (1.0 + e2)
    w2 = e2 / (1.0 + e2)
    gate_ref[...] = jnp.where(pick1, w1, 0.0) + jnp.where(pick2, w2, 0.0)


def _router(x, w_router, tm):
    T = x.shape[0]
    return pl.pallas_call(
        _router_kernel,
        grid=(T // tm,),
        in_specs=[pl.BlockSpec((tm, D_MODEL), lambda i: (i, 0)),
                  pl.BlockSpec((D_MODEL, N_EXPERTS), lambda i: (0, 0))],
        out_specs=pl.BlockSpec((tm, N_EXPERTS), lambda i: (i, 0)),
        out_shape=jax.ShapeDtypeStruct((T, N_EXPERTS), F32),
        compiler_params=_cparams(("parallel",), 32),
        name="moe_router",
    )(x, w_router)


def _moe_dense_kernel(x_ref, gate_ref, wg_ref, wu_ref, wd_ref, g_ref, b_ref, o_ref,
                      xb_sc, ye_sc, y_sc):
    e = pl.program_id(1)
    f = pl.program_id(2)
    n_e = pl.num_programs(1)
    n_f = pl.num_programs(2)

    @pl.when(jnp.logical_and(e == 0, f == 0))
    def _():
        xb_sc[...] = x_ref[...].astype(BF16)
        y_sc[...] = jnp.zeros(y_sc.shape, F32)

    xb = xb_sc[...]
    part = None
    for lo, size in _ffn_chunks(wg_ref.shape[1]):
        gate = _dot(xb, wg_ref[:, lo:lo + size])
        up = _dot(xb, wu_ref[:, lo:lo + size])
        act = (gate * jax.nn.sigmoid(gate) * up).astype(BF16)
        p = _dot(act, wd_ref[lo:lo + size, :])
        part = p if part is None else part + p

    @pl.when(f == 0)
    def _():
        ye_sc[...] = part

    @pl.when(f > 0)
    def _():
        ye_sc[...] += part

    @pl.when(f == n_f - 1)
    def _():
        idx = lax.broadcasted_iota(jnp.int32, gate_ref.shape, 1)
        ge = jnp.sum(jnp.where(idx == e, gate_ref[...], 0.0), axis=-1, keepdims=True)
        y_sc[...] += ge * ye_sc[...]

    @pl.when(jnp.logical_and(e == n_e - 1, f == n_f - 1))
    def _():
        o_ref[...] = _layer_norm(DEEPNORM_ALPHA * x_ref[...] + y_sc[...], g_ref[...], b_ref[...])


def _moe_dense(x, gates, wg, wu, wd, g, b, tm, fc):
    T = x.shape[0]
    E, _, F = wg.shape
    row = pl.BlockSpec((tm, D_MODEL), lambda i, e, f: (i, 0))
    vec = pl.BlockSpec((1, D_MODEL), lambda i, e, f: (0, 0))
    return pl.pallas_call(
        _moe_dense_kernel,
        grid=(T // tm, E, F // fc),
        in_specs=[row,
                  pl.BlockSpec((tm, N_EXPERTS), lambda i, e, f: (i, 0)),
                  pl.BlockSpec((None, D_MODEL, fc), lambda i, e, f: (e, 0, f)),
                  pl.BlockSpec((None, D_MODEL, fc), lambda i, e, f: (e, 0, f)),
                  pl.BlockSpec((None, fc, D_MODEL), lambda i, e, f: (e, f, 0)),
                  vec, vec],
        out_specs=row,
        out_shape=jax.ShapeDtypeStruct((T, D_MODEL), F32),
        scratch_shapes=[pltpu.VMEM((tm, D_MODEL), BF16), pltpu.VMEM((tm, D_MODEL), F32),
                        pltpu.VMEM((tm, D_MODEL), F32)],
        compiler_params=_cparams(("parallel", "arbitrary", "arbitrary"), 56),
        name="moe_dense",
    )(x, gates, wg, wu, wd, g, b)


def _rotary_tables(S):
    pos = jnp.arange(S, dtype=jnp.int32)
    inv = ROPE_THETA ** (-jnp.arange(ROPE_HALF, dtype=F32) / ROPE_HALF)
    ang = pos.astype(F32)[:, None] * inv[None, :]
    cos, sin = jnp.cos(ang), jnp.sin(ang)
    ones = jnp.ones((S, HEAD_DIM - ROPE_DIMS), F32)
    zeros = jnp.zeros((S, HEAD_DIM - ROPE_DIMS), F32)
    z8 = jnp.zeros((S, ROPE_HALF), F32)
    cos_h = jnp.concatenate([cos, cos, ones], axis=1)
    sup_h = jnp.concatenate([-sin, z8, zeros], axis=1)
    sdn_h = jnp.concatenate([z8, sin, zeros], axis=1)
    rep = LANES // HEAD_DIM
    return (jnp.tile(cos_h, (1, rep)), jnp.tile(sup_h, (1, rep)), jnp.tile(sdn_h, (1, rep)))


def _even_layer(h, w_in, b_forget, w_out, ln_mix_g, ln_mix_b, w_gate, w_up, w_down,
                ln_ffn_g, ln_ffn_b):
    B, S, _ = h.shape
    T = B * S
    W = ATT_WIDTH
    w_main = jnp.concatenate([w_in[:, :3 * W], w_in[:, 3 * W + N_HEADS:]], axis=1).astype(BF16)
    wf_t = w_in[:, 3 * W:3 * W + N_HEADS].T.astype(BF16)
    cos_t, sup_t, sdn_t = _rotary_tables(S)
    qa, ka, va, qd, kd, vd, f_t = _proj0(h, w_main, wf_t, cos_t, sup_t, sdn_t, tm=min(512, S))

    bias_col = jnp.tile(b_forget, B)[:, None]
    c = _fox_cumsum(f_t.reshape(B * N_HEADS, S), bias_col).reshape(B, N_HEADS, S)
    o_fox = _fox_attention(qa, ka, va, c, t=min(512, S))
    o_dil = _dilated_attention(qd, kd, vd)

    w_out_b = w_out.astype(BF16)
    tm = 512
    h1 = _outproj_ln([o_fox.reshape(T, W), o_dil.reshape(T, W)], [w_out_b[:W], w_out_b[W:]],
                     h.reshape(T, D_MODEL), ln_mix_g[None, :], ln_mix_b[None, :], tm)
    h2 = _dense_ffn(h1, w_gate.astype(BF16), w_up.astype(BF16), w_down.astype(BF16),
                    ln_ffn_g[None, :], ln_ffn_b[None, :], tm)
    return h2.reshape(B, S, D_MODEL)


def _odd_layer(h, w_in, b_igate, b_fgate, w_conv, norm_g, w_out, ln_mix_g, ln_mix_b, w_router,
               w_gate, w_up, w_down, ln_ffn_g, ln_ffn_b):
    B, S, _ = h.shape
    T = B * S
    D = D_MODEL
    wqk = w_in[:, :2 * D].astype(BF16)
    wv = w_in[:, 2 * D:3 * D].astype(BF16)
    wgt = w_in[:, 3 * D:3 * D + 2 * N_HEADS].astype(BF16)
    wog = w_in[:, 3 * D + 2 * N_HEADS:].astype(BF16)
    q, k, v, og, gates = _proj1(h, wqk, wv, wog, wgt, w_conv, tm=min(512, S))
    gates_t = jnp.swapaxes(gates, 1, 2)
    hm = _mlstm(q, k, v, og, gates, gates_t, b_igate, b_fgate, norm_g, L=min(256, S))

    tm = 512
    h1 = _outproj_ln([hm.reshape(T, D)], [w_out.astype(BF16)], h.reshape(T, D),
                     ln_mix_g[None, :], ln_mix_b[None, :], tm)
    gate_w = _router(h1, w_router, tm=1024)
    h2 = _moe_dense(h1, gate_w, w_gate.astype(BF16), w_up.astype(BF16), w_down.astype(BF16),
                    ln_ffn_g[None, :], ln_ffn_b[None, :], tm=512, fc=D_FF_EXPERT // 2)
    return h2.reshape(B, S, D)


def kernel(x, w_in_e, b_forget_e, w_out_e, ln_mix_g_e, ln_mix_b_e, ffn_w_gate_e, ffn_w_up_e,
           ffn_w_down_e, ln_ffn_g_e, ln_ffn_b_e, w_in_o, b_igate_o, b_fgate_o, w_conv_o,
           mlstm_norm_g_o, w_out_o, ln_mix_g_o, ln_mix_b_o, w_router_o, moe_w_gate_o,
           moe_w_up_o, moe_w_down_o, ln_ffn_g_o, ln_ffn_b_o):
    h = x
    for layer in range(DEPTH):
        i = layer // 2
        if layer % 2 == 0:
            h = _even_layer(h, w_in_e[i], b_forget_e[i], w_out_e[i], ln_mix_g_e[i], ln_mix_b_e[i],
                            ffn_w_gate_e[i], ffn_w_up_e[i], ffn_w_down_e[i], ln_ffn_g_e[i],
                            ln_ffn_b_e[i])
        else:
            h = _odd_layer(h, w_in_o[i], b_igate_o[i], b_fgate_o[i], w_conv_o[i],
                           mlstm_norm_g_o[i], w_out_o[i], ln_mix_g_o[i], ln_mix_b_o[i],
                           w_router_o[i], moe_w_gate_o[i], moe_w_up_o[i], moe_w_down_o[i],
                           ln_ffn_g_o[i], ln_ffn_b_o[i])
    return h
```

```python
import functools
import math

import jax
import jax.numpy as jnp
from jax import lax
from jax.experimental import pallas as pl
from jax.experimental.pallas import tpu as pltpu

F32 = jnp.float32
BF16 = jnp.bfloat16

D_MODEL = 1024
HEAD_DIM = 64
N_HEADS = 8
ATT_WIDTH = N_HEADS * HEAD_DIM
DIL_CONFIGS = ((128, 1), (512, 4), (2048, 16))
ROPE_THETA = 500000.0
ROPE_DIMS = HEAD_DIM // 4
ROPE_HALF = ROPE_DIMS // 2
MLSTM_HEAD_DIM = D_MODEL // N_HEADS
CONV_WIDTH = 4
D_FF_DENSE = 2816
N_EXPERTS = 8
D_FF_EXPERT = 3584
DEPTH = 2
DEEPNORM_ALPHA = (2 * DEPTH) ** 0.25
LN_EPS = 1e-5

LANES = 128
MXU_DIM = 256
BAND = 128
MIB = 1024 * 1024

NEG_INF = float("-inf")


def _cparams(semantics, vmem_mib):
    return pltpu.CompilerParams(dimension_semantics=semantics, vmem_limit_bytes=vmem_mib * MIB)


def _dot(a, b):
    return jnp.dot(a, b, preferred_element_type=F32)


def _dot_nt(a, b):
    return lax.dot_general(a, b, (((1,), (1,)), ((), ())), preferred_element_type=F32)


def _dot_tn(a, b):
    return lax.dot_general(a, b, (((0,), (0,)), ((), ())), preferred_element_type=F32)


def _split3(x):
    hi = x.astype(BF16)
    r = x - hi.astype(F32)
    mid = r.astype(BF16)
    lo = (r - mid.astype(F32)).astype(BF16)
    return hi, mid, lo


def _log_sigmoid(z):
    return -(jnp.maximum(-z, 0.0) + jnp.log1p(jnp.exp(-jnp.abs(z))))


def _layer_norm(z, g, b):
    mu = jnp.mean(z, axis=-1, keepdims=True)
    zc = z - mu
    var = jnp.mean(zc * zc, axis=-1, keepdims=True)
    return zc * lax.rsqrt(var + LN_EPS) * g + b


def _proj0_kernel(x_ref, w_ref, wf_ref, cos_ref, sup_ref, sdn_ref,
                  qa_ref, ka_ref, va_ref, qd_ref, kd_ref, vd_ref, ft_ref):
    xb = x_ref[...].astype(BF16)
    outs = (qa_ref, ka_ref, va_ref, qd_ref, kd_ref, vd_ref)
    scale = HEAD_DIM ** -0.5
    for j, o_ref in enumerate(outs):
        for c in range(ATT_WIDTH // LANES):
            lo = j * ATT_WIDTH + c * LANES
            r = _dot(xb, w_ref[:, lo:lo + LANES])
            if j in (3, 4):
                r = (r * cos_ref[...]
                     + pltpu.roll(r, LANES - ROPE_HALF, axis=1) * sup_ref[...]
                     + pltpu.roll(r, ROPE_HALF, axis=1) * sdn_ref[...])
            if j in (0, 3):
                r = r * scale
            o_ref[:, c * LANES:(c + 1) * LANES] = r.astype(BF16)
    ft_ref[...] = _dot_nt(wf_ref[...], xb)


def _proj0(x, w_main, wf_t, cos_t, sup_t, sdn_t, tm):
    B, S, _ = x.shape
    n_i = S // tm
    act = jax.ShapeDtypeStruct((B, S, ATT_WIDTH), BF16)
    act_spec = pl.BlockSpec((None, tm, ATT_WIDTH), lambda b, i: (b, i, 0))
    tab_spec = pl.BlockSpec((tm, LANES), lambda b, i: (i, 0))
    return pl.pallas_call(
        _proj0_kernel,
        grid=(B, n_i),
        in_specs=[
            pl.BlockSpec((None, tm, D_MODEL), lambda b, i: (b, i, 0)),
            pl.BlockSpec((D_MODEL, 6 * ATT_WIDTH), lambda b, i: (0, 0)),
            pl.BlockSpec((N_HEADS, D_MODEL), lambda b, i: (0, 0)),
            tab_spec, tab_spec, tab_spec,
        ],
        out_specs=[act_spec] * 6 + [pl.BlockSpec((None, N_HEADS, tm), lambda b, i: (b, 0, i))],
        out_shape=[act] * 6 + [jax.ShapeDtypeStruct((B, N_HEADS, S), F32)],
        compiler_params=_cparams(("parallel", "parallel"), 48),
        name="proj0",
    )(x, w_main, wf_t, cos_t, sup_t, sdn_t)


def _fox_cumsum_kernel(f_ref, bias_ref, c_ref):
    S = f_ref.shape[1]
    lf = _log_sigmoid(f_ref[...] + bias_ref[...])
    row = lax.broadcasted_iota(jnp.int32, (S, S), 0)
    col = lax.broadcasted_iota(jnp.int32, (S, S), 1)
    upper = jnp.where(row <= col, 1.0, 0.0).astype(BF16)
    hi, mid, lo = _split3(lf)
    c_ref[...] = _dot(hi, upper) + _dot(mid, upper) + _dot(lo, upper)


def _fox_cumsum(f_t, bias_col):
    R, S = f_t.shape
    return pl.pallas_call(
        _fox_cumsum_kernel,
        grid=(1,),
        in_specs=[pl.BlockSpec((R, S), lambda i: (0, 0)), pl.BlockSpec((R, 1), lambda i: (0, 0))],
        out_specs=pl.BlockSpec((R, S), lambda i: (0, 0)),
        out_shape=jax.ShapeDtypeStruct((R, S), F32),
        compiler_params=_cparams(("arbitrary",), 48),
        name="fox_cumsum",
    )(f_t, bias_col)


def _fox_kernel(q_ref, k_ref, v_ref, c_ref, o_ref, m_sc, l_sc, acc_sc, *, t):
    i = pl.program_id(1)
    j = pl.program_id(2)

    @pl.when(j == 0)
    def _():
        m_sc[...] = jnp.full(m_sc.shape, NEG_INF, F32)
        l_sc[...] = jnp.zeros(l_sc.shape, F32)
        acc_sc[...] = jnp.zeros(acc_sc.shape, F32)

    def step(masked):
        if masked:
            row = lax.broadcasted_iota(jnp.int32, (t, t), 0)
            col = lax.broadcasted_iota(jnp.int32, (t, t), 1)
            keep = col <= row
        for h in range(N_HEADS):
            sl = slice(h * HEAD_DIM, (h + 1) * HEAD_DIM)
            s = _dot_nt(q_ref[:, sl], k_ref[:, sl]) - c_ref[h:h + 1, :]
            if masked:
                s = jnp.where(keep, s, NEG_INF)
            m_prev = m_sc[h]
            m_new = jnp.maximum(m_prev, jnp.max(s, axis=-1, keepdims=True))
            alpha = jnp.exp(m_prev - m_new)
            p = jnp.exp(s - m_new)
            l_sc[h] = alpha * l_sc[h] + jnp.sum(p, axis=-1, keepdims=True)
            acc_sc[h] = alpha * acc_sc[h] + _dot(p.astype(BF16), v_ref[:, sl])
            m_sc[h] = m_new

    @pl.when(j < i)
    def _():
        step(False)

    @pl.when(j == i)
    def _():
        step(True)
        o_ref[...] = jnp.concatenate(
            [acc_sc[h] / l_sc[h] for h in range(N_HEADS)], axis=-1).astype(BF16)


def _fox_attention(q, k, v, c, t):
    B, S, _ = q.shape
    n = S // t
    q_spec = pl.BlockSpec((None, t, ATT_WIDTH), lambda b, i, j: (b, i, 0))
    kv_spec = pl.BlockSpec((None, t, ATT_WIDTH), lambda b, i, j: (b, jnp.minimum(j, i), 0))
    return pl.pallas_call(
        functools.partial(_fox_kernel, t=t),
        grid=(B, n, n),
        in_specs=[q_spec, kv_spec, kv_spec,
                  pl.BlockSpec((None, N_HEADS, t), lambda b, i, j: (b, 0, jnp.minimum(j, i)))],
        out_specs=q_spec,
        out_shape=jax.ShapeDtypeStruct((B, S, ATT_WIDTH), BF16),
        scratch_shapes=[pltpu.VMEM((N_HEADS, t, 1), F32), pltpu.VMEM((N_HEADS, t, 1), F32),
                        pltpu.VMEM((N_HEADS, t, HEAD_DIM), F32)],
        compiler_params=_cparams(("parallel", "parallel", "arbitrary"), 48),
        name="fox_attention",
    )(q, k, v, c)


def _dil_kernel(*refs, rg, sub, has_prev, first, last):
    refs = list(refs)
    q_ref, k_ref, v_ref = refs[:3]
    pos = 3
    if has_prev:
        kp_ref, vp_ref = refs[pos:pos + 2]
        pos += 2
    if not first:
        acc_in, st_in = refs[pos:pos + 2]
        pos += 2
    if last:
        o_ref = refs[pos]
    else:
        acc_out, st_out = refs[pos:pos + 2]

    blk = pl.program_id(1)
    row = lax.broadcasted_iota(jnp.int32, (BAND, BAND), 0)
    col = lax.broadcasted_iota(jnp.int32, (BAND, BAND), 1)
    keep_cur = col <= row
    keep_prev = col >= row
    keep_prev_edge = jnp.logical_and(keep_prev, blk > 0)

    for a in range(sub):
        rows = slice(a * BAND, (a + 1) * BAND)
        for rr in range(rg):
            if not first:
                st_old = st_in[rows, rr * LANES:(rr + 1) * LANES]
                acc_old = acc_in[rows, rr * ATT_WIDTH:(rr + 1) * ATT_WIDTH]
            accs, ms, ls = [], [], []
            for h in range(N_HEADS):
                lo = rr * ATT_WIDTH + h * HEAD_DIM
                sl = slice(lo, lo + HEAD_DIM)
                q = q_ref[rows, sl]
                s1 = jnp.where(keep_cur, _dot_nt(q, k_ref[rows, sl]), NEG_INF)
                m = jnp.max(s1, axis=-1, keepdims=True)
                if has_prev:
                    if a == 0:
                        k0, v0, keep0 = kp_ref[:, sl], vp_ref[:, sl], keep_prev_edge
                    else:
                        prows = slice((a - 1) * BAND, a * BAND)
                        k0, v0, keep0 = k_ref[prows, sl], v_ref[prows, sl], keep_prev
                    s0 = jnp.where(keep0, _dot_nt(q, k0), NEG_INF)
                    m = jnp.maximum(m, jnp.max(s0, axis=-1, keepdims=True))
                if not first:
                    m_old = st_old[:, h:h + 1]
                    l_old = st_old[:, N_HEADS + h:N_HEADS + h + 1]
                    m_new = jnp.maximum(m, m_old)
                    alpha = jnp.exp(m_old - m_new)
                    m = m_new
                p1 = jnp.exp(s1 - m)
                l = jnp.sum(p1, axis=-1, keepdims=True)
                acc = _dot(p1.astype(BF16), v_ref[rows, sl])
                if has_prev:
                    p0 = jnp.exp(s0 - m)
                    l = l + jnp.sum(p0, axis=-1, keepdims=True)
                    acc = acc + _dot(p0.astype(BF16), v0)
                if not first:
                    l = l + alpha * l_old
                    acc = acc + alpha * acc_old[:, h * HEAD_DIM:(h + 1) * HEAD_DIM]
                if last:
                    acc = acc / l
                accs.append(acc)
                ms.append(m)
                ls.append(l)
            out_lanes = slice(rr * ATT_WIDTH, (rr + 1) * ATT_WIDTH)
            if last:
                o_ref[rows, out_lanes] = jnp.concatenate(accs, axis=-1).astype(BF16)
            else:
                acc_out[rows, out_lanes] = jnp.concatenate(accs, axis=-1)
                pad = jnp.zeros((BAND, LANES - 2 * N_HEADS), F32)
                st_out[rows, rr * LANES:(rr + 1) * LANES] = jnp.concatenate(ms + ls + [pad], axis=-1)


def _dilated_branch(q, k, v, state, dilation, last):
    B, S, _ = q.shape
    d = dilation
    L = S // d
    first = state is None
    if d == 1:
        tqb, rg = min(4 * BAND, L), 1
    else:
        tqb, rg = BAND, min(4, d)
    sub = tqb // BAND
    n_blk = L // tqb
    n_grp = d // rg
    has_prev = L > BAND
    wq = rg * ATT_WIDTH
    wst = rg * LANES

    def view(a):
        return a.reshape(B, L, d * a.shape[-1])

    main = lambda b, i, g: (b, i, g)
    prev = lambda b, i, g: (b, jnp.maximum(i * sub - 1, 0), g)
    qkv_spec = pl.BlockSpec((None, tqb, wq), main)
    in_specs = [qkv_spec] * 3
    args = [view(q), view(k), view(v)]
    if has_prev:
        in_specs += [pl.BlockSpec((None, BAND, wq), prev)] * 2
        args += [view(k), view(v)]
    if not first:
        in_specs += [pl.BlockSpec((None, tqb, wq), main), pl.BlockSpec((None, tqb, wst), main)]
        args += [view(state[0]), view(state[1])]
    if last:
        out_specs = pl.BlockSpec((None, tqb, wq), main)
        out_shape = jax.ShapeDtypeStruct((B, L, d * ATT_WIDTH), BF16)
    else:
        out_specs = [pl.BlockSpec((None, tqb, wq), main), pl.BlockSpec((None, tqb, wst), main)]
        out_shape = [jax.ShapeDtypeStruct((B, L, d * ATT_WIDTH), F32),
                     jax.ShapeDtypeStruct((B, L, d * LANES), F32)]
    out = pl.pallas_call(
        functools.partial(_dil_kernel, rg=rg, sub=sub, has_prev=has_prev, first=first, last=last),
        grid=(B, n_blk, n_grp),
        in_specs=in_specs,
        out_specs=out_specs,
        out_shape=out_shape,
        compiler_params=_cparams(("parallel", "parallel", "parallel"), 48),
        name=f"dilated_d{d}",
    )(*args)
    if last:
        return out.reshape(B, S, ATT_WIDTH)
    return out[0].reshape(B, S, ATT_WIDTH), out[1].reshape(B, S, LANES)


def _dilated_attention(q, k, v):
    state = None
    order = sorted(DIL_CONFIGS, key=lambda wd: -wd[1])
    for n, (window, d) in enumerate(order):
        assert window // d == BAND
        state = _dilated_branch(q, k, v, state, d, last=(n == len(order) - 1))
    return state


def _outproj_ln_kernel(*refs, n_in):
    a_refs = refs[:n_in]
    w_refs = refs[n_in:2 * n_in]
    x_ref, g_ref, b_ref, o_ref = refs[2 * n_in:]
    y = _dot(a_refs[0][...], w_refs[0][...])
    for a_ref, w_ref in zip(a_refs[1:], w_refs[1:]):
        y = y + _dot(a_ref[...], w_ref[...])
    o_ref[...] = _layer_norm(DEEPNORM_ALPHA * x_ref[...] + y, g_ref[...], b_ref[...])


def _outproj_ln(acts, ws, x, g, b, tm):
    T = x.shape[0]
    n_in = len(acts)
    in_specs = [pl.BlockSpec((tm, a.shape[1]), lambda i: (i, 0)) for a in acts]
    in_specs += [pl.BlockSpec(w.shape, lambda i: (0, 0)) for w in ws]
    row = pl.BlockSpec((tm, D_MODEL), lambda i: (i, 0))
    vec = pl.BlockSpec((1, D_MODEL), lambda i: (0, 0))
    return pl.pallas_call(
        functools.partial(_outproj_ln_kernel, n_in=n_in),
        grid=(T // tm,),
        in_specs=in_specs + [row, vec, vec],
        out_specs=row,
        out_shape=jax.ShapeDtypeStruct((T, D_MODEL), F32),
        compiler_params=_cparams(("parallel",), 48),
        name="outproj_ln",
    )(*acts, *ws, x, g, b)


def _ffn_chunks(width):
    chunks, lo = [], 0
    while lo < width:
        size = min(2 * MXU_DIM, width - lo)
        chunks.append((lo, size))
        lo += size
    return chunks


def _dense_ffn_kernel(x_ref, wg_ref, wu_ref, wd_ref, g_ref, b_ref, o_ref):
    x = x_ref[...]
    xb = x.astype(BF16)
    y = None
    for lo, size in _ffn_chunks(wg_ref.shape[1]):
        gate = _dot(xb, wg_ref[:, lo:lo + size])
        up = _dot(xb, wu_ref[:, lo:lo + size])
        act = (gate * jax.nn.sigmoid(gate) * up).astype(BF16)
        part = _dot(act, wd_ref[lo:lo + size, :])
        y = part if y is None else y + part
    o_ref[...] = _layer_norm(DEEPNORM_ALPHA * x + y, g_ref[...], b_ref[...])


def _dense_ffn(x, wg, wu, wd, g, b, tm):
    T = x.shape[0]
    F = wg.shape[1]
    row = pl.BlockSpec((tm, D_MODEL), lambda i: (i, 0))
    vec = pl.BlockSpec((1, D_MODEL), lambda i: (0, 0))
    once = pl.Buffered(1)
    return pl.pallas_call(
        _dense_ffn_kernel,
        grid=(T // tm,),
        in_specs=[row,
                  pl.BlockSpec((D_MODEL, F), lambda i: (0, 0), pipeline_mode=once),
                  pl.BlockSpec((D_MODEL, F), lambda i: (0, 0), pipeline_mode=once),
                  pl.BlockSpec((F, D_MODEL), lambda i: (0, 0), pipeline_mode=once),
                  vec, vec],
        out_specs=row,
        out_shape=jax.ShapeDtypeStruct((T, D_MODEL), F32),
        compiler_params=_cparams(("parallel",), 56),
        name="dense_ffn",
    )(x, wg, wu, wd, g, b)


CONV_PAD = 8


def _proj1_kernel(x_ref, wqk_ref, wv_ref, wog_ref, wgt_ref, wconv_ref,
                  q_ref, k_ref, v_ref, og_ref, gt_ref, buf):
    i = pl.program_id(1)
    tm = x_ref.shape[0]
    xb = x_ref[...].astype(BF16)
    kscale = MLSTM_HEAD_DIM ** -0.5

    @pl.when(i == 0)
    def _():
        buf[0:CONV_PAD, :] = jnp.zeros((CONV_PAD, buf.shape[1]), F32)

    for c in range(2 * D_MODEL // LANES):
        lanes = slice(c * LANES, (c + 1) * LANES)
        buf[CONV_PAD:CONV_PAD + tm, lanes] = _dot(xb, wqk_ref[:, lanes])
        y = None
        for tap in range(CONV_WIDTH):
            off = CONV_PAD - (CONV_WIDTH - 1) + tap
            term = buf[off:off + tm, lanes] * wconv_ref[tap:tap + 1, lanes]
            y = term if y is None else y + term
        y = y * jax.nn.sigmoid(y)
        buf[0:CONV_PAD, lanes] = buf[tm:tm + CONV_PAD, lanes]
        if c < D_MODEL // LANES:
            q_ref[:, lanes] = y.astype(BF16)
        else:
            k_ref[:, c * LANES - D_MODEL:(c + 1) * LANES - D_MODEL] = (y * kscale).astype(BF16)
    for c in range(D_MODEL // LANES):
        lanes = slice(c * LANES, (c + 1) * LANES)
        v_ref[:, lanes] = _dot(xb, wv_ref[:, lanes]).astype(BF16)
        og_ref[:, lanes] = _dot(xb, wog_ref[:, lanes])
    gt_ref[...] = _dot(xb, wgt_ref[...])


def _proj1(x, wqk, wv, wog, wgt, wconv, tm):
    B, S, _ = x.shape
    row = lambda b, i: (b, i, 0)
    const = lambda b, i: (0, 0)
    act_spec = pl.BlockSpec((None, tm, D_MODEL), row)
    act = jax.ShapeDtypeStruct((B, S, D_MODEL), BF16)
    return pl.pallas_call(
        _proj1_kernel,
        grid=(B, S // tm),
        in_specs=[act_spec,
                  pl.BlockSpec((D_MODEL, 2 * D_MODEL), const),
                  pl.BlockSpec((D_MODEL, D_MODEL), const),
                  pl.BlockSpec((D_MODEL, D_MODEL), const),
                  pl.BlockSpec((D_MODEL, 2 * N_HEADS), const),
                  pl.BlockSpec((CONV_WIDTH, 2 * D_MODEL), const)],
        out_specs=[act_spec, act_spec, act_spec, act_spec,
                   pl.BlockSpec((None, tm, 2 * N_HEADS), row)],
        out_shape=[act, act, act, jax.ShapeDtypeStruct((B, S, D_MODEL), F32),
                   jax.ShapeDtypeStruct((B, S, 2 * N_HEADS), F32)],
        scratch_shapes=[pltpu.VMEM((tm + CONV_PAD, 2 * D_MODEL), F32)],
        compiler_params=_cparams(("parallel", "arbitrary"), 56),
        name="proj1",
    )(x, wqk, wv, wog, wgt, wconv)


def _mlstm_kernel(q_ref, k_ref, v_ref, og_ref, gn_ref, gt_ref, bi_row, bf_row, bi_col, bf_col,
                  ng_ref, o_ref, c_sc, n_sc, m_sc, *, L):
    ci = pl.program_id(1)

    @pl.when(ci == 0)
    def _():
        c_sc[...] = jnp.zeros(c_sc.shape, F32)
        n_sc[...] = jnp.zeros(n_sc.shape, F32)
        m_sc[...] = jnp.full(m_sc.shape, NEG_INF, F32)

    row = lax.broadcasted_iota(jnp.int32, (L, L), 0)
    col = lax.broadcasted_iota(jnp.int32, (L, L), 1)
    causal = col <= row
    lower = jnp.where(causal, 1.0, 0.0).astype(BF16)
    upper = jnp.where(row <= col, 1.0, 0.0).astype(BF16)

    gn = gn_ref[...]
    gt = gt_ref[...]
    i_col = gn[:, :N_HEADS] + bi_row[...]
    lf_col = _log_sigmoid(gn[:, N_HEADS:] + bf_row[...])
    i_row = gt[:N_HEADS, :] + bi_col[...]
    lf_row = _log_sigmoid(gt[N_HEADS:, :] + bf_col[...])
    b_col = sum(_dot(lower, part) for part in _split3(lf_col))
    b_row = sum(_dot(part, upper) for part in _split3(lf_row))

    for h in range(N_HEADS):
        lanes = slice(h * MLSTM_HEAD_DIM, (h + 1) * MLSTM_HEAD_DIM)
        q = q_ref[:, lanes]
        k = k_ref[:, lanes]
        v = v_ref[:, lanes]
        bc = b_col[:, h:h + 1]
        ic = i_col[:, h:h + 1]
        br = b_row[h:h + 1, :]
        ir = i_row[h:h + 1, :]
        m_prev = m_sc[h]
        c_prev = c_sc[h]
        n_prev = n_sc[h]

        dlog = jnp.where(causal, bc - br + ir, NEG_INF)
        inter = bc + m_prev
        m_t = jnp.maximum(inter, jnp.max(dlog, axis=-1, keepdims=True))
        s = _dot_nt(q, k) * jnp.exp(dlog - m_t)
        inter_w = jnp.exp(inter - m_t)
        num = _dot(s.astype(BF16), v) + inter_w * _dot(q, c_prev.astype(BF16))
        den = (jnp.sum(s, axis=-1, keepdims=True)
               + inter_w * jnp.sum(q.astype(F32) * n_prev, axis=-1, keepdims=True))
        hh = num / jnp.maximum(jnp.abs(den), jnp.exp(-m_t))

        b_last = bc[L - 1:L, :]
        g = b_last - bc + ic
        m_new = jnp.maximum(b_last + m_prev, jnp.max(g, axis=0, keepdims=True))
        w = jnp.exp(g - m_new)
        decay = jnp.exp(b_last + m_prev - m_new)
        kw = k.astype(F32) * w
        c_sc[h] = decay * c_prev + _dot_tn(kw.astype(BF16), v)
        n_sc[h] = decay * n_prev + jnp.sum(kw, axis=0, keepdims=True)
        m_sc[h] = m_new

        mu = jnp.mean(hh, axis=-1, keepdims=True)
        hc = hh - mu
        var = jnp.mean(hc * hc, axis=-1, keepdims=True)
        hn = hc * lax.rsqrt(var + LN_EPS) * ng_ref[:, lanes]
        o_ref[:, lanes] = (hn * jax.nn.sigmoid(og_ref[:, lanes])).astype(BF16)


def _mlstm(q, k, v, og, gates, gates_t, b_i, b_f, norm_g, L):
    B, S, _ = q.shape
    row = lambda b, c: (b, c, 0)
    const = lambda b, c: (0, 0)
    act_spec = pl.BlockSpec((None, L, D_MODEL), row)
    return pl.pallas_call(
        functools.partial(_mlstm_kernel, L=L),
        grid=(B, S // L),
        in_specs=[act_spec, act_spec, act_spec, act_spec,
                  pl.BlockSpec((None, L, 2 * N_HEADS), row),
                  pl.BlockSpec((None, 2 * N_HEADS, L), lambda b, c: (b, 0, c)),
                  pl.BlockSpec((1, N_HEADS), const), pl.BlockSpec((1, N_HEADS), const),
                  pl.BlockSpec((N_HEADS, 1), const), pl.BlockSpec((N_HEADS, 1), const),
                  pl.BlockSpec((1, D_MODEL), const)],
        out_specs=act_spec,
        out_shape=jax.ShapeDtypeStruct((B, S, D_MODEL), BF16),
        scratch_shapes=[pltpu.VMEM((N_HEADS, MLSTM_HEAD_DIM, MLSTM_HEAD_DIM), F32),
                        pltpu.VMEM((N_HEADS, 1, MLSTM_HEAD_DIM), F32),
                        pltpu.VMEM((N_HEADS, 1, 1), F32)],
        compiler_params=_cparams(("parallel", "arbitrary"), 48),
        name="mlstm",
    )(q, k, v, og, gates, gates_t, b_i[None, :], b_f[None, :], b_i[:, None], b_f[:, None],
      norm_g[None, :])


MOE_TOP_K = 2
MOE_TILE = 512
RANK_BLOCK = 1024


def _router_kernel(x_ref, wt_ref, sel_ref, gate_ref):
    logits = lax.dot_general(wt_ref[...], x_ref[...], (((1,), (1,)), ((), ())),
                             preferred_element_type=F32, precision=lax.Precision.HIGHEST)
    idx = lax.broadcasted_iota(jnp.int32, logits.shape, 0)
    m1 = jnp.max(logits, axis=0, keepdims=True)
    i1 = jnp.min(jnp.where(logits == m1, idx, N_EXPERTS), axis=0, keepdims=True)
    pick1 = idx == i1
    rest = jnp.where(pick1, NEG_INF, logits)
    m2 = jnp.max(rest, axis=0, keepdims=True)
    i2 = jnp.min(jnp.where(rest == m2, idx, N_EXPERTS), axis=0, keepdims=True)
    pick2 = idx == i2
    e2 = jnp.exp(m2 - m1)
    w1 = 1.0 / (1.0 + e2)
    w2 = e2 / (1.0 + e2)
    sel_ref[...] = jnp.where(jnp.logical_or(pick1, pick2), 1.0, 0.0)
    gate_ref[...] = jnp.where(pick1, w1, 0.0) + jnp.where(pick2, w2, 0.0)


def _router(x, w_router_t, tm):
    T = x.shape[0]
    out = jax.ShapeDtypeStruct((N_EXPERTS, T), F32)
    out_spec = pl.BlockSpec((N_EXPERTS, tm), lambda i: (0, i))
    return pl.pallas_call(
        _router_kernel,
        grid=(T // tm,),
        in_specs=[pl.BlockSpec((tm, D_MODEL), lambda i: (i, 0)),
                  pl.BlockSpec((N_EXPERTS, D_MODEL), lambda i: (0, 0))],
        out_specs=[out_spec, out_spec],
        out_shape=[out, out],
        compiler_params=_cparams(("parallel",), 32),
        name="moe_router",
    )(x, w_router_t)


def _rank_kernel(sel_ref, rank_ref, count_ref, upper_sc, carry_sc):
    n = sel_ref.shape[1]

    @pl.when(pl.program_id(0) == 0)
    def _():
        row = lax.broadcasted_iota(jnp.int32, (n, n), 0)
        col = lax.broadcasted_iota(jnp.int32, (n, n), 1)
        upper_sc[...] = jnp.where(row <= col, 1.0, 0.0).astype(BF16)
        carry_sc[...] = jnp.zeros(carry_sc.shape, F32)

    sel = sel_ref[...]
    incl = _dot(sel.astype(BF16), upper_sc[...])
    rank_ref[...] = carry_sc[...] + incl - sel
    carry_sc[...] = carry_sc[...] + incl[:, n - 1:n]
    count_ref[...] = carry_sc[...]


def _rank(sel_t):
    E, T = sel_t.shape
    n = min(RANK_BLOCK, T)
    return pl.pallas_call(
        _rank_kernel,
        grid=(T // n,),
        in_specs=[pl.BlockSpec((E, n), lambda i: (0, i))],
        out_specs=[pl.BlockSpec((E, n), lambda i: (0, i)), pl.BlockSpec((E, 1), lambda i: (0, 0))],
        out_shape=[jax.ShapeDtypeStruct((E, T), F32), jax.ShapeDtypeStruct((E, 1), F32)],
        scratch_shapes=[pltpu.VMEM((n, n), BF16), pltpu.VMEM((E, 1), F32)],
        compiler_params=_cparams(("arbitrary",), 32),
        name="moe_rank",
    )(sel_t)


def _slot_kernel(sel_ref, gate_ref, rank_ref, off_ref, pos_ref, w_ref):
    sel = sel_ref[...] > 0.0
    idx = lax.broadcasted_iota(jnp.int32, sel.shape, 0)
    first = jnp.min(jnp.where(sel, idx, N_EXPERTS), axis=0, keepdims=True)
    second = jnp.max(jnp.where(sel, idx, -1), axis=0, keepdims=True)
    slot = off_ref[...] + rank_ref[...]
    gate = gate_ref[...]
    rows_p, rows_w = [], []
    for which in (first, second):
        hit = idx == which
        rows_p.append(jnp.sum(jnp.where(hit, slot, 0.0), axis=0, keepdims=True))
        rows_w.append(jnp.sum(jnp.where(hit, gate, 0.0), axis=0, keepdims=True))
    pos_ref[...] = jnp.concatenate(rows_p, axis=0).astype(jnp.int32)
    w_ref[...] = jnp.concatenate(rows_w, axis=0)


def _slots(sel_t, gate_t, rank_t, offsets_col, tm):
    E, T = sel_t.shape
    spec = pl.BlockSpec((E, tm), lambda i: (0, i))
    out_spec = pl.BlockSpec((MOE_TOP_K, tm), lambda i: (0, i))
    return pl.pallas_call(
        _slot_kernel,
        grid=(T // tm,),
        in_specs=[spec, spec, spec, pl.BlockSpec((E, 1), lambda i: (0, 0))],
        out_specs=[out_spec, out_spec],
        out_shape=[jax.ShapeDtypeStruct((MOE_TOP_K, T), jnp.int32),
                   jax.ShapeDtypeStruct((MOE_TOP_K, T), F32)],
        compiler_params=_cparams(("parallel",), 32),
        name="moe_slots",
    )(sel_t, gate_t, rank_t, offsets_col)


def _row_copy(src_hbm, dst_hbm, src_row, dst_row, sem):
    return pltpu.make_async_copy(src_hbm.at[pl.ds(src_row, 1)], dst_hbm.at[pl.ds(dst_row, 1)], sem)


def _dispatch_kernel(tail_ref, used_ref, pos_ref, x_hbm, xs_hbm, zero_sc, sem, zsem, *, tm):
    i = pl.program_id(0)
    n_steps = pl.num_programs(0)
    n_tiles = xs_hbm.shape[0] // MOE_TILE

    def fill(tile):
        return pltpu.make_async_copy(zero_sc, xs_hbm.at[pl.ds(tile * MOE_TILE, MOE_TILE)], zsem)

    @pl.when(i == 0)
    def _():
        zero_sc[...] = jnp.zeros(zero_sc.shape, F32)
        for e in range(N_EXPERTS):
            fill(tail_ref[e]).start()
        for e in range(N_EXPERTS):
            fill(tail_ref[e]).wait()
        for j in range(N_EXPERTS):
            @pl.when(n_tiles - 1 - j >= used_ref[0])
            def _():
                c = fill(n_tiles - 1 - j)
                c.start()
                c.wait()

    def issue(t, carry):
        for k in range(MOE_TOP_K):
            _row_copy(x_hbm, xs_hbm, i * tm + t, pos_ref[k, t], sem).start()
        return carry

    lax.fori_loop(0, tm, issue, 0, unroll=8)

    def drain():
        for _ in range(MOE_TOP_K):
            pltpu.make_async_copy(x_hbm.at[pl.ds(0, tm)], xs_hbm.at[pl.ds(0, tm)], sem).wait()

    @pl.when(i > 0)
    def _():
        drain()

    @pl.when(i == n_steps - 1)
    def _():
        drain()


def _dispatch(x, pos_t, tail_tiles, n_used, n_rows, tm):
    T = x.shape[0]
    n_steps = T // tm
    pos3 = pos_t.reshape(MOE_TOP_K, n_steps, tm).transpose(1, 0, 2)
    return pl.pallas_call(
        functools.partial(_dispatch_kernel, tm=tm),
        grid_spec=pltpu.PrefetchScalarGridSpec(
            num_scalar_prefetch=2,
            grid=(n_steps,),
            in_specs=[pl.BlockSpec((None, MOE_TOP_K, tm), lambda i, tail, used: (i, 0, 0),
                                   memory_space=pltpu.SMEM),
                      pl.BlockSpec(memory_space=pl.ANY)],
            out_specs=pl.BlockSpec(memory_space=pl.ANY),
            scratch_shapes=[pltpu.VMEM((MOE_TILE, D_MODEL), F32),
                            pltpu.SemaphoreType.DMA, pltpu.SemaphoreType.DMA],
        ),
        out_shape=jax.ShapeDtypeStruct((n_rows, D_MODEL), F32),
        compiler_params=_cparams(("arbitrary",), 32),
        name="moe_dispatch",
    )(tail_tiles, n_used, pos3, x)


def _expert_ffn_kernel(te_ref, used_ref, x_ref, wg_ref, wu_ref, wd_ref, o_ref, xb_sc):
    i = pl.program_id(0)
    f = pl.program_id(1)
    live = i < used_ref[0]

    @pl.when(jnp.logical_and(live, f == 0))
    def _():
        xb_sc[...] = x_ref[...].astype(BF16)

    @pl.when(live)
    def _():
        xb = xb_sc[...]
        part = None
        for lo, size in _ffn_chunks(wg_ref.shape[1]):
            gate = _dot(xb, wg_ref[:, lo:lo + size])
            up = _dot(xb, wu_ref[:, lo:lo + size])
            act = (gate * jax.nn.sigmoid(gate) * up).astype(BF16)
            p = _dot(act, wd_ref[lo:lo + size, :])
            part = p if part is None else part + p

        @pl.when(f == 0)
        def _():
            o_ref[...] = part

        @pl.when(f > 0)
        def _():
            o_ref[...] += part

    @pl.when(jnp.logical_and(jnp.logical_not(live), f == 0))
    def _():
        o_ref[...] = jnp.zeros(o_ref.shape, F32)


def _expert_ffn(xs, tile_expert, n_used, wg, wu, wd, fc):
    n_rows = xs.shape[0]
    E, _, F = wg.shape
    tm = MOE_TILE

    def row(i, f, te, used):
        return (jnp.minimum(i, used[0] - 1), 0)

    return pl.pallas_call(
        _expert_ffn_kernel,
        grid_spec=pltpu.PrefetchScalarGridSpec(
            num_scalar_prefetch=2,
            grid=(n_rows // tm, F // fc),
            in_specs=[pl.BlockSpec((tm, D_MODEL), row),
                      pl.BlockSpec((None, D_MODEL, fc), lambda i, f, te, used: (te[i], 0, f)),
                      pl.BlockSpec((None, D_MODEL, fc), lambda i, f, te, used: (te[i], 0, f)),
                      pl.BlockSpec((None, fc, D_MODEL), lambda i, f, te, used: (te[i], f, 0))],
            out_specs=pl.BlockSpec((tm, D_MODEL), lambda i, f, te, used: (i, 0)),
            scratch_shapes=[pltpu.VMEM((tm, D_MODEL), BF16)],
        ),
        out_shape=jax.ShapeDtypeStruct((n_rows, D_MODEL), F32),
        compiler_params=_cparams(("arbitrary", "arbitrary"), 56),
        name="moe_expert_ffn",
    )(tile_expert, n_used, xs, wg, wu, wd)


def _combine_kernel(pos_ref, w_ref, x_ref, g_ref, b_ref, ys_hbm, o_ref, rows_sc, sem, *, tm):
    def issue(t, carry):
        for k in range(MOE_TOP_K):
            pltpu.make_async_copy(ys_hbm.at[pl.ds(pos_ref[k, t], 1)],
                                  rows_sc.at[k, pl.ds(t, 1)], sem).start()
        return carry

    lax.fori_loop(0, tm, issue, 0, unroll=8)
    for k in range(MOE_TOP_K):
        pltpu.make_async_copy(ys_hbm.at[pl.ds(0, tm)], rows_sc.at[k], sem).wait()
    w = w_ref[...]
    y = w[:, 0:1] * rows_sc[0] + w[:, 1:2] * rows_sc[1]
    o_ref[...] = _layer_norm(DEEPNORM_ALPHA * x_ref[...] + y, g_ref[...], b_ref[...])


def _combine_ln(ys, pos_t, w_nat, x, g, b, tm):
    T = x.shape[0]
    n_steps = T // tm
    pos3 = pos_t.reshape(MOE_TOP_K, n_steps, tm).transpose(1, 0, 2)
    row = pl.BlockSpec((tm, D_MODEL), lambda i: (i, 0))
    vec = pl.BlockSpec((1, D_MODEL), lambda i: (0, 0))
    return pl.pallas_call(
        functools.partial(_combine_kernel, tm=tm),
        grid=(n_steps,),
        in_specs=[pl.BlockSpec((None, MOE_TOP_K, tm), lambda i: (i, 0, 0), memory_space=pltpu.SMEM),
                  pl.BlockSpec((tm, MOE_TOP_K), lambda i: (i, 0)),
                  row, vec, vec,
                  pl.BlockSpec(memory_space=pl.ANY)],
        out_specs=row,
        out_shape=jax.ShapeDtypeStruct((T, D_MODEL), F32),
        scratch_shapes=[pltpu.VMEM((MOE_TOP_K, tm, D_MODEL), F32), pltpu.SemaphoreType.DMA],
        compiler_params=_cparams(("arbitrary",), 32),
        name="moe_combine_ln",
    )(pos3, w_nat, x, g, b, ys)


def _moe_sparse(x, w_router, wg, wu, wd, g, b):
    T = x.shape[0]
    tile = MOE_TILE
    n_rows = MOE_TOP_K * T + N_EXPERTS * tile
    n_tiles = n_rows // tile
    sel_t, gate_t = _router(x, w_router.T, tm=1024)
    rank_t, counts = _rank(sel_t)
    tiles_per = jnp.ceil(counts[:, 0] / tile).astype(jnp.int32)
    tile_end = jnp.cumsum(tiles_per)
    tile_start = tile_end - tiles_per
    n_used = tile_end[-1:]
    tile_expert = jnp.minimum(
        jnp.sum(jnp.arange(n_tiles, dtype=jnp.int32)[:, None] >= tile_end[None, :], axis=1),
        N_EXPERTS - 1).astype(jnp.int32)
    tail_tiles = jnp.maximum(tile_end - 1, 0).astype(jnp.int32)
    offsets_col = (tile_start * tile).astype(F32)[:, None]
    pos_t, w_t = _slots(sel_t, gate_t, rank_t, offsets_col, tm=1024)
    xs = _dispatch(x, pos_t, tail_tiles, n_used, n_rows, tm=512)
    ys = _expert_ffn(xs, tile_expert, n_used, wg, wu, wd, fc=D_FF_EXPERT // 2)
    return _combine_ln(ys, pos_t, w_t.T, x, g, b, tm=256)


def _rotary_tables(S):
    pos = jnp.arange(S, dtype=jnp.int32)
    inv = ROPE_THETA ** (-jnp.arange(ROPE_HALF, dtype=F32) / ROPE_HALF)
    ang = pos.astype(F32)[:, None] * inv[None, :]
    cos, sin = jnp.cos(ang), jnp.sin(ang)
    ones = jnp.ones((S, HEAD_DIM - ROPE_DIMS), F32)
    zeros = jnp.zeros((S, HEAD_DIM - ROPE_DIMS), F32)
    z8 = jnp.zeros((S, ROPE_HALF), F32)
    cos_h = jnp.concatenate([cos, cos, ones], axis=1)
    sup_h = jnp.concatenate([-sin, z8, zeros], axis=1)
    sdn_h = jnp.concatenate([z8, sin, zeros], axis=1)
    rep = LANES // HEAD_DIM
    return (jnp.tile(cos_h, (1, rep)), jnp.tile(sup_h, (1, rep)), jnp.tile(sdn_h, (1, rep)))


def _even_layer(h, w_in, b_forget, w_out, ln_mix_g, ln_mix_b, w_gate, w_up, w_down,
                ln_ffn_g, ln_ffn_b):
    B, S, _ = h.shape
    T = B * S
    W = ATT_WIDTH
    w_main = jnp.concatenate([w_in[:, :3 * W], w_in[:, 3 * W + N_HEADS:]], axis=1).astype(BF16)
    wf_t = w_in[:, 3 * W:3 * W + N_HEADS].T.astype(BF16)
    cos_t, sup_t, sdn_t = _rotary_tables(S)
    qa, ka, va, qd, kd, vd, f_t = _proj0(h, w_main, wf_t, cos_t, sup_t, sdn_t, tm=min(512, S))

    bias_col = jnp.tile(b_forget, B)[:, None]
    c = _fox_cumsum(f_t.reshape(B * N_HEADS, S), bias_col).reshape(B, N_HEADS, S)
    o_fox = _fox_attention(qa, ka, va, c, t=min(512, S))
    o_dil = _dilated_attention(qd, kd, vd)

    w_out_b = w_out.astype(BF16)
    tm = 512
    h1 = _outproj_ln([o_fox.reshape(T, W), o_dil.reshape(T, W)], [w_out_b[:W], w_out_b[W:]],
                     h.reshape(T, D_MODEL), ln_mix_g[None, :], ln_mix_b[None, :], tm)
    h2 = _dense_ffn(h1, w_gate.astype(BF16), w_up.astype(BF16), w_down.astype(BF16),
                    ln_ffn_g[None, :], ln_ffn_b[None, :], tm)
    return h2.reshape(B, S, D_MODEL)


def _odd_layer(h, w_in, b_igate, b_fgate, w_conv, norm_g, w_out, ln_mix_g, ln_mix_b, w_router,
               w_gate, w_up, w_down, ln_ffn_g, ln_ffn_b):
    B, S, _ = h.shape
    T = B * S
    D = D_MODEL
    wqk = w_in[:, :2 * D].astype(BF16)
    wv = w_in[:, 2 * D:3 * D].astype(BF16)
    wgt = w_in[:, 3 * D:3 * D + 2 * N_HEADS].astype(BF16)
    wog = w_in[:, 3 * D + 2 * N_HEADS:].astype(BF16)
    q, k, v, og, gates = _proj1(h, wqk, wv, wog, wgt, w_conv, tm=min(512, S))
    gates_t = jnp.swapaxes(gates, 1, 2)
    hm = _mlstm(q, k, v, og, gates, gates_t, b_igate, b_fgate, norm_g, L=min(256, S))

    tm = 512
    h1 = _outproj_ln([hm.reshape(T, D)], [w_out.astype(BF16)], h.reshape(T, D),
                     ln_mix_g[None, :], ln_mix_b[None, :], tm)
    h2 = _moe_sparse(h1, w_router, w_gate.astype(BF16), w_up.astype(BF16), w_down.astype(BF16),
                     ln_ffn_g[None, :], ln_ffn_b[None, :])
    return h2.reshape(B, S, D)


def kernel(x, w_in_e, b_forget_e, w_out_e, ln_mix_g_e, ln_mix_b_e, ffn_w_gate_e, ffn_w_up_e,
           ffn_w_down_e, ln_ffn_g_e, ln_ffn_b_e, w_in_o, b_igate_o, b_fgate_o, w_conv_o,
           mlstm_norm_g_o, w_out_o, ln_mix_g_o, ln_mix_b_o, w_router_o, moe_w_gate_o,
           moe_w_up_o, moe_w_down_o, ln_ffn_g_o, ln_ffn_b_o):
    h = x
    for layer in range(DEPTH):
        i = layer // 2
        if layer % 2 == 0:
            h = _even_layer(h, w_in_e[i], b_forget_e[i], w_out_e[i], ln_mix_g_e[i], ln_mix_b_e[i],
                            ffn_w_gate_e[i], ffn_w_up_e[i], ffn_w_down_e[i], ln_ffn_g_e[i],
                            ln_ffn_b_e[i])
        else:
            h = _odd_layer(h, w_in_o[i], b_igate_o[i], b_fgate_o[i], w_conv_o[i],
                           mlstm_norm_g_o[i], w_out_o[i], ln_mix_g_o[i], ln_mix_b_o[i],
                           w_router_o[i], moe_w_gate_o[i], moe_w_up_o[i], moe_w_down_o[i],
                           ln_ffn_g_o[i], ln_ffn_b_o[i])
    return h
```

```python
import functools
import math

import jax
import jax.numpy as jnp
from jax import lax
from jax.experimental import pallas as pl
from jax.experimental.pallas import tpu as pltpu

F32 = jnp.float32
BF16 = jnp.bfloat16

D_MODEL = 1024
HEAD_DIM = 64
N_HEADS = 8
ATT_WIDTH = N_HEADS * HEAD_DIM
DIL_CONFIGS = ((128, 1), (512, 4), (2048, 16))
ROPE_THETA = 500000.0
ROPE_DIMS = HEAD_DIM // 4
ROPE_HALF = ROPE_DIMS // 2
MLSTM_HEAD_DIM = D_MODEL // N_HEADS
CONV_WIDTH = 4
D_FF_DENSE = 2816
N_EXPERTS = 8
D_FF_EXPERT = 3584
DEPTH = 2
DEEPNORM_ALPHA = (2 * DEPTH) ** 0.25
LN_EPS = 1e-5

LANES = 128
MXU_DIM = 256
BAND = 128
MIB = 1024 * 1024

NEG_INF = float("-inf")


def _cparams(semantics, vmem_mib):
    return pltpu.CompilerParams(dimension_semantics=semantics, vmem_limit_bytes=vmem_mib * MIB)


def _dot(a, b):
    return jnp.dot(a, b, preferred_element_type=F32)


def _dot_nt(a, b):
    return lax.dot_general(a, b, (((1,), (1,)), ((), ())), preferred_element_type=F32)


def _dot_tn(a, b):
    return lax.dot_general(a, b, (((0,), (0,)), ((), ())), preferred_element_type=F32)


def _split3(x):
    hi = x.astype(BF16)
    r = x - hi.astype(F32)
    mid = r.astype(BF16)
    lo = (r - mid.astype(F32)).astype(BF16)
    return hi, mid, lo


def _log_sigmoid(z):
    return -(jnp.maximum(-z, 0.0) + jnp.log1p(jnp.exp(-jnp.abs(z))))


def _layer_norm(z, g, b):
    mu = jnp.mean(z, axis=-1, keepdims=True)
    zc = z - mu
    var = jnp.mean(zc * zc, axis=-1, keepdims=True)
    return zc * lax.rsqrt(var + LN_EPS) * g + b


def _proj0_kernel(x_ref, w_ref, wf_ref, cos_ref, sup_ref, sdn_ref,
                  qa_ref, ka_ref, va_ref, qd_ref, kd_ref, vd_ref, ft_ref):
    xb = x_ref[...].astype(BF16)
    outs = (qa_ref, ka_ref, va_ref, qd_ref, kd_ref, vd_ref)
    scale = HEAD_DIM ** -0.5
    for j, o_ref in enumerate(outs):
        for c in range(ATT_WIDTH // LANES):
            lo = j * ATT_WIDTH + c * LANES
            r = _dot(xb, w_ref[:, lo:lo + LANES])
            if j in (3, 4):
                r = (r * cos_ref[...]
                     + pltpu.roll(r, LANES - ROPE_HALF, axis=1) * sup_ref[...]
                     + pltpu.roll(r, ROPE_HALF, axis=1) * sdn_ref[...])
            if j in (0, 3):
                r = r * scale
            o_ref[:, c * LANES:(c + 1) * LANES] = r.astype(BF16)
    ft_ref[...] = _dot_nt(wf_ref[...], xb)


def _proj0(x, w_main, wf_t, cos_t, sup_t, sdn_t, tm):
    B, S, _ = x.shape
    n_i = S // tm
    act = jax.ShapeDtypeStruct((B, S, ATT_WIDTH), BF16)
    act_spec = pl.BlockSpec((None, tm, ATT_WIDTH), lambda b, i: (b, i, 0))
    tab_spec = pl.BlockSpec((tm, LANES), lambda b, i: (i, 0))
    return pl.pallas_call(
        _proj0_kernel,
        grid=(B, n_i),
        in_specs=[
            pl.BlockSpec((None, tm, D_MODEL), lambda b, i: (b, i, 0)),
            pl.BlockSpec((D_MODEL, 6 * ATT_WIDTH), lambda b, i: (0, 0)),
            pl.BlockSpec((N_HEADS, D_MODEL), lambda b, i: (0, 0)),
            tab_spec, tab_spec, tab_spec,
        ],
        out_specs=[act_spec] * 6 + [pl.BlockSpec((None, N_HEADS, tm), lambda b, i: (b, 0, i))],
        out_shape=[act] * 6 + [jax.ShapeDtypeStruct((B, N_HEADS, S), F32)],
        compiler_params=_cparams(("parallel", "parallel"), 48),
        name="proj0",
    )(x, w_main, wf_t, cos_t, sup_t, sdn_t)


def _fox_cumsum_kernel(f_ref, bias_ref, c_ref):
    S = f_ref.shape[1]
    lf = _log_sigmoid(f_ref[...] + bias_ref[...])
    row = lax.broadcasted_iota(jnp.int32, (S, S), 0)
    col = lax.broadcasted_iota(jnp.int32, (S, S), 1)
    upper = jnp.where(row <= col, 1.0, 0.0).astype(BF16)
    hi, mid, lo = _split3(lf)
    c_ref[...] = _dot(hi, upper) + _dot(mid, upper) + _dot(lo, upper)


def _fox_cumsum(f_t, bias_col):
    R, S = f_t.shape
    return pl.pallas_call(
        _fox_cumsum_kernel,
        grid=(1,),
        in_specs=[pl.BlockSpec((R, S), lambda i: (0, 0)), pl.BlockSpec((R, 1), lambda i: (0, 0))],
        out_specs=pl.BlockSpec((R, S), lambda i: (0, 0)),
        out_shape=jax.ShapeDtypeStruct((R, S), F32),
        compiler_params=_cparams(("arbitrary",), 48),
        name="fox_cumsum",
    )(f_t, bias_col)


def _fox_kernel(q_ref, k_ref, v_ref, c_ref, o_ref, m_sc, l_sc, acc_sc, *, t):
    i = pl.program_id(1)
    j = pl.program_id(2)

    @pl.when(j == 0)
    def _():
        m_sc[...] = jnp.full(m_sc.shape, NEG_INF, F32)
        l_sc[...] = jnp.zeros(l_sc.shape, F32)
        acc_sc[...] = jnp.zeros(acc_sc.shape, F32)

    def step(masked):
        if masked:
            row = lax.broadcasted_iota(jnp.int32, (t, t), 0)
            col = lax.broadcasted_iota(jnp.int32, (t, t), 1)
            keep = col <= row
        for h in range(N_HEADS):
            sl = slice(h * HEAD_DIM, (h + 1) * HEAD_DIM)
            s = _dot_nt(q_ref[:, sl], k_ref[:, sl]) - c_ref[h:h + 1, :]
            if masked:
                s = jnp.where(keep, s, NEG_INF)
            m_prev = m_sc[h]
            m_new = jnp.maximum(m_prev, jnp.max(s, axis=-1, keepdims=True))
            alpha = jnp.exp(m_prev - m_new)
            p = jnp.exp(s - m_new)
            l_sc[h] = alpha * l_sc[h] + jnp.sum(p, axis=-1, keepdims=True)
            acc_sc[h] = alpha * acc_sc[h] + _dot(p.astype(BF16), v_ref[:, sl])
            m_sc[h] = m_new

    @pl.when(j < i)
    def _():
        step(False)

    @pl.when(j == i)
    def _():
        step(True)
        o_ref[...] = jnp.concatenate(
            [acc_sc[h] / l_sc[h] for h in range(N_HEADS)], axis=-1).astype(BF16)


def _fox_attention(q, k, v, c, t):
    B, S, _ = q.shape
    n = S // t
    q_spec = pl.BlockSpec((None, t, ATT_WIDTH), lambda b, i, j: (b, i, 0))
    kv_spec = pl.BlockSpec((None, t, ATT_WIDTH), lambda b, i, j: (b, jnp.minimum(j, i), 0))
    return pl.pallas_call(
        functools.partial(_fox_kernel, t=t),
        grid=(B, n, n),
        in_specs=[q_spec, kv_spec, kv_spec,
                  pl.BlockSpec((None, N_HEADS, t), lambda b, i, j: (b, 0, jnp.minimum(j, i)))],
        out_specs=q_spec,
        out_shape=jax.ShapeDtypeStruct((B, S, ATT_WIDTH), BF16),
        scratch_shapes=[pltpu.VMEM((N_HEADS, t, 1), F32), pltpu.VMEM((N_HEADS, t, 1), F32),
                        pltpu.VMEM((N_HEADS, t, HEAD_DIM), F32)],
        compiler_params=_cparams(("parallel", "parallel", "arbitrary"), 48),
        name="fox_attention",
    )(q, k, v, c)


def _dil_kernel(*refs, rg, sub, has_prev, first, last):
    refs = list(refs)
    q_ref, k_ref, v_ref = refs[:3]
    pos = 3
    if has_prev:
        kp_ref, vp_ref = refs[pos:pos + 2]
        pos += 2
    if not first:
        acc_in, st_in = refs[pos:pos + 2]
        pos += 2
    if last:
        o_ref = refs[pos]
    else:
        acc_out, st_out = refs[pos:pos + 2]

    blk = pl.program_id(1)
    row = lax.broadcasted_iota(jnp.int32, (BAND, BAND), 0)
    col = lax.broadcasted_iota(jnp.int32, (BAND, BAND), 1)
    keep_cur = col <= row
    keep_prev = col >= row
    keep_prev_edge = jnp.logical_and(keep_prev, blk > 0)

    for a in range(sub):
        rows = slice(a * BAND, (a + 1) * BAND)
        for rr in range(rg):
            if not first:
                st_old = st_in[rows, rr * LANES:(rr + 1) * LANES]
                acc_old = acc_in[rows, rr * ATT_WIDTH:(rr + 1) * ATT_WIDTH]
            accs, ms, ls = [], [], []
            for h in range(N_HEADS):
                lo = rr * ATT_WIDTH + h * HEAD_DIM
                sl = slice(lo, lo + HEAD_DIM)
                q = q_ref[rows, sl]
                s1 = jnp.where(keep_cur, _dot_nt(q, k_ref[rows, sl]), NEG_INF)
                m = jnp.max(s1, axis=-1, keepdims=True)
                if has_prev:
                    if a == 0:
                        k0, v0, keep0 = kp_ref[:, sl], vp_ref[:, sl], keep_prev_edge
                    else:
                        prows = slice((a - 1) * BAND, a * BAND)
                        k0, v0, keep0 = k_ref[prows, sl], v_ref[prows, sl], keep_prev
                    s0 = jnp.where(keep0, _dot_nt(q, k0), NEG_INF)
                    m = jnp.maximum(m, jnp.max(s0, axis=-1, keepdims=True))
                if not first:
                    m_old = st_old[:, h:h + 1]
                    l_old = st_old[:, N_HEADS + h:N_HEADS + h + 1]
                    m_new = jnp.maximum(m, m_old)
                    alpha = jnp.exp(m_old - m_new)
                    m = m_new
                p1 = jnp.exp(s1 - m)
                l = jnp.sum(p1, axis=-1, keepdims=True)
                acc = _dot(p1.astype(BF16), v_ref[rows, sl])
                if has_prev:
                    p0 = jnp.exp(s0 - m)
                    l = l + jnp.sum(p0, axis=-1, keepdims=True)
                    acc = acc + _dot(p0.astype(BF16), v0)
                if not first:
                    l = l + alpha * l_old
                    acc = acc + alpha * acc_old[:, h * HEAD_DIM:(h + 1) * HEAD_DIM]
                if last:
                    acc = acc / l
                accs.append(acc)
                ms.append(m)
                ls.append(l)
            out_lanes = slice(rr * ATT_WIDTH, (rr + 1) * ATT_WIDTH)
            if last:
                o_ref[rows, out_lanes] = jnp.concatenate(accs, axis=-1).astype(BF16)
            else:
                acc_out[rows, out_lanes] = jnp.concatenate(accs, axis=-1)
                pad = jnp.zeros((BAND, LANES - 2 * N_HEADS), F32)
                st_out[rows, rr * LANES:(rr + 1) * LANES] = jnp.concatenate(ms + ls + [pad], axis=-1)


def _dilated_branch(q, k, v, state, dilation, last):
    B, S, _ = q.shape
    d = dilation
    L = S // d
    first = state is None
    if d == 1:
        tqb, rg = min(4 * BAND, L), 1
    else:
        tqb, rg = BAND, min(4, d)
    sub = tqb // BAND
    n_blk = L // tqb
    n_grp = d // rg
    has_prev = L > BAND
    wq = rg * ATT_WIDTH
    wst = rg * LANES

    def view(a):
        return a.reshape(B, L, d * a.shape[-1])

    main = lambda b, i, g: (b, i, g)
    prev = lambda b, i, g: (b, jnp.maximum(i * sub - 1, 0), g)
    qkv_spec = pl.BlockSpec((None, tqb, wq), main)
    in_specs = [qkv_spec] * 3
    args = [view(q), view(k), view(v)]
    if has_prev:
        in_specs += [pl.BlockSpec((None, BAND, wq), prev)] * 2
        args += [view(k), view(v)]
    if not first:
        in_specs += [pl.BlockSpec((None, tqb, wq), main), pl.BlockSpec((None, tqb, wst), main)]
        args += [view(state[0]), view(state[1])]
    if last:
        out_specs = pl.BlockSpec((None, tqb, wq), main)
        out_shape = jax.ShapeDtypeStruct((B, L, d * ATT_WIDTH), BF16)
    else:
        out_specs = [pl.BlockSpec((None, tqb, wq), main), pl.BlockSpec((None, tqb, wst), main)]
        out_shape = [jax.ShapeDtypeStruct((B, L, d * ATT_WIDTH), F32),
                     jax.ShapeDtypeStruct((B, L, d * LANES), F32)]
    out = pl.pallas_call(
        functools.partial(_dil_kernel, rg=rg, sub=sub, has_prev=has_prev, first=first, last=last),
        grid=(B, n_blk, n_grp),
        in_specs=in_specs,
        out_specs=out_specs,
        out_shape=out_shape,
        compiler_params=_cparams(("parallel", "parallel", "parallel"), 48),
        name=f"dilated_d{d}",
    )(*args)
    if last:
        return out.reshape(B, S, ATT_WIDTH)
    return out[0].reshape(B, S, ATT_WIDTH), out[1].reshape(B, S, LANES)


def _dilated_attention(q, k, v):
    state = None
    order = sorted(DIL_CONFIGS, key=lambda wd: -wd[1])
    for n, (window, d) in enumerate(order):
        assert window // d == BAND
        state = _dilated_branch(q, k, v, state, d, last=(n == len(order) - 1))
    return state


def _outproj_ln_kernel(*refs, n_in):
    a_refs = refs[:n_in]
    w_refs = refs[n_in:2 * n_in]
    x_ref, g_ref, b_ref, o_ref = refs[2 * n_in:]
    y = _dot(a_refs[0][...], w_refs[0][...])
    for a_ref, w_ref in zip(a_refs[1:], w_refs[1:]):
        y = y + _dot(a_ref[...], w_ref[...])
    o_ref[...] = _layer_norm(DEEPNORM_ALPHA * x_ref[...] + y, g_ref[...], b_ref[...])


def _outproj_ln(acts, ws, x, g, b, tm):
    T = x.shape[0]
    n_in = len(acts)
    in_specs = [pl.BlockSpec((tm, a.shape[1]), lambda i: (i, 0)) for a in acts]
    in_specs += [pl.BlockSpec(w.shape, lambda i: (0, 0)) for w in ws]
    row = pl.BlockSpec((tm, D_MODEL), lambda i: (i, 0))
    vec = pl.BlockSpec((1, D_MODEL), lambda i: (0, 0))
    return pl.pallas_call(
        functools.partial(_outproj_ln_kernel, n_in=n_in),
        grid=(T // tm,),
        in_specs=in_specs + [row, vec, vec],
        out_specs=row,
        out_shape=jax.ShapeDtypeStruct((T, D_MODEL), F32),
        compiler_params=_cparams(("parallel",), 48),
        name="outproj_ln",
    )(*acts, *ws, x, g, b)


def _ffn_chunks(width):
    chunks, lo = [], 0
    while lo < width:
        size = min(2 * MXU_DIM, width - lo)
        chunks.append((lo, size))
        lo += size
    return chunks


def _dense_ffn_kernel(x_ref, wg_ref, wu_ref, wd_ref, g_ref, b_ref, o_ref):
    x = x_ref[...]
    xb = x.astype(BF16)
    y = None
    for lo, size in _ffn_chunks(wg_ref.shape[1]):
        gate = _dot(xb, wg_ref[:, lo:lo + size])
        up = _dot(xb, wu_ref[:, lo:lo + size])
        act = (gate * jax.nn.sigmoid(gate) * up).astype(BF16)
        part = _dot(act, wd_ref[lo:lo + size, :])
        y = part if y is None else y + part
    o_ref[...] = _layer_norm(DEEPNORM_ALPHA * x + y, g_ref[...], b_ref[...])


def _dense_ffn(x, wg, wu, wd, g, b, tm):
    T = x.shape[0]
    F = wg.shape[1]
    row = pl.BlockSpec((tm, D_MODEL), lambda i: (i, 0))
    vec = pl.BlockSpec((1, D_MODEL), lambda i: (0, 0))
    once = pl.Buffered(1)
    return pl.pallas_call(
        _dense_ffn_kernel,
        grid=(T // tm,),
        in_specs=[row,
                  pl.BlockSpec((D_MODEL, F), lambda i: (0, 0), pipeline_mode=once),
                  pl.BlockSpec((D_MODEL, F), lambda i: (0, 0), pipeline_mode=once),
                  pl.BlockSpec((F, D_MODEL), lambda i: (0, 0), pipeline_mode=once),
                  vec, vec],
        out_specs=row,
        out_shape=jax.ShapeDtypeStruct((T, D_MODEL), F32),
        compiler_params=_cparams(("parallel",), 56),
        name="dense_ffn",
    )(x, wg, wu, wd, g, b)


CONV_PAD = 8


def _proj1_kernel(x_ref, wqk_ref, wv_ref, wog_ref, wgt_ref, wconv_ref,
                  q_ref, k_ref, v_ref, og_ref, gt_ref, buf):
    i = pl.program_id(1)
    tm = x_ref.shape[0]
    xb = x_ref[...].astype(BF16)
    kscale = MLSTM_HEAD_DIM ** -0.5

    @pl.when(i == 0)
    def _():
        buf[0:CONV_PAD, :] = jnp.zeros((CONV_PAD, buf.shape[1]), F32)

    for c in range(2 * D_MODEL // LANES):
        lanes = slice(c * LANES, (c + 1) * LANES)
        buf[CONV_PAD:CONV_PAD + tm, lanes] = _dot(xb, wqk_ref[:, lanes])
        y = None
        for tap in range(CONV_WIDTH):
            off = CONV_PAD - (CONV_WIDTH - 1) + tap
            term = buf[off:off + tm, lanes] * wconv_ref[tap:tap + 1, lanes]
            y = term if y is None else y + term
        y = y * jax.nn.sigmoid(y)
        buf[0:CONV_PAD, lanes] = buf[tm:tm + CONV_PAD, lanes]
        if c < D_MODEL // LANES:
            q_ref[:, lanes] = y.astype(BF16)
        else:
            k_ref[:, c * LANES - D_MODEL:(c + 1) * LANES - D_MODEL] = (y * kscale).astype(BF16)
    for c in range(D_MODEL // LANES):
        lanes = slice(c * LANES, (c + 1) * LANES)
        v_ref[:, lanes] = _dot(xb, wv_ref[:, lanes]).astype(BF16)
        og_ref[:, lanes] = _dot(xb, wog_ref[:, lanes])
    gt_ref[...] = _dot(xb, wgt_ref[...])


def _proj1(x, wqk, wv, wog, wgt, wconv, tm):
    B, S, _ = x.shape
    row = lambda b, i: (b, i, 0)
    const = lambda b, i: (0, 0)
    act_spec = pl.BlockSpec((None, tm, D_MODEL), row)
    act = jax.ShapeDtypeStruct((B, S, D_MODEL), BF16)
    return pl.pallas_call(
        _proj1_kernel,
        grid=(B, S // tm),
        in_specs=[act_spec,
                  pl.BlockSpec((D_MODEL, 2 * D_MODEL), const),
                  pl.BlockSpec((D_MODEL, D_MODEL), const),
                  pl.BlockSpec((D_MODEL, D_MODEL), const),
                  pl.BlockSpec((D_MODEL, 2 * N_HEADS), const),
                  pl.BlockSpec((CONV_WIDTH, 2 * D_MODEL), const)],
        out_specs=[act_spec, act_spec, act_spec, act_spec,
                   pl.BlockSpec((None, tm, 2 * N_HEADS), row)],
        out_shape=[act, act, act, jax.ShapeDtypeStruct((B, S, D_MODEL), F32),
                   jax.ShapeDtypeStruct((B, S, 2 * N_HEADS), F32)],
        scratch_shapes=[pltpu.VMEM((tm + CONV_PAD, 2 * D_MODEL), F32)],
        compiler_params=_cparams(("parallel", "arbitrary"), 56),
        name="proj1",
    )(x, wqk, wv, wog, wgt, wconv)


def _mlstm_kernel(q_ref, k_ref, v_ref, og_ref, gn_ref, gt_ref, bi_row, bf_row, bi_col, bf_col,
                  ng_ref, o_ref, c_sc, n_sc, m_sc, *, L):
    ci = pl.program_id(1)

    @pl.when(ci == 0)
    def _():
        c_sc[...] = jnp.zeros(c_sc.shape, F32)
        n_sc[...] = jnp.zeros(n_sc.shape, F32)
        m_sc[...] = jnp.full(m_sc.shape, NEG_INF, F32)

    row = lax.broadcasted_iota(jnp.int32, (L, L), 0)
    col = lax.broadcasted_iota(jnp.int32, (L, L), 1)
    causal = col <= row
    lower = jnp.where(causal, 1.0, 0.0).astype(BF16)
    upper = jnp.where(row <= col, 1.0, 0.0).astype(BF16)

    gn = gn_ref[...]
    gt = gt_ref[...]
    i_col = gn[:, :N_HEADS] + bi_row[...]
    lf_col = _log_sigmoid(gn[:, N_HEADS:] + bf_row[...])
    i_row = gt[:N_HEADS, :] + bi_col[...]
    lf_row = _log_sigmoid(gt[N_HEADS:, :] + bf_col[...])
    b_col = sum(_dot(lower, part) for part in _split3(lf_col))
    b_row = sum(_dot(part, upper) for part in _split3(lf_row))

    for h in range(N_HEADS):
        lanes = slice(h * MLSTM_HEAD_DIM, (h + 1) * MLSTM_HEAD_DIM)
        q = q_ref[:, lanes]
        k = k_ref[:, lanes]
        v = v_ref[:, lanes]
        bc = b_col[:, h:h + 1]
        ic = i_col[:, h:h + 1]
        br = b_row[h:h + 1, :]
        ir = i_row[h:h + 1, :]
        m_prev = m_sc[h]
        c_prev = c_sc[h]
        n_prev = n_sc[h]

        dlog = jnp.where(causal, bc - br + ir, NEG_INF)
        inter = bc + m_prev
        m_t = jnp.maximum(inter, jnp.max(dlog, axis=-1, keepdims=True))
        s = _dot_nt(q, k) * jnp.exp(dlog - m_t)
        inter_w = jnp.exp(inter - m_t)
        num = _dot(s.astype(BF16), v) + inter_w * _dot(q, c_prev.astype(BF16))
        den = (jnp.sum(s, axis=-1, keepdims=True)
               + inter_w * jnp.sum(q.astype(F32) * n_prev, axis=-1, keepdims=True))
        hh = num / jnp.maximum(jnp.abs(den), jnp.exp(-m_t))

        b_last = bc[L - 1:L, :]
        g = b_last - bc + ic
        m_new = jnp.maximum(b_last + m_prev, jnp.max(g, axis=0, keepdims=True))
        w = jnp.exp(g - m_new)
        decay = jnp.exp(b_last + m_prev - m_new)
        kw = k.astype(F32) * w
        c_sc[h] = decay * c_prev + _dot_tn(kw.astype(BF16), v)
        n_sc[h] = decay * n_prev + jnp.sum(kw, axis=0, keepdims=True)
        m_sc[h] = m_new

        mu = jnp.mean(hh, axis=-1, keepdims=True)
        hc = hh - mu
        var = jnp.mean(hc * hc, axis=-1, keepdims=True)
        hn = hc * lax.rsqrt(var + LN_EPS) * ng_ref[:, lanes]
        o_ref[:, lanes] = (hn * jax.nn.sigmoid(og_ref[:, lanes])).astype(BF16)


def _mlstm(q, k, v, og, gates, gates_t, b_i, b_f, norm_g, L):
    B, S, _ = q.shape
    row = lambda b, c: (b, c, 0)
    const = lambda b, c: (0, 0)
    act_spec = pl.BlockSpec((None, L, D_MODEL), row)
    return pl.pallas_call(
        functools.partial(_mlstm_kernel, L=L),
        grid=(B, S // L),
        in_specs=[act_spec, act_spec, act_spec, act_spec,
                  pl.BlockSpec((None, L, 2 * N_HEADS), row),
                  pl.BlockSpec((None, 2 * N_HEADS, L), lambda b, c: (b, 0, c)),
                  pl.BlockSpec((1, N_HEADS), const), pl.BlockSpec((1, N_HEADS), const),
                  pl.BlockSpec((N_HEADS, 1), const), pl.BlockSpec((N_HEADS, 1), const),
                  pl.BlockSpec((1, D_MODEL), const)],
        out_specs=act_spec,
        out_shape=jax.ShapeDtypeStruct((B, S, D_MODEL), BF16),
        scratch_shapes=[pltpu.VMEM((N_HEADS, MLSTM_HEAD_DIM, MLSTM_HEAD_DIM), F32),
                        pltpu.VMEM((N_HEADS, 1, MLSTM_HEAD_DIM), F32),
                        pltpu.VMEM((N_HEADS, 1, 1), F32)],
        compiler_params=_cparams(("parallel", "arbitrary"), 48),
        name="mlstm",
    )(q, k, v, og, gates, gates_t, b_i[None, :], b_f[None, :], b_i[:, None], b_f[:, None],
      norm_g[None, :])


MOE_TOP_K = 2
MOE_TILE = 512
RANK_BLOCK = 1024


def _router_kernel(x_ref, wt_ref, sel_ref, gate_ref):
    logits = lax.dot_general(wt_ref[...], x_ref[...], (((1,), (1,)), ((), ())),
                             preferred_element_type=F32, precision=lax.Precision.HIGHEST)
    idx = lax.broadcasted_iota(jnp.int32, logits.shape, 0)
    m1 = jnp.max(logits, axis=0, keepdims=True)
    i1 = jnp.min(jnp.where(logits == m1, idx, N_EXPERTS), axis=0, keepdims=True)
    pick1 = idx == i1
    rest = jnp.where(pick1, NEG_INF, logits)
    m2 = jnp.max(rest, axis=0, keepdims=True)
    i2 = jnp.min(jnp.where(rest == m2, idx, N_EXPERTS), axis=0, keepdims=True)
    pick2 = idx == i2
    e2 = jnp.exp(m2 - m1)
    w1 = 1.0 / (1.0 + e2)
    w2 = e2 / (1.0 + e2)
    sel_ref[...] = jnp.where(jnp.logical_or(pick1, pick2), 1.0, 0.0)
    gate_ref[...] = jnp.where(pick1, w1, 0.0) + jnp.where(pick2, w2, 0.0)


def _router(x, w_router_t, tm):
    T = x.shape[0]
    out = jax.ShapeDtypeStruct((N_EXPERTS, T), F32)
    out_spec = pl.BlockSpec((N_EXPERTS, tm), lambda i: (0, i))
    return pl.pallas_call(
        _router_kernel,
        grid=(T // tm,),
        in_specs=[pl.BlockSpec((tm, D_MODEL), lambda i: (i, 0)),
                  pl.BlockSpec((N_EXPERTS, D_MODEL), lambda i: (0, 0))],
        out_specs=[out_spec, out_spec],
        out_shape=[out, out],
        compiler_params=_cparams(("parallel",), 32),
        name="moe_router",
    )(x, w_router_t)


def _rank_kernel(sel_ref, rank_ref, count_ref, upper_sc, carry_sc):
    n = sel_ref.shape[1]

    @pl.when(pl.program_id(0) == 0)
    def _():
        row = lax.broadcasted_iota(jnp.int32, (n, n), 0)
        col = lax.broadcasted_iota(jnp.int32, (n, n), 1)
        upper_sc[...] = jnp.where(row <= col, 1.0, 0.0).astype(BF16)
        carry_sc[...] = jnp.zeros(carry_sc.shape, F32)

    sel = sel_ref[...]
    incl = _dot(sel.astype(BF16), upper_sc[...])
    rank_ref[...] = carry_sc[...] + incl - sel
    carry_sc[...] = carry_sc[...] + incl[:, n - 1:n]
    count_ref[...] = carry_sc[...]


def _rank(sel_t):
    E, T = sel_t.shape
    n = min(RANK_BLOCK, T)
    return pl.pallas_call(
        _rank_kernel,
        grid=(T // n,),
        in_specs=[pl.BlockSpec((E, n), lambda i: (0, i))],
        out_specs=[pl.BlockSpec((E, n), lambda i: (0, i)), pl.BlockSpec((E, 1), lambda i: (0, 0))],
        out_shape=[jax.ShapeDtypeStruct((E, T), F32), jax.ShapeDtypeStruct((E, 1), F32)],
        scratch_shapes=[pltpu.VMEM((n, n), BF16), pltpu.VMEM((E, 1), F32)],
        compiler_params=_cparams(("arbitrary",), 32),
        name="moe_rank",
    )(sel_t)


def _slot_kernel(sel_ref, gate_ref, rank_ref, off_ref, pos_ref, w_ref):
    sel = sel_ref[...] > 0.0
    idx = lax.broadcasted_iota(jnp.int32, sel.shape, 0)
    first = jnp.min(jnp.where(sel, idx, N_EXPERTS), axis=0, keepdims=True)
    second = jnp.max(jnp.where(sel, idx, -1), axis=0, keepdims=True)
    slot = off_ref[...] + rank_ref[...]
    gate = gate_ref[...]
    rows_p, rows_w = [], []
    for which in (first, second):
        hit = idx == which
        rows_p.append(jnp.sum(jnp.where(hit, slot, 0.0), axis=0, keepdims=True))
        rows_w.append(jnp.sum(jnp.where(hit, gate, 0.0), axis=0, keepdims=True))
    pos_ref[...] = jnp.concatenate(rows_p, axis=0).astype(jnp.int32)
    w_ref[...] = jnp.concatenate(rows_w, axis=0)


def _slots(sel_t, gate_t, rank_t, offsets_col, tm):
    E, T = sel_t.shape
    spec = pl.BlockSpec((E, tm), lambda i: (0, i))
    out_spec = pl.BlockSpec((MOE_TOP_K, tm), lambda i: (0, i))
    return pl.pallas_call(
        _slot_kernel,
        grid=(T // tm,),
        in_specs=[spec, spec, spec, pl.BlockSpec((E, 1), lambda i: (0, 0))],
        out_specs=[out_spec, out_spec],
        out_shape=[jax.ShapeDtypeStruct((MOE_TOP_K, T), jnp.int32),
                   jax.ShapeDtypeStruct((MOE_TOP_K, T), F32)],
        compiler_params=_cparams(("parallel",), 32),
        name="moe_slots",
    )(sel_t, gate_t, rank_t, offsets_col)


def _dispatch_kernel(tail_ref, used_ref, pos_ref, x_ref, xs_hbm, zero_sc, sem, zsem, *, tm):
    i = pl.program_id(0)
    n_tiles = xs_hbm.shape[0] // MOE_TILE

    def fill(tile):
        return pltpu.make_async_copy(zero_sc, xs_hbm.at[pl.ds(tile * MOE_TILE, MOE_TILE)], zsem)

    @pl.when(i == 0)
    def _():
        zero_sc[...] = jnp.zeros(zero_sc.shape, F32)
        for e in range(N_EXPERTS):
            fill(tail_ref[e]).start()
        for e in range(N_EXPERTS):
            fill(tail_ref[e]).wait()
        for j in range(N_EXPERTS):
            @pl.when(n_tiles - 1 - j >= used_ref[0])
            def _():
                c = fill(n_tiles - 1 - j)
                c.start()
                c.wait()

    def issue(t, carry):
        for k in range(MOE_TOP_K):
            pltpu.make_async_copy(x_ref.at[pl.ds(t, 1)], xs_hbm.at[pl.ds(pos_ref[k, t], 1)],
                                  sem).start(priority=k)
        return carry

    lax.fori_loop(0, tm, issue, 0, unroll=8)
    for _ in range(MOE_TOP_K):
        pltpu.make_async_copy(x_ref, xs_hbm.at[pl.ds(0, tm)], sem).wait()


def _dispatch(x, pos_t, tail_tiles, n_used, n_rows, tm):
    T = x.shape[0]
    n_steps = T // tm
    pos3 = pos_t.reshape(MOE_TOP_K, n_steps, tm).transpose(1, 0, 2)
    return pl.pallas_call(
        functools.partial(_dispatch_kernel, tm=tm),
        grid_spec=pltpu.PrefetchScalarGridSpec(
            num_scalar_prefetch=2,
            grid=(n_steps,),
            in_specs=[pl.BlockSpec((None, MOE_TOP_K, tm), lambda i, tail, used: (i, 0, 0),
                                   memory_space=pltpu.SMEM),
                      pl.BlockSpec((tm, D_MODEL), lambda i, tail, used: (i, 0))],
            out_specs=pl.BlockSpec(memory_space=pl.ANY),
            scratch_shapes=[pltpu.VMEM((MOE_TILE, D_MODEL), F32),
                            pltpu.SemaphoreType.DMA, pltpu.SemaphoreType.DMA],
        ),
        out_shape=jax.ShapeDtypeStruct((n_rows, D_MODEL), F32),
        compiler_params=_cparams(("arbitrary",), 32),
        name="moe_dispatch",
    )(tail_tiles, n_used, pos3, x)


def _expert_ffn_kernel(te_ref, used_ref, x_ref, wg_ref, wu_ref, wd_ref, o_ref, xb_sc):
    i = pl.program_id(0)
    f = pl.program_id(1)
    live = i < used_ref[0]

    @pl.when(jnp.logical_and(live, f == 0))
    def _():
        xb_sc[...] = x_ref[...].astype(BF16)

    @pl.when(live)
    def _():
        xb = xb_sc[...]
        part = None
        for lo, size in _ffn_chunks(wg_ref.shape[1]):
            gate = _dot(xb, wg_ref[:, lo:lo + size])
            up = _dot(xb, wu_ref[:, lo:lo + size])
            act = (gate * jax.nn.sigmoid(gate) * up).astype(BF16)
            p = _dot(act, wd_ref[lo:lo + size, :])
            part = p if part is None else part + p

        @pl.when(f == 0)
        def _():
            o_ref[...] = part

        @pl.when(f > 0)
        def _():
            o_ref[...] += part

    @pl.when(jnp.logical_and(jnp.logical_not(live), f == 0))
    def _():
        o_ref[...] = jnp.zeros(o_ref.shape, F32)


def _expert_ffn(xs, tile_expert, n_used, wg, wu, wd, fc):
    n_rows = xs.shape[0]
    E, _, F = wg.shape
    tm = MOE_TILE

    def row(i, f, te, used):
        return (jnp.minimum(i, used[0] - 1), 0)

    return pl.pallas_call(
        _expert_ffn_kernel,
        grid_spec=pltpu.PrefetchScalarGridSpec(
            num_scalar_prefetch=2,
            grid=(n_rows // tm, F // fc),
            in_specs=[pl.BlockSpec((tm, D_MODEL), row),
                      pl.BlockSpec((None, D_MODEL, fc), lambda i, f, te, used: (te[i], 0, f)),
                      pl.BlockSpec((None, D_MODEL, fc), lambda i, f, te, used: (te[i], 0, f)),
                      pl.BlockSpec((None, fc, D_MODEL), lambda i, f, te, used: (te[i], f, 0))],
            out_specs=pl.BlockSpec((tm, D_MODEL), lambda i, f, te, used: (i, 0)),
            scratch_shapes=[pltpu.VMEM((tm, D_MODEL), BF16)],
        ),
        out_shape=jax.ShapeDtypeStruct((n_rows, D_MODEL), F32),
        compiler_params=_cparams(("arbitrary", "arbitrary"), 56),
        name="moe_expert_ffn",
    )(tile_expert, n_used, xs, wg, wu, wd)


def _combine_kernel(pos_ref, w_ref, x_ref, g_ref, b_ref, ys_hbm, o_ref, rows_sc, sem, *, tm):
    def issue(t, carry):
        for k in range(MOE_TOP_K):
            pltpu.make_async_copy(ys_hbm.at[pl.ds(pos_ref[k, t], 1)],
                                  rows_sc.at[k, pl.ds(t, 1)], sem).start(priority=k)
        return carry

    lax.fori_loop(0, tm, issue, 0, unroll=8)
    for k in range(MOE_TOP_K):
        pltpu.make_async_copy(ys_hbm.at[pl.ds(0, tm)], rows_sc.at[k], sem).wait()
    w = w_ref[...]
    y = w[:, 0:1] * rows_sc[0] + w[:, 1:2] * rows_sc[1]
    o_ref[...] = _layer_norm(DEEPNORM_ALPHA * x_ref[...] + y, g_ref[...], b_ref[...])


def _combine_ln(ys, pos_t, w_nat, x, g, b, tm):
    T = x.shape[0]
    n_steps = T // tm
    pos3 = pos_t.reshape(MOE_TOP_K, n_steps, tm).transpose(1, 0, 2)
    row = pl.BlockSpec((tm, D_MODEL), lambda i: (i, 0))
    vec = pl.BlockSpec((1, D_MODEL), lambda i: (0, 0))
    return pl.pallas_call(
        functools.partial(_combine_kernel, tm=tm),
        grid=(n_steps,),
        in_specs=[pl.BlockSpec((None, MOE_TOP_K, tm), lambda i: (i, 0, 0), memory_space=pltpu.SMEM),
                  pl.BlockSpec((tm, MOE_TOP_K), lambda i: (i, 0)),
                  row, vec, vec,
                  pl.BlockSpec(memory_space=pl.ANY)],
        out_specs=row,
        out_shape=jax.ShapeDtypeStruct((T, D_MODEL), F32),
        scratch_shapes=[pltpu.VMEM((MOE_TOP_K, tm, D_MODEL), F32), pltpu.SemaphoreType.DMA],
        compiler_params=_cparams(("arbitrary",), 32),
        name="moe_combine_ln",
    )(pos3, w_nat, x, g, b, ys)


def _moe_sparse(x, w_router, wg, wu, wd, g, b):
    T = x.shape[0]
    tile = MOE_TILE
    n_rows = MOE_TOP_K * T + N_EXPERTS * tile
    n_tiles = n_rows // tile
    sel_t, gate_t = _router(x, w_router.T, tm=1024)
    rank_t, counts = _rank(sel_t)
    tiles_per = jnp.ceil(counts[:, 0] / tile).astype(jnp.int32)
    tile_end = jnp.cumsum(tiles_per)
    tile_start = tile_end - tiles_per
    n_used = tile_end[-1:]
    tile_expert = jnp.minimum(
        jnp.sum(jnp.arange(n_tiles, dtype=jnp.int32)[:, None] >= tile_end[None, :], axis=1),
        N_EXPERTS - 1).astype(jnp.int32)
    tail_tiles = jnp.maximum(tile_end - 1, 0).astype(jnp.int32)
    offsets_col = (tile_start * tile).astype(F32)[:, None]
    pos_t, w_t = _slots(sel_t, gate_t, rank_t, offsets_col, tm=1024)
    xs = _dispatch(x, pos_t, tail_tiles, n_used, n_rows, tm=512)
    ys = _expert_ffn(xs, tile_expert, n_used, wg, wu, wd, fc=D_FF_EXPERT // 2)
    return _combine_ln(ys, pos_t, w_t.T, x, g, b, tm=256)


def _rotary_tables(S):
    pos = jnp.arange(S, dtype=jnp.int32)
    inv = ROPE_THETA ** (-jnp.arange(ROPE_HALF, dtype=F32) / ROPE_HALF)
    ang = pos.astype(F32)[:, None] * inv[None, :]
    cos, sin = jnp.cos(ang), jnp.sin(ang)
    ones = jnp.ones((S, HEAD_DIM - ROPE_DIMS), F32)
    zeros = jnp.zeros((S, HEAD_DIM - ROPE_DIMS), F32)
    z8 = jnp.zeros((S, ROPE_HALF), F32)
    cos_h = jnp.concatenate([cos, cos, ones], axis=1)
    sup_h = jnp.concatenate([-sin, z8, zeros], axis=1)
    sdn_h = jnp.concatenate([z8, sin, zeros], axis=1)
    rep = LANES // HEAD_DIM
    return (jnp.tile(cos_h, (1, rep)), jnp.tile(sup_h, (1, rep)), jnp.tile(sdn_h, (1, rep)))


def _even_layer(h, w_in, b_forget, w_out, ln_mix_g, ln_mix_b, w_gate, w_up, w_down,
                ln_ffn_g, ln_ffn_b):
    B, S, _ = h.shape
    T = B * S
    W = ATT_WIDTH
    w_main = jnp.concatenate([w_in[:, :3 * W], w_in[:, 3 * W + N_HEADS:]], axis=1).astype(BF16)
    wf_t = w_in[:, 3 * W:3 * W + N_HEADS].T.astype(BF16)
    cos_t, sup_t, sdn_t = _rotary_tables(S)
    qa, ka, va, qd, kd, vd, f_t = _proj0(h, w_main, wf_t, cos_t, sup_t, sdn_t, tm=min(512, S))

    bias_col = jnp.tile(b_forget, B)[:, None]
    c = _fox_cumsum(f_t.reshape(B * N_HEADS, S), bias_col).reshape(B, N_HEADS, S)
    o_fox = _fox_attention(qa, ka, va, c, t=min(512, S))
    o_dil = _dilated_attention(qd, kd, vd)

    w_out_b = w_out.astype(BF16)
    tm = 512
    h1 = _outproj_ln([o_fox.reshape(T, W), o_dil.reshape(T, W)], [w_out_b[:W], w_out_b[W:]],
                     h.reshape(T, D_MODEL), ln_mix_g[None, :], ln_mix_b[None, :], tm)
    h2 = _dense_ffn(h1, w_gate.astype(BF16), w_up.astype(BF16), w_down.astype(BF16),
                    ln_ffn_g[None, :], ln_ffn_b[None, :], tm)
    return h2.reshape(B, S, D_MODEL)


def _odd_layer(h, w_in, b_igate, b_fgate, w_conv, norm_g, w_out, ln_mix_g, ln_mix_b, w_router,
               w_gate, w_up, w_down, ln_ffn_g, ln_ffn_b):
    B, S, _ = h.shape
    T = B * S
    D = D_MODEL
    wqk = w_in[:, :2 * D].astype(BF16)
    wv = w_in[:, 2 * D:3 * D].astype(BF16)
    wgt = w_in[:, 3 * D:3 * D + 2 * N_HEADS].astype(BF16)
    wog = w_in[:, 3 * D + 2 * N_HEADS:].astype(BF16)
    q, k, v, og, gates = _proj1(h, wqk, wv, wog, wgt, w_conv, tm=min(512, S))
    gates_t = jnp.swapaxes(gates, 1, 2)
    hm = _mlstm(q, k, v, og, gates, gates_t, b_igate, b_fgate, norm_g, L=min(256, S))

    tm = 512
    h1 = _outproj_ln([hm.reshape(T, D)], [w_out.astype(BF16)], h.reshape(T, D),
                     ln_mix_g[None, :], ln_mix_b[None, :], tm)
    h2 = _moe_sparse(h1, w_router, w_gate.astype(BF16), w_up.astype(BF16), w_down.astype(BF16),
                     ln_ffn_g[None, :], ln_ffn_b[None, :])
    return h2.reshape(B, S, D)


def kernel(x, w_in_e, b_forget_e, w_out_e, ln_mix_g_e, ln_mix_b_e, ffn_w_gate_e, ffn_w_up_e,
           ffn_w_down_e, ln_ffn_g_e, ln_ffn_b_e, w_in_o, b_igate_o, b_fgate_o, w_conv_o,
           mlstm_norm_g_o, w_out_o, ln_mix_g_o, ln_mix_b_o, w_router_o, moe_w_gate_o,
           moe_w_up_o, moe_w_down_o, ln_ffn_g_o, ln_ffn_b_o):
    h = x
    for layer in range(DEPTH):
        i = layer // 2
        if layer % 2 == 0:
            h = _even_layer(h, w_in_e[i], b_forget_e[i], w_out_e[i], ln_mix_g_e[i], ln_mix_b_e[i],
                            ffn_w_gate_e[i], ffn_w_up_e[i], ffn_w_down_e[i], ln_ffn_g_e[i],
                            ln_ffn_b_e[i])
        else:
            h = _odd_layer(h, w_in_o[i], b_igate_o[i], b_fgate_o[i], w_conv_o[i],
                           mlstm_norm_g_o[i], w_out_o[i], ln_mix_g_o[i], ln_mix_b_o[i],
                           w_router_o[i], moe_w_gate_o[i], moe_w_up_o[i], moe_w_down_o[i],
                           ln_ffn_g_o[i], ln_ffn_b_o[i])
    return h
```

```python
import functools
import math

import jax
import jax.numpy as jnp
from jax import lax
from jax.experimental import pallas as pl
from jax.experimental.pallas import tpu as pltpu

F32 = jnp.float32
BF16 = jnp.bfloat16

D_MODEL = 1024
HEAD_DIM = 64
N_HEADS = 8
ATT_WIDTH = N_HEADS * HEAD_DIM
DIL_CONFIGS = ((128, 1), (512, 4), (2048, 16))
ROPE_THETA = 500000.0
ROPE_DIMS = HEAD_DIM // 4
ROPE_HALF = ROPE_DIMS // 2
MLSTM_HEAD_DIM = D_MODEL // N_HEADS
CONV_WIDTH = 4
D_FF_DENSE = 2816
N_EXPERTS = 8
D_FF_EXPERT = 3584
DEPTH = 2
DEEPNORM_ALPHA = (2 * DEPTH) ** 0.25
LN_EPS = 1e-5

LANES = 128
MXU_DIM = 256
BAND = 128
MIB = 1024 * 1024

NEG_INF = float("-inf")


def _cparams(semantics, vmem_mib):
    return pltpu.CompilerParams(dimension_semantics=semantics, vmem_limit_bytes=vmem_mib * MIB)


def _dot(a, b):
    return jnp.dot(a, b, preferred_element_type=F32)


def _dot_nt(a, b):
    return lax.dot_general(a, b, (((1,), (1,)), ((), ())), preferred_element_type=F32)


def _dot_tn(a, b):
    return lax.dot_general(a, b, (((0,), (0,)), ((), ())), preferred_element_type=F32)


def _split3(x):
    hi = x.astype(BF16)
    r = x - hi.astype(F32)
    mid = r.astype(BF16)
    lo = (r - mid.astype(F32)).astype(BF16)
    return hi, mid, lo


def _log_sigmoid(z):
    return -(jnp.maximum(-z, 0.0) + jnp.log1p(jnp.exp(-jnp.abs(z))))


def _layer_norm(z, g, b):
    mu = jnp.mean(z, axis=-1, keepdims=True)
    zc = z - mu
    var = jnp.mean(zc * zc, axis=-1, keepdims=True)
    return zc * lax.rsqrt(var + LN_EPS) * g + b


def _proj0_kernel(x_ref, w_ref, wf_ref, cos_ref, sup_ref, sdn_ref,
                  qa_ref, ka_ref, va_ref, qd_ref, kd_ref, vd_ref, ft_ref):
    xb = x_ref[...].astype(BF16)
    outs = (qa_ref, ka_ref, va_ref, qd_ref, kd_ref, vd_ref)
    scale = HEAD_DIM ** -0.5
    for j, o_ref in enumerate(outs):
        for c in range(ATT_WIDTH // LANES):
            lo = j * ATT_WIDTH + c * LANES
            r = _dot(xb, w_ref[:, lo:lo + LANES])
            if j in (3, 4):
                r = (r * cos_ref[...]
                     + pltpu.roll(r, LANES - ROPE_HALF, axis=1) * sup_ref[...]
                     + pltpu.roll(r, ROPE_HALF, axis=1) * sdn_ref[...])
            if j in (0, 3):
                r = r * scale
            o_ref[:, c * LANES:(c + 1) * LANES] = r.astype(BF16)
    ft_ref[...] = _dot_nt(wf_ref[...], xb)


def _proj0(x, w_main, wf_t, cos_t, sup_t, sdn_t, tm):
    B, S, _ = x.shape
    n_i = S // tm
    act = jax.ShapeDtypeStruct((B, S, ATT_WIDTH), BF16)
    act_spec = pl.BlockSpec((None, tm, ATT_WIDTH), lambda b, i: (b, i, 0))
    tab_spec = pl.BlockSpec((tm, LANES), lambda b, i: (i, 0))
    return pl.pallas_call(
        _proj0_kernel,
        grid=(B, n_i),
        in_specs=[
            pl.BlockSpec((None, tm, D_MODEL), lambda b, i: (b, i, 0)),
            pl.BlockSpec((D_MODEL, 6 * ATT_WIDTH), lambda b, i: (0, 0)),
            pl.BlockSpec((N_HEADS, D_MODEL), lambda b, i: (0, 0)),
            tab_spec, tab_spec, tab_spec,
        ],
        out_specs=[act_spec] * 6 + [pl.BlockSpec((None, N_HEADS, tm), lambda b, i: (b, 0, i))],
        out_shape=[act] * 6 + [jax.ShapeDtypeStruct((B, N_HEADS, S), F32)],
        compiler_params=_cparams(("parallel", "parallel"), 48),
        name="proj0",
    )(x, w_main, wf_t, cos_t, sup_t, sdn_t)


def _fox_cumsum_kernel(f_ref, bias_ref, c_ref):
    S = f_ref.shape[1]
    lf = _log_sigmoid(f_ref[...] + bias_ref[...])
    row = lax.broadcasted_iota(jnp.int32, (S, S), 0)
    col = lax.broadcasted_iota(jnp.int32, (S, S), 1)
    upper = jnp.where(row <= col, 1.0, 0.0).astype(BF16)
    hi, mid, lo = _split3(lf)
    c_ref[...] = _dot(hi, upper) + _dot(mid, upper) + _dot(lo, upper)


def _fox_cumsum(f_t, bias_col):
    R, S = f_t.shape
    return pl.pallas_call(
        _fox_cumsum_kernel,
        grid=(1,),
        in_specs=[pl.BlockSpec((R, S), lambda i: (0, 0)), pl.BlockSpec((R, 1), lambda i: (0, 0))],
        out_specs=pl.BlockSpec((R, S), lambda i: (0, 0)),
        out_shape=jax.ShapeDtypeStruct((R, S), F32),
        compiler_params=_cparams(("arbitrary",), 48),
        name="fox_cumsum",
    )(f_t, bias_col)


def _fox_kernel(q_ref, k_ref, v_ref, c_ref, o_ref, m_sc, l_sc, acc_sc, *, t):
    i = pl.program_id(1)
    j = pl.program_id(2)

    @pl.when(j == 0)
    def _():
        m_sc[...] = jnp.full(m_sc.shape, NEG_INF, F32)
        l_sc[...] = jnp.zeros(l_sc.shape, F32)
        acc_sc[...] = jnp.zeros(acc_sc.shape, F32)

    def step(masked):
        if masked:
            key = lax.broadcasted_iota(jnp.int32, (t, t), 0)
            qry = lax.broadcasted_iota(jnp.int32, (t, t), 1)
            keep = key <= qry
        for h in range(N_HEADS):
            sl = slice(h * HEAD_DIM, (h + 1) * HEAD_DIM)
            s = _dot_nt(k_ref[:, sl], q_ref[:, sl]) - c_ref[:, h:h + 1]
            if masked:
                s = jnp.where(keep, s, NEG_INF)
            m_prev = m_sc[h:h + 1, :]
            m_new = jnp.maximum(m_prev, jnp.max(s, axis=0, keepdims=True))
            alpha = jnp.exp(m_prev - m_new)
            p = jnp.exp(s - m_new)
            l_sc[h:h + 1, :] = alpha * l_sc[h:h + 1, :] + jnp.sum(p, axis=0, keepdims=True)
            acc_sc[sl, :] = alpha * acc_sc[sl, :] + _dot_tn(v_ref[:, sl], p.astype(BF16))
            m_sc[h:h + 1, :] = m_new

    @pl.when(j < i)
    def _():
        step(False)

    @pl.when(j == i)
    def _():
        step(True)
        out_t = jnp.concatenate(
            [acc_sc[h * HEAD_DIM:(h + 1) * HEAD_DIM, :] / l_sc[h:h + 1, :] for h in range(N_HEADS)],
            axis=0)
        o_ref[...] = out_t.T.astype(BF16)


def _fox_attention(q, k, v, c, t):
    B, S, _ = q.shape
    n = S // t
    q_spec = pl.BlockSpec((None, t, ATT_WIDTH), lambda b, i, j: (b, i, 0))
    kv_spec = pl.BlockSpec((None, t, ATT_WIDTH), lambda b, i, j: (b, jnp.minimum(j, i), 0))
    return pl.pallas_call(
        functools.partial(_fox_kernel, t=t),
        grid=(B, n, n),
        in_specs=[q_spec, kv_spec, kv_spec,
                  pl.BlockSpec((None, t, N_HEADS), lambda b, i, j: (b, jnp.minimum(j, i), 0))],
        out_specs=q_spec,
        out_shape=jax.ShapeDtypeStruct((B, S, ATT_WIDTH), BF16),
        scratch_shapes=[pltpu.VMEM((N_HEADS, t), F32), pltpu.VMEM((N_HEADS, t), F32),
                        pltpu.VMEM((ATT_WIDTH, t), F32)],
        compiler_params=_cparams(("parallel", "parallel", "arbitrary"), 48),
        name="fox_attention",
    )(q, k, v, c)


def _dil_kernel(*refs, rg, sub, tqu, has_prev, first, last):
    refs = list(refs)
    q_ref, k_ref, v_ref = refs[:3]
    pos = 3
    if has_prev:
        kp_ref, vp_ref = refs[pos:pos + 2]
        pos += 2
    if not first:
        acc_in, st_in = refs[pos:pos + 2]
        pos += 2
    if last:
        o_ref = refs[pos]
    else:
        acc_out, st_out = refs[pos:pos + 2]

    blk = pl.program_id(1)
    nk = tqu + (BAND if has_prev else 0)
    key = lax.broadcasted_iota(jnp.int32, (nk, tqu), 0)
    qry = lax.broadcasted_iota(jnp.int32, (nk, tqu), 1)
    if has_prev:
        keep = jnp.logical_and(key >= qry, key <= qry + BAND)
        keep_edge = jnp.logical_and(keep, jnp.logical_or(key >= BAND, blk > 0))
    else:
        keep = key <= qry
    pair = LANES // HEAD_DIM

    for a in range(sub):
        rows = slice(a * tqu, (a + 1) * tqu)
        for rr in range(rg):
            if not first:
                st_old = st_in[rows, rr * LANES:(rr + 1) * LANES].T
            ms, ls = [], []
            for slab in range(N_HEADS // pair):
                slab_lanes = slice(rr * ATT_WIDTH + slab * LANES, rr * ATT_WIDTH + (slab + 1) * LANES)
                if not first:
                    acc_old = acc_in[rows, slab_lanes].T
                outs = []
                for hh in range(pair):
                    h = slab * pair + hh
                    lo = rr * ATT_WIDTH + h * HEAD_DIM
                    sl = slice(lo, lo + HEAD_DIM)
                    q = q_ref[rows, sl]
                    if not has_prev:
                        k_cat, v_cat, mask = k_ref[rows, sl], v_ref[rows, sl], keep
                    elif a == 0:
                        k_cat = jnp.concatenate([kp_ref[:, sl], k_ref[rows, sl]], axis=0)
                        v_cat = jnp.concatenate([vp_ref[:, sl], v_ref[rows, sl]], axis=0)
                        mask = keep_edge
                    else:
                        krows = slice(a * tqu - BAND, (a + 1) * tqu)
                        k_cat, v_cat, mask = k_ref[krows, sl], v_ref[krows, sl], keep
                    s = jnp.where(mask, _dot_nt(k_cat, q), NEG_INF)
                    m = jnp.max(s, axis=0, keepdims=True)
                    if not first:
                        m_old = st_old[h:h + 1, :]
                        l_old = st_old[N_HEADS + h:N_HEADS + h + 1, :]
                        m_new = jnp.maximum(m, m_old)
                        alpha = jnp.exp(m_old - m_new)
                        m = m_new
                    p = jnp.exp(s - m)
                    l = jnp.sum(p, axis=0, keepdims=True)
                    acc = _dot_tn(v_cat, p.astype(BF16))
                    if not first:
                        l = l + alpha * l_old
                        acc = acc + alpha * acc_old[hh * HEAD_DIM:(hh + 1) * HEAD_DIM, :]
                    if last:
                        acc = acc / l
                    outs.append(acc)
                    ms.append(m)
                    ls.append(l)
                slab_out = jnp.concatenate(outs, axis=0).T
                if last:
                    o_ref[rows, slab_lanes] = slab_out.astype(BF16)
                else:
                    acc_out[rows, slab_lanes] = slab_out
            if not last:
                pad = jnp.zeros((LANES - 2 * N_HEADS, tqu), F32)
                st_out[rows, rr * LANES:(rr + 1) * LANES] = jnp.concatenate(ms + ls + [pad], axis=0).T


def _dilated_branch(q, k, v, state, dilation, last):
    B, S, _ = q.shape
    d = dilation
    L = S // d
    first = state is None
    rg = min(4, d)
    tqu = min(2 * BAND, L)
    tqb = min(L, 4 * tqu // rg)
    sub = tqb // tqu
    n_blk = L // tqb
    n_grp = d // rg
    has_prev = L > BAND
    wq = rg * ATT_WIDTH
    wst = rg * LANES

    def view(a):
        return a.reshape(B, L, d * a.shape[-1])

    main = lambda b, i, g: (b, i, g)
    prev = lambda b, i, g: (b, jnp.maximum(i * (tqb // BAND) - 1, 0), g)
    qkv_spec = pl.BlockSpec((None, tqb, wq), main)
    in_specs = [qkv_spec] * 3
    args = [view(q), view(k), view(v)]
    if has_prev:
        in_specs += [pl.BlockSpec((None, BAND, wq), prev)] * 2
        args += [view(k), view(v)]
    if not first:
        in_specs += [pl.BlockSpec((None, tqb, wq), main), pl.BlockSpec((None, tqb, wst), main)]
        args += [view(state[0]), view(state[1])]
    if last:
        out_specs = pl.BlockSpec((None, tqb, wq), main)
        out_shape = jax.ShapeDtypeStruct((B, L, d * ATT_WIDTH), BF16)
    else:
        out_specs = [pl.BlockSpec((None, tqb, wq), main), pl.BlockSpec((None, tqb, wst), main)]
        out_shape = [jax.ShapeDtypeStruct((B, L, d * ATT_WIDTH), F32),
                     jax.ShapeDtypeStruct((B, L, d * LANES), F32)]
    out = pl.pallas_call(
        functools.partial(_dil_kernel, rg=rg, sub=sub, tqu=tqu, has_prev=has_prev, first=first,
                          last=last),
        grid=(B, n_blk, n_grp),
        in_specs=in_specs,
        out_specs=out_specs,
        out_shape=out_shape,
        compiler_params=_cparams(("parallel", "parallel", "parallel"), 48),
        name=f"dilated_d{d}",
    )(*args)
    if last:
        return out.reshape(B, S, ATT_WIDTH)
    return out[0].reshape(B, S, ATT_WIDTH), out[1].reshape(B, S, LANES)


def _dilated_attention(q, k, v):
    state = None
    order = sorted(DIL_CONFIGS, key=lambda wd: -wd[1])
    for n, (window, d) in enumerate(order):
        assert window // d == BAND
        state = _dilated_branch(q, k, v, state, d, last=(n == len(order) - 1))
    return state


def _outproj_ln_kernel(*refs, n_in):
    a_refs = refs[:n_in]
    w_refs = refs[n_in:2 * n_in]
    x_ref, g_ref, b_ref, o_ref = refs[2 * n_in:]
    y = _dot(a_refs[0][...], w_refs[0][...])
    for a_ref, w_ref in zip(a_refs[1:], w_refs[1:]):
        y = y + _dot(a_ref[...], w_ref[...])
    o_ref[...] = _layer_norm(DEEPNORM_ALPHA * x_ref[...] + y, g_ref[...], b_ref[...])


def _outproj_ln(acts, ws, x, g, b, tm):
    T = x.shape[0]
    n_in = len(acts)
    in_specs = [pl.BlockSpec((tm, a.shape[1]), lambda i: (i, 0)) for a in acts]
    in_specs += [pl.BlockSpec(w.shape, lambda i: (0, 0)) for w in ws]
    row = pl.BlockSpec((tm, D_MODEL), lambda i: (i, 0))
    vec = pl.BlockSpec((1, D_MODEL), lambda i: (0, 0))
    return pl.pallas_call(
        functools.partial(_outproj_ln_kernel, n_in=n_in),
        grid=(T // tm,),
        in_specs=in_specs + [row, vec, vec],
        out_specs=row,
        out_shape=jax.ShapeDtypeStruct((T, D_MODEL), F32),
        compiler_params=_cparams(("parallel",), 48),
        name="outproj_ln",
    )(*acts, *ws, x, g, b)


def _ffn_chunks(width):
    chunks, lo = [], 0
    while lo < width:
        size = min(2 * MXU_DIM, width - lo)
        chunks.append((lo, size))
        lo += size
    return chunks


def _dense_ffn_kernel(x_ref, wg_ref, wu_ref, wd_ref, g_ref, b_ref, o_ref):
    x = x_ref[...]
    xb = x.astype(BF16)
    y = None
    for lo, size in _ffn_chunks(wg_ref.shape[1]):
        gate = _dot(xb, wg_ref[:, lo:lo + size])
        up = _dot(xb, wu_ref[:, lo:lo + size])
        act = (gate * jax.nn.sigmoid(gate) * up).astype(BF16)
        part = _dot(act, wd_ref[lo:lo + size, :])
        y = part if y is None else y + part
    o_ref[...] = _layer_norm(DEEPNORM_ALPHA * x + y, g_ref[...], b_ref[...])


def _dense_ffn(x, wg, wu, wd, g, b, tm):
    T = x.shape[0]
    F = wg.shape[1]
    row = pl.BlockSpec((tm, D_MODEL), lambda i: (i, 0))
    vec = pl.BlockSpec((1, D_MODEL), lambda i: (0, 0))
    once = pl.Buffered(1)
    return pl.pallas_call(
        _dense_ffn_kernel,
        grid=(T // tm,),
        in_specs=[row,
                  pl.BlockSpec((D_MODEL, F), lambda i: (0, 0), pipeline_mode=once),
                  pl.BlockSpec((D_MODEL, F), lambda i: (0, 0), pipeline_mode=once),
                  pl.BlockSpec((F, D_MODEL), lambda i: (0, 0), pipeline_mode=once),
                  vec, vec],
        out_specs=row,
        out_shape=jax.ShapeDtypeStruct((T, D_MODEL), F32),
        compiler_params=_cparams(("parallel",), 56),
        name="dense_ffn",
    )(x, wg, wu, wd, g, b)


CONV_PAD = 8


def _proj1_kernel(x_ref, wqk_ref, wv_ref, wog_ref, wgt_ref, wconv_ref,
                  q_ref, k_ref, v_ref, og_ref, gt_ref, buf):
    i = pl.program_id(1)
    tm = x_ref.shape[0]
    xb = x_ref[...].astype(BF16)
    kscale = MLSTM_HEAD_DIM ** -0.5

    @pl.when(i == 0)
    def _():
        buf[0:CONV_PAD, :] = jnp.zeros((CONV_PAD, buf.shape[1]), F32)

    for c in range(2 * D_MODEL // LANES):
        lanes = slice(c * LANES, (c + 1) * LANES)
        buf[CONV_PAD:CONV_PAD + tm, lanes] = _dot(xb, wqk_ref[:, lanes])
        y = None
        for tap in range(CONV_WIDTH):
            off = CONV_PAD - (CONV_WIDTH - 1) + tap
            term = buf[off:off + tm, lanes] * wconv_ref[tap:tap + 1, lanes]
            y = term if y is None else y + term
        y = y * jax.nn.sigmoid(y)
        buf[0:CONV_PAD, lanes] = buf[tm:tm + CONV_PAD, lanes]
        if c < D_MODEL // LANES:
            q_ref[:, lanes] = y.astype(BF16)
        else:
            k_ref[:, c * LANES - D_MODEL:(c + 1) * LANES - D_MODEL] = (y * kscale).astype(BF16)
    for c in range(D_MODEL // LANES):
        lanes = slice(c * LANES, (c + 1) * LANES)
        v_ref[:, lanes] = _dot(xb, wv_ref[:, lanes]).astype(BF16)
        og_ref[:, lanes] = _dot(xb, wog_ref[:, lanes])
    gt_ref[...] = _dot(xb, wgt_ref[...])


def _proj1(x, wqk, wv, wog, wgt, wconv, tm):
    B, S, _ = x.shape
    row = lambda b, i: (b, i, 0)
    const = lambda b, i: (0, 0)
    act_spec = pl.BlockSpec((None, tm, D_MODEL), row)
    act = jax.ShapeDtypeStruct((B, S, D_MODEL), BF16)
    return pl.pallas_call(
        _proj1_kernel,
        grid=(B, S // tm),
        in_specs=[act_spec,
                  pl.BlockSpec((D_MODEL, 2 * D_MODEL), const),
                  pl.BlockSpec((D_MODEL, D_MODEL), const),
                  pl.BlockSpec((D_MODEL, D_MODEL), const),
                  pl.BlockSpec((D_MODEL, 2 * N_HEADS), const),
                  pl.BlockSpec((CONV_WIDTH, 2 * D_MODEL), const)],
        out_specs=[act_spec, act_spec, act_spec, act_spec,
                   pl.BlockSpec((None, tm, 2 * N_HEADS), row)],
        out_shape=[act, act, act, jax.ShapeDtypeStruct((B, S, D_MODEL), F32),
                   jax.ShapeDtypeStruct((B, S, 2 * N_HEADS), F32)],
        scratch_shapes=[pltpu.VMEM((tm + CONV_PAD, 2 * D_MODEL), F32)],
        compiler_params=_cparams(("parallel", "arbitrary"), 56),
        name="proj1",
    )(x, wqk, wv, wog, wgt, wconv)


def _mlstm_kernel(q_ref, k_ref, v_ref, og_ref, gn_ref, gt_ref, bi_row, bf_row, bi_col, bf_col,
                  ng_ref, o_ref, c_sc, n_sc, m_sc, *, L):
    ci = pl.program_id(1)

    @pl.when(ci == 0)
    def _():
        c_sc[...] = jnp.zeros(c_sc.shape, F32)
        n_sc[...] = jnp.zeros(n_sc.shape, F32)
        m_sc[...] = jnp.full(m_sc.shape, NEG_INF, F32)

    row = lax.broadcasted_iota(jnp.int32, (L, L), 0)
    col = lax.broadcasted_iota(jnp.int32, (L, L), 1)
    causal = col <= row
    lower = jnp.where(causal, 1.0, 0.0).astype(BF16)
    upper = jnp.where(row <= col, 1.0, 0.0).astype(BF16)

    gn = gn_ref[...]
    gt = gt_ref[...]
    i_col = gn[:, :N_HEADS] + bi_row[...]
    lf_col = _log_sigmoid(gn[:, N_HEADS:] + bf_row[...])
    i_row = gt[:N_HEADS, :] + bi_col[...]
    lf_row = _log_sigmoid(gt[N_HEADS:, :] + bf_col[...])
    b_col = sum(_dot(lower, part) for part in _split3(lf_col))
    b_row = sum(_dot(part, upper) for part in _split3(lf_row))

    for h in range(N_HEADS):
        lanes = slice(h * MLSTM_HEAD_DIM, (h + 1) * MLSTM_HEAD_DIM)
        q = q_ref[:, lanes]
        k = k_ref[:, lanes]
        v = v_ref[:, lanes]
        bc = b_col[:, h:h + 1]
        ic = i_col[:, h:h + 1]
        br = b_row[h:h + 1, :]
        ir = i_row[h:h + 1, :]
        m_prev = m_sc[h]
        c_prev = c_sc[h]
        n_prev = n_sc[h]

        dlog = jnp.where(causal, bc - br + ir, NEG_INF)
        inter = bc + m_prev
        m_t = jnp.maximum(inter, jnp.max(dlog, axis=-1, keepdims=True))
        s = _dot_nt(q, k) * jnp.exp(dlog - m_t)
        inter_w = jnp.exp(inter - m_t)
        num = _dot(s.astype(BF16), v) + inter_w * _dot(q, c_prev.astype(BF16))
        den = (jnp.sum(s, axis=-1, keepdims=True)
               + inter_w * jnp.sum(q.astype(F32) * n_prev, axis=-1, keepdims=True))
        hh = num / jnp.maximum(jnp.abs(den), jnp.exp(-m_t))

        b_last = bc[L - 1:L, :]
        g = b_last - bc + ic
        m_new = jnp.maximum(b_last + m_prev, jnp.max(g, axis=0, keepdims=True))
        w = jnp.exp(g - m_new)
        decay = jnp.exp(b_last + m_prev - m_new)
        kw = k.astype(F32) * w
        c_sc[h] = decay * c_prev + _dot_tn(kw.astype(BF16), v)
        n_sc[h] = decay * n_prev + jnp.sum(kw, axis=0, keepdims=True)
        m_sc[h] = m_new

        mu = jnp.mean(hh, axis=-1, keepdims=True)
        hc = hh - mu
        var = jnp.mean(hc * hc, axis=-1, keepdims=True)
        hn = hc * lax.rsqrt(var + LN_EPS) * ng_ref[:, lanes]
        o_ref[:, lanes] = (hn * jax.nn.sigmoid(og_ref[:, lanes])).astype(BF16)


def _mlstm(q, k, v, og, gates, gates_t, b_i, b_f, norm_g, L):
    B, S, _ = q.shape
    row = lambda b, c: (b, c, 0)
    const = lambda b, c: (0, 0)
    act_spec = pl.BlockSpec((None, L, D_MODEL), row)
    return pl.pallas_call(
        functools.partial(_mlstm_kernel, L=L),
        grid=(B, S // L),
        in_specs=[act_spec, act_spec, act_spec, act_spec,
                  pl.BlockSpec((None, L, 2 * N_HEADS), row),
                  pl.BlockSpec((None, 2 * N_HEADS, L), lambda b, c: (b, 0, c)),
                  pl.BlockSpec((1, N_HEADS), const), pl.BlockSpec((1, N_HEADS), const),
                  pl.BlockSpec((N_HEADS, 1), const), pl.BlockSpec((N_HEADS, 1), const),
                  pl.BlockSpec((1, D_MODEL), const)],
        out_specs=act_spec,
        out_shape=jax.ShapeDtypeStruct((B, S, D_MODEL), BF16),
        scratch_shapes=[pltpu.VMEM((N_HEADS, MLSTM_HEAD_DIM, MLSTM_HEAD_DIM), F32),
                        pltpu.VMEM((N_HEADS, 1, MLSTM_HEAD_DIM), F32),
                        pltpu.VMEM((N_HEADS, 1, 1), F32)],
        compiler_params=_cparams(("parallel", "arbitrary"), 48),
        name="mlstm",
    )(q, k, v, og, gates, gates_t, b_i[None, :], b_f[None, :], b_i[:, None], b_f[:, None],
      norm_g[None, :])


MOE_TOP_K = 2
MOE_TILE = 512
RANK_BLOCK = 1024


def _router_kernel(x_ref, wt_ref, sel_ref, gate_ref):
    logits = lax.dot_general(wt_ref[...], x_ref[...], (((1,), (1,)), ((), ())),
                             preferred_element_type=F32, precision=lax.Precision.HIGHEST)
    idx = lax.broadcasted_iota(jnp.int32, logits.shape, 0)
    m1 = jnp.max(logits, axis=0, keepdims=True)
    i1 = jnp.min(jnp.where(logits == m1, idx, N_EXPERTS), axis=0, keepdims=True)
    pick1 = idx == i1
    rest = jnp.where(pick1, NEG_INF, logits)
    m2 = jnp.max(rest, axis=0, keepdims=True)
    i2 = jnp.min(jnp.where(rest == m2, idx, N_EXPERTS), axis=0, keepdims=True)
    pick2 = idx == i2
    e2 = jnp.exp(m2 - m1)
    w1 = 1.0 / (1.0 + e2)
    w2 = e2 / (1.0 + e2)
    sel_ref[...] = jnp.where(jnp.logical_or(pick1, pick2), 1.0, 0.0)
    gate_ref[...] = jnp.where(pick1, w1, 0.0) + jnp.where(pick2, w2, 0.0)


def _router(x, w_router_t, tm):
    T = x.shape[0]
    out = jax.ShapeDtypeStruct((N_EXPERTS, T), F32)
    out_spec = pl.BlockSpec((N_EXPERTS, tm), lambda i: (0, i))
    return pl.pallas_call(
        _router_kernel,
        grid=(T // tm,),
        in_specs=[pl.BlockSpec((tm, D_MODEL), lambda i: (i, 0)),
                  pl.BlockSpec((N_EXPERTS, D_MODEL), lambda i: (0, 0))],
        out_specs=[out_spec, out_spec],
        out_shape=[out, out],
        compiler_params=_cparams(("parallel",), 32),
        name="moe_router",
    )(x, w_router_t)


def _rank_kernel(sel_ref, rank_ref, count_ref, upper_sc, carry_sc):
    n = sel_ref.shape[1]

    @pl.when(pl.program_id(0) == 0)
    def _():
        row = lax.broadcasted_iota(jnp.int32, (n, n), 0)
        col = lax.broadcasted_iota(jnp.int32, (n, n), 1)
        upper_sc[...] = jnp.where(row <= col, 1.0, 0.0).astype(BF16)
        carry_sc[...] = jnp.zeros(carry_sc.shape, F32)

    sel = sel_ref[...]
    incl = _dot(sel.astype(BF16), upper_sc[...])
    rank_ref[...] = carry_sc[...] + incl - sel
    carry_sc[...] = carry_sc[...] + incl[:, n - 1:n]
    count_ref[...] = carry_sc[...]


def _rank(sel_t):
    E, T = sel_t.shape
    n = min(RANK_BLOCK, T)
    return pl.pallas_call(
        _rank_kernel,
        grid=(T // n,),
        in_specs=[pl.BlockSpec((E, n), lambda i: (0, i))],
        out_specs=[pl.BlockSpec((E, n), lambda i: (0, i)), pl.BlockSpec((E, 1), lambda i: (0, 0))],
        out_shape=[jax.ShapeDtypeStruct((E, T), F32), jax.ShapeDtypeStruct((E, 1), F32)],
        scratch_shapes=[pltpu.VMEM((n, n), BF16), pltpu.VMEM((E, 1), F32)],
        compiler_params=_cparams(("arbitrary",), 32),
        name="moe_rank",
    )(sel_t)


def _slot_kernel(sel_ref, gate_ref, rank_ref, off_ref, pos_ref, w_ref):
    sel = sel_ref[...] > 0.0
    idx = lax.broadcasted_iota(jnp.int32, sel.shape, 0)
    first = jnp.min(jnp.where(sel, idx, N_EXPERTS), axis=0, keepdims=True)
    second = jnp.max(jnp.where(sel, idx, -1), axis=0, keepdims=True)
    slot = off_ref[...] + rank_ref[...]
    gate = gate_ref[...]
    rows_p, rows_w = [], []
    for which in (first, second):
        hit = idx == which
        rows_p.append(jnp.sum(jnp.where(hit, slot, 0.0), axis=0, keepdims=True))
        rows_w.append(jnp.sum(jnp.where(hit, gate, 0.0), axis=0, keepdims=True))
    pos_ref[...] = jnp.concatenate(rows_p, axis=0).astype(jnp.int32)
    w_ref[...] = jnp.concatenate(rows_w, axis=0)


def _slots(sel_t, gate_t, rank_t, offsets_col, tm):
    E, T = sel_t.shape
    spec = pl.BlockSpec((E, tm), lambda i: (0, i))
    out_spec = pl.BlockSpec((MOE_TOP_K, tm), lambda i: (0, i))
    return pl.pallas_call(
        _slot_kernel,
        grid=(T // tm,),
        in_specs=[spec, spec, spec, pl.BlockSpec((E, 1), lambda i: (0, 0))],
        out_specs=[out_spec, out_spec],
        out_shape=[jax.ShapeDtypeStruct((MOE_TOP_K, T), jnp.int32),
                   jax.ShapeDtypeStruct((MOE_TOP_K, T), F32)],
        compiler_params=_cparams(("parallel",), 32),
        name="moe_slots",
    )(sel_t, gate_t, rank_t, offsets_col)


def _dispatch_kernel(tail_ref, used_ref, pos_ref, x_ref, xs_hbm, zero_sc, sem, zsem, *, tm):
    i = pl.program_id(0)
    n_tiles = xs_hbm.shape[0] // MOE_TILE

    def fill(tile):
        return pltpu.make_async_copy(zero_sc, xs_hbm.at[pl.ds(tile * MOE_TILE, MOE_TILE)], zsem)

    @pl.when(i == 0)
    def _():
        zero_sc[...] = jnp.zeros(zero_sc.shape, F32)
        for e in range(N_EXPERTS):
            fill(tail_ref[e]).start()
        for e in range(N_EXPERTS):
            fill(tail_ref[e]).wait()
        for j in range(N_EXPERTS):
            @pl.when(n_tiles - 1 - j >= used_ref[0])
            def _():
                c = fill(n_tiles - 1 - j)
                c.start()
                c.wait()

    def issue(t, carry):
        for k in range(MOE_TOP_K):
            pltpu.make_async_copy(x_ref.at[pl.ds(t, 1)], xs_hbm.at[pl.ds(pos_ref[k, t], 1)],
                                  sem).start(priority=k)
        return carry

    lax.fori_loop(0, tm, issue, 0, unroll=8)
    for _ in range(MOE_TOP_K):
        pltpu.make_async_copy(x_ref, xs_hbm.at[pl.ds(0, tm)], sem).wait()


def _dispatch(x, pos_t, tail_tiles, n_used, n_rows, tm):
    T = x.shape[0]
    n_steps = T // tm
    pos3 = pos_t.reshape(MOE_TOP_K, n_steps, tm).transpose(1, 0, 2)
    return pl.pallas_call(
        functools.partial(_dispatch_kernel, tm=tm),
        grid_spec=pltpu.PrefetchScalarGridSpec(
            num_scalar_prefetch=2,
            grid=(n_steps,),
            in_specs=[pl.BlockSpec((None, MOE_TOP_K, tm), lambda i, tail, used: (i, 0, 0),
                                   memory_space=pltpu.SMEM),
                      pl.BlockSpec((tm, D_MODEL), lambda i, tail, used: (i, 0))],
            out_specs=pl.BlockSpec(memory_space=pl.ANY),
            scratch_shapes=[pltpu.VMEM((MOE_TILE, D_MODEL), F32),
                            pltpu.SemaphoreType.DMA, pltpu.SemaphoreType.DMA],
        ),
        out_shape=jax.ShapeDtypeStruct((n_rows, D_MODEL), F32),
        compiler_params=_cparams(("arbitrary",), 32),
        name="moe_dispatch",
    )(tail_tiles, n_used, pos3, x)


def _expert_ffn_kernel(te_ref, used_ref, x_ref, wg_ref, wu_ref, wd_ref, o_ref, xb_sc):
    i = pl.program_id(0)
    f = pl.program_id(1)
    live = i < used_ref[0]

    @pl.when(jnp.logical_and(live, f == 0))
    def _():
        xb_sc[...] = x_ref[...].astype(BF16)

    @pl.when(live)
    def _():
        xb = xb_sc[...]
        part = None
        for lo, size in _ffn_chunks(wg_ref.shape[1]):
            gate = _dot(xb, wg_ref[:, lo:lo + size])
            up = _dot(xb, wu_ref[:, lo:lo + size])
            act = (gate * jax.nn.sigmoid(gate) * up).astype(BF16)
            p = _dot(act, wd_ref[lo:lo + size, :])
            part = p if part is None else part + p

        @pl.when(f == 0)
        def _():
            o_ref[...] = part

        @pl.when(f > 0)
        def _():
            o_ref[...] += part

    @pl.when(jnp.logical_and(jnp.logical_not(live), f == 0))
    def _():
        o_ref[...] = jnp.zeros(o_ref.shape, F32)


def _expert_ffn(xs, tile_expert, n_used, wg, wu, wd, fc):
    n_rows = xs.shape[0]
    E, _, F = wg.shape
    tm = MOE_TILE

    def row(i, f, te, used):
        return (jnp.minimum(i, used[0] - 1), 0)

    return pl.pallas_call(
        _expert_ffn_kernel,
        grid_spec=pltpu.PrefetchScalarGridSpec(
            num_scalar_prefetch=2,
            grid=(n_rows // tm, F // fc),
            in_specs=[pl.BlockSpec((tm, D_MODEL), row),
                      pl.BlockSpec((None, D_MODEL, fc), lambda i, f, te, used: (te[i], 0, f)),
                      pl.BlockSpec((None, D_MODEL, fc), lambda i, f, te, used: (te[i], 0, f)),
                      pl.BlockSpec((None, fc, D_MODEL), lambda i, f, te, used: (te[i], f, 0))],
            out_specs=pl.BlockSpec((tm, D_MODEL), lambda i, f, te, used: (i, 0)),
            scratch_shapes=[pltpu.VMEM((tm, D_MODEL), BF16)],
        ),
        out_shape=jax.ShapeDtypeStruct((n_rows, D_MODEL), F32),
        compiler_params=_cparams(("arbitrary", "arbitrary"), 56),
        name="moe_expert_ffn",
    )(tile_expert, n_used, xs, wg, wu, wd)


def _combine_kernel(pos_ref, w_ref, x_ref, g_ref, b_ref, ys_hbm, o_ref, rows_sc, sem, *, tm):
    def issue(t, carry):
        for k in range(MOE_TOP_K):
            pltpu.make_async_copy(ys_hbm.at[pl.ds(pos_ref[k, t], 1)],
                                  rows_sc.at[k, pl.ds(t, 1)], sem).start(priority=k)
        return carry

    lax.fori_loop(0, tm, issue, 0, unroll=8)
    for k in range(MOE_TOP_K):
        pltpu.make_async_copy(ys_hbm.at[pl.ds(0, tm)], rows_sc.at[k], sem).wait()
    w = w_ref[...]
    y = w[:, 0:1] * rows_sc[0] + w[:, 1:2] * rows_sc[1]
    o_ref[...] = _layer_norm(DEEPNORM_ALPHA * x_ref[...] + y, g_ref[...], b_ref[...])


def _combine_ln(ys, pos_t, w_nat, x, g, b, tm):
    T = x.shape[0]
    n_steps = T // tm
    pos3 = pos_t.reshape(MOE_TOP_K, n_steps, tm).transpose(1, 0, 2)
    row = pl.BlockSpec((tm, D_MODEL), lambda i: (i, 0))
    vec = pl.BlockSpec((1, D_MODEL), lambda i: (0, 0))
    return pl.pallas_call(
        functools.partial(_combine_kernel, tm=tm),
        grid=(n_steps,),
        in_specs=[pl.BlockSpec((None, MOE_TOP_K, tm), lambda i: (i, 0, 0), memory_space=pltpu.SMEM),
                  pl.BlockSpec((tm, MOE_TOP_K), lambda i: (i, 0)),
                  row, vec, vec,
                  pl.BlockSpec(memory_space=pl.ANY)],
        out_specs=row,
        out_shape=jax.ShapeDtypeStruct((T, D_MODEL), F32),
        scratch_shapes=[pltpu.VMEM((MOE_TOP_K, tm, D_MODEL), F32), pltpu.SemaphoreType.DMA],
        compiler_params=_cparams(("arbitrary",), 32),
        name="moe_combine_ln",
    )(pos3, w_nat, x, g, b, ys)


def _moe_sparse(x, w_router, wg, wu, wd, g, b):
    T = x.shape[0]
    tile = MOE_TILE
    n_rows = MOE_TOP_K * T + N_EXPERTS * tile
    n_tiles = n_rows // tile
    sel_t, gate_t = _router(x, w_router.T, tm=1024)
    rank_t, counts = _rank(sel_t)
    tiles_per = jnp.ceil(counts[:, 0] / tile).astype(jnp.int32)
    tile_end = jnp.cumsum(tiles_per)
    tile_start = tile_end - tiles_per
    n_used = tile_end[-1:]
    tile_expert = jnp.minimum(
        jnp.sum(jnp.arange(n_tiles, dtype=jnp.int32)[:, None] >= tile_end[None, :], axis=1),
        N_EXPERTS - 1).astype(jnp.int32)
    tail_tiles = jnp.maximum(tile_end - 1, 0).astype(jnp.int32)
    offsets_col = (tile_start * tile).astype(F32)[:, None]
    pos_t, w_t = _slots(sel_t, gate_t, rank_t, offsets_col, tm=1024)
    xs = _dispatch(x, pos_t, tail_tiles, n_used, n_rows, tm=512)
    ys = _expert_ffn(xs, tile_expert, n_used, wg, wu, wd, fc=D_FF_EXPERT // 2)
    return _combine_ln(ys, pos_t, w_t.T, x, g, b, tm=256)


def _rotary_tables(S):
    pos = jnp.arange(S, dtype=jnp.int32)
    inv = ROPE_THETA ** (-jnp.arange(ROPE_HALF, dtype=F32) / ROPE_HALF)
    ang = pos.astype(F32)[:, None] * inv[None, :]
    cos, sin = jnp.cos(ang), jnp.sin(ang)
    ones = jnp.ones((S, HEAD_DIM - ROPE_DIMS), F32)
    zeros = jnp.zeros((S, HEAD_DIM - ROPE_DIMS), F32)
    z8 = jnp.zeros((S, ROPE_HALF), F32)
    cos_h = jnp.concatenate([cos, cos, ones], axis=1)
    sup_h = jnp.concatenate([-sin, z8, zeros], axis=1)
    sdn_h = jnp.concatenate([z8, sin, zeros], axis=1)
    rep = LANES // HEAD_DIM
    return (jnp.tile(cos_h, (1, rep)), jnp.tile(sup_h, (1, rep)), jnp.tile(sdn_h, (1, rep)))


def _even_layer(h, w_in, b_forget, w_out, ln_mix_g, ln_mix_b, w_gate, w_up, w_down,
                ln_ffn_g, ln_ffn_b):
    B, S, _ = h.shape
    T = B * S
    W = ATT_WIDTH
    w_main = jnp.concatenate([w_in[:, :3 * W], w_in[:, 3 * W + N_HEADS:]], axis=1).astype(BF16)
    wf_t = w_in[:, 3 * W:3 * W + N_HEADS].T.astype(BF16)
    cos_t, sup_t, sdn_t = _rotary_tables(S)
    qa, ka, va, qd, kd, vd, f_t = _proj0(h, w_main, wf_t, cos_t, sup_t, sdn_t, tm=min(512, S))

    bias_col = jnp.tile(b_forget, B)[:, None]
    c = _fox_cumsum(f_t.reshape(B * N_HEADS, S), bias_col).reshape(B, N_HEADS, S)
    o_fox = _fox_attention(qa, ka, va, jnp.swapaxes(c, 1, 2), t=min(512, S))
    o_dil = _dilated_attention(qd, kd, vd)

    w_out_b = w_out.astype(BF16)
    tm = 512
    h1 = _outproj_ln([o_fox.reshape(T, W), o_dil.reshape(T, W)], [w_out_b[:W], w_out_b[W:]],
                     h.reshape(T, D_MODEL), ln_mix_g[None, :], ln_mix_b[None, :], tm)
    h2 = _dense_ffn(h1, w_gate.astype(BF16), w_up.astype(BF16), w_down.astype(BF16),
                    ln_ffn_g[None, :], ln_ffn_b[None, :], tm)
    return h2.reshape(B, S, D_MODEL)


def _odd_layer(h, w_in, b_igate, b_fgate, w_conv, norm_g, w_out, ln_mix_g, ln_mix_b, w_router,
               w_gate, w_up, w_down, ln_ffn_g, ln_ffn_b):
    B, S, _ = h.shape
    T = B * S
    D = D_MODEL
    wqk = w_in[:, :2 * D].astype(BF16)
    wv = w_in[:, 2 * D:3 * D].astype(BF16)
    wgt = w_in[:, 3 * D:3 * D + 2 * N_HEADS].astype(BF16)
    wog = w_in[:, 3 * D + 2 * N_HEADS:].astype(BF16)
    q, k, v, og, gates = _proj1(h, wqk, wv, wog, wgt, w_conv, tm=min(512, S))
    gates_t = jnp.swapaxes(gates, 1, 2)
    hm = _mlstm(q, k, v, og, gates, gates_t, b_igate, b_fgate, norm_g, L=min(256, S))

    tm = 512
    h1 = _outproj_ln([hm.reshape(T, D)], [w_out.astype(BF16)], h.reshape(T, D),
                     ln_mix_g[None, :], ln_mix_b[None, :], tm)
    h2 = _moe_sparse(h1, w_router, w_gate.astype(BF16), w_up.astype(BF16), w_down.astype(BF16),
                     ln_ffn_g[None, :], ln_ffn_b[None, :])
    return h2.reshape(B, S, D)


def kernel(x, w_in_e, b_forget_e, w_out_e, ln_mix_g_e, ln_mix_b_e, ffn_w_gate_e, ffn_w_up_e,
           ffn_w_down_e, ln_ffn_g_e, ln_ffn_b_e, w_in_o, b_igate_o, b_fgate_o, w_conv_o,
           mlstm_norm_g_o, w_out_o, ln_mix_g_o, ln_mix_b_o, w_router_o, moe_w_gate_o,
           moe_w_up_o, moe_w_down_o, ln_ffn_g_o, ln_ffn_b_o):
    h = x
    for layer in range(DEPTH):
        i = layer // 2
        if layer % 2 == 0:
            h = _even_layer(h, w_in_e[i], b_forget_e[i], w_out_e[i], ln_mix_g_e[i], ln_mix_b_e[i],
                            ffn_w_gate_e[i], ffn_w_up_e[i], ffn_w_down_e[i], ln_ffn_g_e[i],
                            ln_ffn_b_e[i])
        else:
            h = _odd_layer(h, w_in_o[i], b_igate_o[i], b_fgate_o[i], w_conv_o[i],
                           mlstm_norm_g_o[i], w_out_o[i], ln_mix_g_o[i], ln_mix_b_o[i],
                           w_router_o[i], moe_w_gate_o[i], moe_w_up_o[i], moe_w_down_o[i],
                           ln_ffn_g_o[i], ln_ffn_b_o[i])
    return h
```

```python
import functools
import math

import jax
import jax.numpy as jnp
from jax import lax
from jax.experimental import pallas as pl
from jax.experimental.pallas import tpu as pltpu

F32 = jnp.float32
BF16 = jnp.bfloat16

D_MODEL = 1024
HEAD_DIM = 64
N_HEADS = 8
ATT_WIDTH = N_HEADS * HEAD_DIM
DIL_CONFIGS = ((128, 1), (512, 4), (2048, 16))
ROPE_THETA = 500000.0
ROPE_DIMS = HEAD_DIM // 4
ROPE_HALF = ROPE_DIMS // 2
MLSTM_HEAD_DIM = D_MODEL // N_HEADS
CONV_WIDTH = 4
D_FF_DENSE = 2816
N_EXPERTS = 8
D_FF_EXPERT = 3584
DEPTH = 2
DEEPNORM_ALPHA = (2 * DEPTH) ** 0.25
LN_EPS = 1e-5

LANES = 128
MXU_DIM = 256
BAND = 128
MIB = 1024 * 1024

NEG_INF = float("-inf")


def _cparams(semantics, vmem_mib):
    return pltpu.CompilerParams(dimension_semantics=semantics, vmem_limit_bytes=vmem_mib * MIB)


def _dot(a, b):
    return jnp.dot(a, b, preferred_element_type=F32)


def _dot_nt(a, b):
    return lax.dot_general(a, b, (((1,), (1,)), ((), ())), preferred_element_type=F32)


def _dot_tn(a, b):
    return lax.dot_general(a, b, (((0,), (0,)), ((), ())), preferred_element_type=F32)


def _split3(x):
    hi = x.astype(BF16)
    r = x - hi.astype(F32)
    mid = r.astype(BF16)
    lo = (r - mid.astype(F32)).astype(BF16)
    return hi, mid, lo


def _log_sigmoid(z):
    return -(jnp.maximum(-z, 0.0) + jnp.log1p(jnp.exp(-jnp.abs(z))))


def _layer_norm(z, g, b):
    mu = jnp.mean(z, axis=-1, keepdims=True)
    zc = z - mu
    var = jnp.mean(zc * zc, axis=-1, keepdims=True)
    return zc * lax.rsqrt(var + LN_EPS) * g + b


def _proj0_kernel(x_ref, w_ref, wf_ref, cos_ref, sup_ref, sdn_ref,
                  qa_ref, ka_ref, va_ref, qd_ref, kd_ref, vd_ref, ft_ref):
    xb = x_ref[...].astype(BF16)
    outs = (qa_ref, ka_ref, va_ref, qd_ref, kd_ref, vd_ref)
    scale = HEAD_DIM ** -0.5
    for j, o_ref in enumerate(outs):
        full = _dot(xb, w_ref[:, j * ATT_WIDTH:(j + 1) * ATT_WIDTH])
        for c in range(ATT_WIDTH // LANES):
            r = full[:, c * LANES:(c + 1) * LANES]
            if j in (3, 4):
                r = (r * cos_ref[...]
                     + pltpu.roll(r, LANES - ROPE_HALF, axis=1) * sup_ref[...]
                     + pltpu.roll(r, ROPE_HALF, axis=1) * sdn_ref[...])
            if j in (0, 3):
                r = r * scale
            o_ref[:, c * LANES:(c + 1) * LANES] = r.astype(BF16)
    ft_ref[...] = _dot_nt(wf_ref[...], xb)


def _proj0(x, w_main, wf_t, cos_t, sup_t, sdn_t, tm):
    B, S, _ = x.shape
    n_i = S // tm
    act = jax.ShapeDtypeStruct((B, S, ATT_WIDTH), BF16)
    act_spec = pl.BlockSpec((None, tm, ATT_WIDTH), lambda b, i: (b, i, 0))
    tab_spec = pl.BlockSpec((tm, LANES), lambda b, i: (i, 0))
    return pl.pallas_call(
        _proj0_kernel,
        grid=(B, n_i),
        in_specs=[
            pl.BlockSpec((None, tm, D_MODEL), lambda b, i: (b, i, 0)),
            pl.BlockSpec((D_MODEL, 6 * ATT_WIDTH), lambda b, i: (0, 0)),
            pl.BlockSpec((N_HEADS, D_MODEL), lambda b, i: (0, 0)),
            tab_spec, tab_spec, tab_spec,
        ],
        out_specs=[act_spec] * 6 + [pl.BlockSpec((None, N_HEADS, tm), lambda b, i: (b, 0, i))],
        out_shape=[act] * 6 + [jax.ShapeDtypeStruct((B, N_HEADS, S), F32)],
        compiler_params=_cparams(("parallel", "parallel"), 48),
        name="proj0",
    )(x, w_main, wf_t, cos_t, sup_t, sdn_t)


def _fox_cumsum_kernel(f_ref, bias_ref, c_ref):
    S = f_ref.shape[1]
    lf = _log_sigmoid(f_ref[...] + bias_ref[...])
    row = lax.broadcasted_iota(jnp.int32, (S, S), 0)
    col = lax.broadcasted_iota(jnp.int32, (S, S), 1)
    upper = jnp.where(row <= col, 1.0, 0.0).astype(BF16)
    hi, mid, lo = _split3(lf)
    c_ref[...] = _dot(hi, upper) + _dot(mid, upper) + _dot(lo, upper)


def _fox_cumsum(f_t, bias_col):
    R, S = f_t.shape
    return pl.pallas_call(
        _fox_cumsum_kernel,
        grid=(1,),
        in_specs=[pl.BlockSpec((R, S), lambda i: (0, 0)), pl.BlockSpec((R, 1), lambda i: (0, 0))],
        out_specs=pl.BlockSpec((R, S), lambda i: (0, 0)),
        out_shape=jax.ShapeDtypeStruct((R, S), F32),
        compiler_params=_cparams(("arbitrary",), 48),
        name="fox_cumsum",
    )(f_t, bias_col)


def _fox_kernel(q_ref, k_ref, v_ref, c_ref, o_ref, m_sc, l_sc, acc_sc, *, t):
    i = pl.program_id(1)
    j = pl.program_id(2)

    @pl.when(j == 0)
    def _():
        m_sc[...] = jnp.full(m_sc.shape, NEG_INF, F32)
        l_sc[...] = jnp.zeros(l_sc.shape, F32)
        acc_sc[...] = jnp.zeros(acc_sc.shape, F32)

    def step(masked):
        if masked:
            key = lax.broadcasted_iota(jnp.int32, (t, t), 0)
            qry = lax.broadcasted_iota(jnp.int32, (t, t), 1)
            keep = key <= qry
        for h in range(N_HEADS):
            sl = slice(h * HEAD_DIM, (h + 1) * HEAD_DIM)
            s = _dot_nt(k_ref[:, sl], q_ref[:, sl]) - c_ref[:, h:h + 1]
            if masked:
                s = jnp.where(keep, s, NEG_INF)
            m_prev = m_sc[h:h + 1, :]
            m_new = jnp.maximum(m_prev, jnp.max(s, axis=0, keepdims=True))
            alpha = jnp.exp(m_prev - m_new)
            p = jnp.exp(s - m_new)
            l_sc[h:h + 1, :] = alpha * l_sc[h:h + 1, :] + jnp.sum(p, axis=0, keepdims=True)
            acc_sc[sl, :] = alpha * acc_sc[sl, :] + _dot_tn(v_ref[:, sl], p.astype(BF16))
            m_sc[h:h + 1, :] = m_new

    @pl.when(j < i)
    def _():
        step(False)

    @pl.when(j == i)
    def _():
        step(True)
        out_t = jnp.concatenate(
            [acc_sc[h * HEAD_DIM:(h + 1) * HEAD_DIM, :] / l_sc[h:h + 1, :] for h in range(N_HEADS)],
            axis=0)
        o_ref[...] = out_t.T.astype(BF16)


def _fox_attention(q, k, v, c, t):
    B, S, _ = q.shape
    n = S // t
    q_spec = pl.BlockSpec((None, t, ATT_WIDTH), lambda b, i, j: (b, i, 0))
    kv_spec = pl.BlockSpec((None, t, ATT_WIDTH), lambda b, i, j: (b, jnp.minimum(j, i), 0))
    return pl.pallas_call(
        functools.partial(_fox_kernel, t=t),
        grid=(B, n, n),
        in_specs=[q_spec, kv_spec, kv_spec,
                  pl.BlockSpec((None, t, N_HEADS), lambda b, i, j: (b, jnp.minimum(j, i), 0))],
        out_specs=q_spec,
        out_shape=jax.ShapeDtypeStruct((B, S, ATT_WIDTH), BF16),
        scratch_shapes=[pltpu.VMEM((N_HEADS, t), F32), pltpu.VMEM((N_HEADS, t), F32),
                        pltpu.VMEM((ATT_WIDTH, t), F32)],
        compiler_params=_cparams(("parallel", "parallel", "arbitrary"), 48),
        name="fox_attention",
    )(q, k, v, c)


def _dil_kernel(*refs, rg, sub, tqu, has_prev, first, last):
    refs = list(refs)
    q_ref, k_ref, v_ref = refs[:3]
    pos = 3
    if has_prev:
        kp_ref, vp_ref = refs[pos:pos + 2]
        pos += 2
    if not first:
        acc_in, st_in = refs[pos:pos + 2]
        pos += 2
    if last:
        o_ref = refs[pos]
    else:
        acc_out, st_out = refs[pos:pos + 2]

    blk = pl.program_id(1)
    nk = tqu + (BAND if has_prev else 0)
    key = lax.broadcasted_iota(jnp.int32, (nk, tqu), 0)
    qry = lax.broadcasted_iota(jnp.int32, (nk, tqu), 1)
    if has_prev:
        keep = jnp.logical_and(key >= qry, key <= qry + BAND)
        keep_edge = jnp.logical_and(keep, jnp.logical_or(key >= BAND, blk > 0))
    else:
        keep = key <= qry
    pair = LANES // HEAD_DIM

    for a in range(sub):
        rows = slice(a * tqu, (a + 1) * tqu)
        for rr in range(rg):
            if not first:
                st_old = st_in[rows, rr * LANES:(rr + 1) * LANES].T
            ms, ls = [], []
            for slab in range(N_HEADS // pair):
                slab_lanes = slice(rr * ATT_WIDTH + slab * LANES, rr * ATT_WIDTH + (slab + 1) * LANES)
                if not first:
                    acc_old = acc_in[rows, slab_lanes].T
                outs = []
                for hh in range(pair):
                    h = slab * pair + hh
                    lo = rr * ATT_WIDTH + h * HEAD_DIM
                    sl = slice(lo, lo + HEAD_DIM)
                    q = q_ref[rows, sl]
                    if not has_prev:
                        k_cat, v_cat, mask = k_ref[rows, sl], v_ref[rows, sl], keep
                    elif a == 0:
                        k_cat = jnp.concatenate([kp_ref[:, sl], k_ref[rows, sl]], axis=0)
                        v_cat = jnp.concatenate([vp_ref[:, sl], v_ref[rows, sl]], axis=0)
                        mask = keep_edge
                    else:
                        krows = slice(a * tqu - BAND, (a + 1) * tqu)
                        k_cat, v_cat, mask = k_ref[krows, sl], v_ref[krows, sl], keep
                    s = jnp.where(mask, _dot_nt(k_cat, q), NEG_INF)
                    m = jnp.max(s, axis=0, keepdims=True)
                    if not first:
                        m_old = st_old[h:h + 1, :]
                        l_old = st_old[N_HEADS + h:N_HEADS + h + 1, :]
                        m_new = jnp.maximum(m, m_old)
                        alpha = jnp.exp(m_old - m_new)
                        m = m_new
                    p = jnp.exp(s - m)
                    l = jnp.sum(p, axis=0, keepdims=True)
                    acc = _dot_tn(v_cat, p.astype(BF16))
                    if not first:
                        l = l + alpha * l_old
                        acc = acc + alpha * acc_old[hh * HEAD_DIM:(hh + 1) * HEAD_DIM, :]
                    if last:
                        acc = acc / l
                    outs.append(acc)
                    ms.append(m)
                    ls.append(l)
                slab_out = jnp.concatenate(outs, axis=0).T
                if last:
                    o_ref[rows, slab_lanes] = slab_out.astype(BF16)
                else:
                    acc_out[rows, slab_lanes] = slab_out
            if not last:
                pad = jnp.zeros((LANES - 2 * N_HEADS, tqu), F32)
                st_out[rows, rr * LANES:(rr + 1) * LANES] = jnp.concatenate(ms + ls + [pad], axis=0).T


def _dilated_branch(q, k, v, state, dilation, last):
    B, S, _ = q.shape
    d = dilation
    L = S // d
    first = state is None
    rg = min(4, d)
    tqu = min(2 * BAND, L)
    tqb = min(L, 4 * tqu // rg)
    sub = tqb // tqu
    n_blk = L // tqb
    n_grp = d // rg
    has_prev = L > BAND
    wq = rg * ATT_WIDTH
    wst = rg * LANES

    def view(a):
        return a.reshape(B, L, d * a.shape[-1])

    main = lambda b, i, g: (b, i, g)
    prev = lambda b, i, g: (b, jnp.maximum(i * (tqb // BAND) - 1, 0), g)
    qkv_spec = pl.BlockSpec((None, tqb, wq), main)
    in_specs = [qkv_spec] * 3
    args = [view(q), view(k), view(v)]
    if has_prev:
        in_specs += [pl.BlockSpec((None, BAND, wq), prev)] * 2
        args += [view(k), view(v)]
    if not first:
        in_specs += [pl.BlockSpec((None, tqb, wq), main), pl.BlockSpec((None, tqb, wst), main)]
        args += [view(state[0]), view(state[1])]
    if last:
        out_specs = pl.BlockSpec((None, tqb, wq), main)
        out_shape = jax.ShapeDtypeStruct((B, L, d * ATT_WIDTH), BF16)
    else:
        out_specs = [pl.BlockSpec((None, tqb, wq), main), pl.BlockSpec((None, tqb, wst), main)]
        out_shape = [jax.ShapeDtypeStruct((B, L, d * ATT_WIDTH), F32),
                     jax.ShapeDtypeStruct((B, L, d * LANES), F32)]
    out = pl.pallas_call(
        functools.partial(_dil_kernel, rg=rg, sub=sub, tqu=tqu, has_prev=has_prev, first=first,
                          last=last),
        grid=(B, n_blk, n_grp),
        in_specs=in_specs,
        out_specs=out_specs,
        out_shape=out_shape,
        compiler_params=_cparams(("parallel", "parallel", "parallel"), 48),
        name=f"dilated_d{d}",
    )(*args)
    if last:
        return out.reshape(B, S, ATT_WIDTH)
    return out[0].reshape(B, S, ATT_WIDTH), out[1].reshape(B, S, LANES)


def _dilated_attention(q, k, v):
    state = None
    order = sorted(DIL_CONFIGS, key=lambda wd: -wd[1])
    for n, (window, d) in enumerate(order):
        assert window // d == BAND
        state = _dilated_branch(q, k, v, state, d, last=(n == len(order) - 1))
    return state


def _outproj_ln_kernel(*refs, n_in):
    a_refs = refs[:n_in]
    w_refs = refs[n_in:2 * n_in]
    x_ref, g_ref, b_ref, o_ref = refs[2 * n_in:]
    y = _dot(a_refs[0][...], w_refs[0][...])
    for a_ref, w_ref in zip(a_refs[1:], w_refs[1:]):
        y = y + _dot(a_ref[...], w_ref[...])
    o_ref[...] = _layer_norm(DEEPNORM_ALPHA * x_ref[...] + y, g_ref[...], b_ref[...])


def _outproj_ln(acts, ws, x, g, b, tm):
    T = x.shape[0]
    n_in = len(acts)
    in_specs = [pl.BlockSpec((tm, a.shape[1]), lambda i: (i, 0)) for a in acts]
    in_specs += [pl.BlockSpec(w.shape, lambda i: (0, 0)) for w in ws]
    row = pl.BlockSpec((tm, D_MODEL), lambda i: (i, 0))
    vec = pl.BlockSpec((1, D_MODEL), lambda i: (0, 0))
    return pl.pallas_call(
        functools.partial(_outproj_ln_kernel, n_in=n_in),
        grid=(T // tm,),
        in_specs=in_specs + [row, vec, vec],
        out_specs=row,
        out_shape=jax.ShapeDtypeStruct((T, D_MODEL), F32),
        compiler_params=_cparams(("parallel",), 48),
        name="outproj_ln",
    )(*acts, *ws, x, g, b)


def _ffn_chunks(width):
    chunks, lo = [], 0
    while lo < width:
        size = min(2 * MXU_DIM, width - lo)
        chunks.append((lo, size))
        lo += size
    return chunks


def _dense_ffn_kernel(x_ref, wg_ref, wu_ref, wd_ref, g_ref, b_ref, o_ref):
    x = x_ref[...]
    xb = x.astype(BF16)
    y = None
    for lo, size in _ffn_chunks(wg_ref.shape[1]):
        gate = _dot(xb, wg_ref[:, lo:lo + size])
        up = _dot(xb, wu_ref[:, lo:lo + size])
        act = (gate * jax.nn.sigmoid(gate) * up).astype(BF16)
        part = _dot(act, wd_ref[lo:lo + size, :])
        y = part if y is None else y + part
    o_ref[...] = _layer_norm(DEEPNORM_ALPHA * x + y, g_ref[...], b_ref[...])


def _dense_ffn(x, wg, wu, wd, g, b, tm):
    T = x.shape[0]
    F = wg.shape[1]
    row = pl.BlockSpec((tm, D_MODEL), lambda i: (i, 0))
    vec = pl.BlockSpec((1, D_MODEL), lambda i: (0, 0))
    once = pl.Buffered(1)
    return pl.pallas_call(
        _dense_ffn_kernel,
        grid=(T // tm,),
        in_specs=[row,
                  pl.BlockSpec((D_MODEL, F), lambda i: (0, 0), pipeline_mode=once),
                  pl.BlockSpec((D_MODEL, F), lambda i: (0, 0), pipeline_mode=once),
                  pl.BlockSpec((F, D_MODEL), lambda i: (0, 0), pipeline_mode=once),
                  vec, vec],
        out_specs=row,
        out_shape=jax.ShapeDtypeStruct((T, D_MODEL), F32),
        compiler_params=_cparams(("parallel",), 56),
        name="dense_ffn",
    )(x, wg, wu, wd, g, b)


CONV_PAD = 8


def _proj1_kernel(x_ref, wqk_ref, wv_ref, wog_ref, wgt_ref, wconv_ref,
                  q_ref, k_ref, v_ref, og_ref, gt_ref, buf):
    i = pl.program_id(1)
    tm = x_ref.shape[0]
    xb = x_ref[...].astype(BF16)
    kscale = MLSTM_HEAD_DIM ** -0.5

    @pl.when(i == 0)
    def _():
        buf[0:CONV_PAD, :] = jnp.zeros((CONV_PAD, buf.shape[1]), F32)

    wide = 2 * MXU_DIM
    for c in range(2 * D_MODEL // wide):
        lanes = slice(c * wide, (c + 1) * wide)
        buf[CONV_PAD:CONV_PAD + tm, lanes] = _dot(xb, wqk_ref[:, lanes])
    for c in range(2 * D_MODEL // LANES):
        lanes = slice(c * LANES, (c + 1) * LANES)
        y = None
        for tap in range(CONV_WIDTH):
            off = CONV_PAD - (CONV_WIDTH - 1) + tap
            term = buf[off:off + tm, lanes] * wconv_ref[tap:tap + 1, lanes]
            y = term if y is None else y + term
        y = y * jax.nn.sigmoid(y)
        buf[0:CONV_PAD, lanes] = buf[tm:tm + CONV_PAD, lanes]
        if c < D_MODEL // LANES:
            q_ref[:, lanes] = y.astype(BF16)
        else:
            k_ref[:, c * LANES - D_MODEL:(c + 1) * LANES - D_MODEL] = (y * kscale).astype(BF16)
    for c in range(D_MODEL // wide):
        lanes = slice(c * wide, (c + 1) * wide)
        v_ref[:, lanes] = _dot(xb, wv_ref[:, lanes]).astype(BF16)
        og_ref[:, lanes] = _dot(xb, wog_ref[:, lanes])
    gt_ref[...] = _dot(xb, wgt_ref[...])


def _proj1(x, wqk, wv, wog, wgt, wconv, tm):
    B, S, _ = x.shape
    row = lambda b, i: (b, i, 0)
    const = lambda b, i: (0, 0)
    act_spec = pl.BlockSpec((None, tm, D_MODEL), row)
    act = jax.ShapeDtypeStruct((B, S, D_MODEL), BF16)
    return pl.pallas_call(
        _proj1_kernel,
        grid=(B, S // tm),
        in_specs=[act_spec,
                  pl.BlockSpec((D_MODEL, 2 * D_MODEL), const),
                  pl.BlockSpec((D_MODEL, D_MODEL), const),
                  pl.BlockSpec((D_MODEL, D_MODEL), const),
                  pl.BlockSpec((D_MODEL, 2 * N_HEADS), const),
                  pl.BlockSpec((CONV_WIDTH, 2 * D_MODEL), const)],
        out_specs=[act_spec, act_spec, act_spec, act_spec,
                   pl.BlockSpec((None, tm, 2 * N_HEADS), row)],
        out_shape=[act, act, act, jax.ShapeDtypeStruct((B, S, D_MODEL), F32),
                   jax.ShapeDtypeStruct((B, S, 2 * N_HEADS), F32)],
        scratch_shapes=[pltpu.VMEM((tm + CONV_PAD, 2 * D_MODEL), F32)],
        compiler_params=_cparams(("parallel", "arbitrary"), 56),
        name="proj1",
    )(x, wqk, wv, wog, wgt, wconv)


def _mlstm_kernel(q_ref, k_ref, v_ref, og_ref, gn_ref, gt_ref, bi_row, bf_row, bi_col, bf_col,
                  ng_ref, o_ref, c_sc, n_sc, m_sc, *, L):
    ci = pl.program_id(1)

    @pl.when(ci == 0)
    def _():
        c_sc[...] = jnp.zeros(c_sc.shape, F32)
        n_sc[...] = jnp.zeros(n_sc.shape, F32)
        m_sc[...] = jnp.full(m_sc.shape, NEG_INF, F32)

    row = lax.broadcasted_iota(jnp.int32, (L, L), 0)
    col = lax.broadcasted_iota(jnp.int32, (L, L), 1)
    causal = col <= row
    lower = jnp.where(causal, 1.0, 0.0).astype(BF16)
    upper = jnp.where(row <= col, 1.0, 0.0).astype(BF16)

    gn = gn_ref[...]
    gt = gt_ref[...]
    i_col = gn[:, :N_HEADS] + bi_row[...]
    lf_col = _log_sigmoid(gn[:, N_HEADS:] + bf_row[...])
    i_row = gt[:N_HEADS, :] + bi_col[...]
    lf_row = _log_sigmoid(gt[N_HEADS:, :] + bf_col[...])
    b_col = sum(_dot(lower, part) for part in _split3(lf_col))
    b_row = sum(_dot(part, upper) for part in _split3(lf_row))

    keep = row <= col
    n_pad = jnp.zeros((2 * N_HEADS - 3, MLSTM_HEAD_DIM), F32)
    for h in range(N_HEADS):
        lanes = slice(h * MLSTM_HEAD_DIM, (h + 1) * MLSTM_HEAD_DIM)
        q = q_ref[:, lanes]
        k = k_ref[:, lanes]
        v = v_ref[:, lanes]
        bt = b_row[h:h + 1, :]
        key_term = i_col[:, h:h + 1] - b_col[:, h:h + 1]
        m_prev = m_sc[h]
        ct_prev = c_sc[h]
        n_prev = n_sc[h]

        dlog = jnp.where(keep, bt + key_term, NEG_INF)
        inter = bt + m_prev
        m_t = jnp.maximum(inter, jnp.max(dlog, axis=0, keepdims=True))
        s = _dot_nt(k, q) * jnp.exp(dlog - m_t)
        inter_w = jnp.exp(inter - m_t)
        num = _dot_tn(v, s.astype(BF16)) + inter_w * _dot_nt(ct_prev.astype(BF16), q)
        n_parts = jnp.concatenate([p.astype(F32) for p in _split3(n_prev)] + [n_pad], axis=0)
        qn = jnp.sum(_dot_nt(n_parts.astype(BF16), q), axis=0, keepdims=True)
        den = jnp.sum(s, axis=0, keepdims=True) + inter_w * qn
        hh = num / jnp.maximum(jnp.abs(den), jnp.exp(-m_t))

        b_last = bt[:, L - 1:L]
        g = b_last + key_term
        m_new = jnp.maximum(b_last + m_prev, jnp.max(g, axis=0, keepdims=True))
        w = jnp.exp(g - m_new)
        decay = jnp.exp(b_last + m_prev - m_new)
        kw = k.astype(F32) * w
        c_sc[h] = decay * ct_prev + _dot_tn(v, kw.astype(BF16))
        n_sc[h] = decay * n_prev + jnp.sum(kw, axis=0, keepdims=True)
        m_sc[h] = m_new

        mu = jnp.mean(hh, axis=0, keepdims=True)
        hc = hh - mu
        var = jnp.mean(hc * hc, axis=0, keepdims=True)
        hn = (hc * lax.rsqrt(var + LN_EPS)).T * ng_ref[:, lanes]
        o_ref[:, lanes] = (hn * jax.nn.sigmoid(og_ref[:, lanes])).astype(BF16)


def _mlstm(q, k, v, og, gates, gates_t, b_i, b_f, norm_g, L):
    B, S, _ = q.shape
    row = lambda b, c: (b, c, 0)
    const = lambda b, c: (0, 0)
    act_spec = pl.BlockSpec((None, L, D_MODEL), row)
    return pl.pallas_call(
        functools.partial(_mlstm_kernel, L=L),
        grid=(B, S // L),
        in_specs=[act_spec, act_spec, act_spec, act_spec,
                  pl.BlockSpec((None, L, 2 * N_HEADS), row),
                  pl.BlockSpec((None, 2 * N_HEADS, L), lambda b, c: (b, 0, c)),
                  pl.BlockSpec((1, N_HEADS), const), pl.BlockSpec((1, N_HEADS), const),
                  pl.BlockSpec((N_HEADS, 1), const), pl.BlockSpec((N_HEADS, 1), const),
                  pl.BlockSpec((1, D_MODEL), const)],
        out_specs=act_spec,
        out_shape=jax.ShapeDtypeStruct((B, S, D_MODEL), BF16),
        scratch_shapes=[pltpu.VMEM((N_HEADS, MLSTM_HEAD_DIM, MLSTM_HEAD_DIM), F32),
                        pltpu.VMEM((N_HEADS, 1, MLSTM_HEAD_DIM), F32),
                        pltpu.VMEM((N_HEADS, 1, 1), F32)],
        compiler_params=_cparams(("parallel", "arbitrary"), 48),
        name="mlstm",
    )(q, k, v, og, gates, gates_t, b_i[None, :], b_f[None, :], b_i[:, None], b_f[:, None],
      norm_g[None, :])


MOE_TOP_K = 2
MOE_TILE = 512
RANK_BLOCK = 1024


def _router_kernel(x_ref, wt_ref, sel_ref, gate_ref):
    logits = lax.dot_general(wt_ref[...], x_ref[...], (((1,), (1,)), ((), ())),
                             preferred_element_type=F32, precision=lax.Precision.HIGHEST)
    idx = lax.broadcasted_iota(jnp.int32, logits.shape, 0)
    m1 = jnp.max(logits, axis=0, keepdims=True)
    i1 = jnp.min(jnp.where(logits == m1, idx, N_EXPERTS), axis=0, keepdims=True)
    pick1 = idx == i1
    rest = jnp.where(pick1, NEG_INF, logits)
    m2 = jnp.max(rest, axis=0, keepdims=True)
    i2 = jnp.min(jnp.where(rest == m2, idx, N_EXPERTS), axis=0, keepdims=True)
    pick2 = idx == i2
    e2 = jnp.exp(m2 - m1)
    w1 = 1.0 / (1.0 + e2)
    w2 = e2 / (1.0 + e2)
    sel_ref[...] = jnp.where(jnp.logical_or(pick1, pick2), 1.0, 0.0)
    gate_ref[...] = jnp.where(pick1, w1, 0.0) + jnp.where(pick2, w2, 0.0)


def _router(x, w_router_t, tm):
    T = x.shape[0]
    out = jax.ShapeDtypeStruct((N_EXPERTS, T), F32)
    out_spec = pl.BlockSpec((N_EXPERTS, tm), lambda i: (0, i))
    return pl.pallas_call(
        _router_kernel,
        grid=(T // tm,),
        in_specs=[pl.BlockSpec((tm, D_MODEL), lambda i: (i, 0)),
                  pl.BlockSpec((N_EXPERTS, D_MODEL), lambda i: (0, 0))],
        out_specs=[out_spec, out_spec],
        out_shape=[out, out],
        compiler_params=_cparams(("parallel",), 32),
        name="moe_router",
    )(x, w_router_t)


def _rank_kernel(sel_ref, rank_ref, count_ref, upper_sc, carry_sc):
    n = sel_ref.shape[1]

    @pl.when(pl.program_id(0) == 0)
    def _():
        row = lax.broadcasted_iota(jnp.int32, (n, n), 0)
        col = lax.broadcasted_iota(jnp.int32, (n, n), 1)
        upper_sc[...] = jnp.where(row <= col, 1.0, 0.0).astype(BF16)
        carry_sc[...] = jnp.zeros(carry_sc.shape, F32)

    sel = sel_ref[...]
    incl = _dot(sel.astype(BF16), upper_sc[...])
    rank_ref[...] = carry_sc[...] + incl - sel
    carry_sc[...] = carry_sc[...] + incl[:, n - 1:n]
    count_ref[...] = carry_sc[...]


def _rank(sel_t):
    E, T = sel_t.shape
    n = min(RANK_BLOCK, T)
    return pl.pallas_call(
        _rank_kernel,
        grid=(T // n,),
        in_specs=[pl.BlockSpec((E, n), lambda i: (0, i))],
        out_specs=[pl.BlockSpec((E, n), lambda i: (0, i)), pl.BlockSpec((E, 1), lambda i: (0, 0))],
        out_shape=[jax.ShapeDtypeStruct((E, T), F32), jax.ShapeDtypeStruct((E, 1), F32)],
        scratch_shapes=[pltpu.VMEM((n, n), BF16), pltpu.VMEM((E, 1), F32)],
        compiler_params=_cparams(("arbitrary",), 32),
        name="moe_rank",
    )(sel_t)


def _slot_kernel(sel_ref, gate_ref, rank_ref, off_ref, pos_ref, w_ref):
    sel = sel_ref[...] > 0.0
    idx = lax.broadcasted_iota(jnp.int32, sel.shape, 0)
    first = jnp.min(jnp.where(sel, idx, N_EXPERTS), axis=0, keepdims=True)
    second = jnp.max(jnp.where(sel, idx, -1), axis=0, keepdims=True)
    slot = off_ref[...] + rank_ref[...]
    gate = gate_ref[...]
    rows_p, rows_w = [], []
    for which in (first, second):
        hit = idx == which
        rows_p.append(jnp.sum(jnp.where(hit, slot, 0.0), axis=0, keepdims=True))
        rows_w.append(jnp.sum(jnp.where(hit, gate, 0.0), axis=0, keepdims=True))
    pos_ref[...] = jnp.concatenate(rows_p, axis=0).astype(jnp.int32)
    w_ref[...] = jnp.concatenate(rows_w, axis=0)


def _slots(sel_t, gate_t, rank_t, offsets_col, tm):
    E, T = sel_t.shape
    spec = pl.BlockSpec((E, tm), lambda i: (0, i))
    out_spec = pl.BlockSpec((MOE_TOP_K, tm), lambda i: (0, i))
    return pl.pallas_call(
        _slot_kernel,
        grid=(T // tm,),
        in_specs=[spec, spec, spec, pl.BlockSpec((E, 1), lambda i: (0, 0))],
        out_specs=[out_spec, out_spec],
        out_shape=[jax.ShapeDtypeStruct((MOE_TOP_K, T), jnp.int32),
                   jax.ShapeDtypeStruct((MOE_TOP_K, T), F32)],
        compiler_params=_cparams(("parallel",), 32),
        name="moe_slots",
    )(sel_t, gate_t, rank_t, offsets_col)


def _dispatch_kernel(tail_ref, used_ref, pos_ref, x_ref, xs_hbm, zero_sc, sem, zsem, *, tm):
    i = pl.program_id(0)
    n_tiles = xs_hbm.shape[0] // MOE_TILE

    def fill(tile):
        return pltpu.make_async_copy(zero_sc, xs_hbm.at[pl.ds(tile * MOE_TILE, MOE_TILE)], zsem)

    @pl.when(i == 0)
    def _():
        zero_sc[...] = jnp.zeros(zero_sc.shape, F32)
        for e in range(N_EXPERTS):
            fill(tail_ref[e]).start()
        for e in range(N_EXPERTS):
            fill(tail_ref[e]).wait()
        for j in range(N_EXPERTS):
            @pl.when(n_tiles - 1 - j >= used_ref[0])
            def _():
                c = fill(n_tiles - 1 - j)
                c.start()
                c.wait()

    def issue(t, carry):
        for k in range(MOE_TOP_K):
            pltpu.make_async_copy(x_ref.at[pl.ds(t, 1)], xs_hbm.at[pl.ds(pos_ref[k, t], 1)],
                                  sem).start(priority=k)
        return carry

    lax.fori_loop(0, tm, issue, 0, unroll=8)
    for _ in range(MOE_TOP_K):
        pltpu.make_async_copy(x_ref, xs_hbm.at[pl.ds(0, tm)], sem).wait()


def _dispatch(x, pos_t, tail_tiles, n_used, n_rows, tm):
    T = x.shape[0]
    n_steps = T // tm
    pos3 = pos_t.reshape(MOE_TOP_K, n_steps, tm).transpose(1, 0, 2)
    return pl.pallas_call(
        functools.partial(_dispatch_kernel, tm=tm),
        grid_spec=pltpu.PrefetchScalarGridSpec(
            num_scalar_prefetch=2,
            grid=(n_steps,),
            in_specs=[pl.BlockSpec((None, MOE_TOP_K, tm), lambda i, tail, used: (i, 0, 0),
                                   memory_space=pltpu.SMEM),
                      pl.BlockSpec((tm, D_MODEL), lambda i, tail, used: (i, 0))],
            out_specs=pl.BlockSpec(memory_space=pl.ANY),
            scratch_shapes=[pltpu.VMEM((MOE_TILE, D_MODEL), F32),
                            pltpu.SemaphoreType.DMA, pltpu.SemaphoreType.DMA],
        ),
        out_shape=jax.ShapeDtypeStruct((n_rows, D_MODEL), F32),
        compiler_params=_cparams(("arbitrary",), 32),
        name="moe_dispatch",
    )(tail_tiles, n_used, pos3, x)


def _expert_ffn_kernel(te_ref, used_ref, x_ref, wg_ref, wu_ref, wd_ref, o_ref, xb_sc):
    i = pl.program_id(0)
    f = pl.program_id(1)
    live = i < used_ref[0]

    @pl.when(jnp.logical_and(live, f == 0))
    def _():
        xb_sc[...] = x_ref[...].astype(BF16)

    @pl.when(live)
    def _():
        xb = xb_sc[...]
        part = None
        for lo, size in _ffn_chunks(wg_ref.shape[1]):
            gate = _dot(xb, wg_ref[:, lo:lo + size])
            up = _dot(xb, wu_ref[:, lo:lo + size])
            act = (gate * jax.nn.sigmoid(gate) * up).astype(BF16)
            p = _dot(act, wd_ref[lo:lo + size, :])
            part = p if part is None else part + p

        @pl.when(f == 0)
        def _():
            o_ref[...] = part

        @pl.when(f > 0)
        def _():
            o_ref[...] += part

    @pl.when(jnp.logical_and(jnp.logical_not(live), f == 0))
    def _():
        o_ref[...] = jnp.zeros(o_ref.shape, F32)


def _expert_ffn(xs, tile_expert, n_used, wg, wu, wd, fc):
    n_rows = xs.shape[0]
    E, _, F = wg.shape
    tm = MOE_TILE

    def row(i, f, te, used):
        return (jnp.minimum(i, used[0] - 1), 0)

    return pl.pallas_call(
        _expert_ffn_kernel,
        grid_spec=pltpu.PrefetchScalarGridSpec(
            num_scalar_prefetch=2,
            grid=(n_rows // tm, F // fc),
            in_specs=[pl.BlockSpec((tm, D_MODEL), row),
                      pl.BlockSpec((None, D_MODEL, fc), lambda i, f, te, used: (te[i], 0, f)),
                      pl.BlockSpec((None, D_MODEL, fc), lambda i, f, te, used: (te[i], 0, f)),
                      pl.BlockSpec((None, fc, D_MODEL), lambda i, f, te, used: (te[i], f, 0))],
            out_specs=pl.BlockSpec((tm, D_MODEL), lambda i, f, te, used: (i, 0)),
            scratch_shapes=[pltpu.VMEM((tm, D_MODEL), BF16)],
        ),
        out_shape=jax.ShapeDtypeStruct((n_rows, D_MODEL), F32),
        compiler_params=_cparams(("arbitrary", "arbitrary"), 56),
        name="moe_expert_ffn",
    )(tile_expert, n_used, xs, wg, wu, wd)


def _combine_kernel(pos_ref, w_ref, x_ref, g_ref, b_ref, ys_hbm, o_ref, rows_sc, sem, *, tm):
    def issue(t, carry):
        for k in range(MOE_TOP_K):
            pltpu.make_async_copy(ys_hbm.at[pl.ds(pos_ref[k, t], 1)],
                                  rows_sc.at[k, pl.ds(t, 1)], sem).start(priority=k)
        return carry

    lax.fori_loop(0, tm, issue, 0, unroll=8)
    for k in range(MOE_TOP_K):
        pltpu.make_async_copy(ys_hbm.at[pl.ds(0, tm)], rows_sc.at[k], sem).wait()
    w = w_ref[...]
    y = w[:, 0:1] * rows_sc[0] + w[:, 1:2] * rows_sc[1]
    o_ref[...] = _layer_norm(DEEPNORM_ALPHA * x_ref[...] + y, g_ref[...], b_ref[...])


def _combine_ln(ys, pos_t, w_nat, x, g, b, tm):
    T = x.shape[0]
    n_steps = T // tm
    pos3 = pos_t.reshape(MOE_TOP_K, n_steps, tm).transpose(1, 0, 2)
    row = pl.BlockSpec((tm, D_MODEL), lambda i: (i, 0))
    vec = pl.BlockSpec((1, D_MODEL), lambda i: (0, 0))
    return pl.pallas_call(
        functools.partial(_combine_kernel, tm=tm),
        grid=(n_steps,),
        in_specs=[pl.BlockSpec((None, MOE_TOP_K, tm), lambda i: (i, 0, 0), memory_space=pltpu.SMEM),
                  pl.BlockSpec((tm, MOE_TOP_K), lambda i: (i, 0)),
                  row, vec, vec,
                  pl.BlockSpec(memory_space=pl.ANY)],
        out_specs=row,
        out_shape=jax.ShapeDtypeStruct((T, D_MODEL), F32),
        scratch_shapes=[pltpu.VMEM((MOE_TOP_K, tm, D_MODEL), F32), pltpu.SemaphoreType.DMA],
        compiler_params=_cparams(("arbitrary",), 32),
        name="moe_combine_ln",
    )(pos3, w_nat, x, g, b, ys)


def _moe_sparse(x, w_router, wg, wu, wd, g, b):
    T = x.shape[0]
    tile = MOE_TILE
    n_rows = MOE_TOP_K * T + N_EXPERTS * tile
    n_tiles = n_rows // tile
    sel_t, gate_t = _router(x, w_router.T, tm=1024)
    rank_t, counts = _rank(sel_t)
    tiles_per = jnp.ceil(counts[:, 0] / tile).astype(jnp.int32)
    tile_end = jnp.cumsum(tiles_per)
    tile_start = tile_end - tiles_per
    n_used = tile_end[-1:]
    tile_expert = jnp.minimum(
        jnp.sum(jnp.arange(n_tiles, dtype=jnp.int32)[:, None] >= tile_end[None, :], axis=1),
        N_EXPERTS - 1).astype(jnp.int32)
    tail_tiles = jnp.maximum(tile_end - 1, 0).astype(jnp.int32)
    offsets_col = (tile_start * tile).astype(F32)[:, None]
    pos_t, w_t = _slots(sel_t, gate_t, rank_t, offsets_col, tm=1024)
    xs = _dispatch(x, pos_t, tail_tiles, n_used, n_rows, tm=512)
    ys = _expert_ffn(xs, tile_expert, n_used, wg, wu, wd, fc=D_FF_EXPERT // 2)
    return _combine_ln(ys, pos_t, w_t.T, x, g, b, tm=256)


def _rotary_tables(S):
    pos = jnp.arange(S, dtype=jnp.int32)
    inv = ROPE_THETA ** (-jnp.arange(ROPE_HALF, dtype=F32) / ROPE_HALF)
    ang = pos.astype(F32)[:, None] * inv[None, :]
    cos, sin = jnp.cos(ang), jnp.sin(ang)
    ones = jnp.ones((S, HEAD_DIM - ROPE_DIMS), F32)
    zeros = jnp.zeros((S, HEAD_DIM - ROPE_DIMS), F32)
    z8 = jnp.zeros((S, ROPE_HALF), F32)
    cos_h = jnp.concatenate([cos, cos, ones], axis=1)
    sup_h = jnp.concatenate([-sin, z8, zeros], axis=1)
    sdn_h = jnp.concatenate([z8, sin, zeros], axis=1)
    rep = LANES // HEAD_DIM
    return (jnp.tile(cos_h, (1, rep)), jnp.tile(sup_h, (1, rep)), jnp.tile(sdn_h, (1, rep)))


def _even_layer(h, w_in, b_forget, w_out, ln_mix_g, ln_mix_b, w_gate, w_up, w_down,
                ln_ffn_g, ln_ffn_b):
    B, S, _ = h.shape
    T = B * S
    W = ATT_WIDTH
    w_main = jnp.concatenate([w_in[:, :3 * W], w_in[:, 3 * W + N_HEADS:]], axis=1).astype(BF16)
    wf_t = w_in[:, 3 * W:3 * W + N_HEADS].T.astype(BF16)
    cos_t, sup_t, sdn_t = _rotary_tables(S)
    qa, ka, va, qd, kd, vd, f_t = _proj0(h, w_main, wf_t, cos_t, sup_t, sdn_t, tm=min(512, S))

    bias_col = jnp.tile(b_forget, B)[:, None]
    c = _fox_cumsum(f_t.reshape(B * N_HEADS, S), bias_col).reshape(B, N_HEADS, S)
    o_fox = _fox_attention(qa, ka, va, jnp.swapaxes(c, 1, 2), t=min(512, S))
    o_dil = _dilated_attention(qd, kd, vd)

    w_out_b = w_out.astype(BF16)
    tm = 512
    h1 = _outproj_ln([o_fox.reshape(T, W), o_dil.reshape(T, W)], [w_out_b[:W], w_out_b[W:]],
                     h.reshape(T, D_MODEL), ln_mix_g[None, :], ln_mix_b[None, :], tm)
    h2 = _dense_ffn(h1, w_gate.astype(BF16), w_up.astype(BF16), w_down.astype(BF16),
                    ln_ffn_g[None, :], ln_ffn_b[None, :], tm)
    return h2.reshape(B, S, D_MODEL)


def _odd_layer(h, w_in, b_igate, b_fgate, w_conv, norm_g, w_out, ln_mix_g, ln_mix_b, w_router,
               w_gate, w_up, w_down, ln_ffn_g, ln_ffn_b):
    B, S, _ = h.shape
    T = B * S
    D = D_MODEL
    wqk = w_in[:, :2 * D].astype(BF16)
    wv = w_in[:, 2 * D:3 * D].astype(BF16)
    wgt = w_in[:, 3 * D:3 * D + 2 * N_HEADS].astype(BF16)
    wog = w_in[:, 3 * D + 2 * N_HEADS:].astype(BF16)
    q, k, v, og, gates = _proj1(h, wqk, wv, wog, wgt, w_conv, tm=min(512, S))
    gates_t = jnp.swapaxes(gates, 1, 2)
    hm = _mlstm(q, k, v, og, gates, gates_t, b_igate, b_fgate, norm_g, L=min(256, S))

    tm = 512
    h1 = _outproj_ln([hm.reshape(T, D)], [w_out.astype(BF16)], h.reshape(T, D),
                     ln_mix_g[None, :], ln_mix_b[None, :], tm)
    h2 = _moe_sparse(h1, w_router, w_gate.astype(BF16), w_up.astype(BF16), w_down.astype(BF16),
                     ln_ffn_g[None, :], ln_ffn_b[None, :])
    return h2.reshape(B, S, D)


def kernel(x, w_in_e, b_forget_e, w_out_e, ln_mix_g_e, ln_mix_b_e, ffn_w_gate_e, ffn_w_up_e,
           ffn_w_down_e, ln_ffn_g_e, ln_ffn_b_e, w_in_o, b_igate_o, b_fgate_o, w_conv_o,
           mlstm_norm_g_o, w_out_o, ln_mix_g_o, ln_mix_b_o, w_router_o, moe_w_gate_o,
           moe_w_up_o, moe_w_down_o, ln_ffn_g_o, ln_ffn_b_o):
    h = x
    for layer in range(DEPTH):
        i = layer // 2
        if layer % 2 == 0:
            h = _even_layer(h, w_in_e[i], b_forget_e[i], w_out_e[i], ln_mix_g_e[i], ln_mix_b_e[i],
                            ffn_w_gate_e[i], ffn_w_up_e[i], ffn_w_down_e[i], ln_ffn_g_e[i],
                            ln_ffn_b_e[i])
        else:
            h = _odd_layer(h, w_in_o[i], b_igate_o[i], b_fgate_o[i], w_conv_o[i],
                           mlstm_norm_g_o[i], w_out_o[i], ln_mix_g_o[i], ln_mix_b_o[i],
                           w_router_o[i], moe_w_gate_o[i], moe_w_up_o[i], moe_w_down_o[i],
                           ln_ffn_g_o[i], ln_ffn_b_o[i])
    return h
```

```python
import functools
import math

import jax
import jax.numpy as jnp
from jax import lax
from jax.experimental import pallas as pl
from jax.experimental.pallas import tpu as pltpu

F32 = jnp.float32
BF16 = jnp.bfloat16

D_MODEL = 1024
HEAD_DIM = 64
N_HEADS = 8
ATT_WIDTH = N_HEADS * HEAD_DIM
DIL_CONFIGS = ((128, 1), (512, 4), (2048, 16))
ROPE_THETA = 500000.0
ROPE_DIMS = HEAD_DIM // 4
ROPE_HALF = ROPE_DIMS // 2
MLSTM_HEAD_DIM = D_MODEL // N_HEADS
CONV_WIDTH = 4
D_FF_DENSE = 2816
N_EXPERTS = 8
D_FF_EXPERT = 3584
DEPTH = 2
DEEPNORM_ALPHA = (2 * DEPTH) ** 0.25
LN_EPS = 1e-5

LANES = 128
MXU_DIM = 256
BAND = 128
MIB = 1024 * 1024

NEG_INF = float("-inf")


def _cparams(semantics, vmem_mib):
    return pltpu.CompilerParams(dimension_semantics=semantics, vmem_limit_bytes=vmem_mib * MIB)


def _dot(a, b):
    return jnp.dot(a, b, preferred_element_type=F32)


def _dot_nt(a, b):
    return lax.dot_general(a, b, (((1,), (1,)), ((), ())), preferred_element_type=F32)


def _dot_tn(a, b):
    return lax.dot_general(a, b, (((0,), (0,)), ((), ())), preferred_element_type=F32)


def _split3(x):
    hi = x.astype(BF16)
    r = x - hi.astype(F32)
    mid = r.astype(BF16)
    lo = (r - mid.astype(F32)).astype(BF16)
    return hi, mid, lo


def _log_sigmoid(z):
    return -(jnp.maximum(-z, 0.0) + jnp.log1p(jnp.exp(-jnp.abs(z))))


def _layer_norm(z, g, b):
    mu = jnp.mean(z, axis=-1, keepdims=True)
    zc = z - mu
    var = jnp.mean(zc * zc, axis=-1, keepdims=True)
    return zc * lax.rsqrt(var + LN_EPS) * g + b


DILATIONS = tuple(sorted(d for _, d in DIL_CONFIGS))
N_PROJ0 = 6


def _proj0_kernel(x_ref, w_ref, wf_ref, cos_ref, sup_ref, sdn_ref, *refs):
    tm = x_ref.shape[0]
    outs = refs[:N_PROJ0]
    strided = refs[N_PROJ0:-2]
    ft_ref, row_sc = refs[-2:]
    xb = x_ref[...].astype(BF16)
    scale = HEAD_DIM ** -0.5
    for j, o_ref in enumerate(outs):
        full = _dot(xb, w_ref[:, j * ATT_WIDTH:(j + 1) * ATT_WIDTH])
        for c in range(ATT_WIDTH // LANES):
            r = full[:, c * LANES:(c + 1) * LANES]
            if j in (3, 4):
                r = (r * cos_ref[...]
                     + pltpu.roll(r, LANES - ROPE_HALF, axis=1) * sup_ref[...]
                     + pltpu.roll(r, ROPE_HALF, axis=1) * sdn_ref[...])
            if j in (0, 3):
                r = r * scale
            o_ref[:, c * LANES:(c + 1) * LANES] = r.astype(BF16)
            if j >= 3:
                row_sc[...] = r
                for n, d in enumerate(DILATIONS[1:]):
                    s_ref = strided[3 * n + (j - 3)]
                    for res in range(d):
                        lo = res * ATT_WIDTH + c * LANES
                        s_ref[:, lo:lo + LANES] = row_sc[pl.ds(res, tm // d, stride=d), :].astype(BF16)
    ft_ref[...] = _dot_nt(wf_ref[...], xb)


def _proj0(x, w_main, wf_t, cos_t, sup_t, sdn_t, tm):
    B, S, _ = x.shape
    n_i = S // tm
    act = jax.ShapeDtypeStruct((B, S, ATT_WIDTH), BF16)
    act_spec = pl.BlockSpec((None, tm, ATT_WIDTH), lambda b, i: (b, i, 0))
    tab_spec = pl.BlockSpec((tm, LANES), lambda b, i: (i, 0))
    out_specs = [act_spec] * N_PROJ0
    out_shape = [act] * N_PROJ0
    for d in DILATIONS[1:]:
        out_specs += [pl.BlockSpec((None, tm // d, d * ATT_WIDTH), lambda b, i: (b, i, 0))] * 3
        out_shape += [jax.ShapeDtypeStruct((B, S // d, d * ATT_WIDTH), BF16)] * 3
    out_specs.append(pl.BlockSpec((None, N_HEADS, tm), lambda b, i: (b, 0, i)))
    out_shape.append(jax.ShapeDtypeStruct((B, N_HEADS, S), F32))
    return pl.pallas_call(
        _proj0_kernel,
        grid=(B, n_i),
        in_specs=[
            pl.BlockSpec((None, tm, D_MODEL), lambda b, i: (b, i, 0)),
            pl.BlockSpec((D_MODEL, N_PROJ0 * ATT_WIDTH), lambda b, i: (0, 0)),
            pl.BlockSpec((N_HEADS, D_MODEL), lambda b, i: (0, 0)),
            tab_spec, tab_spec, tab_spec,
        ],
        out_specs=out_specs,
        out_shape=out_shape,
        scratch_shapes=[pltpu.VMEM((tm, LANES), F32)],
        compiler_params=_cparams(("parallel", "parallel"), 48),
        name="proj0",
    )(x, w_main, wf_t, cos_t, sup_t, sdn_t)


def _fox_cumsum_kernel(f_ref, bias_ref, c_ref):
    S = f_ref.shape[1]
    lf = _log_sigmoid(f_ref[...] + bias_ref[...])
    row = lax.broadcasted_iota(jnp.int32, (S, S), 0)
    col = lax.broadcasted_iota(jnp.int32, (S, S), 1)
    upper = jnp.where(row <= col, 1.0, 0.0).astype(BF16)
    hi, mid, lo = _split3(lf)
    c_ref[...] = _dot(hi, upper) + _dot(mid, upper) + _dot(lo, upper)


def _fox_cumsum(f_t, bias_col):
    R, S = f_t.shape
    return pl.pallas_call(
        _fox_cumsum_kernel,
        grid=(1,),
        in_specs=[pl.BlockSpec((R, S), lambda i: (0, 0)), pl.BlockSpec((R, 1), lambda i: (0, 0))],
        out_specs=pl.BlockSpec((R, S), lambda i: (0, 0)),
        out_shape=jax.ShapeDtypeStruct((R, S), F32),
        compiler_params=_cparams(("arbitrary",), 48),
        name="fox_cumsum",
    )(f_t, bias_col)


def _fox_kernel(q_ref, k_ref, v_ref, c_ref, o_ref, m_sc, l_sc, acc_sc, *, t):
    i = pl.program_id(1)
    j = pl.program_id(2)

    @pl.when(j == 0)
    def _():
        m_sc[...] = jnp.full(m_sc.shape, NEG_INF, F32)
        l_sc[...] = jnp.zeros(l_sc.shape, F32)
        acc_sc[...] = jnp.zeros(acc_sc.shape, F32)

    def step(masked):
        if masked:
            key = lax.broadcasted_iota(jnp.int32, (t, t), 0)
            qry = lax.broadcasted_iota(jnp.int32, (t, t), 1)
            keep = key <= qry
        for h in range(N_HEADS):
            sl = slice(h * HEAD_DIM, (h + 1) * HEAD_DIM)
            s = _dot_nt(k_ref[:, sl], q_ref[:, sl]) - c_ref[:, h:h + 1]
            if masked:
                s = jnp.where(keep, s, NEG_INF)
            m_prev = m_sc[h:h + 1, :]
            m_new = jnp.maximum(m_prev, jnp.max(s, axis=0, keepdims=True))
            alpha = jnp.exp(m_prev - m_new)
            p = jnp.exp(s - m_new)
            l_sc[h:h + 1, :] = alpha * l_sc[h:h + 1, :] + jnp.sum(p, axis=0, keepdims=True)
            acc_sc[sl, :] = alpha * acc_sc[sl, :] + _dot_tn(v_ref[:, sl], p.astype(BF16))
            m_sc[h:h + 1, :] = m_new

    @pl.when(j < i)
    def _():
        step(False)

    @pl.when(j == i)
    def _():
        step(True)
        out_t = jnp.concatenate(
            [acc_sc[h * HEAD_DIM:(h + 1) * HEAD_DIM, :] / l_sc[h:h + 1, :] for h in range(N_HEADS)],
            axis=0)
        o_ref[...] = out_t.T.astype(BF16)


def _fox_attention(q, k, v, c, t):
    B, S, _ = q.shape
    n = S // t
    q_spec = pl.BlockSpec((None, t, ATT_WIDTH), lambda b, i, j: (b, i, 0))
    kv_spec = pl.BlockSpec((None, t, ATT_WIDTH), lambda b, i, j: (b, jnp.minimum(j, i), 0))
    return pl.pallas_call(
        functools.partial(_fox_kernel, t=t),
        grid=(B, n, n),
        in_specs=[q_spec, kv_spec, kv_spec,
                  pl.BlockSpec((None, t, N_HEADS), lambda b, i, j: (b, jnp.minimum(j, i), 0))],
        out_specs=q_spec,
        out_shape=jax.ShapeDtypeStruct((B, S, ATT_WIDTH), BF16),
        scratch_shapes=[pltpu.VMEM((N_HEADS, t), F32), pltpu.VMEM((N_HEADS, t), F32),
                        pltpu.VMEM((ATT_WIDTH, t), F32)],
        compiler_params=_cparams(("parallel", "parallel", "arbitrary"), 48),
        name="fox_attention",
    )(q, k, v, c)


def _dil_kernel(*refs, d, rg, sub, tqu, has_prev, first, last):
    refs = list(refs)
    q_ref, k_ref, v_ref = refs[:3]
    pos = 3
    if has_prev:
        kp_ref, vp_ref = refs[pos:pos + 2]
        pos += 2
    if not first:
        acc_in, st_in = refs[pos:pos + 2]
        pos += 2
    if last:
        o_ref = refs[pos]
    else:
        acc_out, st_out = refs[pos:pos + 2]

    blk = pl.program_id(1)
    nk = tqu + (BAND if has_prev else 0)
    key = lax.broadcasted_iota(jnp.int32, (nk, tqu), 0)
    qry = lax.broadcasted_iota(jnp.int32, (nk, tqu), 1)
    if has_prev:
        keep = jnp.logical_and(key >= qry, key <= qry + BAND)
        keep_edge = jnp.logical_and(keep, jnp.logical_or(key >= BAND, blk > 0))
    else:
        keep = key <= qry
    pair = LANES // HEAD_DIM

    for a in range(sub):
        rows = slice(a * tqu, (a + 1) * tqu)
        for rr in range(rg):
            if d == 1:
                nat = rows
            else:
                nat = pl.ds(a * tqu * d + pl.program_id(2) * rg + rr, tqu, stride=d)
            if not first:
                st_old = st_in[nat, :].T
            ms, ls = [], []
            for slab in range(N_HEADS // pair):
                slab_lanes = slice(rr * ATT_WIDTH + slab * LANES, rr * ATT_WIDTH + (slab + 1) * LANES)
                if not first:
                    acc_old = acc_in[slab, nat, :].T
                outs = []
                for hh in range(pair):
                    h = slab * pair + hh
                    lo = rr * ATT_WIDTH + h * HEAD_DIM
                    sl = slice(lo, lo + HEAD_DIM)
                    q = q_ref[rows, sl]
                    if not has_prev:
                        k_cat, v_cat, mask = k_ref[rows, sl], v_ref[rows, sl], keep
                    elif a == 0:
                        k_cat = jnp.concatenate([kp_ref[:, sl], k_ref[rows, sl]], axis=0)
                        v_cat = jnp.concatenate([vp_ref[:, sl], v_ref[rows, sl]], axis=0)
                        mask = keep_edge
                    else:
                        krows = slice(a * tqu - BAND, (a + 1) * tqu)
                        k_cat, v_cat, mask = k_ref[krows, sl], v_ref[krows, sl], keep
                    s = jnp.where(mask, _dot_nt(k_cat, q), NEG_INF)
                    m = jnp.max(s, axis=0, keepdims=True)
                    if not first:
                        m_old = st_old[h:h + 1, :]
                        l_old = st_old[N_HEADS + h:N_HEADS + h + 1, :]
                        m_new = jnp.maximum(m, m_old)
                        alpha = jnp.exp(m_old - m_new)
                        m = m_new
                    p = jnp.exp(s - m)
                    l = jnp.sum(p, axis=0, keepdims=True)
                    acc = _dot_tn(v_cat, p.astype(BF16))
                    if not first:
                        l = l + alpha * l_old
                        acc = acc + alpha * acc_old[hh * HEAD_DIM:(hh + 1) * HEAD_DIM, :]
                    if last:
                        acc = acc / l
                    outs.append(acc)
                    ms.append(m)
                    ls.append(l)
                slab_out = jnp.concatenate(outs, axis=0).T
                if last:
                    o_ref[rows, slab_lanes] = slab_out.astype(BF16)
                else:
                    acc_out[slab, nat, :] = slab_out
            if not last:
                pad = jnp.zeros((LANES - 2 * N_HEADS, tqu), F32)
                st_out[nat, :] = jnp.concatenate(ms + ls + [pad], axis=0).T


def _dilated_branch(q, k, v, state, dilation, last):
    B, L, _ = q.shape
    d = dilation
    S = L * d
    first = state is None
    assert not (last and d != 1)
    rg = min(4, d)
    tqu = min(2 * BAND, L)
    tqb = min(L, 4 * tqu // rg)
    sub = tqb // tqu
    n_blk = L // tqb
    n_grp = d // rg
    has_prev = L > BAND
    wq = rg * ATT_WIDTH
    n_slab = ATT_WIDTH // LANES

    main = lambda b, i, g: (b, i, g)
    prev = lambda b, i, g: (b, jnp.maximum(i * (tqb // BAND) - 1, 0), g)
    qkv_spec = pl.BlockSpec((None, tqb, wq), main)
    acc_spec = pl.BlockSpec((None, n_slab, tqb * d, LANES), lambda b, i, g: (b, 0, i, 0))
    st_spec = pl.BlockSpec((None, tqb * d, LANES), lambda b, i, g: (b, i, 0))
    in_specs = [qkv_spec] * 3
    args = [q, k, v]
    if has_prev:
        in_specs += [pl.BlockSpec((None, BAND, wq), prev)] * 2
        args += [k, v]
    if not first:
        in_specs += [acc_spec, st_spec]
        args += list(state)
    if last:
        out_specs = pl.BlockSpec((None, tqb, wq), main)
        out_shape = jax.ShapeDtypeStruct((B, S, ATT_WIDTH), BF16)
    else:
        out_specs = [acc_spec, st_spec]
        out_shape = [jax.ShapeDtypeStruct((B, n_slab, S, LANES), F32),
                     jax.ShapeDtypeStruct((B, S, LANES), F32)]
    return pl.pallas_call(
        functools.partial(_dil_kernel, d=d, rg=rg, sub=sub, tqu=tqu, has_prev=has_prev, first=first,
                          last=last),
        grid=(B, n_blk, n_grp),
        in_specs=in_specs,
        out_specs=out_specs,
        out_shape=out_shape,
        compiler_params=_cparams(("parallel", "parallel", "arbitrary"), 48),
        name=f"dilated_d{d}",
    )(*args)


def _dilated_attention(qkv_by_dilation):
    state = None
    order = sorted(DIL_CONFIGS, key=lambda wd: -wd[1])
    for n, (window, d) in enumerate(order):
        assert window // d == BAND
        state = _dilated_branch(*qkv_by_dilation[d], state, d, last=(n == len(order) - 1))
    return state


def _outproj_ln_kernel(*refs, n_in):
    a_refs = refs[:n_in]
    w_refs = refs[n_in:2 * n_in]
    x_ref, g_ref, b_ref, o_ref = refs[2 * n_in:]
    y = _dot(a_refs[0][...], w_refs[0][...])
    for a_ref, w_ref in zip(a_refs[1:], w_refs[1:]):
        y = y + _dot(a_ref[...], w_ref[...])
    o_ref[...] = _layer_norm(DEEPNORM_ALPHA * x_ref[...] + y, g_ref[...], b_ref[...])


def _outproj_ln(acts, ws, x, g, b, tm):
    T = x.shape[0]
    n_in = len(acts)
    in_specs = [pl.BlockSpec((tm, a.shape[1]), lambda i: (i, 0)) for a in acts]
    in_specs += [pl.BlockSpec(w.shape, lambda i: (0, 0)) for w in ws]
    row = pl.BlockSpec((tm, D_MODEL), lambda i: (i, 0))
    vec = pl.BlockSpec((1, D_MODEL), lambda i: (0, 0))
    return pl.pallas_call(
        functools.partial(_outproj_ln_kernel, n_in=n_in),
        grid=(T // tm,),
        in_specs=in_specs + [row, vec, vec],
        out_specs=row,
        out_shape=jax.ShapeDtypeStruct((T, D_MODEL), F32),
        compiler_params=_cparams(("parallel",), 48),
        name="outproj_ln",
    )(*acts, *ws, x, g, b)


def _ffn_chunks(width):
    chunks, lo = [], 0
    while lo < width:
        size = min(2 * MXU_DIM, width - lo)
        chunks.append((lo, size))
        lo += size
    return chunks


def _dense_ffn_kernel(x_ref, wg_ref, wu_ref, wd_ref, g_ref, b_ref, o_ref):
    x = x_ref[...]
    xb = x.astype(BF16)
    y = None
    for lo, size in _ffn_chunks(wg_ref.shape[1]):
        gate = _dot(xb, wg_ref[:, lo:lo + size])
        up = _dot(xb, wu_ref[:, lo:lo + size])
        act = (gate * jax.nn.sigmoid(gate) * up).astype(BF16)
        part = _dot(act, wd_ref[lo:lo + size, :])
        y = part if y is None else y + part
    o_ref[...] = _layer_norm(DEEPNORM_ALPHA * x + y, g_ref[...], b_ref[...])


def _dense_ffn(x, wg, wu, wd, g, b, tm):
    T = x.shape[0]
    F = wg.shape[1]
    row = pl.BlockSpec((tm, D_MODEL), lambda i: (i, 0))
    vec = pl.BlockSpec((1, D_MODEL), lambda i: (0, 0))
    once = pl.Buffered(1)
    return pl.pallas_call(
        _dense_ffn_kernel,
        grid=(T // tm,),
        in_specs=[row,
                  pl.BlockSpec((D_MODEL, F), lambda i: (0, 0), pipeline_mode=once),
                  pl.BlockSpec((D_MODEL, F), lambda i: (0, 0), pipeline_mode=once),
                  pl.BlockSpec((F, D_MODEL), lambda i: (0, 0), pipeline_mode=once),
                  vec, vec],
        out_specs=row,
        out_shape=jax.ShapeDtypeStruct((T, D_MODEL), F32),
        compiler_params=_cparams(("parallel",), 56),
        name="dense_ffn",
    )(x, wg, wu, wd, g, b)


CONV_PAD = 8


def _proj1_kernel(x_ref, wqk_ref, wv_ref, wog_ref, wgt_ref, wconv_ref,
                  q_ref, k_ref, v_ref, og_ref, gt_ref, buf):
    i = pl.program_id(1)
    tm = x_ref.shape[0]
    xb = x_ref[...].astype(BF16)
    kscale = MLSTM_HEAD_DIM ** -0.5

    @pl.when(i == 0)
    def _():
        buf[0:CONV_PAD, :] = jnp.zeros((CONV_PAD, buf.shape[1]), F32)

    wide = 2 * MXU_DIM
    for c in range(2 * D_MODEL // wide):
        lanes = slice(c * wide, (c + 1) * wide)
        buf[CONV_PAD:CONV_PAD + tm, lanes] = _dot(xb, wqk_ref[:, lanes])
    for c in range(2 * D_MODEL // LANES):
        lanes = slice(c * LANES, (c + 1) * LANES)
        y = None
        for tap in range(CONV_WIDTH):
            off = CONV_PAD - (CONV_WIDTH - 1) + tap
            term = buf[off:off + tm, lanes] * wconv_ref[tap:tap + 1, lanes]
            y = term if y is None else y + term
        y = y * jax.nn.sigmoid(y)
        buf[0:CONV_PAD, lanes] = buf[tm:tm + CONV_PAD, lanes]
        if c < D_MODEL // LANES:
            q_ref[:, lanes] = y.astype(BF16)
        else:
            k_ref[:, c * LANES - D_MODEL:(c + 1) * LANES - D_MODEL] = (y * kscale).astype(BF16)
    for c in range(D_MODEL // wide):
        lanes = slice(c * wide, (c + 1) * wide)
        v_ref[:, lanes] = _dot(xb, wv_ref[:, lanes]).astype(BF16)
        og_ref[:, lanes] = _dot(xb, wog_ref[:, lanes])
    gt_ref[...] = _dot(xb, wgt_ref[...])


def _proj1(x, wqk, wv, wog, wgt, wconv, tm):
    B, S, _ = x.shape
    row = lambda b, i: (b, i, 0)
    const = lambda b, i: (0, 0)
    act_spec = pl.BlockSpec((None, tm, D_MODEL), row)
    act = jax.ShapeDtypeStruct((B, S, D_MODEL), BF16)
    return pl.pallas_call(
        _proj1_kernel,
        grid=(B, S // tm),
        in_specs=[act_spec,
                  pl.BlockSpec((D_MODEL, 2 * D_MODEL), const),
                  pl.BlockSpec((D_MODEL, D_MODEL), const),
                  pl.BlockSpec((D_MODEL, D_MODEL), const),
                  pl.BlockSpec((D_MODEL, 2 * N_HEADS), const),
                  pl.BlockSpec((CONV_WIDTH, 2 * D_MODEL), const)],
        out_specs=[act_spec, act_spec, act_spec, act_spec,
                   pl.BlockSpec((None, tm, 2 * N_HEADS), row)],
        out_shape=[act, act, act, jax.ShapeDtypeStruct((B, S, D_MODEL), F32),
                   jax.ShapeDtypeStruct((B, S, 2 * N_HEADS), F32)],
        scratch_shapes=[pltpu.VMEM((tm + CONV_PAD, 2 * D_MODEL), F32)],
        compiler_params=_cparams(("parallel", "arbitrary"), 56),
        name="proj1",
    )(x, wqk, wv, wog, wgt, wconv)


def _mlstm_kernel(q_ref, k_ref, v_ref, og_ref, gn_ref, gt_ref, bi_row, bf_row, bi_col, bf_col,
                  ng_ref, o_ref, c_sc, n_sc, m_sc, *, L):
    ci = pl.program_id(1)

    @pl.when(ci == 0)
    def _():
        c_sc[...] = jnp.zeros(c_sc.shape, F32)
        n_sc[...] = jnp.zeros(n_sc.shape, F32)
        m_sc[...] = jnp.full(m_sc.shape, NEG_INF, F32)

    row = lax.broadcasted_iota(jnp.int32, (L, L), 0)
    col = lax.broadcasted_iota(jnp.int32, (L, L), 1)
    causal = col <= row
    lower = jnp.where(causal, 1.0, 0.0).astype(BF16)
    upper = jnp.where(row <= col, 1.0, 0.0).astype(BF16)

    gn = gn_ref[...]
    gt = gt_ref[...]
    i_col = gn[:, :N_HEADS] + bi_row[...]
    lf_col = _log_sigmoid(gn[:, N_HEADS:] + bf_row[...])
    i_row = gt[:N_HEADS, :] + bi_col[...]
    lf_row = _log_sigmoid(gt[N_HEADS:, :] + bf_col[...])
    b_col = sum(_dot(lower, part) for part in _split3(lf_col))
    b_row = sum(_dot(part, upper) for part in _split3(lf_row))

    keep = row <= col
    n_pad = jnp.zeros((2 * N_HEADS - 3, MLSTM_HEAD_DIM), F32)
    for h in range(N_HEADS):
        lanes = slice(h * MLSTM_HEAD_DIM, (h + 1) * MLSTM_HEAD_DIM)
        q = q_ref[:, lanes]
        k = k_ref[:, lanes]
        v = v_ref[:, lanes]
        bt = b_row[h:h + 1, :]
        key_term = i_col[:, h:h + 1] - b_col[:, h:h + 1]
        m_prev = m_sc[h]
        ct_prev = c_sc[h]
        n_prev = n_sc[h]

        dlog = jnp.where(keep, bt + key_term, NEG_INF)
        inter = bt + m_prev
        m_t = jnp.maximum(inter, jnp.max(dlog, axis=0, keepdims=True))
        s = _dot_nt(k, q) * jnp.exp(dlog - m_t)
        inter_w = jnp.exp(inter - m_t)
        num = _dot_tn(v, s.astype(BF16)) + inter_w * _dot_nt(ct_prev.astype(BF16), q)
        n_parts = jnp.concatenate([p.astype(F32) for p in _split3(n_prev)] + [n_pad], axis=0)
        qn = jnp.sum(_dot_nt(n_parts.astype(BF16), q), axis=0, keepdims=True)
        den = jnp.sum(s, axis=0, keepdims=True) + inter_w * qn
        hh = num / jnp.maximum(jnp.abs(den), jnp.exp(-m_t))

        b_last = bt[:, L - 1:L]
        g = b_last + key_term
        m_new = jnp.maximum(b_last + m_prev, jnp.max(g, axis=0, keepdims=True))
        w = jnp.exp(g - m_new)
        decay = jnp.exp(b_last + m_prev - m_new)
        kw = k.astype(F32) * w
        c_sc[h] = decay * ct_prev + _dot_tn(v, kw.astype(BF16))
        n_sc[h] = decay * n_prev + jnp.sum(kw, axis=0, keepdims=True)
        m_sc[h] = m_new

        mu = jnp.mean(hh, axis=0, keepdims=True)
        hc = hh - mu
        var = jnp.mean(hc * hc, axis=0, keepdims=True)
        hn = (hc * lax.rsqrt(var + LN_EPS)).T * ng_ref[:, lanes]
        o_ref[:, lanes] = (hn * jax.nn.sigmoid(og_ref[:, lanes])).astype(BF16)


def _mlstm(q, k, v, og, gates, gates_t, b_i, b_f, norm_g, L):
    B, S, _ = q.shape
    row = lambda b, c: (b, c, 0)
    const = lambda b, c: (0, 0)
    act_spec = pl.BlockSpec((None, L, D_MODEL), row)
    return pl.pallas_call(
        functools.partial(_mlstm_kernel, L=L),
        grid=(B, S // L),
        in_specs=[act_spec, act_spec, act_spec, act_spec,
                  pl.BlockSpec((None, L, 2 * N_HEADS), row),
                  pl.BlockSpec((None, 2 * N_HEADS, L), lambda b, c: (b, 0, c)),
                  pl.BlockSpec((1, N_HEADS), const), pl.BlockSpec((1, N_HEADS), const),
                  pl.BlockSpec((N_HEADS, 1), const), pl.BlockSpec((N_HEADS, 1), const),
                  pl.BlockSpec((1, D_MODEL), const)],
        out_specs=act_spec,
        out_shape=jax.ShapeDtypeStruct((B, S, D_MODEL), BF16),
        scratch_shapes=[pltpu.VMEM((N_HEADS, MLSTM_HEAD_DIM, MLSTM_HEAD_DIM), F32),
                        pltpu.VMEM((N_HEADS, 1, MLSTM_HEAD_DIM), F32),
                        pltpu.VMEM((N_HEADS, 1, 1), F32)],
        compiler_params=_cparams(("parallel", "arbitrary"), 48),
        name="mlstm",
    )(q, k, v, og, gates, gates_t, b_i[None, :], b_f[None, :], b_i[:, None], b_f[:, None],
      norm_g[None, :])


MOE_TOP_K = 2
MOE_TILE = 512
RANK_BLOCK = 1024


def _router_kernel(x_ref, wt_ref, sel_ref, gate_ref):
    logits = lax.dot_general(wt_ref[...], x_ref[...], (((1,), (1,)), ((), ())),
                             preferred_element_type=F32, precision=lax.Precision.HIGHEST)
    idx = lax.broadcasted_iota(jnp.int32, logits.shape, 0)
    m1 = jnp.max(logits, axis=0, keepdims=True)
    i1 = jnp.min(jnp.where(logits == m1, idx, N_EXPERTS), axis=0, keepdims=True)
    pick1 = idx == i1
    rest = jnp.where(pick1, NEG_INF, logits)
    m2 = jnp.max(rest, axis=0, keepdims=True)
    i2 = jnp.min(jnp.where(rest == m2, idx, N_EXPERTS), axis=0, keepdims=True)
    pick2 = idx == i2
    e2 = jnp.exp(m2 - m1)
    w1 = 1.0 / (1.0 + e2)
    w2 = e2 / (1.0 + e2)
    sel_ref[...] = jnp.where(jnp.logical_or(pick1, pick2), 1.0, 0.0)
    gate_ref[...] = jnp.where(pick1, w1, 0.0) + jnp.where(pick2, w2, 0.0)


def _router(x, w_router_t, tm):
    T = x.shape[0]
    out = jax.ShapeDtypeStruct((N_EXPERTS, T), F32)
    out_spec = pl.BlockSpec((N_EXPERTS, tm), lambda i: (0, i))
    return pl.pallas_call(
        _router_kernel,
        grid=(T // tm,),
        in_specs=[pl.BlockSpec((tm, D_MODEL), lambda i: (i, 0)),
                  pl.BlockSpec((N_EXPERTS, D_MODEL), lambda i: (0, 0))],
        out_specs=[out_spec, out_spec],
        out_shape=[out, out],
        compiler_params=_cparams(("parallel",), 32),
        name="moe_router",
    )(x, w_router_t)


def _rank_kernel(sel_ref, rank_ref, count_ref, upper_sc, carry_sc):
    n = sel_ref.shape[1]

    @pl.when(pl.program_id(0) == 0)
    def _():
        row = lax.broadcasted_iota(jnp.int32, (n, n), 0)
        col = lax.broadcasted_iota(jnp.int32, (n, n), 1)
        upper_sc[...] = jnp.where(row <= col, 1.0, 0.0).astype(BF16)
        carry_sc[...] = jnp.zeros(carry_sc.shape, F32)

    sel = sel_ref[...]
    incl = _dot(sel.astype(BF16), upper_sc[...])
    rank_ref[...] = carry_sc[...] + incl - sel
    carry_sc[...] = carry_sc[...] + incl[:, n - 1:n]
    count_ref[...] = carry_sc[...]


def _rank(sel_t):
    E, T = sel_t.shape
    n = min(RANK_BLOCK, T)
    return pl.pallas_call(
        _rank_kernel,
        grid=(T // n,),
        in_specs=[pl.BlockSpec((E, n), lambda i: (0, i))],
        out_specs=[pl.BlockSpec((E, n), lambda i: (0, i)), pl.BlockSpec((E, 1), lambda i: (0, 0))],
        out_shape=[jax.ShapeDtypeStruct((E, T), F32), jax.ShapeDtypeStruct((E, 1), F32)],
        scratch_shapes=[pltpu.VMEM((n, n), BF16), pltpu.VMEM((E, 1), F32)],
        compiler_params=_cparams(("arbitrary",), 32),
        name="moe_rank",
    )(sel_t)


def _slot_kernel(sel_ref, gate_ref, rank_ref, off_ref, pos_ref, w_ref):
    sel = sel_ref[...] > 0.0
    idx = lax.broadcasted_iota(jnp.int32, sel.shape, 0)
    first = jnp.min(jnp.where(sel, idx, N_EXPERTS), axis=0, keepdims=True)
    second = jnp.max(jnp.where(sel, idx, -1), axis=0, keepdims=True)
    slot = off_ref[...] + rank_ref[...]
    gate = gate_ref[...]
    rows_p, rows_w = [], []
    for which in (first, second):
        hit = idx == which
        rows_p.append(jnp.sum(jnp.where(hit, slot, 0.0), axis=0, keepdims=True))
        rows_w.append(jnp.sum(jnp.where(hit, gate, 0.0), axis=0, keepdims=True))
    pos_ref[...] = jnp.concatenate(rows_p, axis=0).astype(jnp.int32)
    w_ref[...] = jnp.concatenate(rows_w, axis=0)


def _slots(sel_t, gate_t, rank_t, offsets_col, tm):
    E, T = sel_t.shape
    spec = pl.BlockSpec((E, tm), lambda i: (0, i))
    out_spec = pl.BlockSpec((MOE_TOP_K, tm), lambda i: (0, i))
    return pl.pallas_call(
        _slot_kernel,
        grid=(T // tm,),
        in_specs=[spec, spec, spec, pl.BlockSpec((E, 1), lambda i: (0, 0))],
        out_specs=[out_spec, out_spec],
        out_shape=[jax.ShapeDtypeStruct((MOE_TOP_K, T), jnp.int32),
                   jax.ShapeDtypeStruct((MOE_TOP_K, T), F32)],
        compiler_params=_cparams(("parallel",), 32),
        name="moe_slots",
    )(sel_t, gate_t, rank_t, offsets_col)


def _dispatch_kernel(tail_ref, used_ref, pos_ref, x_ref, xs_hbm, zero_sc, sem, zsem, *, tm):
    i = pl.program_id(0)
    n_tiles = xs_hbm.shape[0] // MOE_TILE

    def fill(tile):
        return pltpu.make_async_copy(zero_sc, xs_hbm.at[pl.ds(tile * MOE_TILE, MOE_TILE)], zsem)

    @pl.when(i == 0)
    def _():
        zero_sc[...] = jnp.zeros(zero_sc.shape, F32)
        for e in range(N_EXPERTS):
            fill(tail_ref[e]).start()
        for e in range(N_EXPERTS):
            fill(tail_ref[e]).wait()
        for j in range(N_EXPERTS):
            @pl.when(n_tiles - 1 - j >= used_ref[0])
            def _():
                c = fill(n_tiles - 1 - j)
                c.start()
                c.wait()

    def issue(t, carry):
        for k in range(MOE_TOP_K):
            pltpu.make_async_copy(x_ref.at[pl.ds(t, 1)], xs_hbm.at[pl.ds(pos_ref[k, t], 1)],
                                  sem).start(priority=k)
        return carry

    lax.fori_loop(0, tm, issue, 0, unroll=8)
    for _ in range(MOE_TOP_K):
        pltpu.make_async_copy(x_ref, xs_hbm.at[pl.ds(0, tm)], sem).wait()


def _dispatch(x, pos_t, tail_tiles, n_used, n_rows, tm):
    T = x.shape[0]
    n_steps = T // tm
    pos3 = pos_t.reshape(MOE_TOP_K, n_steps, tm).transpose(1, 0, 2)
    return pl.pallas_call(
        functools.partial(_dispatch_kernel, tm=tm),
        grid_spec=pltpu.PrefetchScalarGridSpec(
            num_scalar_prefetch=2,
            grid=(n_steps,),
            in_specs=[pl.BlockSpec((None, MOE_TOP_K, tm), lambda i, tail, used: (i, 0, 0),
                                   memory_space=pltpu.SMEM),
                      pl.BlockSpec((tm, D_MODEL), lambda i, tail, used: (i, 0))],
            out_specs=pl.BlockSpec(memory_space=pl.ANY),
            scratch_shapes=[pltpu.VMEM((MOE_TILE, D_MODEL), F32),
                            pltpu.SemaphoreType.DMA, pltpu.SemaphoreType.DMA],
        ),
        out_shape=jax.ShapeDtypeStruct((n_rows, D_MODEL), F32),
        compiler_params=_cparams(("arbitrary",), 32),
        name="moe_dispatch",
    )(tail_tiles, n_used, pos3, x)


def _expert_ffn_kernel(te_ref, used_ref, x_ref, wg_ref, wu_ref, wd_ref, o_ref, xb_sc):
    i = pl.program_id(0)
    f = pl.program_id(1)
    live = i < used_ref[0]

    @pl.when(jnp.logical_and(live, f == 0))
    def _():
        xb_sc[...] = x_ref[...].astype(BF16)

    @pl.when(live)
    def _():
        xb = xb_sc[...]
        part = None
        for lo, size in _ffn_chunks(wg_ref.shape[1]):
            gate = _dot(xb, wg_ref[:, lo:lo + size])
            up = _dot(xb, wu_ref[:, lo:lo + size])
            act = (gate * jax.nn.sigmoid(gate) * up).astype(BF16)
            p = _dot(act, wd_ref[lo:lo + size, :])
            part = p if part is None else part + p

        @pl.when(f == 0)
        def _():
            o_ref[...] = part

        @pl.when(f > 0)
        def _():
            o_ref[...] += part

    @pl.when(jnp.logical_and(jnp.logical_not(live), f == 0))
    def _():
        o_ref[...] = jnp.zeros(o_ref.shape, F32)


def _expert_ffn(xs, tile_expert, n_used, wg, wu, wd, fc):
    n_rows = xs.shape[0]
    E, _, F = wg.shape
    tm = MOE_TILE

    def row(i, f, te, used):
        return (jnp.minimum(i, used[0] - 1), 0)

    return pl.pallas_call(
        _expert_ffn_kernel,
        grid_spec=pltpu.PrefetchScalarGridSpec(
            num_scalar_prefetch=2,
            grid=(n_rows // tm, F // fc),
            in_specs=[pl.BlockSpec((tm, D_MODEL), row),
                      pl.BlockSpec((None, D_MODEL, fc), lambda i, f, te, used: (te[i], 0, f)),
                      pl.BlockSpec((None, D_MODEL, fc), lambda i, f, te, used: (te[i], 0, f)),
                      pl.BlockSpec((None, fc, D_MODEL), lambda i, f, te, used: (te[i], f, 0))],
            out_specs=pl.BlockSpec((tm, D_MODEL), lambda i, f, te, used: (i, 0)),
            scratch_shapes=[pltpu.VMEM((tm, D_MODEL), BF16)],
        ),
        out_shape=jax.ShapeDtypeStruct((n_rows, D_MODEL), F32),
        compiler_params=_cparams(("arbitrary", "arbitrary"), 56),
        name="moe_expert_ffn",
    )(tile_expert, n_used, xs, wg, wu, wd)


def _combine_kernel(pos_ref, w_ref, x_ref, g_ref, b_ref, ys_hbm, o_ref, rows_sc, sem, *, tm):
    def issue(t, carry):
        for k in range(MOE_TOP_K):
            pltpu.make_async_copy(ys_hbm.at[pl.ds(pos_ref[k, t], 1)],
                                  rows_sc.at[k, pl.ds(t, 1)], sem).start(priority=k)
        return carry

    lax.fori_loop(0, tm, issue, 0, unroll=8)
    for k in range(MOE_TOP_K):
        pltpu.make_async_copy(ys_hbm.at[pl.ds(0, tm)], rows_sc.at[k], sem).wait()
    w = w_ref[...]
    y = w[:, 0:1] * rows_sc[0] + w[:, 1:2] * rows_sc[1]
    o_ref[...] = _layer_norm(DEEPNORM_ALPHA * x_ref[...] + y, g_ref[...], b_ref[...])


def _combine_ln(ys, pos_t, w_nat, x, g, b, tm):
    T = x.shape[0]
    n_steps = T // tm
    pos3 = pos_t.reshape(MOE_TOP_K, n_steps, tm).transpose(1, 0, 2)
    row = pl.BlockSpec((tm, D_MODEL), lambda i: (i, 0))
    vec = pl.BlockSpec((1, D_MODEL), lambda i: (0, 0))
    return pl.pallas_call(
        functools.partial(_combine_kernel, tm=tm),
        grid=(n_steps,),
        in_specs=[pl.BlockSpec((None, MOE_TOP_K, tm), lambda i: (i, 0, 0), memory_space=pltpu.SMEM),
                  pl.BlockSpec((tm, MOE_TOP_K), lambda i: (i, 0)),
                  row, vec, vec,
                  pl.BlockSpec(memory_space=pl.ANY)],
        out_specs=row,
        out_shape=jax.ShapeDtypeStruct((T, D_MODEL), F32),
        scratch_shapes=[pltpu.VMEM((MOE_TOP_K, tm, D_MODEL), F32), pltpu.SemaphoreType.DMA],
        compiler_params=_cparams(("arbitrary",), 32),
        name="moe_combine_ln",
    )(pos3, w_nat, x, g, b, ys)


def _moe_sparse(x, w_router, wg, wu, wd, g, b):
    T = x.shape[0]
    tile = MOE_TILE
    n_rows = MOE_TOP_K * T + N_EXPERTS * tile
    n_tiles = n_rows // tile
    sel_t, gate_t = _router(x, w_router.T, tm=1024)
    rank_t, counts = _rank(sel_t)
    tiles_per = jnp.ceil(counts[:, 0] / tile).astype(jnp.int32)
    tile_end = jnp.cumsum(tiles_per)
    tile_start = tile_end - tiles_per
    n_used = tile_end[-1:]
    tile_expert = jnp.minimum(
        jnp.sum(jnp.arange(n_tiles, dtype=jnp.int32)[:, None] >= tile_end[None, :], axis=1),
        N_EXPERTS - 1).astype(jnp.int32)
    tail_tiles = jnp.maximum(tile_end - 1, 0).astype(jnp.int32)
    offsets_col = (tile_start * tile).astype(F32)[:, None]
    pos_t, w_t = _slots(sel_t, gate_t, rank_t, offsets_col, tm=1024)
    xs = _dispatch(x, pos_t, tail_tiles, n_used, n_rows, tm=512)
    ys = _expert_ffn(xs, tile_expert, n_used, wg, wu, wd, fc=D_FF_EXPERT // 2)
    return _combine_ln(ys, pos_t, w_t.T, x, g, b, tm=256)


def _rotary_tables(S):
    pos = jnp.arange(S, dtype=jnp.int32)
    inv = ROPE_THETA ** (-jnp.arange(ROPE_HALF, dtype=F32) / ROPE_HALF)
    ang = pos.astype(F32)[:, None] * inv[None, :]
    cos, sin = jnp.cos(ang), jnp.sin(ang)
    ones = jnp.ones((S, HEAD_DIM - ROPE_DIMS), F32)
    zeros = jnp.zeros((S, HEAD_DIM - ROPE_DIMS), F32)
    z8 = jnp.zeros((S, ROPE_HALF), F32)
    cos_h = jnp.concatenate([cos, cos, ones], axis=1)
    sup_h = jnp.concatenate([-sin, z8, zeros], axis=1)
    sdn_h = jnp.concatenate([z8, sin, zeros], axis=1)
    rep = LANES // HEAD_DIM
    return (jnp.tile(cos_h, (1, rep)), jnp.tile(sup_h, (1, rep)), jnp.tile(sdn_h, (1, rep)))


def _even_layer(h, w_in, b_forget, w_out, ln_mix_g, ln_mix_b, w_gate, w_up, w_down,
                ln_ffn_g, ln_ffn_b):
    B, S, _ = h.shape
    T = B * S
    W = ATT_WIDTH
    w_main = jnp.concatenate([w_in[:, :3 * W], w_in[:, 3 * W + N_HEADS:]], axis=1).astype(BF16)
    wf_t = w_in[:, 3 * W:3 * W + N_HEADS].T.astype(BF16)
    cos_t, sup_t, sdn_t = _rotary_tables(S)
    outs = _proj0(h, w_main, wf_t, cos_t, sup_t, sdn_t, tm=min(512, S))
    qa, ka, va = outs[:3]
    qkv_by_dilation = {d: tuple(outs[3 + 3 * n:6 + 3 * n]) for n, d in enumerate(DILATIONS)}
    f_t = outs[-1]

    bias_col = jnp.tile(b_forget, B)[:, None]
    c = _fox_cumsum(f_t.reshape(B * N_HEADS, S), bias_col).reshape(B, N_HEADS, S)
    o_fox = _fox_attention(qa, ka, va, jnp.swapaxes(c, 1, 2), t=min(512, S))
    o_dil = _dilated_attention(qkv_by_dilation)

    w_out_b = w_out.astype(BF16)
    tm = 512
    h1 = _outproj_ln([o_fox.reshape(T, W), o_dil.reshape(T, W)], [w_out_b[:W], w_out_b[W:]],
                     h.reshape(T, D_MODEL), ln_mix_g[None, :], ln_mix_b[None, :], tm)
    h2 = _dense_ffn(h1, w_gate.astype(BF16), w_up.astype(BF16), w_down.astype(BF16),
                    ln_ffn_g[None, :], ln_ffn_b[None, :], tm)
    return h2.reshape(B, S, D_MODEL)


def _odd_layer(h, w_in, b_igate, b_fgate, w_conv, norm_g, w_out, ln_mix_g, ln_mix_b, w_router,
               w_gate, w_up, w_down, ln_ffn_g, ln_ffn_b):
    B, S, _ = h.shape
    T = B * S
    D = D_MODEL
    wqk = w_in[:, :2 * D].astype(BF16)
    wv = w_in[:, 2 * D:3 * D].astype(BF16)
    wgt = w_in[:, 3 * D:3 * D + 2 * N_HEADS].astype(BF16)
    wog = w_in[:, 3 * D + 2 * N_HEADS:].astype(BF16)
    q, k, v, og, gates = _proj1(h, wqk, wv, wog, wgt, w_conv, tm=min(512, S))
    gates_t = jnp.swapaxes(gates, 1, 2)
    hm = _mlstm(q, k, v, og, gates, gates_t, b_igate, b_fgate, norm_g, L=min(256, S))

    tm = 512
    h1 = _outproj_ln([hm.reshape(T, D)], [w_out.astype(BF16)], h.reshape(T, D),
                     ln_mix_g[None, :], ln_mix_b[None, :], tm)
    h2 = _moe_sparse(h1, w_router, w_gate.astype(BF16), w_up.astype(BF16), w_down.astype(BF16),
                     ln_ffn_g[None, :], ln_ffn_b[None, :])
    return h2.reshape(B, S, D)


def kernel(x, w_in_e, b_forget_e, w_out_e, ln_mix_g_e, ln_mix_b_e, ffn_w_gate_e, ffn_w_up_e,
           ffn_w_down_e, ln_ffn_g_e, ln_ffn_b_e, w_in_o, b_igate_o, b_fgate_o, w_conv_o,
           mlstm_norm_g_o, w_out_o, ln_mix_g_o, ln_mix_b_o, w_router_o, moe_w_gate_o,
           moe_w_up_o, moe_w_down_o, ln_ffn_g_o, ln_ffn_b_o):
    h = x
    for layer in range(DEPTH):
        i = layer // 2
        if layer % 2 == 0:
            h = _even_layer(h, w_in_e[i], b_forget_e[i], w_out_e[i], ln_mix_g_e[i], ln_mix_b_e[i],
                            ffn_w_gate_e[i], ffn_w_up_e[i], ffn_w_down_e[i], ln_ffn_g_e[i],
                            ln_ffn_b_e[i])
        else:
            h = _odd_layer(h, w_in_o[i], b_igate_o[i], b_fgate_o[i], w_conv_o[i],
                           mlstm_norm_g_o[i], w_out_o[i], ln_mix_g_o[i], ln_mix_b_o[i],
                           w_router_o[i], moe_w_gate_o[i], moe_w_up_o[i], moe_w_down_o[i],
                           ln_ffn_g_o[i], ln_ffn_b_o[i])
    return h
```

```python
import functools
import math

import jax
import jax.numpy as jnp
from jax import lax
from jax.experimental import pallas as pl
from jax.experimental.pallas import tpu as pltpu

F32 = jnp.float32
BF16 = jnp.bfloat16

D_MODEL = 1024
HEAD_DIM = 64
N_HEADS = 8
ATT_WIDTH = N_HEADS * HEAD_DIM
DIL_CONFIGS = ((128, 1), (512, 4), (2048, 16))
ROPE_THETA = 500000.0
ROPE_DIMS = HEAD_DIM // 4
ROPE_HALF = ROPE_DIMS // 2
MLSTM_HEAD_DIM = D_MODEL // N_HEADS
CONV_WIDTH = 4
D_FF_DENSE = 2816
N_EXPERTS = 8
D_FF_EXPERT = 3584
DEPTH = 2
DEEPNORM_ALPHA = (2 * DEPTH) ** 0.25
LN_EPS = 1e-5

LANES = 128
MXU_DIM = 256
BAND = 128
MIB = 1024 * 1024

NEG_INF = float("-inf")


def _cparams(semantics, vmem_mib):
    return pltpu.CompilerParams(dimension_semantics=semantics, vmem_limit_bytes=vmem_mib * MIB)


def _dot(a, b):
    return jnp.dot(a, b, preferred_element_type=F32)


def _dot_nt(a, b):
    return lax.dot_general(a, b, (((1,), (1,)), ((), ())), preferred_element_type=F32)


def _dot_tn(a, b):
    return lax.dot_general(a, b, (((0,), (0,)), ((), ())), preferred_element_type=F32)


def _split3(x):
    hi = x.astype(BF16)
    r = x - hi.astype(F32)
    mid = r.astype(BF16)
    lo = (r - mid.astype(F32)).astype(BF16)
    return hi, mid, lo


def _log_sigmoid(z):
    return -(jnp.maximum(-z, 0.0) + jnp.log1p(jnp.exp(-jnp.abs(z))))


def _layer_norm(z, g, b):
    mu = jnp.mean(z, axis=-1, keepdims=True)
    zc = z - mu
    var = jnp.mean(zc * zc, axis=-1, keepdims=True)
    return zc * lax.rsqrt(var + LN_EPS) * g + b


DILATIONS = tuple(sorted(d for _, d in DIL_CONFIGS))
N_PROJ0 = 6


def _proj0_kernel(x_ref, w_ref, wf_ref, cos_ref, sup_ref, sdn_ref, *refs):
    tm = x_ref.shape[0]
    outs = refs[:N_PROJ0]
    strided = refs[N_PROJ0:-2]
    ft_ref, row_sc = refs[-2:]
    xb = x_ref[...].astype(BF16)
    scale = HEAD_DIM ** -0.5
    for j, o_ref in enumerate(outs):
        full = _dot(xb, w_ref[:, j * ATT_WIDTH:(j + 1) * ATT_WIDTH])
        for c in range(ATT_WIDTH // LANES):
            r = full[:, c * LANES:(c + 1) * LANES]
            if j in (3, 4):
                r = (r * cos_ref[...]
                     + pltpu.roll(r, LANES - ROPE_HALF, axis=1) * sup_ref[...]
                     + pltpu.roll(r, ROPE_HALF, axis=1) * sdn_ref[...])
            if j in (0, 3):
                r = r * scale
            o_ref[:, c * LANES:(c + 1) * LANES] = r.astype(BF16)
            if j >= 3:
                row_sc[...] = r
                for n, d in enumerate(DILATIONS[1:]):
                    s_ref = strided[3 * n + (j - 3)]
                    for res in range(d):
                        lo = res * ATT_WIDTH + c * LANES
                        s_ref[:, lo:lo + LANES] = row_sc[pl.ds(res, tm // d, stride=d), :].astype(BF16)
    ft_ref[...] = _dot_nt(wf_ref[...], xb)


def _proj0(x, w_main, wf_t, cos_t, sup_t, sdn_t, tm):
    B, S, _ = x.shape
    n_i = S // tm
    act = jax.ShapeDtypeStruct((B, S, ATT_WIDTH), BF16)
    act_spec = pl.BlockSpec((None, tm, ATT_WIDTH), lambda b, i: (b, i, 0))
    tab_spec = pl.BlockSpec((tm, LANES), lambda b, i: (i, 0))
    out_specs = [act_spec] * N_PROJ0
    out_shape = [act] * N_PROJ0
    for d in DILATIONS[1:]:
        out_specs += [pl.BlockSpec((None, tm // d, d * ATT_WIDTH), lambda b, i: (b, i, 0))] * 3
        out_shape += [jax.ShapeDtypeStruct((B, S // d, d * ATT_WIDTH), BF16)] * 3
    out_specs.append(pl.BlockSpec((None, N_HEADS, tm), lambda b, i: (b, 0, i)))
    out_shape.append(jax.ShapeDtypeStruct((B, N_HEADS, S), F32))
    return pl.pallas_call(
        _proj0_kernel,
        grid=(B, n_i),
        in_specs=[
            pl.BlockSpec((None, tm, D_MODEL), lambda b, i: (b, i, 0)),
            pl.BlockSpec((D_MODEL, N_PROJ0 * ATT_WIDTH), lambda b, i: (0, 0)),
            pl.BlockSpec((N_HEADS, D_MODEL), lambda b, i: (0, 0)),
            tab_spec, tab_spec, tab_spec,
        ],
        out_specs=out_specs,
        out_shape=out_shape,
        scratch_shapes=[pltpu.VMEM((tm, LANES), F32)],
        compiler_params=_cparams(("parallel", "parallel"), 48),
        name="proj0",
    )(x, w_main, wf_t, cos_t, sup_t, sdn_t)


def _fox_cumsum_kernel(f_ref, bias_ref, c_ref):
    S = f_ref.shape[1]
    lf = _log_sigmoid(f_ref[...] + bias_ref[...])
    row = lax.broadcasted_iota(jnp.int32, (S, S), 0)
    col = lax.broadcasted_iota(jnp.int32, (S, S), 1)
    upper = jnp.where(row <= col, 1.0, 0.0).astype(BF16)
    hi, mid, lo = _split3(lf)
    c_ref[...] = _dot(hi, upper) + _dot(mid, upper) + _dot(lo, upper)


def _fox_cumsum(f_t, bias_col):
    R, S = f_t.shape
    return pl.pallas_call(
        _fox_cumsum_kernel,
        grid=(1,),
        in_specs=[pl.BlockSpec((R, S), lambda i: (0, 0)), pl.BlockSpec((R, 1), lambda i: (0, 0))],
        out_specs=pl.BlockSpec((R, S), lambda i: (0, 0)),
        out_shape=jax.ShapeDtypeStruct((R, S), F32),
        compiler_params=_cparams(("arbitrary",), 48),
        name="fox_cumsum",
    )(f_t, bias_col)


def _fox_kernel(q_ref, k_ref, v_ref, c_ref, o_ref, m_sc, l_sc, acc_sc, *, t):
    i = pl.program_id(1)
    j = pl.program_id(2)

    @pl.when(j == 0)
    def _():
        m_sc[...] = jnp.full(m_sc.shape, NEG_INF, F32)
        l_sc[...] = jnp.zeros(l_sc.shape, F32)
        acc_sc[...] = jnp.zeros(acc_sc.shape, F32)

    def step(masked):
        if masked:
            key = lax.broadcasted_iota(jnp.int32, (t, t), 0)
            qry = lax.broadcasted_iota(jnp.int32, (t, t), 1)
            keep = key <= qry
        for h in range(N_HEADS):
            sl = slice(h * HEAD_DIM, (h + 1) * HEAD_DIM)
            s = _dot_nt(k_ref[:, sl], q_ref[:, sl]) - c_ref[:, h:h + 1]
            if masked:
                s = jnp.where(keep, s, NEG_INF)
            m_prev = m_sc[h:h + 1, :]
            m_new = jnp.maximum(m_prev, jnp.max(s, axis=0, keepdims=True))
            alpha = jnp.exp(m_prev - m_new)
            p = jnp.exp(s - m_new)
            l_sc[h:h + 1, :] = alpha * l_sc[h:h + 1, :] + jnp.sum(p, axis=0, keepdims=True)
            acc_sc[sl, :] = alpha * acc_sc[sl, :] + _dot_tn(v_ref[:, sl], p.astype(BF16))
            m_sc[h:h + 1, :] = m_new

    @pl.when(j < i)
    def _():
        step(False)

    @pl.when(j == i)
    def _():
        step(True)
        out_t = jnp.concatenate(
            [acc_sc[h * HEAD_DIM:(h + 1) * HEAD_DIM, :] / l_sc[h:h + 1, :] for h in range(N_HEADS)],
            axis=0)
        o_ref[...] = out_t.T.astype(BF16)


def _fox_attention(q, k, v, c, t):
    B, S, _ = q.shape
    n = S // t
    q_spec = pl.BlockSpec((None, t, ATT_WIDTH), lambda b, i, j: (b, i, 0))
    kv_spec = pl.BlockSpec((None, t, ATT_WIDTH), lambda b, i, j: (b, jnp.minimum(j, i), 0))
    return pl.pallas_call(
        functools.partial(_fox_kernel, t=t),
        grid=(B, n, n),
        in_specs=[q_spec, kv_spec, kv_spec,
                  pl.BlockSpec((None, t, N_HEADS), lambda b, i, j: (b, jnp.minimum(j, i), 0))],
        out_specs=q_spec,
        out_shape=jax.ShapeDtypeStruct((B, S, ATT_WIDTH), BF16),
        scratch_shapes=[pltpu.VMEM((N_HEADS, t), F32), pltpu.VMEM((N_HEADS, t), F32),
                        pltpu.VMEM((ATT_WIDTH, t), F32)],
        compiler_params=_cparams(("parallel", "parallel", "arbitrary"), 48),
        name="fox_attention",
    )(q, k, v, c)


def _dil_kernel(*refs, d, rg, sub, tqu, has_prev, first, last):
    refs = list(refs)
    q_ref, k_ref, v_ref = refs[:3]
    pos = 3
    if has_prev:
        kp_ref, vp_ref = refs[pos:pos + 2]
        pos += 2
    if not first:
        acc_in, st_in = refs[pos:pos + 2]
        pos += 2
    if last:
        o_ref = refs[pos]
    else:
        acc_out, st_out = refs[pos:pos + 2]

    blk = pl.program_id(1)
    nk = tqu + (BAND if has_prev else 0)
    key = lax.broadcasted_iota(jnp.int32, (nk, tqu), 0)
    qry = lax.broadcasted_iota(jnp.int32, (nk, tqu), 1)
    if has_prev:
        keep = jnp.logical_and(key >= qry, key <= qry + BAND)
        keep_edge = jnp.logical_and(keep, jnp.logical_or(key >= BAND, blk > 0))
    else:
        keep = key <= qry
    pair = LANES // HEAD_DIM

    for a in range(sub):
        rows = slice(a * tqu, (a + 1) * tqu)
        for rr in range(rg):
            if d == 1:
                nat = rows
            else:
                nat = pl.ds(a * tqu * d + pl.program_id(2) * rg + rr, tqu, stride=d)
            if not first:
                st_old = st_in[nat, :].T
            ms, ls = [], []
            for slab in range(N_HEADS // pair):
                slab_lanes = slice(rr * ATT_WIDTH + slab * LANES, rr * ATT_WIDTH + (slab + 1) * LANES)
                if not first:
                    acc_old = acc_in[slab, nat, :].T
                outs = []
                for hh in range(pair):
                    h = slab * pair + hh
                    lo = rr * ATT_WIDTH + h * HEAD_DIM
                    sl = slice(lo, lo + HEAD_DIM)
                    q = q_ref[rows, sl]
                    if not has_prev:
                        k_cat, v_cat, mask = k_ref[rows, sl], v_ref[rows, sl], keep
                    elif a == 0:
                        k_cat = jnp.concatenate([kp_ref[:, sl], k_ref[rows, sl]], axis=0)
                        v_cat = jnp.concatenate([vp_ref[:, sl], v_ref[rows, sl]], axis=0)
                        mask = keep_edge
                    else:
                        krows = slice(a * tqu - BAND, (a + 1) * tqu)
                        k_cat, v_cat, mask = k_ref[krows, sl], v_ref[krows, sl], keep
                    s = jnp.where(mask, _dot_nt(k_cat, q), NEG_INF)
                    m = jnp.max(s, axis=0, keepdims=True)
                    if not first:
                        m_old = st_old[h:h + 1, :]
                        l_old = st_old[N_HEADS + h:N_HEADS + h + 1, :]
                        m_new = jnp.maximum(m, m_old)
                        alpha = jnp.exp(m_old - m_new)
                        m = m_new
                    p = jnp.exp(s - m)
                    l = jnp.sum(p, axis=0, keepdims=True)
                    acc = _dot_tn(v_cat, p.astype(BF16))
                    if not first:
                        l = l + alpha * l_old
                        acc = acc + alpha * acc_old[hh * HEAD_DIM:(hh + 1) * HEAD_DIM, :]
                    if last:
                        acc = acc / l
                    outs.append(acc)
                    ms.append(m)
                    ls.append(l)
                slab_out = jnp.concatenate(outs, axis=0).T
                if last:
                    o_ref[rows, slab_lanes] = slab_out.astype(BF16)
                else:
                    acc_out[slab, nat, :] = slab_out
            if not last:
                pad = jnp.zeros((LANES - 2 * N_HEADS, tqu), F32)
                st_out[nat, :] = jnp.concatenate(ms + ls + [pad], axis=0).T


def _dilated_branch(q, k, v, state, dilation, last):
    B, L, _ = q.shape
    d = dilation
    S = L * d
    first = state is None
    assert not (last and d != 1)
    rg = min(4, d)
    tqu = min(2 * BAND, L)
    tqb = min(L, 4 * tqu // rg)
    sub = tqb // tqu
    n_blk = L // tqb
    n_grp = d // rg
    has_prev = L > BAND
    wq = rg * ATT_WIDTH
    n_slab = ATT_WIDTH // LANES

    main = lambda b, i, g: (b, i, g)
    prev = lambda b, i, g: (b, jnp.maximum(i * (tqb // BAND) - 1, 0), g)
    qkv_spec = pl.BlockSpec((None, tqb, wq), main)
    acc_spec = pl.BlockSpec((None, n_slab, tqb * d, LANES), lambda b, i, g: (b, 0, i, 0))
    st_spec = pl.BlockSpec((None, tqb * d, LANES), lambda b, i, g: (b, i, 0))
    in_specs = [qkv_spec] * 3
    args = [q, k, v]
    if has_prev:
        in_specs += [pl.BlockSpec((None, BAND, wq), prev)] * 2
        args += [k, v]
    if not first:
        in_specs += [acc_spec, st_spec]
        args += list(state)
    if last:
        out_specs = pl.BlockSpec((None, tqb, wq), main)
        out_shape = jax.ShapeDtypeStruct((B, S, ATT_WIDTH), BF16)
    else:
        out_specs = [acc_spec, st_spec]
        out_shape = [jax.ShapeDtypeStruct((B, n_slab, S, LANES), F32),
                     jax.ShapeDtypeStruct((B, S, LANES), F32)]
    return pl.pallas_call(
        functools.partial(_dil_kernel, d=d, rg=rg, sub=sub, tqu=tqu, has_prev=has_prev, first=first,
                          last=last),
        grid=(B, n_blk, n_grp),
        in_specs=in_specs,
        out_specs=out_specs,
        out_shape=out_shape,
        compiler_params=_cparams(("parallel", "parallel", "arbitrary"), 48),
        name=f"dilated_d{d}",
    )(*args)


def _dilated_attention(qkv_by_dilation):
    state = None
    order = sorted(DIL_CONFIGS, key=lambda wd: -wd[1])
    for n, (window, d) in enumerate(order):
        assert window // d == BAND
        state = _dilated_branch(*qkv_by_dilation[d], state, d, last=(n == len(order) - 1))
    return state


def _outproj_ln_rows(a_refs, w_refs, x_ref, g_ref, b_ref):
    y = _dot(a_refs[0][...], w_refs[0][...])
    for a_ref, w_ref in zip(a_refs[1:], w_refs[1:]):
        y = y + _dot(a_ref[...], w_ref[...])
    return _layer_norm(DEEPNORM_ALPHA * x_ref[...] + y, g_ref[...], b_ref[...])


def _route_top2(h, wt_ref, sel_ref, gate_ref):
    logits = lax.dot_general(wt_ref[...], h, (((1,), (1,)), ((), ())),
                             preferred_element_type=F32, precision=lax.Precision.HIGHEST)
    idx = lax.broadcasted_iota(jnp.int32, logits.shape, 0)
    m1 = jnp.max(logits, axis=0, keepdims=True)
    i1 = jnp.min(jnp.where(logits == m1, idx, N_EXPERTS), axis=0, keepdims=True)
    pick1 = idx == i1
    rest = jnp.where(pick1, NEG_INF, logits)
    m2 = jnp.max(rest, axis=0, keepdims=True)
    i2 = jnp.min(jnp.where(rest == m2, idx, N_EXPERTS), axis=0, keepdims=True)
    pick2 = idx == i2
    e2 = jnp.exp(m2 - m1)
    w1 = 1.0 / (1.0 + e2)
    w2 = e2 / (1.0 + e2)
    sel_ref[...] = jnp.where(jnp.logical_or(pick1, pick2), 1.0, 0.0)
    gate_ref[...] = jnp.where(pick1, w1, 0.0) + jnp.where(pick2, w2, 0.0)


def _outproj_router_kernel(a_ref, w_ref, x_ref, g_ref, b_ref, wt_ref, o_ref, sel_ref, gate_ref):
    h = _outproj_ln_rows([a_ref], [w_ref], x_ref, g_ref, b_ref)
    o_ref[...] = h
    _route_top2(h, wt_ref, sel_ref, gate_ref)


def _outproj_router(a, w, x, g, b, w_router_t, tm):
    T = x.shape[0]
    row = pl.BlockSpec((tm, D_MODEL), lambda i: (i, 0))
    vec = pl.BlockSpec((1, D_MODEL), lambda i: (0, 0))
    route = pl.BlockSpec((N_EXPERTS, tm), lambda i: (0, i))
    route_shape = jax.ShapeDtypeStruct((N_EXPERTS, T), F32)
    return pl.pallas_call(
        _outproj_router_kernel,
        grid=(T // tm,),
        in_specs=[pl.BlockSpec((tm, a.shape[1]), lambda i: (i, 0)),
                  pl.BlockSpec(w.shape, lambda i: (0, 0)),
                  row, vec, vec,
                  pl.BlockSpec((N_EXPERTS, D_MODEL), lambda i: (0, 0))],
        out_specs=[row, route, route],
        out_shape=[jax.ShapeDtypeStruct((T, D_MODEL), F32), route_shape, route_shape],
        compiler_params=_cparams(("parallel",), 48),
        name="outproj_router",
    )(a, w, x, g, b, w_router_t)


def _ffn_chunks(width):
    chunks, lo = [], 0
    while lo < width:
        size = min(2 * MXU_DIM, width - lo)
        chunks.append((lo, size))
        lo += size
    return chunks


def _dense_ffn_kernel(a0_ref, a1_ref, w0_ref, w1_ref, r_ref, g0_ref, b0_ref,
                      wg_ref, wu_ref, wd_ref, g_ref, b_ref, o_ref):
    x = _outproj_ln_rows([a0_ref, a1_ref], [w0_ref, w1_ref], r_ref, g0_ref, b0_ref)
    xb = x.astype(BF16)
    y = None
    for lo, size in _ffn_chunks(wg_ref.shape[1]):
        gate = _dot(xb, wg_ref[:, lo:lo + size])
        up = _dot(xb, wu_ref[:, lo:lo + size])
        act = (gate * jax.nn.sigmoid(gate) * up).astype(BF16)
        part = _dot(act, wd_ref[lo:lo + size, :])
        y = part if y is None else y + part
    o_ref[...] = _layer_norm(DEEPNORM_ALPHA * x + y, g_ref[...], b_ref[...])


def _mixer_out_dense_ffn(acts, ws, resid, g_mix, b_mix, wg, wu, wd, g, b, tm):
    T = resid.shape[0]
    F = wg.shape[1]
    row = pl.BlockSpec((tm, D_MODEL), lambda i: (i, 0))
    vec = pl.BlockSpec((1, D_MODEL), lambda i: (0, 0))
    once = pl.Buffered(1)
    const = lambda i: (0, 0)
    return pl.pallas_call(
        _dense_ffn_kernel,
        grid=(T // tm,),
        in_specs=[pl.BlockSpec((tm, a.shape[1]), lambda i: (i, 0)) for a in acts]
        + [pl.BlockSpec(w.shape, const, pipeline_mode=once) for w in ws]
        + [row, vec, vec,
           pl.BlockSpec((D_MODEL, F), const, pipeline_mode=once),
           pl.BlockSpec((D_MODEL, F), const, pipeline_mode=once),
           pl.BlockSpec((F, D_MODEL), const, pipeline_mode=once),
           vec, vec],
        out_specs=row,
        out_shape=jax.ShapeDtypeStruct((T, D_MODEL), F32),
        compiler_params=_cparams(("parallel",), 56),
        name="mixer_out_dense_ffn",
    )(*acts, *ws, resid, g_mix, b_mix, wg, wu, wd, g, b)


CONV_PAD = 8


def _proj1_kernel(x_ref, wqk_ref, wv_ref, wog_ref, wgt_ref, wconv_ref,
                  q_ref, k_ref, v_ref, og_ref, gt_ref, buf):
    i = pl.program_id(1)
    tm = x_ref.shape[0]
    xb = x_ref[...].astype(BF16)
    kscale = MLSTM_HEAD_DIM ** -0.5

    @pl.when(i == 0)
    def _():
        buf[0:CONV_PAD, :] = jnp.zeros((CONV_PAD, buf.shape[1]), F32)

    wide = 2 * MXU_DIM
    for c in range(2 * D_MODEL // wide):
        lanes = slice(c * wide, (c + 1) * wide)
        buf[CONV_PAD:CONV_PAD + tm, lanes] = _dot(xb, wqk_ref[:, lanes])
    for c in range(2 * D_MODEL // LANES):
        lanes = slice(c * LANES, (c + 1) * LANES)
        y = None
        for tap in range(CONV_WIDTH):
            off = CONV_PAD - (CONV_WIDTH - 1) + tap
            term = buf[off:off + tm, lanes] * wconv_ref[tap:tap + 1, lanes]
            y = term if y is None else y + term
        y = y * jax.nn.sigmoid(y)
        buf[0:CONV_PAD, lanes] = buf[tm:tm + CONV_PAD, lanes]
        if c < D_MODEL // LANES:
            q_ref[:, lanes] = y.astype(BF16)
        else:
            k_ref[:, c * LANES - D_MODEL:(c + 1) * LANES - D_MODEL] = (y * kscale).astype(BF16)
    for c in range(D_MODEL // wide):
        lanes = slice(c * wide, (c + 1) * wide)
        v_ref[:, lanes] = _dot(xb, wv_ref[:, lanes]).astype(BF16)
        og_ref[:, lanes] = _dot(xb, wog_ref[:, lanes])
    gt_ref[...] = _dot(xb, wgt_ref[...])


def _proj1(x, wqk, wv, wog, wgt, wconv, tm):
    B, S, _ = x.shape
    row = lambda b, i: (b, i, 0)
    const = lambda b, i: (0, 0)
    act_spec = pl.BlockSpec((None, tm, D_MODEL), row)
    act = jax.ShapeDtypeStruct((B, S, D_MODEL), BF16)
    return pl.pallas_call(
        _proj1_kernel,
        grid=(B, S // tm),
        in_specs=[act_spec,
                  pl.BlockSpec((D_MODEL, 2 * D_MODEL), const),
                  pl.BlockSpec((D_MODEL, D_MODEL), const),
                  pl.BlockSpec((D_MODEL, D_MODEL), const),
                  pl.BlockSpec((D_MODEL, 2 * N_HEADS), const),
                  pl.BlockSpec((CONV_WIDTH, 2 * D_MODEL), const)],
        out_specs=[act_spec, act_spec, act_spec, act_spec,
                   pl.BlockSpec((None, tm, 2 * N_HEADS), row)],
        out_shape=[act, act, act, jax.ShapeDtypeStruct((B, S, D_MODEL), F32),
                   jax.ShapeDtypeStruct((B, S, 2 * N_HEADS), F32)],
        scratch_shapes=[pltpu.VMEM((tm + CONV_PAD, 2 * D_MODEL), F32)],
        compiler_params=_cparams(("parallel", "arbitrary"), 56),
        name="proj1",
    )(x, wqk, wv, wog, wgt, wconv)


def _mlstm_kernel(q_ref, k_ref, v_ref, og_ref, gn_ref, gt_ref, bi_row, bf_row, bi_col, bf_col,
                  ng_ref, o_ref, c_sc, n_sc, m_sc, *, L):
    ci = pl.program_id(1)

    @pl.when(ci == 0)
    def _():
        c_sc[...] = jnp.zeros(c_sc.shape, F32)
        n_sc[...] = jnp.zeros(n_sc.shape, F32)
        m_sc[...] = jnp.full(m_sc.shape, NEG_INF, F32)

    row = lax.broadcasted_iota(jnp.int32, (L, L), 0)
    col = lax.broadcasted_iota(jnp.int32, (L, L), 1)
    causal = col <= row
    lower = jnp.where(causal, 1.0, 0.0).astype(BF16)
    upper = jnp.where(row <= col, 1.0, 0.0).astype(BF16)

    gn = gn_ref[...]
    gt = gt_ref[...]
    i_col = gn[:, :N_HEADS] + bi_row[...]
    lf_col = _log_sigmoid(gn[:, N_HEADS:] + bf_row[...])
    i_row = gt[:N_HEADS, :] + bi_col[...]
    lf_row = _log_sigmoid(gt[N_HEADS:, :] + bf_col[...])
    b_col = sum(_dot(lower, part) for part in _split3(lf_col))
    b_row = sum(_dot(part, upper) for part in _split3(lf_row))

    keep = row <= col
    n_pad = jnp.zeros((2 * N_HEADS - 3, MLSTM_HEAD_DIM), F32)
    for h in range(N_HEADS):
        lanes = slice(h * MLSTM_HEAD_DIM, (h + 1) * MLSTM_HEAD_DIM)
        q = q_ref[:, lanes]
        k = k_ref[:, lanes]
        v = v_ref[:, lanes]
        bt = b_row[h:h + 1, :]
        key_term = i_col[:, h:h + 1] - b_col[:, h:h + 1]
        m_prev = m_sc[h]
        ct_prev = c_sc[h]
        n_prev = n_sc[h]

        dlog = jnp.where(keep, bt + key_term, NEG_INF)
        inter = bt + m_prev
        m_t = jnp.maximum(inter, jnp.max(dlog, axis=0, keepdims=True))
        s = _dot_nt(k, q) * jnp.exp(dlog - m_t)
        inter_w = jnp.exp(inter - m_t)
        num = _dot_tn(v, s.astype(BF16)) + inter_w * _dot_nt(ct_prev.astype(BF16), q)
        n_parts = jnp.concatenate([p.astype(F32) for p in _split3(n_prev)] + [n_pad], axis=0)
        qn = jnp.sum(_dot_nt(n_parts.astype(BF16), q), axis=0, keepdims=True)
        den = jnp.sum(s, axis=0, keepdims=True) + inter_w * qn
        hh = num / jnp.maximum(jnp.abs(den), jnp.exp(-m_t))

        b_last = bt[:, L - 1:L]
        g = b_last + key_term
        m_new = jnp.maximum(b_last + m_prev, jnp.max(g, axis=0, keepdims=True))
        w = jnp.exp(g - m_new)
        decay = jnp.exp(b_last + m_prev - m_new)
        kw = k.astype(F32) * w
        c_sc[h] = decay * ct_prev + _dot_tn(v, kw.astype(BF16))
        n_sc[h] = decay * n_prev + jnp.sum(kw, axis=0, keepdims=True)
        m_sc[h] = m_new

        mu = jnp.mean(hh, axis=0, keepdims=True)
        hc = hh - mu
        var = jnp.mean(hc * hc, axis=0, keepdims=True)
        hn = (hc * lax.rsqrt(var + LN_EPS)).T * ng_ref[:, lanes]
        o_ref[:, lanes] = (hn * jax.nn.sigmoid(og_ref[:, lanes])).astype(BF16)


def _mlstm(q, k, v, og, gates, gates_t, b_i, b_f, norm_g, L):
    B, S, _ = q.shape
    row = lambda b, c: (b, c, 0)
    const = lambda b, c: (0, 0)
    act_spec = pl.BlockSpec((None, L, D_MODEL), row)
    return pl.pallas_call(
        functools.partial(_mlstm_kernel, L=L),
        grid=(B, S // L),
        in_specs=[act_spec, act_spec, act_spec, act_spec,
                  pl.BlockSpec((None, L, 2 * N_HEADS), row),
                  pl.BlockSpec((None, 2 * N_HEADS, L), lambda b, c: (b, 0, c)),
                  pl.BlockSpec((1, N_HEADS), const), pl.BlockSpec((1, N_HEADS), const),
                  pl.BlockSpec((N_HEADS, 1), const), pl.BlockSpec((N_HEADS, 1), const),
                  pl.BlockSpec((1, D_MODEL), const)],
        out_specs=act_spec,
        out_shape=jax.ShapeDtypeStruct((B, S, D_MODEL), BF16),
        scratch_shapes=[pltpu.VMEM((N_HEADS, MLSTM_HEAD_DIM, MLSTM_HEAD_DIM), F32),
                        pltpu.VMEM((N_HEADS, 1, MLSTM_HEAD_DIM), F32),
                        pltpu.VMEM((N_HEADS, 1, 1), F32)],
        compiler_params=_cparams(("parallel", "arbitrary"), 48),
        name="mlstm",
    )(q, k, v, og, gates, gates_t, b_i[None, :], b_f[None, :], b_i[:, None], b_f[:, None],
      norm_g[None, :])


MOE_TOP_K = 2
MOE_TILE = 512
RANK_BLOCK = 1024


def _rank_kernel(sel_ref, rank_ref, count_ref, upper_sc, carry_sc):
    n = sel_ref.shape[1]

    @pl.when(pl.program_id(0) == 0)
    def _():
        row = lax.broadcasted_iota(jnp.int32, (n, n), 0)
        col = lax.broadcasted_iota(jnp.int32, (n, n), 1)
        upper_sc[...] = jnp.where(row <= col, 1.0, 0.0).astype(BF16)
        carry_sc[...] = jnp.zeros(carry_sc.shape, F32)

    sel = sel_ref[...]
    incl = _dot(sel.astype(BF16), upper_sc[...])
    rank_ref[...] = carry_sc[...] + incl - sel
    carry_sc[...] = carry_sc[...] + incl[:, n - 1:n]
    count_ref[...] = carry_sc[...]


def _rank(sel_t):
    E, T = sel_t.shape
    n = min(RANK_BLOCK, T)
    return pl.pallas_call(
        _rank_kernel,
        grid=(T // n,),
        in_specs=[pl.BlockSpec((E, n), lambda i: (0, i))],
        out_specs=[pl.BlockSpec((E, n), lambda i: (0, i)), pl.BlockSpec((E, 1), lambda i: (0, 0))],
        out_shape=[jax.ShapeDtypeStruct((E, T), F32), jax.ShapeDtypeStruct((E, 1), F32)],
        scratch_shapes=[pltpu.VMEM((n, n), BF16), pltpu.VMEM((E, 1), F32)],
        compiler_params=_cparams(("arbitrary",), 32),
        name="moe_rank",
    )(sel_t)


def _slot_kernel(sel_ref, gate_ref, rank_ref, off_ref, pos_ref, w_ref):
    sel = sel_ref[...] > 0.0
    idx = lax.broadcasted_iota(jnp.int32, sel.shape, 0)
    first = jnp.min(jnp.where(sel, idx, N_EXPERTS), axis=0, keepdims=True)
    second = jnp.max(jnp.where(sel, idx, -1), axis=0, keepdims=True)
    slot = off_ref[...] + rank_ref[...]
    gate = gate_ref[...]
    rows_p, rows_w = [], []
    for which in (first, second):
        hit = idx == which
        rows_p.append(jnp.sum(jnp.where(hit, slot, 0.0), axis=0, keepdims=True))
        rows_w.append(jnp.sum(jnp.where(hit, gate, 0.0), axis=0, keepdims=True))
    pos_ref[...] = jnp.concatenate(rows_p, axis=0).astype(jnp.int32)
    w_ref[...] = jnp.concatenate(rows_w, axis=0)


def _slots(sel_t, gate_t, rank_t, offsets_col, tm):
    E, T = sel_t.shape
    spec = pl.BlockSpec((E, tm), lambda i: (0, i))
    out_spec = pl.BlockSpec((MOE_TOP_K, tm), lambda i: (0, i))
    return pl.pallas_call(
        _slot_kernel,
        grid=(T // tm,),
        in_specs=[spec, spec, spec, pl.BlockSpec((E, 1), lambda i: (0, 0))],
        out_specs=[out_spec, out_spec],
        out_shape=[jax.ShapeDtypeStruct((MOE_TOP_K, T), jnp.int32),
                   jax.ShapeDtypeStruct((MOE_TOP_K, T), F32)],
        compiler_params=_cparams(("parallel",), 32),
        name="moe_slots",
    )(sel_t, gate_t, rank_t, offsets_col)


def _dispatch_kernel(tail_ref, used_ref, pos_ref, x_ref, xs_hbm, zero_sc, sem, zsem, *, tm):
    i = pl.program_id(0)
    n_tiles = xs_hbm.shape[0] // MOE_TILE

    def fill(tile):
        return pltpu.make_async_copy(zero_sc, xs_hbm.at[pl.ds(tile * MOE_TILE, MOE_TILE)], zsem)

    @pl.when(i == 0)
    def _():
        zero_sc[...] = jnp.zeros(zero_sc.shape, F32)
        for e in range(N_EXPERTS):
            fill(tail_ref[e]).start()
        for e in range(N_EXPERTS):
            fill(tail_ref[e]).wait()
        for j in range(N_EXPERTS):
            @pl.when(n_tiles - 1 - j >= used_ref[0])
            def _():
                c = fill(n_tiles - 1 - j)
                c.start()
                c.wait()

    def issue(t, carry):
        for k in range(MOE_TOP_K):
            pltpu.make_async_copy(x_ref.at[pl.ds(t, 1)], xs_hbm.at[pl.ds(pos_ref[k, t], 1)],
                                  sem).start(priority=k)
        return carry

    lax.fori_loop(0, tm, issue, 0, unroll=8)
    for _ in range(MOE_TOP_K):
        pltpu.make_async_copy(x_ref, xs_hbm.at[pl.ds(0, tm)], sem).wait()


def _dispatch(x, pos_t, tail_tiles, n_used, n_rows, tm):
    T = x.shape[0]
    n_steps = T // tm
    pos3 = pos_t.reshape(MOE_TOP_K, n_steps, tm).transpose(1, 0, 2)
    return pl.pallas_call(
        functools.partial(_dispatch_kernel, tm=tm),
        grid_spec=pltpu.PrefetchScalarGridSpec(
            num_scalar_prefetch=2,
            grid=(n_steps,),
            in_specs=[pl.BlockSpec((None, MOE_TOP_K, tm), lambda i, tail, used: (i, 0, 0),
                                   memory_space=pltpu.SMEM),
                      pl.BlockSpec((tm, D_MODEL), lambda i, tail, used: (i, 0))],
            out_specs=pl.BlockSpec(memory_space=pl.ANY),
            scratch_shapes=[pltpu.VMEM((MOE_TILE, D_MODEL), F32),
                            pltpu.SemaphoreType.DMA, pltpu.SemaphoreType.DMA],
        ),
        out_shape=jax.ShapeDtypeStruct((n_rows, D_MODEL), F32),
        compiler_params=_cparams(("arbitrary",), 32),
        name="moe_dispatch",
    )(tail_tiles, n_used, pos3, x)


def _expert_ffn_kernel(te_ref, used_ref, x_ref, wg_ref, wu_ref, wd_ref, o_ref, xb_sc):
    i = pl.program_id(0)
    f = pl.program_id(1)
    live = i < used_ref[0]

    @pl.when(jnp.logical_and(live, f == 0))
    def _():
        xb_sc[...] = x_ref[...].astype(BF16)

    @pl.when(live)
    def _():
        xb = xb_sc[...]
        part = None
        for lo, size in _ffn_chunks(wg_ref.shape[1]):
            gate = _dot(xb, wg_ref[:, lo:lo + size])
            up = _dot(xb, wu_ref[:, lo:lo + size])
            act = (gate * jax.nn.sigmoid(gate) * up).astype(BF16)
            p = _dot(act, wd_ref[lo:lo + size, :])
            part = p if part is None else part + p

        @pl.when(f == 0)
        def _():
            o_ref[...] = part

        @pl.when(f > 0)
        def _():
            o_ref[...] += part

    @pl.when(jnp.logical_and(jnp.logical_not(live), f == 0))
    def _():
        o_ref[...] = jnp.zeros(o_ref.shape, F32)


def _expert_ffn(xs, tile_expert, n_used, wg, wu, wd, fc):
    n_rows = xs.shape[0]
    E, _, F = wg.shape
    tm = MOE_TILE

    def row(i, f, te, used):
        return (jnp.minimum(i, used[0] - 1), 0)

    return pl.pallas_call(
        _expert_ffn_kernel,
        grid_spec=pltpu.PrefetchScalarGridSpec(
            num_scalar_prefetch=2,
            grid=(n_rows // tm, F // fc),
            in_specs=[pl.BlockSpec((tm, D_MODEL), row),
                      pl.BlockSpec((None, D_MODEL, fc), lambda i, f, te, used: (te[i], 0, f)),
                      pl.BlockSpec((None, D_MODEL, fc), lambda i, f, te, used: (te[i], 0, f)),
                      pl.BlockSpec((None, fc, D_MODEL), lambda i, f, te, used: (te[i], f, 0))],
            out_specs=pl.BlockSpec((tm, D_MODEL), lambda i, f, te, used: (i, 0)),
            scratch_shapes=[pltpu.VMEM((tm, D_MODEL), BF16)],
        ),
        out_shape=jax.ShapeDtypeStruct((n_rows, D_MODEL), F32),
        compiler_params=_cparams(("arbitrary", "arbitrary"), 56),
        name="moe_expert_ffn",
    )(tile_expert, n_used, xs, wg, wu, wd)


COMBINE_GROUP = 8


def _combine_kernel(pos_ref, posn_ref, w_ref, x_ref, g_ref, b_ref, ys_hbm, o_ref, rows_sc, sem,
                    *, tm):
    i = pl.program_id(0)
    n_steps = pl.num_programs(0)
    slot = lax.rem(i, 2)
    other = 1 - slot
    grp = COMBINE_GROUP
    n_grp = tm // grp
    half = n_grp // 2

    def issue(p_ref, dst, t0, count):
        for j in range(count):
            for k in range(MOE_TOP_K):
                pltpu.make_async_copy(ys_hbm.at[pl.ds(p_ref[k, t0 + j], 1)],
                                      rows_sc.at[dst, k, pl.ds(t0 + j, 1)],
                                      sem.at[dst]).start(priority=k)

    def combine(t0):
        rows = pl.ds(t0, grp)
        w = w_ref[rows, :]
        y = w[:, 0:1] * rows_sc[slot, 0, rows, :] + w[:, 1:2] * rows_sc[slot, 1, rows, :]
        o_ref[rows, :] = _layer_norm(DEEPNORM_ALPHA * x_ref[rows, :] + y, g_ref[...], b_ref[...])

    @pl.when(i == 0)
    def _():
        def first(gi, carry):
            issue(pos_ref, 0, pl.multiple_of(gi * grp, grp), grp)
            return carry
        lax.fori_loop(0, n_grp, first, 0)

    for k in range(MOE_TOP_K):
        pltpu.make_async_copy(ys_hbm.at[pl.ds(0, tm)], rows_sc.at[slot, k], sem.at[slot]).wait()

    @pl.when(i + 1 < n_steps)
    def _():
        def both(gi, carry):
            issue(posn_ref, other, pl.multiple_of(gi * 2 * grp, 2 * grp), 2 * grp)
            combine(pl.multiple_of(gi * grp, grp))
            return carry
        lax.fori_loop(0, half, both, 0)

    @pl.when(i + 1 == n_steps)
    def _():
        def only(gi, carry):
            combine(pl.multiple_of(gi * grp, grp))
            return carry
        lax.fori_loop(0, half, only, 0)

    def rest(gi, carry):
        combine(pl.multiple_of(gi * grp, grp))
        return carry
    lax.fori_loop(half, n_grp, rest, 0)


def _combine_ln(ys, pos_t, w_nat, x, g, b, tm):
    T = x.shape[0]
    n_steps = T // tm
    pos3 = pos_t.reshape(MOE_TOP_K, n_steps, tm).transpose(1, 0, 2)
    row = pl.BlockSpec((tm, D_MODEL), lambda i: (i, 0))
    vec = pl.BlockSpec((1, D_MODEL), lambda i: (0, 0))
    return pl.pallas_call(
        functools.partial(_combine_kernel, tm=tm),
        grid=(n_steps,),
        in_specs=[pl.BlockSpec((None, MOE_TOP_K, tm), lambda i: (i, 0, 0), memory_space=pltpu.SMEM),
                  pl.BlockSpec((None, MOE_TOP_K, tm),
                               lambda i: (jnp.minimum(i + 1, n_steps - 1), 0, 0),
                               memory_space=pltpu.SMEM),
                  pl.BlockSpec((tm, MOE_TOP_K), lambda i: (i, 0)),
                  row, vec, vec,
                  pl.BlockSpec(memory_space=pl.ANY)],
        out_specs=row,
        out_shape=jax.ShapeDtypeStruct((T, D_MODEL), F32),
        scratch_shapes=[pltpu.VMEM((2, MOE_TOP_K, tm, D_MODEL), F32), pltpu.SemaphoreType.DMA((2,))],
        compiler_params=_cparams(("arbitrary",), 32),
        name="moe_combine_ln",
    )(pos3, pos3, w_nat, x, g, b, ys)


def _moe_sparse(x, sel_t, gate_t, wg, wu, wd, g, b):
    T = x.shape[0]
    tile = MOE_TILE
    n_rows = MOE_TOP_K * T + N_EXPERTS * tile
    n_tiles = n_rows // tile
    rank_t, counts = _rank(sel_t)
    tiles_per = jnp.ceil(counts[:, 0] / tile).astype(jnp.int32)
    tile_end = jnp.cumsum(tiles_per)
    tile_start = tile_end - tiles_per
    n_used = tile_end[-1:]
    tile_expert = jnp.minimum(
        jnp.sum(jnp.arange(n_tiles, dtype=jnp.int32)[:, None] >= tile_end[None, :], axis=1),
        N_EXPERTS - 1).astype(jnp.int32)
    tail_tiles = jnp.maximum(tile_end - 1, 0).astype(jnp.int32)
    offsets_col = (tile_start * tile).astype(F32)[:, None]
    pos_t, w_t = _slots(sel_t, gate_t, rank_t, offsets_col, tm=1024)
    xs = _dispatch(x, pos_t, tail_tiles, n_used, n_rows, tm=512)
    ys = _expert_ffn(xs, tile_expert, n_used, wg, wu, wd, fc=D_FF_EXPERT // 2)
    return _combine_ln(ys, pos_t, w_t.T, x, g, b, tm=256)


def _rotary_tables(S):
    pos = jnp.arange(S, dtype=jnp.int32)
    inv = ROPE_THETA ** (-jnp.arange(ROPE_HALF, dtype=F32) / ROPE_HALF)
    ang = pos.astype(F32)[:, None] * inv[None, :]
    cos, sin = jnp.cos(ang), jnp.sin(ang)
    ones = jnp.ones((S, HEAD_DIM - ROPE_DIMS), F32)
    zeros = jnp.zeros((S, HEAD_DIM - ROPE_DIMS), F32)
    z8 = jnp.zeros((S, ROPE_HALF), F32)
    cos_h = jnp.concatenate([cos, cos, ones], axis=1)
    sup_h = jnp.concatenate([-sin, z8, zeros], axis=1)
    sdn_h = jnp.concatenate([z8, sin, zeros], axis=1)
    rep = LANES // HEAD_DIM
    return (jnp.tile(cos_h, (1, rep)), jnp.tile(sup_h, (1, rep)), jnp.tile(sdn_h, (1, rep)))


def _even_layer(h, w_in, b_forget, w_out, ln_mix_g, ln_mix_b, w_gate, w_up, w_down,
                ln_ffn_g, ln_ffn_b):
    B, S, _ = h.shape
    T = B * S
    W = ATT_WIDTH
    w_main = jnp.concatenate([w_in[:, :3 * W], w_in[:, 3 * W + N_HEADS:]], axis=1).astype(BF16)
    wf_t = w_in[:, 3 * W:3 * W + N_HEADS].T.astype(BF16)
    cos_t, sup_t, sdn_t = _rotary_tables(S)
    outs = _proj0(h, w_main, wf_t, cos_t, sup_t, sdn_t, tm=min(512, S))
    qa, ka, va = outs[:3]
    qkv_by_dilation = {d: tuple(outs[3 + 3 * n:6 + 3 * n]) for n, d in enumerate(DILATIONS)}
    f_t = outs[-1]

    bias_col = jnp.tile(b_forget, B)[:, None]
    c = _fox_cumsum(f_t.reshape(B * N_HEADS, S), bias_col).reshape(B, N_HEADS, S)
    o_fox = _fox_attention(qa, ka, va, jnp.swapaxes(c, 1, 2), t=min(512, S))
    o_dil = _dilated_attention(qkv_by_dilation)

    w_out_b = w_out.astype(BF16)
    tm = 512
    h2 = _mixer_out_dense_ffn(
        [o_fox.reshape(T, W), o_dil.reshape(T, W)], [w_out_b[:W], w_out_b[W:]],
        h.reshape(T, D_MODEL), ln_mix_g[None, :], ln_mix_b[None, :],
        w_gate.astype(BF16), w_up.astype(BF16), w_down.astype(BF16),
        ln_ffn_g[None, :], ln_ffn_b[None, :], tm)
    return h2.reshape(B, S, D_MODEL)


def _odd_layer(h, w_in, b_igate, b_fgate, w_conv, norm_g, w_out, ln_mix_g, ln_mix_b, w_router,
               w_gate, w_up, w_down, ln_ffn_g, ln_ffn_b):
    B, S, _ = h.shape
    T = B * S
    D = D_MODEL
    wqk = w_in[:, :2 * D].astype(BF16)
    wv = w_in[:, 2 * D:3 * D].astype(BF16)
    wgt = w_in[:, 3 * D:3 * D + 2 * N_HEADS].astype(BF16)
    wog = w_in[:, 3 * D + 2 * N_HEADS:].astype(BF16)
    q, k, v, og, gates = _proj1(h, wqk, wv, wog, wgt, w_conv, tm=min(512, S))
    gates_t = jnp.swapaxes(gates, 1, 2)
    hm = _mlstm(q, k, v, og, gates, gates_t, b_igate, b_fgate, norm_g, L=min(256, S))

    tm = 512
    h1, sel_t, gate_t = _outproj_router(hm.reshape(T, D), w_out.astype(BF16), h.reshape(T, D),
                                        ln_mix_g[None, :], ln_mix_b[None, :], w_router.T, tm)
    h2 = _moe_sparse(h1, sel_t, gate_t, w_gate.astype(BF16), w_up.astype(BF16),
                     w_down.astype(BF16), ln_ffn_g[None, :], ln_ffn_b[None, :])
    return h2.reshape(B, S, D)


def kernel(x, w_in_e, b_forget_e, w_out_e, ln_mix_g_e, ln_mix_b_e, ffn_w_gate_e, ffn_w_up_e,
           ffn_w_down_e, ln_ffn_g_e, ln_ffn_b_e, w_in_o, b_igate_o, b_fgate_o, w_conv_o,
           mlstm_norm_g_o, w_out_o, ln_mix_g_o, ln_mix_b_o, w_router_o, moe_w_gate_o,
           moe_w_up_o, moe_w_down_o, ln_ffn_g_o, ln_ffn_b_o):
    h = x
    for layer in range(DEPTH):
        i = layer // 2
        if layer % 2 == 0:
            h = _even_layer(h, w_in_e[i], b_forget_e[i], w_out_e[i], ln_mix_g_e[i], ln_mix_b_e[i],
                            ffn_w_gate_e[i], ffn_w_up_e[i], ffn_w_down_e[i], ln_ffn_g_e[i],
                            ln_ffn_b_e[i])
        else:
            h = _odd_layer(h, w_in_o[i], b_igate_o[i], b_fgate_o[i], w_conv_o[i],
                           mlstm_norm_g_o[i], w_out_o[i], ln_mix_g_o[i], ln_mix_b_o[i],
                           w_router_o[i], moe_w_gate_o[i], moe_w_up_o[i], moe_w_down_o[i],
                           ln_ffn_g_o[i], ln_ffn_b_o[i])
    return h
```

```python
import functools
import math

import jax
import jax.numpy as jnp
from jax import lax
from jax.experimental import pallas as pl
from jax.experimental.pallas import tpu as pltpu

F32 = jnp.float32
BF16 = jnp.bfloat16

D_MODEL = 1024
HEAD_DIM = 64
N_HEADS = 8
ATT_WIDTH = N_HEADS * HEAD_DIM
DIL_CONFIGS = ((128, 1), (512, 4), (2048, 16))
ROPE_THETA = 500000.0
ROPE_DIMS = HEAD_DIM // 4
ROPE_HALF = ROPE_DIMS // 2
MLSTM_HEAD_DIM = D_MODEL // N_HEADS
CONV_WIDTH = 4
D_FF_DENSE = 2816
N_EXPERTS = 8
D_FF_EXPERT = 3584
DEPTH = 2
DEEPNORM_ALPHA = (2 * DEPTH) ** 0.25
LN_EPS = 1e-5

LANES = 128
MXU_DIM = 256
BAND = 128
MIB = 1024 * 1024

NEG_INF = float("-inf")


def _cparams(semantics, vmem_mib):
    return pltpu.CompilerParams(dimension_semantics=semantics, vmem_limit_bytes=vmem_mib * MIB)


def _dot(a, b):
    return jnp.dot(a, b, preferred_element_type=F32)


def _dot_nt(a, b):
    return lax.dot_general(a, b, (((1,), (1,)), ((), ())), preferred_element_type=F32)


def _dot_tn(a, b):
    return lax.dot_general(a, b, (((0,), (0,)), ((), ())), preferred_element_type=F32)


def _split3(x):
    hi = x.astype(BF16)
    r = x - hi.astype(F32)
    mid = r.astype(BF16)
    lo = (r - mid.astype(F32)).astype(BF16)
    return hi, mid, lo


def _log_sigmoid(z):
    return -(jnp.maximum(-z, 0.0) + jnp.log1p(jnp.exp(-jnp.abs(z))))


def _layer_norm(z, g, b):
    mu = jnp.mean(z, axis=-1, keepdims=True)
    zc = z - mu
    var = jnp.mean(zc * zc, axis=-1, keepdims=True)
    return zc * lax.rsqrt(var + LN_EPS) * g + b


DILATIONS = tuple(sorted(d for _, d in DIL_CONFIGS))
N_PROJ0 = 6


def _proj0_kernel(x_ref, w_ref, wf_ref, cos_ref, sup_ref, sdn_ref, *refs):
    tm = x_ref.shape[0]
    outs = refs[:N_PROJ0]
    strided = refs[N_PROJ0:-2]
    ft_ref, row_sc = refs[-2:]
    xb = x_ref[...].astype(BF16)
    scale = HEAD_DIM ** -0.5
    for j, o_ref in enumerate(outs):
        full = _dot(xb, w_ref[:, j * ATT_WIDTH:(j + 1) * ATT_WIDTH])
        for c in range(ATT_WIDTH // LANES):
            r = full[:, c * LANES:(c + 1) * LANES]
            if j in (3, 4):
                r = (r * cos_ref[...]
                     + pltpu.roll(r, LANES - ROPE_HALF, axis=1) * sup_ref[...]
                     + pltpu.roll(r, ROPE_HALF, axis=1) * sdn_ref[...])
            if j in (0, 3):
                r = r * scale
            o_ref[:, c * LANES:(c + 1) * LANES] = r.astype(BF16)
            if j >= 3:
                row_sc[...] = r
                for n, d in enumerate(DILATIONS[1:]):
                    s_ref = strided[3 * n + (j - 3)]
                    for res in range(d):
                        lo = res * ATT_WIDTH + c * LANES
                        s_ref[:, lo:lo + LANES] = row_sc[pl.ds(res, tm // d, stride=d), :].astype(BF16)
    ft_ref[...] = _dot_nt(wf_ref[...], xb)


def _proj0(x, w_main, wf_t, cos_t, sup_t, sdn_t, tm):
    B, S, _ = x.shape
    n_i = S // tm
    act = jax.ShapeDtypeStruct((B, S, ATT_WIDTH), BF16)
    act_spec = pl.BlockSpec((None, tm, ATT_WIDTH), lambda b, i: (b, i, 0))
    tab_spec = pl.BlockSpec((tm, LANES), lambda b, i: (i, 0))
    out_specs = [act_spec] * N_PROJ0
    out_shape = [act] * N_PROJ0
    for d in DILATIONS[1:]:
        out_specs += [pl.BlockSpec((None, tm // d, d * ATT_WIDTH), lambda b, i: (b, i, 0))] * 3
        out_shape += [jax.ShapeDtypeStruct((B, S // d, d * ATT_WIDTH), BF16)] * 3
    out_specs.append(pl.BlockSpec((None, N_HEADS, tm), lambda b, i: (b, 0, i)))
    out_shape.append(jax.ShapeDtypeStruct((B, N_HEADS, S), F32))
    return pl.pallas_call(
        _proj0_kernel,
        grid=(B, n_i),
        in_specs=[
            pl.BlockSpec((None, tm, D_MODEL), lambda b, i: (b, i, 0)),
            pl.BlockSpec((D_MODEL, N_PROJ0 * ATT_WIDTH), lambda b, i: (0, 0)),
            pl.BlockSpec((N_HEADS, D_MODEL), lambda b, i: (0, 0)),
            tab_spec, tab_spec, tab_spec,
        ],
        out_specs=out_specs,
        out_shape=out_shape,
        scratch_shapes=[pltpu.VMEM((tm, LANES), F32)],
        compiler_params=_cparams(("parallel", "parallel"), 48),
        name="proj0",
    )(x, w_main, wf_t, cos_t, sup_t, sdn_t)


def _fox_cumsum_kernel(f_ref, bias_ref, c_ref):
    S = f_ref.shape[1]
    lf = _log_sigmoid(f_ref[...] + bias_ref[...])
    row = lax.broadcasted_iota(jnp.int32, (S, S), 0)
    col = lax.broadcasted_iota(jnp.int32, (S, S), 1)
    upper = jnp.where(row <= col, 1.0, 0.0).astype(BF16)
    hi, mid, lo = _split3(lf)
    c_ref[...] = _dot(hi, upper) + _dot(mid, upper) + _dot(lo, upper)


def _fox_cumsum(f_t, bias_col):
    R, S = f_t.shape
    return pl.pallas_call(
        _fox_cumsum_kernel,
        grid=(1,),
        in_specs=[pl.BlockSpec((R, S), lambda i: (0, 0)), pl.BlockSpec((R, 1), lambda i: (0, 0))],
        out_specs=pl.BlockSpec((R, S), lambda i: (0, 0)),
        out_shape=jax.ShapeDtypeStruct((R, S), F32),
        compiler_params=_cparams(("arbitrary",), 48),
        name="fox_cumsum",
    )(f_t, bias_col)


def _fox_kernel(q_ref, k_ref, v_ref, c_ref, o_ref, m_sc, l_sc, acc_sc, *, t):
    i = pl.program_id(1)
    j = pl.program_id(2)

    @pl.when(j == 0)
    def _():
        m_sc[...] = jnp.full(m_sc.shape, NEG_INF, F32)
        l_sc[...] = jnp.zeros(l_sc.shape, F32)
        acc_sc[...] = jnp.zeros(acc_sc.shape, F32)

    def step(masked):
        if masked:
            key = lax.broadcasted_iota(jnp.int32, (t, t), 0)
            qry = lax.broadcasted_iota(jnp.int32, (t, t), 1)
            keep = key <= qry
        for h in range(N_HEADS):
            sl = slice(h * HEAD_DIM, (h + 1) * HEAD_DIM)
            s = _dot_nt(k_ref[:, sl], q_ref[:, sl]) - c_ref[:, h:h + 1]
            if masked:
                s = jnp.where(keep, s, NEG_INF)
            m_prev = m_sc[h:h + 1, :]
            m_new = jnp.maximum(m_prev, jnp.max(s, axis=0, keepdims=True))
            alpha = jnp.exp(m_prev - m_new)
            p = jnp.exp(s - m_new)
            l_sc[h:h + 1, :] = alpha * l_sc[h:h + 1, :] + jnp.sum(p, axis=0, keepdims=True)
            acc_sc[sl, :] = alpha * acc_sc[sl, :] + _dot_tn(v_ref[:, sl], p.astype(BF16))
            m_sc[h:h + 1, :] = m_new

    @pl.when(j < i)
    def _():
        step(False)

    @pl.when(j == i)
    def _():
        step(True)
        out_t = jnp.concatenate(
            [acc_sc[h * HEAD_DIM:(h + 1) * HEAD_DIM, :] / l_sc[h:h + 1, :] for h in range(N_HEADS)],
            axis=0)
        o_ref[...] = out_t.T.astype(BF16)


def _fox_attention(q, k, v, c, t):
    B, S, _ = q.shape
    n = S // t
    q_spec = pl.BlockSpec((None, t, ATT_WIDTH), lambda b, i, j: (b, i, 0))
    kv_spec = pl.BlockSpec((None, t, ATT_WIDTH), lambda b, i, j: (b, jnp.minimum(j, i), 0))
    return pl.pallas_call(
        functools.partial(_fox_kernel, t=t),
        grid=(B, n, n),
        in_specs=[q_spec, kv_spec, kv_spec,
                  pl.BlockSpec((None, t, N_HEADS), lambda b, i, j: (b, jnp.minimum(j, i), 0))],
        out_specs=q_spec,
        out_shape=jax.ShapeDtypeStruct((B, S, ATT_WIDTH), BF16),
        scratch_shapes=[pltpu.VMEM((N_HEADS, t), F32), pltpu.VMEM((N_HEADS, t), F32),
                        pltpu.VMEM((ATT_WIDTH, t), F32)],
        compiler_params=_cparams(("parallel", "parallel", "arbitrary"), 48),
        name="fox_attention",
    )(q, k, v, c)


def _dil_kernel(*refs, d, rg, sub, tqu, has_prev, first, last):
    refs = list(refs)
    q_ref, k_ref, v_ref = refs[:3]
    pos = 3
    if has_prev:
        kp_ref, vp_ref = refs[pos:pos + 2]
        pos += 2
    if not first:
        acc_in, st_in = refs[pos:pos + 2]
        pos += 2
    if last:
        o_ref = refs[pos]
    else:
        acc_out, st_out = refs[pos:pos + 2]

    blk = pl.program_id(1)
    nk = tqu + (BAND if has_prev else 0)
    key = lax.broadcasted_iota(jnp.int32, (nk, tqu), 0)
    qry = lax.broadcasted_iota(jnp.int32, (nk, tqu), 1)
    if has_prev:
        keep = jnp.logical_and(key >= qry, key <= qry + BAND)
        keep_edge = jnp.logical_and(keep, jnp.logical_or(key >= BAND, blk > 0))
    else:
        keep = key <= qry
    pair = LANES // HEAD_DIM

    for a in range(sub):
        rows = slice(a * tqu, (a + 1) * tqu)
        for rr in range(rg):
            if d == 1:
                nat = rows
            else:
                nat = pl.ds(a * tqu * d + pl.program_id(2) * rg + rr, tqu, stride=d)
            if not first:
                st_old = st_in[nat, :].T
            ms, ls = [], []
            for slab in range(N_HEADS // pair):
                slab_lanes = slice(rr * ATT_WIDTH + slab * LANES, rr * ATT_WIDTH + (slab + 1) * LANES)
                if not first:
                    acc_old = acc_in[slab, nat, :].T
                outs = []
                for hh in range(pair):
                    h = slab * pair + hh
                    lo = rr * ATT_WIDTH + h * HEAD_DIM
                    sl = slice(lo, lo + HEAD_DIM)
                    q = q_ref[rows, sl]
                    if not has_prev:
                        k_cat, v_cat, mask = k_ref[rows, sl], v_ref[rows, sl], keep
                    elif a == 0:
                        k_cat = jnp.concatenate([kp_ref[:, sl], k_ref[rows, sl]], axis=0)
                        v_cat = jnp.concatenate([vp_ref[:, sl], v_ref[rows, sl]], axis=0)
                        mask = keep_edge
                    else:
                        krows = slice(a * tqu - BAND, (a + 1) * tqu)
                        k_cat, v_cat, mask = k_ref[krows, sl], v_ref[krows, sl], keep
                    s = jnp.where(mask, _dot_nt(k_cat, q), NEG_INF)
                    m = jnp.max(s, axis=0, keepdims=True)
                    if not first:
                        m_old = st_old[h:h + 1, :]
                        l_old = st_old[N_HEADS + h:N_HEADS + h + 1, :]
                        m_new = jnp.maximum(m, m_old)
                        alpha = jnp.exp(m_old - m_new)
                        m = m_new
                    p = jnp.exp(s - m)
                    l = jnp.sum(p, axis=0, keepdims=True)
                    acc = _dot_tn(v_cat, p.astype(BF16))
                    if not first:
                        l = l + alpha * l_old
                        acc = acc + alpha * acc_old[hh * HEAD_DIM:(hh + 1) * HEAD_DIM, :]
                    if last:
                        acc = acc / l
                    outs.append(acc)
                    ms.append(m)
                    ls.append(l)
                slab_out = jnp.concatenate(outs, axis=0).T
                if last:
                    o_ref[rows, slab_lanes] = slab_out.astype(BF16)
                else:
                    acc_out[slab, nat, :] = slab_out
            if not last:
                pad = jnp.zeros((LANES - 2 * N_HEADS, tqu), F32)
                st_out[nat, :] = jnp.concatenate(ms + ls + [pad], axis=0).T


def _dilated_branch(q, k, v, state, dilation, last):
    B, L, _ = q.shape
    d = dilation
    S = L * d
    first = state is None
    assert not (last and d != 1)
    rg = min(4, d)
    tqu = min(2 * BAND, L)
    tqb = min(L, 4 * tqu // rg)
    sub = tqb // tqu
    n_blk = L // tqb
    n_grp = d // rg
    has_prev = L > BAND
    wq = rg * ATT_WIDTH
    n_slab = ATT_WIDTH // LANES

    main = lambda b, i, g: (b, i, g)
    prev = lambda b, i, g: (b, jnp.maximum(i * (tqb // BAND) - 1, 0), g)
    qkv_spec = pl.BlockSpec((None, tqb, wq), main)
    acc_spec = pl.BlockSpec((None, n_slab, tqb * d, LANES), lambda b, i, g: (b, 0, i, 0))
    st_spec = pl.BlockSpec((None, tqb * d, LANES), lambda b, i, g: (b, i, 0))
    in_specs = [qkv_spec] * 3
    args = [q, k, v]
    if has_prev:
        in_specs += [pl.BlockSpec((None, BAND, wq), prev)] * 2
        args += [k, v]
    if not first:
        in_specs += [acc_spec, st_spec]
        args += list(state)
    if last:
        out_specs = pl.BlockSpec((None, tqb, wq), main)
        out_shape = jax.ShapeDtypeStruct((B, S, ATT_WIDTH), BF16)
    else:
        out_specs = [acc_spec, st_spec]
        out_shape = [jax.ShapeDtypeStruct((B, n_slab, S, LANES), F32),
                     jax.ShapeDtypeStruct((B, S, LANES), F32)]
    return pl.pallas_call(
        functools.partial(_dil_kernel, d=d, rg=rg, sub=sub, tqu=tqu, has_prev=has_prev, first=first,
                          last=last),
        grid=(B, n_blk, n_grp),
        in_specs=in_specs,
        out_specs=out_specs,
        out_shape=out_shape,
        compiler_params=_cparams(("parallel", "parallel", "arbitrary"), 48),
        name=f"dilated_d{d}",
    )(*args)


def _dilated_attention(qkv_by_dilation):
    state = None
    order = sorted(DIL_CONFIGS, key=lambda wd: -wd[1])
    for n, (window, d) in enumerate(order):
        assert window // d == BAND
        state = _dilated_branch(*qkv_by_dilation[d], state, d, last=(n == len(order) - 1))
    return state


def _outproj_ln_rows(a_refs, w_refs, x_ref, g_ref, b_ref):
    y = _dot(a_refs[0][...], w_refs[0][...])
    for a_ref, w_ref in zip(a_refs[1:], w_refs[1:]):
        y = y + _dot(a_ref[...], w_ref[...])
    return _layer_norm(DEEPNORM_ALPHA * x_ref[...] + y, g_ref[...], b_ref[...])


def _route_top2(h, wt_ref, sel_ref, gate_ref):
    logits = lax.dot_general(wt_ref[...], h, (((1,), (1,)), ((), ())),
                             preferred_element_type=F32, precision=lax.Precision.HIGHEST)
    idx = lax.broadcasted_iota(jnp.int32, logits.shape, 0)
    m1 = jnp.max(logits, axis=0, keepdims=True)
    i1 = jnp.min(jnp.where(logits == m1, idx, N_EXPERTS), axis=0, keepdims=True)
    pick1 = idx == i1
    rest = jnp.where(pick1, NEG_INF, logits)
    m2 = jnp.max(rest, axis=0, keepdims=True)
    i2 = jnp.min(jnp.where(rest == m2, idx, N_EXPERTS), axis=0, keepdims=True)
    pick2 = idx == i2
    e2 = jnp.exp(m2 - m1)
    w1 = 1.0 / (1.0 + e2)
    w2 = e2 / (1.0 + e2)
    sel_ref[...] = jnp.where(jnp.logical_or(pick1, pick2), 1.0, 0.0)
    gate_ref[...] = jnp.where(pick1, w1, 0.0) + jnp.where(pick2, w2, 0.0)


def _outproj_router_kernel(a_ref, w_ref, x_ref, g_ref, b_ref, wt_ref, o_ref, sel_ref, gate_ref):
    h = _outproj_ln_rows([a_ref], [w_ref], x_ref, g_ref, b_ref)
    o_ref[...] = h
    _route_top2(h, wt_ref, sel_ref, gate_ref)


def _outproj_router(a, w, x, g, b, w_router_t, tm):
    T = x.shape[0]
    row = pl.BlockSpec((tm, D_MODEL), lambda i: (i, 0))
    vec = pl.BlockSpec((1, D_MODEL), lambda i: (0, 0))
    route = pl.BlockSpec((N_EXPERTS, tm), lambda i: (0, i))
    route_shape = jax.ShapeDtypeStruct((N_EXPERTS, T), F32)
    return pl.pallas_call(
        _outproj_router_kernel,
        grid=(T // tm,),
        in_specs=[pl.BlockSpec((tm, a.shape[1]), lambda i: (i, 0)),
                  pl.BlockSpec(w.shape, lambda i: (0, 0)),
                  row, vec, vec,
                  pl.BlockSpec((N_EXPERTS, D_MODEL), lambda i: (0, 0))],
        out_specs=[row, route, route],
        out_shape=[jax.ShapeDtypeStruct((T, D_MODEL), F32), route_shape, route_shape],
        compiler_params=_cparams(("parallel",), 48),
        name="outproj_router",
    )(a, w, x, g, b, w_router_t)


def _ffn_chunks(width):
    chunks, lo = [], 0
    while lo < width:
        size = min(2 * MXU_DIM, width - lo)
        chunks.append((lo, size))
        lo += size
    return chunks


def _dense_ffn_kernel(a0_ref, a1_ref, w0_ref, w1_ref, r_ref, g0_ref, b0_ref,
                      wg_ref, wu_ref, wd_ref, g_ref, b_ref, o_ref):
    x = _outproj_ln_rows([a0_ref, a1_ref], [w0_ref, w1_ref], r_ref, g0_ref, b0_ref)
    xb = x.astype(BF16)
    y = None
    for lo, size in _ffn_chunks(wg_ref.shape[1]):
        gate = _dot(xb, wg_ref[:, lo:lo + size])
        up = _dot(xb, wu_ref[:, lo:lo + size])
        act = (gate * jax.nn.sigmoid(gate) * up).astype(BF16)
        part = _dot(act, wd_ref[lo:lo + size, :])
        y = part if y is None else y + part
    o_ref[...] = _layer_norm(DEEPNORM_ALPHA * x + y, g_ref[...], b_ref[...])


def _mixer_out_dense_ffn(acts, ws, resid, g_mix, b_mix, wg, wu, wd, g, b, tm):
    T = resid.shape[0]
    F = wg.shape[1]
    row = pl.BlockSpec((tm, D_MODEL), lambda i: (i, 0))
    vec = pl.BlockSpec((1, D_MODEL), lambda i: (0, 0))
    once = pl.Buffered(1)
    const = lambda i: (0, 0)
    return pl.pallas_call(
        _dense_ffn_kernel,
        grid=(T // tm,),
        in_specs=[pl.BlockSpec((tm, a.shape[1]), lambda i: (i, 0)) for a in acts]
        + [pl.BlockSpec(w.shape, const, pipeline_mode=once) for w in ws]
        + [row, vec, vec,
           pl.BlockSpec((D_MODEL, F), const, pipeline_mode=once),
           pl.BlockSpec((D_MODEL, F), const, pipeline_mode=once),
           pl.BlockSpec((F, D_MODEL), const, pipeline_mode=once),
           vec, vec],
        out_specs=row,
        out_shape=jax.ShapeDtypeStruct((T, D_MODEL), F32),
        compiler_params=_cparams(("parallel",), 56),
        name="mixer_out_dense_ffn",
    )(*acts, *ws, resid, g_mix, b_mix, wg, wu, wd, g, b)


CONV_PAD = 8


def _proj1_kernel(x_ref, wqk_ref, wv_ref, wog_ref, wgt_ref, wconv_ref,
                  q_ref, k_ref, v_ref, og_ref, gt_ref, buf):
    i = pl.program_id(1)
    tm = x_ref.shape[0]
    xb = x_ref[...].astype(BF16)
    kscale = MLSTM_HEAD_DIM ** -0.5

    @pl.when(i == 0)
    def _():
        buf[0:CONV_PAD, :] = jnp.zeros((CONV_PAD, buf.shape[1]), F32)

    wide = 2 * MXU_DIM
    for c in range(2 * D_MODEL // wide):
        lanes = slice(c * wide, (c + 1) * wide)
        buf[CONV_PAD:CONV_PAD + tm, lanes] = _dot(xb, wqk_ref[:, lanes])
    for c in range(2 * D_MODEL // LANES):
        lanes = slice(c * LANES, (c + 1) * LANES)
        ext = buf[0:CONV_PAD + tm, lanes]
        y = ext[CONV_PAD:, :] * wconv_ref[CONV_WIDTH - 1:CONV_WIDTH, lanes]
        for back in range(1, CONV_WIDTH):
            tap = CONV_WIDTH - 1 - back
            y = y + pltpu.roll(ext, back, axis=0)[CONV_PAD:, :] * wconv_ref[tap:tap + 1, lanes]
        y = y * jax.nn.sigmoid(y)
        buf[0:CONV_PAD, lanes] = buf[tm:tm + CONV_PAD, lanes]
        if c < D_MODEL // LANES:
            q_ref[:, lanes] = y.astype(BF16)
        else:
            k_ref[:, c * LANES - D_MODEL:(c + 1) * LANES - D_MODEL] = (y * kscale).astype(BF16)
    for c in range(D_MODEL // wide):
        lanes = slice(c * wide, (c + 1) * wide)
        v_ref[:, lanes] = _dot(xb, wv_ref[:, lanes]).astype(BF16)
        og_ref[:, lanes] = _dot(xb, wog_ref[:, lanes])
    gt_ref[...] = _dot(xb, wgt_ref[...])


def _proj1(x, wqk, wv, wog, wgt, wconv, tm):
    B, S, _ = x.shape
    row = lambda b, i: (b, i, 0)
    const = lambda b, i: (0, 0)
    act_spec = pl.BlockSpec((None, tm, D_MODEL), row)
    act = jax.ShapeDtypeStruct((B, S, D_MODEL), BF16)
    return pl.pallas_call(
        _proj1_kernel,
        grid=(B, S // tm),
        in_specs=[act_spec,
                  pl.BlockSpec((D_MODEL, 2 * D_MODEL), const),
                  pl.BlockSpec((D_MODEL, D_MODEL), const),
                  pl.BlockSpec((D_MODEL, D_MODEL), const),
                  pl.BlockSpec((D_MODEL, 2 * N_HEADS), const),
                  pl.BlockSpec((CONV_WIDTH, 2 * D_MODEL), const)],
        out_specs=[act_spec, act_spec, act_spec, act_spec,
                   pl.BlockSpec((None, tm, 2 * N_HEADS), row)],
        out_shape=[act, act, act, jax.ShapeDtypeStruct((B, S, D_MODEL), F32),
                   jax.ShapeDtypeStruct((B, S, 2 * N_HEADS), F32)],
        scratch_shapes=[pltpu.VMEM((tm + CONV_PAD, 2 * D_MODEL), F32)],
        compiler_params=_cparams(("parallel", "arbitrary"), 56),
        name="proj1",
    )(x, wqk, wv, wog, wgt, wconv)


def _mlstm_kernel(q_ref, k_ref, v_ref, og_ref, gn_ref, gt_ref, bi_row, bf_row, bi_col, bf_col,
                  ng_ref, o_ref, c_sc, n_sc, m_sc, *, L):
    ci = pl.program_id(1)

    @pl.when(ci == 0)
    def _():
        c_sc[...] = jnp.zeros(c_sc.shape, F32)
        n_sc[...] = jnp.zeros(n_sc.shape, F32)
        m_sc[...] = jnp.full(m_sc.shape, NEG_INF, F32)

    row = lax.broadcasted_iota(jnp.int32, (L, L), 0)
    col = lax.broadcasted_iota(jnp.int32, (L, L), 1)
    causal = col <= row
    lower = jnp.where(causal, 1.0, 0.0).astype(BF16)
    upper = jnp.where(row <= col, 1.0, 0.0).astype(BF16)

    gn = gn_ref[...]
    gt = gt_ref[...]
    i_col = gn[:, :N_HEADS] + bi_row[...]
    lf_col = _log_sigmoid(gn[:, N_HEADS:] + bf_row[...])
    i_row = gt[:N_HEADS, :] + bi_col[...]
    lf_row = _log_sigmoid(gt[N_HEADS:, :] + bf_col[...])
    b_col = sum(_dot(lower, part) for part in _split3(lf_col))
    b_row = sum(_dot(part, upper) for part in _split3(lf_row))

    keep = row <= col
    n_pad = jnp.zeros((2 * N_HEADS - 3, MLSTM_HEAD_DIM), F32)
    for h in range(N_HEADS):
        lanes = slice(h * MLSTM_HEAD_DIM, (h + 1) * MLSTM_HEAD_DIM)
        q = q_ref[:, lanes]
        k = k_ref[:, lanes]
        v = v_ref[:, lanes]
        bt = b_row[h:h + 1, :]
        key_term = i_col[:, h:h + 1] - b_col[:, h:h + 1]
        m_prev = m_sc[h]
        ct_prev = c_sc[h]
        n_prev = n_sc[h]

        dlog = jnp.where(keep, bt + key_term, NEG_INF)
        inter = bt + m_prev
        m_t = jnp.maximum(inter, jnp.max(dlog, axis=0, keepdims=True))
        s = _dot_nt(k, q) * jnp.exp(dlog - m_t)
        inter_w = jnp.exp(inter - m_t)
        num = _dot_tn(v, s.astype(BF16)) + inter_w * _dot_nt(ct_prev.astype(BF16), q)
        n_parts = jnp.concatenate([p.astype(F32) for p in _split3(n_prev)] + [n_pad], axis=0)
        qn = jnp.sum(_dot_nt(n_parts.astype(BF16), q), axis=0, keepdims=True)
        den = jnp.sum(s, axis=0, keepdims=True) + inter_w * qn
        hh = num / jnp.maximum(jnp.abs(den), jnp.exp(-m_t))

        b_last = bt[:, L - 1:L]
        g = b_last + key_term
        m_new = jnp.maximum(b_last + m_prev, jnp.max(g, axis=0, keepdims=True))
        w = jnp.exp(g - m_new)
        decay = jnp.exp(b_last + m_prev - m_new)
        kw = k.astype(F32) * w
        c_sc[h] = decay * ct_prev + _dot_tn(v, kw.astype(BF16))
        n_sc[h] = decay * n_prev + jnp.sum(kw, axis=0, keepdims=True)
        m_sc[h] = m_new

        mu = jnp.mean(hh, axis=0, keepdims=True)
        hc = hh - mu
        var = jnp.mean(hc * hc, axis=0, keepdims=True)
        hn = (hc * lax.rsqrt(var + LN_EPS)).T * ng_ref[:, lanes]
        o_ref[:, lanes] = (hn * jax.nn.sigmoid(og_ref[:, lanes])).astype(BF16)


def _mlstm(q, k, v, og, gates, gates_t, b_i, b_f, norm_g, L):
    B, S, _ = q.shape
    row = lambda b, c: (b, c, 0)
    const = lambda b, c: (0, 0)
    act_spec = pl.BlockSpec((None, L, D_MODEL), row)
    return pl.pallas_call(
        functools.partial(_mlstm_kernel, L=L),
        grid=(B, S // L),
        in_specs=[act_spec, act_spec, act_spec, act_spec,
                  pl.BlockSpec((None, L, 2 * N_HEADS), row),
                  pl.BlockSpec((None, 2 * N_HEADS, L), lambda b, c: (b, 0, c)),
                  pl.BlockSpec((1, N_HEADS), const), pl.BlockSpec((1, N_HEADS), const),
                  pl.BlockSpec((N_HEADS, 1), const), pl.BlockSpec((N_HEADS, 1), const),
                  pl.BlockSpec((1, D_MODEL), const)],
        out_specs=act_spec,
        out_shape=jax.ShapeDtypeStruct((B, S, D_MODEL), BF16),
        scratch_shapes=[pltpu.VMEM((N_HEADS, MLSTM_HEAD_DIM, MLSTM_HEAD_DIM), F32),
                        pltpu.VMEM((N_HEADS, 1, MLSTM_HEAD_DIM), F32),
                        pltpu.VMEM((N_HEADS, 1, 1), F32)],
        compiler_params=_cparams(("parallel", "arbitrary"), 48),
        name="mlstm",
    )(q, k, v, og, gates, gates_t, b_i[None, :], b_f[None, :], b_i[:, None], b_f[:, None],
      norm_g[None, :])


MOE_TOP_K = 2
MOE_TILE = 512
RANK_BLOCK = 1024


def _rank_kernel(sel_ref, rank_ref, count_ref, upper_sc, carry_sc):
    n = sel_ref.shape[1]

    @pl.when(pl.program_id(0) == 0)
    def _():
        row = lax.broadcasted_iota(jnp.int32, (n, n), 0)
        col = lax.broadcasted_iota(jnp.int32, (n, n), 1)
        upper_sc[...] = jnp.where(row <= col, 1.0, 0.0).astype(BF16)
        carry_sc[...] = jnp.zeros(carry_sc.shape, F32)

    sel = sel_ref[...]
    incl = _dot(sel.astype(BF16), upper_sc[...])
    rank_ref[...] = carry_sc[...] + incl - sel
    carry_sc[...] = carry_sc[...] + incl[:, n - 1:n]
    count_ref[...] = carry_sc[...]


def _rank(sel_t):
    E, T = sel_t.shape
    n = min(RANK_BLOCK, T)
    return pl.pallas_call(
        _rank_kernel,
        grid=(T // n,),
        in_specs=[pl.BlockSpec((E, n), lambda i: (0, i))],
        out_specs=[pl.BlockSpec((E, n), lambda i: (0, i)), pl.BlockSpec((E, 1), lambda i: (0, 0))],
        out_shape=[jax.ShapeDtypeStruct((E, T), F32), jax.ShapeDtypeStruct((E, 1), F32)],
        scratch_shapes=[pltpu.VMEM((n, n), BF16), pltpu.VMEM((E, 1), F32)],
        compiler_params=_cparams(("arbitrary",), 32),
        name="moe_rank",
    )(sel_t)


def _slot_kernel(sel_ref, gate_ref, rank_ref, off_ref, pos_ref, w_ref):
    sel = sel_ref[...] > 0.0
    idx = lax.broadcasted_iota(jnp.int32, sel.shape, 0)
    first = jnp.min(jnp.where(sel, idx, N_EXPERTS), axis=0, keepdims=True)
    second = jnp.max(jnp.where(sel, idx, -1), axis=0, keepdims=True)
    slot = off_ref[...] + rank_ref[...]
    gate = gate_ref[...]
    rows_p, rows_w = [], []
    for which in (first, second):
        hit = idx == which
        rows_p.append(jnp.sum(jnp.where(hit, slot, 0.0), axis=0, keepdims=True))
        rows_w.append(jnp.sum(jnp.where(hit, gate, 0.0), axis=0, keepdims=True))
    pos_ref[...] = jnp.concatenate(rows_p, axis=0).astype(jnp.int32)
    w_ref[...] = jnp.concatenate(rows_w, axis=0)


def _slots(sel_t, gate_t, rank_t, offsets_col, tm):
    E, T = sel_t.shape
    spec = pl.BlockSpec((E, tm), lambda i: (0, i))
    out_spec = pl.BlockSpec((MOE_TOP_K, tm), lambda i: (0, i))
    return pl.pallas_call(
        _slot_kernel,
        grid=(T // tm,),
        in_specs=[spec, spec, spec, pl.BlockSpec((E, 1), lambda i: (0, 0))],
        out_specs=[out_spec, out_spec],
        out_shape=[jax.ShapeDtypeStruct((MOE_TOP_K, T), jnp.int32),
                   jax.ShapeDtypeStruct((MOE_TOP_K, T), F32)],
        compiler_params=_cparams(("parallel",), 32),
        name="moe_slots",
    )(sel_t, gate_t, rank_t, offsets_col)


def _dispatch_kernel(tail_ref, used_ref, pos_ref, x_ref, xs_hbm, zero_sc, sem, zsem, *, tm):
    i = pl.program_id(0)
    n_tiles = xs_hbm.shape[0] // MOE_TILE

    def fill(tile):
        return pltpu.make_async_copy(zero_sc, xs_hbm.at[pl.ds(tile * MOE_TILE, MOE_TILE)], zsem)

    @pl.when(i == 0)
    def _():
        zero_sc[...] = jnp.zeros(zero_sc.shape, F32)
        for e in range(N_EXPERTS):
            fill(tail_ref[e]).start()
        for e in range(N_EXPERTS):
            fill(tail_ref[e]).wait()
        for j in range(N_EXPERTS):
            @pl.when(n_tiles - 1 - j >= used_ref[0])
            def _():
                c = fill(n_tiles - 1 - j)
                c.start()
                c.wait()

    def issue(t, carry):
        for k in range(MOE_TOP_K):
            pltpu.make_async_copy(x_ref.at[pl.ds(t, 1)], xs_hbm.at[pl.ds(pos_ref[k, t], 1)],
                                  sem).start(priority=k)
        return carry

    lax.fori_loop(0, tm, issue, 0, unroll=8)
    for _ in range(MOE_TOP_K):
        pltpu.make_async_copy(x_ref, xs_hbm.at[pl.ds(0, tm)], sem).wait()


def _dispatch(x, pos_t, tail_tiles, n_used, n_rows, tm):
    T = x.shape[0]
    n_steps = T // tm
    pos3 = pos_t.reshape(MOE_TOP_K, n_steps, tm).transpose(1, 0, 2)
    return pl.pallas_call(
        functools.partial(_dispatch_kernel, tm=tm),
        grid_spec=pltpu.PrefetchScalarGridSpec(
            num_scalar_prefetch=2,
            grid=(n_steps,),
            in_specs=[pl.BlockSpec((None, MOE_TOP_K, tm), lambda i, tail, used: (i, 0, 0),
                                   memory_space=pltpu.SMEM),
                      pl.BlockSpec((tm, D_MODEL), lambda i, tail, used: (i, 0))],
            out_specs=pl.BlockSpec(memory_space=pl.ANY),
            scratch_shapes=[pltpu.VMEM((MOE_TILE, D_MODEL), F32),
                            pltpu.SemaphoreType.DMA, pltpu.SemaphoreType.DMA],
        ),
        out_shape=jax.ShapeDtypeStruct((n_rows, D_MODEL), F32),
        compiler_params=_cparams(("arbitrary",), 32),
        name="moe_dispatch",
    )(tail_tiles, n_used, pos3, x)


def _expert_ffn_kernel(te_ref, used_ref, x_ref, wg_ref, wu_ref, wd_ref, o_ref):
    live = pl.program_id(0) < used_ref[0]

    @pl.when(live)
    def _():
        xb = x_ref[...].astype(BF16)
        y = None
        for lo, size in _ffn_chunks(wg_ref.shape[1]):
            gate = _dot(xb, wg_ref[:, lo:lo + size])
            up = _dot(xb, wu_ref[:, lo:lo + size])
            act = (gate * jax.nn.sigmoid(gate) * up).astype(BF16)
            part = _dot(act, wd_ref[lo:lo + size, :])
            y = part if y is None else y + part
        o_ref[...] = y

    @pl.when(jnp.logical_not(live))
    def _():
        o_ref[...] = jnp.zeros(o_ref.shape, F32)


def _expert_ffn(xs, tile_expert, n_used, wg, wu, wd):
    n_rows = xs.shape[0]
    E, _, F = wg.shape
    tm = MOE_TILE
    once = pl.Buffered(1)

    def row(i, te, used):
        return (jnp.minimum(i, used[0] - 1), 0)

    return pl.pallas_call(
        _expert_ffn_kernel,
        grid_spec=pltpu.PrefetchScalarGridSpec(
            num_scalar_prefetch=2,
            grid=(n_rows // tm,),
            in_specs=[pl.BlockSpec((tm, D_MODEL), row),
                      pl.BlockSpec((None, D_MODEL, F), lambda i, te, used: (te[i], 0, 0),
                                   pipeline_mode=once),
                      pl.BlockSpec((None, D_MODEL, F), lambda i, te, used: (te[i], 0, 0),
                                   pipeline_mode=once),
                      pl.BlockSpec((None, F, D_MODEL), lambda i, te, used: (te[i], 0, 0),
                                   pipeline_mode=once)],
            out_specs=pl.BlockSpec((tm, D_MODEL), lambda i, te, used: (i, 0)),
        ),
        out_shape=jax.ShapeDtypeStruct((n_rows, D_MODEL), F32),
        compiler_params=_cparams(("arbitrary",), 56),
        name="moe_expert_ffn",
    )(tile_expert, n_used, xs, wg, wu, wd)


def _combine_kernel(pos_ref, w_ref, x_ref, g_ref, b_ref, ys_hbm, o_ref, rows_sc, sem, *, tm):
    def issue(t, carry):
        for k in range(MOE_TOP_K):
            pltpu.make_async_copy(ys_hbm.at[pl.ds(pos_ref[k, t], 1)],
                                  rows_sc.at[k, pl.ds(t, 1)], sem).start(priority=k)
        return carry

    lax.fori_loop(0, tm, issue, 0, unroll=8)
    for k in range(MOE_TOP_K):
        pltpu.make_async_copy(ys_hbm.at[pl.ds(0, tm)], rows_sc.at[k], sem).wait()
    w = w_ref[...]
    y = w[:, 0:1] * rows_sc[0] + w[:, 1:2] * rows_sc[1]
    o_ref[...] = _layer_norm(DEEPNORM_ALPHA * x_ref[...] + y, g_ref[...], b_ref[...])


def _combine_ln(ys, pos_t, w_nat, x, g, b, tm):
    T = x.shape[0]
    n_steps = T // tm
    pos3 = pos_t.reshape(MOE_TOP_K, n_steps, tm).transpose(1, 0, 2)
    row = pl.BlockSpec((tm, D_MODEL), lambda i: (i, 0))
    vec = pl.BlockSpec((1, D_MODEL), lambda i: (0, 0))
    return pl.pallas_call(
        functools.partial(_combine_kernel, tm=tm),
        grid=(n_steps,),
        in_specs=[pl.BlockSpec((None, MOE_TOP_K, tm), lambda i: (i, 0, 0), memory_space=pltpu.SMEM),
                  pl.BlockSpec((tm, MOE_TOP_K), lambda i: (i, 0)),
                  row, vec, vec,
                  pl.BlockSpec(memory_space=pl.ANY)],
        out_specs=row,
        out_shape=jax.ShapeDtypeStruct((T, D_MODEL), F32),
        scratch_shapes=[pltpu.VMEM((MOE_TOP_K, tm, D_MODEL), F32), pltpu.SemaphoreType.DMA],
        compiler_params=_cparams(("arbitrary",), 32),
        name="moe_combine_ln",
    )(pos3, w_nat, x, g, b, ys)


def _moe_sparse(x, sel_t, gate_t, wg, wu, wd, g, b):
    T = x.shape[0]
    tile = MOE_TILE
    n_rows = MOE_TOP_K * T + N_EXPERTS * tile
    n_tiles = n_rows // tile
    rank_t, counts = _rank(sel_t)
    tiles_per = jnp.ceil(counts[:, 0] / tile).astype(jnp.int32)
    tile_end = jnp.cumsum(tiles_per)
    tile_start = tile_end - tiles_per
    n_used = tile_end[-1:]
    tile_expert = jnp.minimum(
        jnp.sum(jnp.arange(n_tiles, dtype=jnp.int32)[:, None] >= tile_end[None, :], axis=1),
        N_EXPERTS - 1).astype(jnp.int32)
    tail_tiles = jnp.maximum(tile_end - 1, 0).astype(jnp.int32)
    offsets_col = (tile_start * tile).astype(F32)[:, None]
    pos_t, w_t = _slots(sel_t, gate_t, rank_t, offsets_col, tm=1024)
    xs = _dispatch(x, pos_t, tail_tiles, n_used, n_rows, tm=512)
    ys = _expert_ffn(xs, tile_expert, n_used, wg, wu, wd)
    return _combine_ln(ys, pos_t, w_t.T, x, g, b, tm=256)


def _rotary_tables(S):
    pos = jnp.arange(S, dtype=jnp.int32)
    inv = ROPE_THETA ** (-jnp.arange(ROPE_HALF, dtype=F32) / ROPE_HALF)
    ang = pos.astype(F32)[:, None] * inv[None, :]
    cos, sin = jnp.cos(ang), jnp.sin(ang)
    ones = jnp.ones((S, HEAD_DIM - ROPE_DIMS), F32)
    zeros = jnp.zeros((S, HEAD_DIM - ROPE_DIMS), F32)
    z8 = jnp.zeros((S, ROPE_HALF), F32)
    cos_h = jnp.concatenate([cos, cos, ones], axis=1)
    sup_h = jnp.concatenate([-sin, z8, zeros], axis=1)
    sdn_h = jnp.concatenate([z8, sin, zeros], axis=1)
    rep = LANES // HEAD_DIM
    return (jnp.tile(cos_h, (1, rep)), jnp.tile(sup_h, (1, rep)), jnp.tile(sdn_h, (1, rep)))


def _even_layer(h, w_in, b_forget, w_out, ln_mix_g, ln_mix_b, w_gate, w_up, w_down,
                ln_ffn_g, ln_ffn_b):
    B, S, _ = h.shape
    T = B * S
    W = ATT_WIDTH
    w_main = jnp.concatenate([w_in[:, :3 * W], w_in[:, 3 * W + N_HEADS:]], axis=1).astype(BF16)
    wf_t = w_in[:, 3 * W:3 * W + N_HEADS].T.astype(BF16)
    cos_t, sup_t, sdn_t = _rotary_tables(S)
    outs = _proj0(h, w_main, wf_t, cos_t, sup_t, sdn_t, tm=min(512, S))
    qa, ka, va = outs[:3]
    qkv_by_dilation = {d: tuple(outs[3 + 3 * n:6 + 3 * n]) for n, d in enumerate(DILATIONS)}
    f_t = outs[-1]

    bias_col = jnp.tile(b_forget, B)[:, None]
    c = _fox_cumsum(f_t.reshape(B * N_HEADS, S), bias_col).reshape(B, N_HEADS, S)
    o_fox = _fox_attention(qa, ka, va, jnp.swapaxes(c, 1, 2), t=min(512, S))
    o_dil = _dilated_attention(qkv_by_dilation)

    w_out_b = w_out.astype(BF16)
    tm = 512
    h2 = _mixer_out_dense_ffn(
        [o_fox.reshape(T, W), o_dil.reshape(T, W)], [w_out_b[:W], w_out_b[W:]],
        h.reshape(T, D_MODEL), ln_mix_g[None, :], ln_mix_b[None, :],
        w_gate.astype(BF16), w_up.astype(BF16), w_down.astype(BF16),
        ln_ffn_g[None, :], ln_ffn_b[None, :], tm)
    return h2.reshape(B, S, D_MODEL)


def _odd_layer(h, w_in, b_igate, b_fgate, w_conv, norm_g, w_out, ln_mix_g, ln_mix_b, w_router,
               w_gate, w_up, w_down, ln_ffn_g, ln_ffn_b):
    B, S, _ = h.shape
    T = B * S
    D = D_MODEL
    wqk = w_in[:, :2 * D].astype(BF16)
    wv = w_in[:, 2 * D:3 * D].astype(BF16)
    wgt = w_in[:, 3 * D:3 * D + 2 * N_HEADS].astype(BF16)
    wog = w_in[:, 3 * D + 2 * N_HEADS:].astype(BF16)
    q, k, v, og, gates = _proj1(h, wqk, wv, wog, wgt, w_conv, tm=min(512, S))
    gates_t = jnp.swapaxes(gates, 1, 2)
    hm = _mlstm(q, k, v, og, gates, gates_t, b_igate, b_fgate, norm_g, L=min(256, S))

    tm = 512
    h1, sel_t, gate_t = _outproj_router(hm.reshape(T, D), w_out.astype(BF16), h.reshape(T, D),
                                        ln_mix_g[None, :], ln_mix_b[None, :], w_router.T, tm)
    h2 = _moe_sparse(h1, sel_t, gate_t, w_gate.astype(BF16), w_up.astype(BF16),
                     w_down.astype(BF16), ln_ffn_g[None, :], ln_ffn_b[None, :])
    return h2.reshape(B, S, D)


def kernel(x, w_in_e, b_forget_e, w_out_e, ln_mix_g_e, ln_mix_b_e, ffn_w_gate_e, ffn_w_up_e,
           ffn_w_down_e, ln_ffn_g_e, ln_ffn_b_e, w_in_o, b_igate_o, b_fgate_o, w_conv_o,
           mlstm_norm_g_o, w_out_o, ln_mix_g_o, ln_mix_b_o, w_router_o, moe_w_gate_o,
           moe_w_up_o, moe_w_down_o, ln_ffn_g_o, ln_ffn_b_o):
    h = x
    for layer in range(DEPTH):
        i = layer // 2
        if layer % 2 == 0:
            h = _even_layer(h, w_in_e[i], b_forget_e[i], w_out_e[i], ln_mix_g_e[i], ln_mix_b_e[i],
                            ffn_w_gate_e[i], ffn_w_up_e[i], ffn_w_down_e[i], ln_ffn_g_e[i],
                            ln_ffn_b_e[i])
        else:
            h = _odd_layer(h, w_in_o[i], b_igate_o[i], b_fgate_o[i], w_conv_o[i],
                           mlstm_norm_g_o[i], w_out_o[i], ln_mix_g_o[i], ln_mix_b_o[i],
                           w_router_o[i], moe_w_gate_o[i], moe_w_up_o[i], moe_w_down_o[i],
                           ln_ffn_g_o[i], ln_ffn_b_o[i])
    return h
```

```python
import functools
import math

import jax
import jax.numpy as jnp
from jax import lax
from jax.experimental import pallas as pl
from jax.experimental.pallas import tpu as pltpu

F32 = jnp.float32
BF16 = jnp.bfloat16

D_MODEL = 1024
HEAD_DIM = 64
N_HEADS = 8
ATT_WIDTH = N_HEADS * HEAD_DIM
DIL_CONFIGS = ((128, 1), (512, 4), (2048, 16))
ROPE_THETA = 500000.0
ROPE_DIMS = HEAD_DIM // 4
ROPE_HALF = ROPE_DIMS // 2
MLSTM_HEAD_DIM = D_MODEL // N_HEADS
CONV_WIDTH = 4
D_FF_DENSE = 2816
N_EXPERTS = 8
D_FF_EXPERT = 3584
DEPTH = 2
DEEPNORM_ALPHA = (2 * DEPTH) ** 0.25
LN_EPS = 1e-5

LANES = 128
MXU_DIM = 256
BAND = 128
MIB = 1024 * 1024

NEG_INF = float("-inf")


def _cparams(semantics, vmem_mib):
    return pltpu.CompilerParams(dimension_semantics=semantics, vmem_limit_bytes=vmem_mib * MIB)


def _dot(a, b):
    return jnp.dot(a, b, preferred_element_type=F32)


def _dot_nt(a, b):
    return lax.dot_general(a, b, (((1,), (1,)), ((), ())), preferred_element_type=F32)


def _dot_tn(a, b):
    return lax.dot_general(a, b, (((0,), (0,)), ((), ())), preferred_element_type=F32)


def _split3(x):
    hi = x.astype(BF16)
    r = x - hi.astype(F32)
    mid = r.astype(BF16)
    lo = (r - mid.astype(F32)).astype(BF16)
    return hi, mid, lo


def _log_sigmoid(z):
    return -(jnp.maximum(-z, 0.0) + jnp.log1p(jnp.exp(-jnp.abs(z))))


def _layer_norm(z, g, b):
    mu = jnp.mean(z, axis=-1, keepdims=True)
    zc = z - mu
    var = jnp.mean(zc * zc, axis=-1, keepdims=True)
    return zc * lax.rsqrt(var + LN_EPS) * g + b


DILATIONS = tuple(sorted(d for _, d in DIL_CONFIGS))
N_PROJ0 = 6


def _proj0_kernel(x_ref, w_ref, wf_ref, cos_ref, sup_ref, sdn_ref, *refs):
    tm = x_ref.shape[0]
    outs = refs[:N_PROJ0]
    strided = refs[N_PROJ0:-2]
    ft_ref, row_sc = refs[-2:]
    xb = x_ref[...].astype(BF16)
    scale = HEAD_DIM ** -0.5
    for j, o_ref in enumerate(outs):
        full = _dot(xb, w_ref[:, j * ATT_WIDTH:(j + 1) * ATT_WIDTH])
        for c in range(ATT_WIDTH // LANES):
            r = full[:, c * LANES:(c + 1) * LANES]
            if j in (3, 4):
                r = (r * cos_ref[...]
                     + pltpu.roll(r, LANES - ROPE_HALF, axis=1) * sup_ref[...]
                     + pltpu.roll(r, ROPE_HALF, axis=1) * sdn_ref[...])
            if j in (0, 3):
                r = r * scale
            o_ref[:, c * LANES:(c + 1) * LANES] = r.astype(BF16)
            if j >= 3:
                row_sc[0:tm, :] = r
                pieces = [(0, 1)]
                for n, d in enumerate(DILATIONS[1:]):
                    s_ref = strided[3 * n + (j - 3)]
                    step = d // pieces[0][1]
                    rows_prev = tm // pieces[0][1]
                    new_pieces = []
                    vals = []
                    for idx, (res_prev, d_prev) in enumerate(pieces):
                        for b in range(step):
                            src = pl.ds(idx * rows_prev + b, rows_prev // step, stride=step)
                            vals.append(row_sc[src, :])
                            new_pieces.append((res_prev + d_prev * b, d))
                    for idx, ((res, _), val) in enumerate(zip(new_pieces, vals)):
                        lo = res * ATT_WIDTH + c * LANES
                        s_ref[:, lo:lo + LANES] = val.astype(BF16)
                        if d != DILATIONS[-1]:
                            row_sc[idx * (tm // d):(idx + 1) * (tm // d), :] = val
                    pieces = new_pieces
    ft_ref[...] = _dot_nt(wf_ref[...], xb)


def _proj0(x, w_main, wf_t, cos_t, sup_t, sdn_t, tm):
    B, S, _ = x.shape
    n_i = S // tm
    act = jax.ShapeDtypeStruct((B, S, ATT_WIDTH), BF16)
    act_spec = pl.BlockSpec((None, tm, ATT_WIDTH), lambda b, i: (b, i, 0))
    tab_spec = pl.BlockSpec((tm, LANES), lambda b, i: (i, 0))
    out_specs = [act_spec] * N_PROJ0
    out_shape = [act] * N_PROJ0
    for d in DILATIONS[1:]:
        out_specs += [pl.BlockSpec((None, tm // d, d * ATT_WIDTH), lambda b, i: (b, i, 0))] * 3
        out_shape += [jax.ShapeDtypeStruct((B, S // d, d * ATT_WIDTH), BF16)] * 3
    out_specs.append(pl.BlockSpec((None, N_HEADS, tm), lambda b, i: (b, 0, i)))
    out_shape.append(jax.ShapeDtypeStruct((B, N_HEADS, S), F32))
    return pl.pallas_call(
        _proj0_kernel,
        grid=(B, n_i),
        in_specs=[
            pl.BlockSpec((None, tm, D_MODEL), lambda b, i: (b, i, 0)),
            pl.BlockSpec((D_MODEL, N_PROJ0 * ATT_WIDTH), lambda b, i: (0, 0)),
            pl.BlockSpec((N_HEADS, D_MODEL), lambda b, i: (0, 0)),
            tab_spec, tab_spec, tab_spec,
        ],
        out_specs=out_specs,
        out_shape=out_shape,
        scratch_shapes=[pltpu.VMEM((tm, LANES), F32)],
        compiler_params=_cparams(("parallel", "parallel"), 48),
        name="proj0",
    )(x, w_main, wf_t, cos_t, sup_t, sdn_t)


def _fox_cumsum_kernel(f_ref, bias_ref, c_ref):
    S = f_ref.shape[1]
    lf = _log_sigmoid(f_ref[...] + bias_ref[...])
    row = lax.broadcasted_iota(jnp.int32, (S, S), 0)
    col = lax.broadcasted_iota(jnp.int32, (S, S), 1)
    upper = jnp.where(row <= col, 1.0, 0.0).astype(BF16)
    hi, mid, lo = _split3(lf)
    c_ref[...] = _dot(hi, upper) + _dot(mid, upper) + _dot(lo, upper)


def _fox_cumsum(f_t, bias_col):
    R, S = f_t.shape
    return pl.pallas_call(
        _fox_cumsum_kernel,
        grid=(1,),
        in_specs=[pl.BlockSpec((R, S), lambda i: (0, 0)), pl.BlockSpec((R, 1), lambda i: (0, 0))],
        out_specs=pl.BlockSpec((R, S), lambda i: (0, 0)),
        out_shape=jax.ShapeDtypeStruct((R, S), F32),
        compiler_params=_cparams(("arbitrary",), 48),
        name="fox_cumsum",
    )(f_t, bias_col)


def _fox_kernel(q_ref, k_ref, v_ref, c_ref, o_ref, m_sc, l_sc, acc_sc, *, t):
    i = pl.program_id(1)
    j = pl.program_id(2)

    @pl.when(j == 0)
    def _():
        m_sc[...] = jnp.full(m_sc.shape, NEG_INF, F32)
        l_sc[...] = jnp.zeros(l_sc.shape, F32)
        acc_sc[...] = jnp.zeros(acc_sc.shape, F32)

    def step(masked):
        if masked:
            key = lax.broadcasted_iota(jnp.int32, (t, t), 0)
            qry = lax.broadcasted_iota(jnp.int32, (t, t), 1)
            keep = key <= qry
        for h in range(N_HEADS):
            sl = slice(h * HEAD_DIM, (h + 1) * HEAD_DIM)
            s = _dot_nt(k_ref[:, sl], q_ref[:, sl]) - c_ref[:, h:h + 1]
            if masked:
                s = jnp.where(keep, s, NEG_INF)
            m_prev = m_sc[h:h + 1, :]
            m_new = jnp.maximum(m_prev, jnp.max(s, axis=0, keepdims=True))
            alpha = jnp.exp(m_prev - m_new)
            p = jnp.exp(s - m_new)
            l_sc[h:h + 1, :] = alpha * l_sc[h:h + 1, :] + jnp.sum(p, axis=0, keepdims=True)
            acc_sc[sl, :] = alpha * acc_sc[sl, :] + _dot_tn(v_ref[:, sl], p.astype(BF16))
            m_sc[h:h + 1, :] = m_new

    @pl.when(j < i)
    def _():
        step(False)

    @pl.when(j == i)
    def _():
        step(True)
        out_t = jnp.concatenate(
            [acc_sc[h * HEAD_DIM:(h + 1) * HEAD_DIM, :] / l_sc[h:h + 1, :] for h in range(N_HEADS)],
            axis=0)
        o_ref[...] = out_t.T.astype(BF16)


def _fox_attention(q, k, v, c, t):
    B, S, _ = q.shape
    n = S // t
    q_spec = pl.BlockSpec((None, t, ATT_WIDTH), lambda b, i, j: (b, i, 0))
    kv_spec = pl.BlockSpec((None, t, ATT_WIDTH), lambda b, i, j: (b, jnp.minimum(j, i), 0))
    return pl.pallas_call(
        functools.partial(_fox_kernel, t=t),
        grid=(B, n, n),
        in_specs=[q_spec, kv_spec, kv_spec,
                  pl.BlockSpec((None, t, N_HEADS), lambda b, i, j: (b, jnp.minimum(j, i), 0))],
        out_specs=q_spec,
        out_shape=jax.ShapeDtypeStruct((B, S, ATT_WIDTH), BF16),
        scratch_shapes=[pltpu.VMEM((N_HEADS, t), F32), pltpu.VMEM((N_HEADS, t), F32),
                        pltpu.VMEM((ATT_WIDTH, t), F32)],
        compiler_params=_cparams(("parallel", "parallel", "arbitrary"), 48),
        name="fox_attention",
    )(q, k, v, c)


def _dil_kernel(*refs, d, rg, sub, tqu, has_prev, first, last):
    refs = list(refs)
    q_ref, k_ref, v_ref = refs[:3]
    pos = 3
    if has_prev:
        kp_ref, vp_ref = refs[pos:pos + 2]
        pos += 2
    if not first:
        acc_in, st_in = refs[pos:pos + 2]
        pos += 2
    if last:
        o_ref = refs[pos]
    else:
        acc_out, st_out = refs[pos:pos + 2]

    blk = pl.program_id(1)
    nk = tqu + (BAND if has_prev else 0)
    key = lax.broadcasted_iota(jnp.int32, (nk, tqu), 0)
    qry = lax.broadcasted_iota(jnp.int32, (nk, tqu), 1)
    if has_prev:
        keep = jnp.logical_and(key >= qry, key <= qry + BAND)
        keep_edge = jnp.logical_and(keep, jnp.logical_or(key >= BAND, blk > 0))
    else:
        keep = key <= qry
    pair = LANES // HEAD_DIM

    if first and not has_prev and not last and rg % 2 == 0:
        n2 = 2 * tqu
        kk = lax.broadcasted_iota(jnp.int32, (n2, n2), 0)
        qq = lax.broadcasted_iota(jnp.int32, (n2, n2), 1)
        same = (kk >= tqu) == (qq >= tqu)
        keep2 = jnp.logical_and(same, kk <= qq)
        for r2 in range(rg // 2):
            res = (2 * r2, 2 * r2 + 1)
            nats = [pl.ds(pl.program_id(2) * rg + rr, tqu, stride=d) for rr in res]
            ms, ls = [], []
            for slab in range(N_HEADS // pair):
                outs = []
                for hh in range(pair):
                    h = slab * pair + hh
                    sls = [slice(rr * ATT_WIDTH + h * HEAD_DIM, rr * ATT_WIDTH + (h + 1) * HEAD_DIM)
                           for rr in res]
                    q = jnp.concatenate([q_ref[:, sl] for sl in sls], axis=0)
                    k_cat = jnp.concatenate([k_ref[:, sl] for sl in sls], axis=0)
                    v_cat = jnp.concatenate([v_ref[:, sl] for sl in sls], axis=0)
                    s = jnp.where(keep2, _dot_nt(k_cat, q), NEG_INF)
                    m = jnp.max(s, axis=0, keepdims=True)
                    p = jnp.exp(s - m)
                    ls.append(jnp.sum(p, axis=0, keepdims=True))
                    ms.append(m)
                    outs.append(_dot_tn(v_cat, p.astype(BF16)))
                slab_out = jnp.concatenate(outs, axis=0).T
                for j, nat in enumerate(nats):
                    acc_out[slab, nat, :] = slab_out[j * tqu:(j + 1) * tqu, :]
            pad = jnp.zeros((LANES - 2 * N_HEADS, n2), F32)
            st_t = jnp.concatenate(ms + ls + [pad], axis=0).T
            for j, nat in enumerate(nats):
                st_out[nat, :] = st_t[j * tqu:(j + 1) * tqu, :]
        return

    for a in range(sub):
        rows = slice(a * tqu, (a + 1) * tqu)
        for rr in range(rg):
            if d == 1:
                nat = rows
            else:
                nat = pl.ds(a * tqu * d + pl.program_id(2) * rg + rr, tqu, stride=d)
            if not first:
                st_old = st_in[nat, :].T
            ms, ls = [], []
            for slab in range(N_HEADS // pair):
                slab_lanes = slice(rr * ATT_WIDTH + slab * LANES, rr * ATT_WIDTH + (slab + 1) * LANES)
                if not first:
                    acc_old = acc_in[slab, nat, :].T
                outs = []
                for hh in range(pair):
                    h = slab * pair + hh
                    lo = rr * ATT_WIDTH + h * HEAD_DIM
                    sl = slice(lo, lo + HEAD_DIM)
                    q = q_ref[rows, sl]
                    if not has_prev:
                        k_cat, v_cat, mask = k_ref[rows, sl], v_ref[rows, sl], keep
                    elif a == 0:
                        k_cat = jnp.concatenate([kp_ref[:, sl], k_ref[rows, sl]], axis=0)
                        v_cat = jnp.concatenate([vp_ref[:, sl], v_ref[rows, sl]], axis=0)
                        mask = keep_edge
                    else:
                        krows = slice(a * tqu - BAND, (a + 1) * tqu)
                        k_cat, v_cat, mask = k_ref[krows, sl], v_ref[krows, sl], keep
                    s = jnp.where(mask, _dot_nt(k_cat, q), NEG_INF)
                    m = jnp.max(s, axis=0, keepdims=True)
                    if not first:
                        m_old = st_old[h:h + 1, :]
                        l_old = st_old[N_HEADS + h:N_HEADS + h + 1, :]
                        m_new = jnp.maximum(m, m_old)
                        alpha = jnp.exp(m_old - m_new)
                        m = m_new
                    p = jnp.exp(s - m)
                    l = jnp.sum(p, axis=0, keepdims=True)
                    acc = _dot_tn(v_cat, p.astype(BF16))
                    if not first:
                        l = l + alpha * l_old
                        acc = acc + alpha * acc_old[hh * HEAD_DIM:(hh + 1) * HEAD_DIM, :]
                    if last:
                        acc = acc / l
                    outs.append(acc)
                    ms.append(m)
                    ls.append(l)
                slab_out = jnp.concatenate(outs, axis=0).T
                if last:
                    o_ref[rows, slab_lanes] = slab_out.astype(BF16)
                else:
                    acc_out[slab, nat, :] = slab_out
            if not last:
                pad = jnp.zeros((LANES - 2 * N_HEADS, tqu), F32)
                st_out[nat, :] = jnp.concatenate(ms + ls + [pad], axis=0).T


def _dilated_branch(q, k, v, state, dilation, last):
    B, L, _ = q.shape
    d = dilation
    S = L * d
    first = state is None
    assert not (last and d != 1)
    rg = min(4, d)
    tqu = min(2 * BAND, L)
    tqb = min(L, 4 * tqu // rg)
    sub = tqb // tqu
    n_blk = L // tqb
    n_grp = d // rg
    has_prev = L > BAND
    wq = rg * ATT_WIDTH
    n_slab = ATT_WIDTH // LANES

    main = lambda b, i, g: (b, i, g)
    prev = lambda b, i, g: (b, jnp.maximum(i * (tqb // BAND) - 1, 0), g)
    qkv_spec = pl.BlockSpec((None, tqb, wq), main)
    acc_spec = pl.BlockSpec((None, n_slab, tqb * d, LANES), lambda b, i, g: (b, 0, i, 0))
    st_spec = pl.BlockSpec((None, tqb * d, LANES), lambda b, i, g: (b, i, 0))
    in_specs = [qkv_spec] * 3
    args = [q, k, v]
    if has_prev:
        in_specs += [pl.BlockSpec((None, BAND, wq), prev)] * 2
        args += [k, v]
    if not first:
        in_specs += [acc_spec, st_spec]
        args += list(state)
    if last:
        out_specs = pl.BlockSpec((None, tqb, wq), main)
        out_shape = jax.ShapeDtypeStruct((B, S, ATT_WIDTH), BF16)
    else:
        out_specs = [acc_spec, st_spec]
        out_shape = [jax.ShapeDtypeStruct((B, n_slab, S, LANES), F32),
                     jax.ShapeDtypeStruct((B, S, LANES), F32)]
    return pl.pallas_call(
        functools.partial(_dil_kernel, d=d, rg=rg, sub=sub, tqu=tqu, has_prev=has_prev, first=first,
                          last=last),
        grid=(B, n_blk, n_grp),
        in_specs=in_specs,
        out_specs=out_specs,
        out_shape=out_shape,
        compiler_params=_cparams(("parallel", "parallel", "arbitrary"), 48),
        name=f"dilated_d{d}",
    )(*args)


def _dilated_attention(qkv_by_dilation):
    state = None
    order = sorted(DIL_CONFIGS, key=lambda wd: -wd[1])
    for n, (window, d) in enumerate(order):
        assert window // d == BAND
        state = _dilated_branch(*qkv_by_dilation[d], state, d, last=(n == len(order) - 1))
    return state


def _outproj_ln_rows(a_refs, w_refs, x_ref, g_ref, b_ref):
    y = _dot(a_refs[0][...], w_refs[0][...])
    for a_ref, w_ref in zip(a_refs[1:], w_refs[1:]):
        y = y + _dot(a_ref[...], w_ref[...])
    return _layer_norm(DEEPNORM_ALPHA * x_ref[...] + y, g_ref[...], b_ref[...])


def _route_top2(h, wt_ref, sel_ref, gate_ref):
    logits = lax.dot_general(wt_ref[...], h, (((1,), (1,)), ((), ())),
                             preferred_element_type=F32, precision=lax.Precision.HIGHEST)
    idx = lax.broadcasted_iota(jnp.int32, logits.shape, 0)
    m1 = jnp.max(logits, axis=0, keepdims=True)
    i1 = jnp.min(jnp.where(logits == m1, idx, N_EXPERTS), axis=0, keepdims=True)
    pick1 = idx == i1
    rest = jnp.where(pick1, NEG_INF, logits)
    m2 = jnp.max(rest, axis=0, keepdims=True)
    i2 = jnp.min(jnp.where(rest == m2, idx, N_EXPERTS), axis=0, keepdims=True)
    pick2 = idx == i2
    e2 = jnp.exp(m2 - m1)
    w1 = 1.0 / (1.0 + e2)
    w2 = e2 / (1.0 + e2)
    sel_ref[...] = jnp.where(jnp.logical_or(pick1, pick2), 1.0, 0.0)
    gate_ref[...] = jnp.where(pick1, w1, 0.0) + jnp.where(pick2, w2, 0.0)


def _outproj_router_kernel(a_ref, w_ref, x_ref, g_ref, b_ref, wt_ref, o_ref, sel_ref, gate_ref):
    h = _outproj_ln_rows([a_ref], [w_ref], x_ref, g_ref, b_ref)
    o_ref[...] = h
    _route_top2(h, wt_ref, sel_ref, gate_ref)


def _outproj_router(a, w, x, g, b, w_router_t, tm):
    T = x.shape[0]
    row = pl.BlockSpec((tm, D_MODEL), lambda i: (i, 0))
    vec = pl.BlockSpec((1, D_MODEL), lambda i: (0, 0))
    route = pl.BlockSpec((N_EXPERTS, tm), lambda i: (0, i))
    route_shape = jax.ShapeDtypeStruct((N_EXPERTS, T), F32)
    return pl.pallas_call(
        _outproj_router_kernel,
        grid=(T // tm,),
        in_specs=[pl.BlockSpec((tm, a.shape[1]), lambda i: (i, 0)),
                  pl.BlockSpec(w.shape, lambda i: (0, 0)),
                  row, vec, vec,
                  pl.BlockSpec((N_EXPERTS, D_MODEL), lambda i: (0, 0))],
        out_specs=[row, route, route],
        out_shape=[jax.ShapeDtypeStruct((T, D_MODEL), F32), route_shape, route_shape],
        compiler_params=_cparams(("parallel",), 48),
        name="outproj_router",
    )(a, w, x, g, b, w_router_t)


def _ffn_chunks(width):
    chunks, lo = [], 0
    while lo < width:
        size = min(2 * MXU_DIM, width - lo)
        chunks.append((lo, size))
        lo += size
    return chunks


def _dense_ffn_kernel(a0_ref, a1_ref, w0_ref, w1_ref, r_ref, g0_ref, b0_ref,
                      wg_ref, wu_ref, wd_ref, g_ref, b_ref, o_ref):
    x = _outproj_ln_rows([a0_ref, a1_ref], [w0_ref, w1_ref], r_ref, g0_ref, b0_ref)
    xb = x.astype(BF16)
    y = None
    for lo, size in _ffn_chunks(wg_ref.shape[1]):
        gate = _dot(xb, wg_ref[:, lo:lo + size])
        up = _dot(xb, wu_ref[:, lo:lo + size])
        act = (gate * jax.nn.sigmoid(gate) * up).astype(BF16)
        part = _dot(act, wd_ref[lo:lo + size, :])
        y = part if y is None else y + part
    o_ref[...] = _layer_norm(DEEPNORM_ALPHA * x + y, g_ref[...], b_ref[...])


def _mixer_out_dense_ffn(acts, ws, resid, g_mix, b_mix, wg, wu, wd, g, b, tm):
    T = resid.shape[0]
    F = wg.shape[1]
    row = pl.BlockSpec((tm, D_MODEL), lambda i: (i, 0))
    vec = pl.BlockSpec((1, D_MODEL), lambda i: (0, 0))
    once = pl.Buffered(1)
    const = lambda i: (0, 0)
    return pl.pallas_call(
        _dense_ffn_kernel,
        grid=(T // tm,),
        in_specs=[pl.BlockSpec((tm, a.shape[1]), lambda i: (i, 0)) for a in acts]
        + [pl.BlockSpec(w.shape, const, pipeline_mode=once) for w in ws]
        + [row, vec, vec,
           pl.BlockSpec((D_MODEL, F), const, pipeline_mode=once),
           pl.BlockSpec((D_MODEL, F), const, pipeline_mode=once),
           pl.BlockSpec((F, D_MODEL), const, pipeline_mode=once),
           vec, vec],
        out_specs=row,
        out_shape=jax.ShapeDtypeStruct((T, D_MODEL), F32),
        compiler_params=_cparams(("parallel",), 56),
        name="mixer_out_dense_ffn",
    )(*acts, *ws, resid, g_mix, b_mix, wg, wu, wd, g, b)


CONV_PAD = 8


def _proj1_kernel(x_ref, wqk_ref, wv_ref, wgt_ref, wconv_ref,
                  q_ref, k_ref, v_ref, gt_ref, buf):
    i = pl.program_id(1)
    tm = x_ref.shape[0]
    xb = x_ref[...].astype(BF16)
    kscale = MLSTM_HEAD_DIM ** -0.5

    @pl.when(i == 0)
    def _():
        buf[0:CONV_PAD, :] = jnp.zeros((CONV_PAD, buf.shape[1]), F32)

    wide = 2 * MXU_DIM
    for c in range(2 * D_MODEL // wide):
        lanes = slice(c * wide, (c + 1) * wide)
        buf[CONV_PAD:CONV_PAD + tm, lanes] = _dot(xb, wqk_ref[:, lanes])
    for c in range(2 * D_MODEL // LANES):
        lanes = slice(c * LANES, (c + 1) * LANES)
        ext = buf[0:CONV_PAD + tm, lanes]
        y = ext[CONV_PAD:, :] * wconv_ref[CONV_WIDTH - 1:CONV_WIDTH, lanes]
        for back in range(1, CONV_WIDTH):
            tap = CONV_WIDTH - 1 - back
            y = y + pltpu.roll(ext, back, axis=0)[CONV_PAD:, :] * wconv_ref[tap:tap + 1, lanes]
        y = y * jax.nn.sigmoid(y)
        buf[0:CONV_PAD, lanes] = buf[tm:tm + CONV_PAD, lanes]
        if c < D_MODEL // LANES:
            q_ref[:, lanes] = y.astype(BF16)
        else:
            k_ref[:, c * LANES - D_MODEL:(c + 1) * LANES - D_MODEL] = (y * kscale).astype(BF16)
    for c in range(D_MODEL // wide):
        lanes = slice(c * wide, (c + 1) * wide)
        v_ref[:, lanes] = _dot(xb, wv_ref[:, lanes]).astype(BF16)
    gt_ref[...] = _dot(xb, wgt_ref[...])


def _proj1(x, wqk, wv, wgt, wconv, tm):
    B, S, _ = x.shape
    row = lambda b, i: (b, i, 0)
    const = lambda b, i: (0, 0)
    act_spec = pl.BlockSpec((None, tm, D_MODEL), row)
    act = jax.ShapeDtypeStruct((B, S, D_MODEL), BF16)
    return pl.pallas_call(
        _proj1_kernel,
        grid=(B, S // tm),
        in_specs=[act_spec,
                  pl.BlockSpec((D_MODEL, 2 * D_MODEL), const),
                  pl.BlockSpec((D_MODEL, D_MODEL), const),
                  pl.BlockSpec((D_MODEL, 2 * N_HEADS), const),
                  pl.BlockSpec((CONV_WIDTH, 2 * D_MODEL), const)],
        out_specs=[act_spec, act_spec, act_spec,
                   pl.BlockSpec((None, tm, 2 * N_HEADS), row)],
        out_shape=[act, act, act, jax.ShapeDtypeStruct((B, S, 2 * N_HEADS), F32)],
        scratch_shapes=[pltpu.VMEM((tm + CONV_PAD, 2 * D_MODEL), F32)],
        compiler_params=_cparams(("parallel", "arbitrary"), 56),
        name="proj1",
    )(x, wqk, wv, wgt, wconv)


def _mlstm_kernel(q_ref, k_ref, v_ref, x_ref, wog_ref, gn_ref, gt_ref, bi_row, bf_row, bi_col,
                  bf_col, ng_ref, o_ref, c_sc, n_sc, m_sc, og_sc, *, L):
    ci = pl.program_id(1)

    @pl.when(ci == 0)
    def _():
        c_sc[...] = jnp.zeros(c_sc.shape, F32)
        n_sc[...] = jnp.zeros(n_sc.shape, F32)
        m_sc[...] = jnp.full(m_sc.shape, NEG_INF, F32)

    xb = x_ref[...].astype(BF16)
    wide = 2 * MXU_DIM
    for c in range(D_MODEL // wide):
        og_sc[:, c * wide:(c + 1) * wide] = _dot(xb, wog_ref[:, c * wide:(c + 1) * wide])

    row = lax.broadcasted_iota(jnp.int32, (L, L), 0)
    col = lax.broadcasted_iota(jnp.int32, (L, L), 1)
    causal = col <= row
    lower = jnp.where(causal, 1.0, 0.0).astype(BF16)
    upper = jnp.where(row <= col, 1.0, 0.0).astype(BF16)

    gn = gn_ref[...]
    gt = gt_ref[...]
    i_col = gn[:, :N_HEADS] + bi_row[...]
    lf_col = _log_sigmoid(gn[:, N_HEADS:] + bf_row[...])
    i_row = gt[:N_HEADS, :] + bi_col[...]
    lf_row = _log_sigmoid(gt[N_HEADS:, :] + bf_col[...])
    b_col = sum(_dot(lower, part) for part in _split3(lf_col))
    b_row = sum(_dot(part, upper) for part in _split3(lf_row))

    keep = row <= col
    n_pad = jnp.zeros((2 * N_HEADS - 3, MLSTM_HEAD_DIM), F32)
    for h in range(N_HEADS):
        lanes = slice(h * MLSTM_HEAD_DIM, (h + 1) * MLSTM_HEAD_DIM)
        q = q_ref[:, lanes]
        k = k_ref[:, lanes]
        v = v_ref[:, lanes]
        bt = b_row[h:h + 1, :]
        key_term = i_col[:, h:h + 1] - b_col[:, h:h + 1]
        m_prev = m_sc[h]
        ct_prev = c_sc[h]
        n_prev = n_sc[h]

        dlog = jnp.where(keep, bt + key_term, NEG_INF)
        inter = bt + m_prev
        m_t = jnp.maximum(inter, jnp.max(dlog, axis=0, keepdims=True))
        s = _dot_nt(k, q) * jnp.exp(dlog - m_t)
        inter_w = jnp.exp(inter - m_t)
        num = _dot_tn(v, s.astype(BF16)) + inter_w * _dot_nt(ct_prev.astype(BF16), q)
        n_parts = jnp.concatenate([p.astype(F32) for p in _split3(n_prev)] + [n_pad], axis=0)
        qn = jnp.sum(_dot_nt(n_parts.astype(BF16), q), axis=0, keepdims=True)
        den = jnp.sum(s, axis=0, keepdims=True) + inter_w * qn
        hh = num / jnp.maximum(jnp.abs(den), jnp.exp(-m_t))

        b_last = bt[:, L - 1:L]
        g = b_last + key_term
        m_new = jnp.maximum(b_last + m_prev, jnp.max(g, axis=0, keepdims=True))
        w = jnp.exp(g - m_new)
        decay = jnp.exp(b_last + m_prev - m_new)
        kw = k.astype(F32) * w
        c_sc[h] = decay * ct_prev + _dot_tn(v, kw.astype(BF16))
        n_sc[h] = decay * n_prev + jnp.sum(kw, axis=0, keepdims=True)
        m_sc[h] = m_new

        mu = jnp.mean(hh, axis=0, keepdims=True)
        hc = hh - mu
        var = jnp.mean(hc * hc, axis=0, keepdims=True)
        hn = (hc * lax.rsqrt(var + LN_EPS)).T * ng_ref[:, lanes]
        o_ref[:, lanes] = (hn * jax.nn.sigmoid(og_sc[:, lanes])).astype(BF16)


def _mlstm(q, k, v, x, wog, gates, gates_t, b_i, b_f, norm_g, L):
    B, S, _ = q.shape
    row = lambda b, c: (b, c, 0)
    const = lambda b, c: (0, 0)
    act_spec = pl.BlockSpec((None, L, D_MODEL), row)
    return pl.pallas_call(
        functools.partial(_mlstm_kernel, L=L),
        grid=(B, S // L),
        in_specs=[act_spec, act_spec, act_spec, act_spec,
                  pl.BlockSpec((D_MODEL, D_MODEL), const),
                  pl.BlockSpec((None, L, 2 * N_HEADS), row),
                  pl.BlockSpec((None, 2 * N_HEADS, L), lambda b, c: (b, 0, c)),
                  pl.BlockSpec((1, N_HEADS), const), pl.BlockSpec((1, N_HEADS), const),
                  pl.BlockSpec((N_HEADS, 1), const), pl.BlockSpec((N_HEADS, 1), const),
                  pl.BlockSpec((1, D_MODEL), const)],
        out_specs=act_spec,
        out_shape=jax.ShapeDtypeStruct((B, S, D_MODEL), BF16),
        scratch_shapes=[pltpu.VMEM((N_HEADS, MLSTM_HEAD_DIM, MLSTM_HEAD_DIM), F32),
                        pltpu.VMEM((N_HEADS, 1, MLSTM_HEAD_DIM), F32),
                        pltpu.VMEM((N_HEADS, 1, 1), F32),
                        pltpu.VMEM((L, D_MODEL), F32)],
        compiler_params=_cparams(("parallel", "arbitrary"), 48),
        name="mlstm",
    )(q, k, v, x, wog, gates, gates_t, b_i[None, :], b_f[None, :], b_i[:, None], b_f[:, None],
      norm_g[None, :])


MOE_TOP_K = 2
MOE_TILE = 512
RANK_BLOCK = 1024


def _rank_kernel(sel_ref, rank_ref, count_ref, upper_sc, carry_sc):
    n = sel_ref.shape[1]

    @pl.when(pl.program_id(0) == 0)
    def _():
        row = lax.broadcasted_iota(jnp.int32, (n, n), 0)
        col = lax.broadcasted_iota(jnp.int32, (n, n), 1)
        upper_sc[...] = jnp.where(row <= col, 1.0, 0.0).astype(BF16)
        carry_sc[...] = jnp.zeros(carry_sc.shape, F32)

    sel = sel_ref[...]
    incl = _dot(sel.astype(BF16), upper_sc[...])
    rank_ref[...] = carry_sc[...] + incl - sel
    carry_sc[...] = carry_sc[...] + incl[:, n - 1:n]
    count_ref[...] = carry_sc[...]


def _rank(sel_t):
    E, T = sel_t.shape
    n = min(RANK_BLOCK, T)
    return pl.pallas_call(
        _rank_kernel,
        grid=(T // n,),
        in_specs=[pl.BlockSpec((E, n), lambda i: (0, i))],
        out_specs=[pl.BlockSpec((E, n), lambda i: (0, i)), pl.BlockSpec((E, 1), lambda i: (0, 0))],
        out_shape=[jax.ShapeDtypeStruct((E, T), F32), jax.ShapeDtypeStruct((E, 1), F32)],
        scratch_shapes=[pltpu.VMEM((n, n), BF16), pltpu.VMEM((E, 1), F32)],
        compiler_params=_cparams(("arbitrary",), 32),
        name="moe_rank",
    )(sel_t)


def _slot_kernel(sel_ref, gate_ref, rank_ref, off_ref, pos_ref, w_ref):
    sel = sel_ref[...] > 0.0
    idx = lax.broadcasted_iota(jnp.int32, sel.shape, 0)
    first = jnp.min(jnp.where(sel, idx, N_EXPERTS), axis=0, keepdims=True)
    second = jnp.max(jnp.where(sel, idx, -1), axis=0, keepdims=True)
    slot = off_ref[...] + rank_ref[...]
    gate = gate_ref[...]
    rows_p, rows_w = [], []
    for which in (first, second):
        hit = idx == which
        rows_p.append(jnp.sum(jnp.where(hit, slot, 0.0), axis=0, keepdims=True))
        rows_w.append(jnp.sum(jnp.where(hit, gate, 0.0), axis=0, keepdims=True))
    pos_ref[...] = jnp.concatenate(rows_p, axis=0).astype(jnp.int32)
    w_ref[...] = jnp.concatenate(rows_w, axis=0)


def _slots(sel_t, gate_t, rank_t, offsets_col, tm):
    E, T = sel_t.shape
    spec = pl.BlockSpec((E, tm), lambda i: (0, i))
    out_spec = pl.BlockSpec((MOE_TOP_K, tm), lambda i: (0, i))
    return pl.pallas_call(
        _slot_kernel,
        grid=(T // tm,),
        in_specs=[spec, spec, spec, pl.BlockSpec((E, 1), lambda i: (0, 0))],
        out_specs=[out_spec, out_spec],
        out_shape=[jax.ShapeDtypeStruct((MOE_TOP_K, T), jnp.int32),
                   jax.ShapeDtypeStruct((MOE_TOP_K, T), F32)],
        compiler_params=_cparams(("parallel",), 32),
        name="moe_slots",
    )(sel_t, gate_t, rank_t, offsets_col)


def _dispatch_kernel(tail_ref, used_ref, pos_ref, x_ref, xs_hbm, zero_sc, sem, zsem, *, tm):
    i = pl.program_id(0)
    n_tiles = xs_hbm.shape[0] // MOE_TILE

    def fill(tile):
        return pltpu.make_async_copy(zero_sc, xs_hbm.at[pl.ds(tile * MOE_TILE, MOE_TILE)], zsem)

    @pl.when(i == 0)
    def _():
        zero_sc[...] = jnp.zeros(zero_sc.shape, F32)
        for e in range(N_EXPERTS):
            fill(tail_ref[e]).start()
        for e in range(N_EXPERTS):
            fill(tail_ref[e]).wait()
        for j in range(N_EXPERTS):
            @pl.when(n_tiles - 1 - j >= used_ref[0])
            def _():
                c = fill(n_tiles - 1 - j)
                c.start()
                c.wait()

    def issue(t, carry):
        for k in range(MOE_TOP_K):
            pltpu.make_async_copy(x_ref.at[pl.ds(t, 1)], xs_hbm.at[pl.ds(pos_ref[k, t], 1)],
                                  sem).start(priority=k)
        return carry

    lax.fori_loop(0, tm, issue, 0, unroll=8)
    for _ in range(MOE_TOP_K):
        pltpu.make_async_copy(x_ref, xs_hbm.at[pl.ds(0, tm)], sem).wait()


def _dispatch(x, pos_t, tail_tiles, n_used, n_rows, tm):
    T = x.shape[0]
    n_steps = T // tm
    pos3 = pos_t.reshape(MOE_TOP_K, n_steps, tm).transpose(1, 0, 2)
    return pl.pallas_call(
        functools.partial(_dispatch_kernel, tm=tm),
        grid_spec=pltpu.PrefetchScalarGridSpec(
            num_scalar_prefetch=2,
            grid=(n_steps,),
            in_specs=[pl.BlockSpec((None, MOE_TOP_K, tm), lambda i, tail, used: (i, 0, 0),
                                   memory_space=pltpu.SMEM),
                      pl.BlockSpec((tm, D_MODEL), lambda i, tail, used: (i, 0))],
            out_specs=pl.BlockSpec(memory_space=pl.ANY),
            scratch_shapes=[pltpu.VMEM((MOE_TILE, D_MODEL), F32),
                            pltpu.SemaphoreType.DMA, pltpu.SemaphoreType.DMA],
        ),
        out_shape=jax.ShapeDtypeStruct((n_rows, D_MODEL), F32),
        compiler_params=_cparams(("arbitrary",), 32),
        name="moe_dispatch",
    )(tail_tiles, n_used, pos3, x)


def _expert_ffn_kernel(te_ref, used_ref, x_ref, wg_ref, wu_ref, wd_ref, o_ref):
    live = pl.program_id(0) < used_ref[0]

    @pl.when(live)
    def _():
        xb = x_ref[...].astype(BF16)
        y = None
        for lo, size in _ffn_chunks(wg_ref.shape[1]):
            gate = _dot(xb, wg_ref[:, lo:lo + size])
            up = _dot(xb, wu_ref[:, lo:lo + size])
            act = (gate * jax.nn.sigmoid(gate) * up).astype(BF16)
            part = _dot(act, wd_ref[lo:lo + size, :])
            y = part if y is None else y + part
        o_ref[...] = y

    @pl.when(jnp.logical_not(live))
    def _():
        o_ref[...] = jnp.zeros(o_ref.shape, F32)


def _expert_ffn(xs, tile_expert, n_used, wg, wu, wd):
    n_rows = xs.shape[0]
    E, _, F = wg.shape
    tm = MOE_TILE
    once = pl.Buffered(1)

    def row(i, te, used):
        return (jnp.minimum(i, used[0] - 1), 0)

    return pl.pallas_call(
        _expert_ffn_kernel,
        grid_spec=pltpu.PrefetchScalarGridSpec(
            num_scalar_prefetch=2,
            grid=(n_rows // tm,),
            in_specs=[pl.BlockSpec((tm, D_MODEL), row),
                      pl.BlockSpec((None, D_MODEL, F), lambda i, te, used: (te[i], 0, 0),
                                   pipeline_mode=once),
                      pl.BlockSpec((None, D_MODEL, F), lambda i, te, used: (te[i], 0, 0),
                                   pipeline_mode=once),
                      pl.BlockSpec((None, F, D_MODEL), lambda i, te, used: (te[i], 0, 0),
                                   pipeline_mode=once)],
            out_specs=pl.BlockSpec((tm, D_MODEL), lambda i, te, used: (i, 0)),
        ),
        out_shape=jax.ShapeDtypeStruct((n_rows, D_MODEL), F32),
        compiler_params=_cparams(("arbitrary",), 56),
        name="moe_expert_ffn",
    )(tile_expert, n_used, xs, wg, wu, wd)


def _combine_kernel(pos_ref, w_ref, x_ref, g_ref, b_ref, ys_hbm, o_ref, rows_sc, sem, *, tm):
    def issue(t, carry):
        for k in range(MOE_TOP_K):
            pltpu.make_async_copy(ys_hbm.at[pl.ds(pos_ref[k, t], 1)],
                                  rows_sc.at[k, pl.ds(t, 1)], sem).start(priority=k)
        return carry

    lax.fori_loop(0, tm, issue, 0, unroll=8)
    for k in range(MOE_TOP_K):
        pltpu.make_async_copy(ys_hbm.at[pl.ds(0, tm)], rows_sc.at[k], sem).wait()
    w = w_ref[...]
    y = w[:, 0:1] * rows_sc[0] + w[:, 1:2] * rows_sc[1]
    o_ref[...] = _layer_norm(DEEPNORM_ALPHA * x_ref[...] + y, g_ref[...], b_ref[...])


def _combine_ln(ys, pos_t, w_nat, x, g, b, tm):
    T = x.shape[0]
    n_steps = T // tm
    pos3 = pos_t.reshape(MOE_TOP_K, n_steps, tm).transpose(1, 0, 2)
    row = pl.BlockSpec((tm, D_MODEL), lambda i: (i, 0))
    vec = pl.BlockSpec((1, D_MODEL), lambda i: (0, 0))
    return pl.pallas_call(
        functools.partial(_combine_kernel, tm=tm),
        grid=(n_steps,),
        in_specs=[pl.BlockSpec((None, MOE_TOP_K, tm), lambda i: (i, 0, 0), memory_space=pltpu.SMEM),
                  pl.BlockSpec((tm, MOE_TOP_K), lambda i: (i, 0)),
                  row, vec, vec,
                  pl.BlockSpec(memory_space=pl.ANY)],
        out_specs=row,
        out_shape=jax.ShapeDtypeStruct((T, D_MODEL), F32),
        scratch_shapes=[pltpu.VMEM((MOE_TOP_K, tm, D_MODEL), F32), pltpu.SemaphoreType.DMA],
        compiler_params=_cparams(("arbitrary",), 32),
        name="moe_combine_ln",
    )(pos3, w_nat, x, g, b, ys)


def _moe_sparse(x, sel_t, gate_t, wg, wu, wd, g, b):
    T = x.shape[0]
    tile = MOE_TILE
    n_rows = MOE_TOP_K * T + N_EXPERTS * tile
    n_tiles = n_rows // tile
    rank_t, counts = _rank(sel_t)
    tiles_per = jnp.ceil(counts[:, 0] / tile).astype(jnp.int32)
    tile_end = jnp.cumsum(tiles_per)
    tile_start = tile_end - tiles_per
    n_used = tile_end[-1:]
    tile_expert = jnp.minimum(
        jnp.sum(jnp.arange(n_tiles, dtype=jnp.int32)[:, None] >= tile_end[None, :], axis=1),
        N_EXPERTS - 1).astype(jnp.int32)
    tail_tiles = jnp.maximum(tile_end - 1, 0).astype(jnp.int32)
    offsets_col = (tile_start * tile).astype(F32)[:, None]
    pos_t, w_t = _slots(sel_t, gate_t, rank_t, offsets_col, tm=1024)
    xs = _dispatch(x, pos_t, tail_tiles, n_used, n_rows, tm=512)
    ys = _expert_ffn(xs, tile_expert, n_used, wg, wu, wd)
    return _combine_ln(ys, pos_t, w_t.T, x, g, b, tm=256)


def _rotary_tables(S):
    pos = jnp.arange(S, dtype=jnp.int32)
    inv = ROPE_THETA ** (-jnp.arange(ROPE_HALF, dtype=F32) / ROPE_HALF)
    ang = pos.astype(F32)[:, None] * inv[None, :]
    cos, sin = jnp.cos(ang), jnp.sin(ang)
    ones = jnp.ones((S, HEAD_DIM - ROPE_DIMS), F32)
    zeros = jnp.zeros((S, HEAD_DIM - ROPE_DIMS), F32)
    z8 = jnp.zeros((S, ROPE_HALF), F32)
    cos_h = jnp.concatenate([cos, cos, ones], axis=1)
    sup_h = jnp.concatenate([-sin, z8, zeros], axis=1)
    sdn_h = jnp.concatenate([z8, sin, zeros], axis=1)
    rep = LANES // HEAD_DIM
    return (jnp.tile(cos_h, (1, rep)), jnp.tile(sup_h, (1, rep)), jnp.tile(sdn_h, (1, rep)))


def _even_layer(h, w_in, b_forget, w_out, ln_mix_g, ln_mix_b, w_gate, w_up, w_down,
                ln_ffn_g, ln_ffn_b):
    B, S, _ = h.shape
    T = B * S
    W = ATT_WIDTH
    w_main = jnp.concatenate([w_in[:, :3 * W], w_in[:, 3 * W + N_HEADS:]], axis=1).astype(BF16)
    wf_t = w_in[:, 3 * W:3 * W + N_HEADS].T.astype(BF16)
    cos_t, sup_t, sdn_t = _rotary_tables(S)
    outs = _proj0(h, w_main, wf_t, cos_t, sup_t, sdn_t, tm=min(512, S))
    qa, ka, va = outs[:3]
    qkv_by_dilation = {d: tuple(outs[3 + 3 * n:6 + 3 * n]) for n, d in enumerate(DILATIONS)}
    f_t = outs[-1]

    bias_col = jnp.tile(b_forget, B)[:, None]
    c = _fox_cumsum(f_t.reshape(B * N_HEADS, S), bias_col).reshape(B, N_HEADS, S)
    o_fox = _fox_attention(qa, ka, va, jnp.swapaxes(c, 1, 2), t=min(512, S))
    o_dil = _dilated_attention(qkv_by_dilation)

    w_out_b = w_out.astype(BF16)
    tm = 512
    h2 = _mixer_out_dense_ffn(
        [o_fox.reshape(T, W), o_dil.reshape(T, W)], [w_out_b[:W], w_out_b[W:]],
        h.reshape(T, D_MODEL), ln_mix_g[None, :], ln_mix_b[None, :],
        w_gate.astype(BF16), w_up.astype(BF16), w_down.astype(BF16),
        ln_ffn_g[None, :], ln_ffn_b[None, :], tm)
    return h2.reshape(B, S, D_MODEL)


def _odd_layer(h, w_in, b_igate, b_fgate, w_conv, norm_g, w_out, ln_mix_g, ln_mix_b, w_router,
               w_gate, w_up, w_down, ln_ffn_g, ln_ffn_b):
    B, S, _ = h.shape
    T = B * S
    D = D_MODEL
    wqk = w_in[:, :2 * D].astype(BF16)
    wv = w_in[:, 2 * D:3 * D].astype(BF16)
    wgt = w_in[:, 3 * D:3 * D + 2 * N_HEADS].astype(BF16)
    wog = w_in[:, 3 * D + 2 * N_HEADS:].astype(BF16)
    q, k, v, gates = _proj1(h, wqk, wv, wgt, w_conv, tm=min(512, S))
    gates_t = jnp.swapaxes(gates, 1, 2)
    hm = _mlstm(q, k, v, h, wog, gates, gates_t, b_igate, b_fgate, norm_g, L=min(256, S))

    tm = 512
    h1, sel_t, gate_t = _outproj_router(hm.reshape(T, D), w_out.astype(BF16), h.reshape(T, D),
                                        ln_mix_g[None, :], ln_mix_b[None, :], w_router.T, tm)
    h2 = _moe_sparse(h1, sel_t, gate_t, w_gate.astype(BF16), w_up.astype(BF16),
                     w_down.astype(BF16), ln_ffn_g[None, :], ln_ffn_b[None, :])
    return h2.reshape(B, S, D)


def kernel(x, w_in_e, b_forget_e, w_out_e, ln_mix_g_e, ln_mix_b_e, ffn_w_gate_e, ffn_w_up_e,
           ffn_w_down_e, ln_ffn_g_e, ln_ffn_b_e, w_in_o, b_igate_o, b_fgate_o, w_conv_o,
           mlstm_norm_g_o, w_out_o, ln_mix_g_o, ln_mix_b_o, w_router_o, moe_w_gate_o,
           moe_w_up_o, moe_w_down_o, ln_ffn_g_o, ln_ffn_b_o):
    h = x
    for layer in range(DEPTH):
        i = layer // 2
        if layer % 2 == 0:
            h = _even_layer(h, w_in_e[i], b_forget_e[i], w_out_e[i], ln_mix_g_e[i], ln_mix_b_e[i],
                            ffn_w_gate_e[i], ffn_w_up_e[i], ffn_w_down_e[i], ln_ffn_g_e[i],
                            ln_ffn_b_e[i])
        else:
            h = _odd_layer(h, w_in_o[i], b_igate_o[i], b_fgate_o[i], w_conv_o[i],
                           mlstm_norm_g_o[i], w_out_o[i], ln_mix_g_o[i], ln_mix_b_o[i],
                           w_router_o[i], moe_w_gate_o[i], moe_w_up_o[i], moe_w_down_o[i],
                           ln_ffn_g_o[i], ln_ffn_b_o[i])
    return h
```

```python
import functools
import math

import jax
import jax.numpy as jnp
from jax import lax
from jax.experimental import pallas as pl
from jax.experimental.pallas import tpu as pltpu

F32 = jnp.float32
BF16 = jnp.bfloat16

D_MODEL = 1024
HEAD_DIM = 64
N_HEADS = 8
ATT_WIDTH = N_HEADS * HEAD_DIM
DIL_CONFIGS = ((128, 1), (512, 4), (2048, 16))
ROPE_THETA = 500000.0
ROPE_DIMS = HEAD_DIM // 4
ROPE_HALF = ROPE_DIMS // 2
MLSTM_HEAD_DIM = D_MODEL // N_HEADS
CONV_WIDTH = 4
D_FF_DENSE = 2816
N_EXPERTS = 8
D_FF_EXPERT = 3584
DEPTH = 2
DEEPNORM_ALPHA = (2 * DEPTH) ** 0.25
LN_EPS = 1e-5

LANES = 128
MXU_DIM = 256
BAND = 128
MIB = 1024 * 1024

NEG_INF = float("-inf")


def _cparams(semantics, vmem_mib):
    return pltpu.CompilerParams(dimension_semantics=semantics, vmem_limit_bytes=vmem_mib * MIB)


def _dot(a, b):
    return jnp.dot(a, b, preferred_element_type=F32)


def _dot_nt(a, b):
    return lax.dot_general(a, b, (((1,), (1,)), ((), ())), preferred_element_type=F32)


def _dot_tn(a, b):
    return lax.dot_general(a, b, (((0,), (0,)), ((), ())), preferred_element_type=F32)


def _split3(x):
    hi = x.astype(BF16)
    r = x - hi.astype(F32)
    mid = r.astype(BF16)
    lo = (r - mid.astype(F32)).astype(BF16)
    return hi, mid, lo


def _log_sigmoid(z):
    return -(jnp.maximum(-z, 0.0) + jnp.log1p(jnp.exp(-jnp.abs(z))))


def _layer_norm(z, g, b):
    mu = jnp.mean(z, axis=-1, keepdims=True)
    zc = z - mu
    var = jnp.mean(zc * zc, axis=-1, keepdims=True)
    return zc * lax.rsqrt(var + LN_EPS) * g + b


DILATIONS = tuple(sorted(d for _, d in DIL_CONFIGS))
N_PROJ0 = 6


def _proj0_kernel(x_ref, w_ref, wf_ref, cos_ref, sup_ref, sdn_ref, *refs):
    tm = x_ref.shape[0]
    outs = refs[:N_PROJ0]
    strided = refs[N_PROJ0:-2]
    ft_ref, row_sc = refs[-2:]
    xb = x_ref[...].astype(BF16)
    scale = HEAD_DIM ** -0.5
    for j, o_ref in enumerate(outs):
        full = _dot(xb, w_ref[:, j * ATT_WIDTH:(j + 1) * ATT_WIDTH])
        for c in range(ATT_WIDTH // LANES):
            r = full[:, c * LANES:(c + 1) * LANES]
            if j in (3, 4):
                r = (r * cos_ref[...]
                     + pltpu.roll(r, LANES - ROPE_HALF, axis=1) * sup_ref[...]
                     + pltpu.roll(r, ROPE_HALF, axis=1) * sdn_ref[...])
            if j in (0, 3):
                r = r * scale
            o_ref[:, c * LANES:(c + 1) * LANES] = r.astype(BF16)
            if j >= 3:
                row_sc[0:tm, :] = r
                pieces = [(0, 1)]
                for n, d in enumerate(DILATIONS[1:]):
                    s_ref = strided[3 * n + (j - 3)]
                    step = d // pieces[0][1]
                    rows_prev = tm // pieces[0][1]
                    new_pieces = []
                    vals = []
                    for idx, (res_prev, d_prev) in enumerate(pieces):
                        for b in range(step):
                            src = pl.ds(idx * rows_prev + b, rows_prev // step, stride=step)
                            vals.append(row_sc[src, :])
                            new_pieces.append((res_prev + d_prev * b, d))
                    for idx, ((res, _), val) in enumerate(zip(new_pieces, vals)):
                        lo = res * ATT_WIDTH + c * LANES
                        s_ref[:, lo:lo + LANES] = val.astype(BF16)
                        if d != DILATIONS[-1]:
                            row_sc[idx * (tm // d):(idx + 1) * (tm // d), :] = val
                    pieces = new_pieces
    ft_ref[...] = _dot_nt(wf_ref[...], xb)


def _proj0(x, w_main, wf_t, cos_t, sup_t, sdn_t, tm):
    B, S, _ = x.shape
    n_i = S // tm
    act = jax.ShapeDtypeStruct((B, S, ATT_WIDTH), BF16)
    act_spec = pl.BlockSpec((None, tm, ATT_WIDTH), lambda b, i: (b, i, 0))
    tab_spec = pl.BlockSpec((tm, LANES), lambda b, i: (i, 0))
    out_specs = [act_spec] * N_PROJ0
    out_shape = [act] * N_PROJ0
    for d in DILATIONS[1:]:
        out_specs += [pl.BlockSpec((None, tm // d, d * ATT_WIDTH), lambda b, i: (b, i, 0))] * 3
        out_shape += [jax.ShapeDtypeStruct((B, S // d, d * ATT_WIDTH), BF16)] * 3
    out_specs.append(pl.BlockSpec((None, N_HEADS, tm), lambda b, i: (b, 0, i)))
    out_shape.append(jax.ShapeDtypeStruct((B, N_HEADS, S), F32))
    return pl.pallas_call(
        _proj0_kernel,
        grid=(B, n_i),
        in_specs=[
            pl.BlockSpec((None, tm, D_MODEL), lambda b, i: (b, i, 0)),
            pl.BlockSpec((D_MODEL, N_PROJ0 * ATT_WIDTH), lambda b, i: (0, 0)),
            pl.BlockSpec((N_HEADS, D_MODEL), lambda b, i: (0, 0)),
            tab_spec, tab_spec, tab_spec,
        ],
        out_specs=out_specs,
        out_shape=out_shape,
        scratch_shapes=[pltpu.VMEM((tm, LANES), F32)],
        compiler_params=_cparams(("parallel", "parallel"), 48),
        name="proj0",
    )(x, w_main, wf_t, cos_t, sup_t, sdn_t)


def _fox_cumsum_kernel(f_ref, bias_ref, c_ref):
    S = f_ref.shape[1]
    lf = _log_sigmoid(f_ref[...] + bias_ref[...])
    row = lax.broadcasted_iota(jnp.int32, (S, S), 0)
    col = lax.broadcasted_iota(jnp.int32, (S, S), 1)
    upper = jnp.where(row <= col, 1.0, 0.0).astype(BF16)
    hi, mid, lo = _split3(lf)
    c_ref[...] = _dot(hi, upper) + _dot(mid, upper) + _dot(lo, upper)


def _fox_cumsum(f_t, bias_col):
    R, S = f_t.shape
    return pl.pallas_call(
        _fox_cumsum_kernel,
        grid=(1,),
        in_specs=[pl.BlockSpec((R, S), lambda i: (0, 0)), pl.BlockSpec((R, 1), lambda i: (0, 0))],
        out_specs=pl.BlockSpec((R, S), lambda i: (0, 0)),
        out_shape=jax.ShapeDtypeStruct((R, S), F32),
        compiler_params=_cparams(("arbitrary",), 48),
        name="fox_cumsum",
    )(f_t, bias_col)


def _fox_kernel(q_ref, k_ref, v_ref, c_ref, o_ref, m_sc, l_sc, acc_sc, *, t):
    i = pl.program_id(1)
    j = pl.program_id(2)

    @pl.when(j == 0)
    def _():
        m_sc[...] = jnp.full(m_sc.shape, NEG_INF, F32)
        l_sc[...] = jnp.zeros(l_sc.shape, F32)
        acc_sc[...] = jnp.zeros(acc_sc.shape, F32)

    def step(masked):
        if masked:
            key = lax.broadcasted_iota(jnp.int32, (t, t), 0)
            qry = lax.broadcasted_iota(jnp.int32, (t, t), 1)
            keep = key <= qry
        for h in range(N_HEADS):
            sl = slice(h * HEAD_DIM, (h + 1) * HEAD_DIM)
            s = _dot_nt(k_ref[:, sl], q_ref[:, sl]) - c_ref[:, h:h + 1]
            if masked:
                s = jnp.where(keep, s, NEG_INF)
            m_prev = m_sc[h:h + 1, :]
            m_new = jnp.maximum(m_prev, jnp.max(s, axis=0, keepdims=True))
            alpha = jnp.exp(m_prev - m_new)
            p = jnp.exp(s - m_new)
            l_sc[h:h + 1, :] = alpha * l_sc[h:h + 1, :] + jnp.sum(p, axis=0, keepdims=True)
            acc_sc[sl, :] = alpha * acc_sc[sl, :] + _dot_tn(v_ref[:, sl], p.astype(BF16))
            m_sc[h:h + 1, :] = m_new

    @pl.when(j < i)
    def _():
        step(False)

    @pl.when(j == i)
    def _():
        step(True)
        out_t = jnp.concatenate(
            [acc_sc[h * HEAD_DIM:(h + 1) * HEAD_DIM, :] / l_sc[h:h + 1, :] for h in range(N_HEADS)],
            axis=0)
        o_ref[...] = out_t.T.astype(BF16)


def _fox_attention(q, k, v, c, t):
    B, S, _ = q.shape
    n = S // t
    q_spec = pl.BlockSpec((None, t, ATT_WIDTH), lambda b, i, j: (b, i, 0))
    kv_spec = pl.BlockSpec((None, t, ATT_WIDTH), lambda b, i, j: (b, jnp.minimum(j, i), 0))
    return pl.pallas_call(
        functools.partial(_fox_kernel, t=t),
        grid=(B, n, n),
        in_specs=[q_spec, kv_spec, kv_spec,
                  pl.BlockSpec((None, t, N_HEADS), lambda b, i, j: (b, jnp.minimum(j, i), 0))],
        out_specs=q_spec,
        out_shape=jax.ShapeDtypeStruct((B, S, ATT_WIDTH), BF16),
        scratch_shapes=[pltpu.VMEM((N_HEADS, t), F32), pltpu.VMEM((N_HEADS, t), F32),
                        pltpu.VMEM((ATT_WIDTH, t), F32)],
        compiler_params=_cparams(("parallel", "parallel", "arbitrary"), 48),
        name="fox_attention",
    )(q, k, v, c)


def _dil_kernel(*refs, d, rg, sub, tqu, has_prev, first, last):
    refs = list(refs)
    q_ref, k_ref, v_ref = refs[:3]
    pos = 3
    if has_prev:
        kp_ref, vp_ref = refs[pos:pos + 2]
        pos += 2
    if not first:
        acc_in, st_in = refs[pos:pos + 2]
        pos += 2
    if last:
        o_ref = refs[pos]
    else:
        acc_out, st_out = refs[pos:pos + 2]

    blk = pl.program_id(1)
    if has_prev:
        key = lax.broadcasted_iota(jnp.int32, (2 * BAND, BAND), 0)
        qry = lax.broadcasted_iota(jnp.int32, (2 * BAND, BAND), 1)
        band = jnp.logical_and(key >= qry, key <= qry + BAND)
        band_edge = jnp.logical_and(band, jnp.logical_or(key >= BAND, blk > 0))
    else:
        key = lax.broadcasted_iota(jnp.int32, (tqu, tqu), 0)
        qry = lax.broadcasted_iota(jnp.int32, (tqu, tqu), 1)
        keep = key <= qry
    pair = LANES // HEAD_DIM

    if first and not has_prev and not last and rg % 2 == 0:
        n2 = 2 * tqu
        kk = lax.broadcasted_iota(jnp.int32, (n2, n2), 0)
        qq = lax.broadcasted_iota(jnp.int32, (n2, n2), 1)
        same = (kk >= tqu) == (qq >= tqu)
        keep2 = jnp.logical_and(same, kk <= qq)
        for r2 in range(rg // 2):
            res = (2 * r2, 2 * r2 + 1)
            nats = [pl.ds(pl.program_id(2) * rg + rr, tqu, stride=d) for rr in res]
            ms, ls = [], []
            for slab in range(N_HEADS // pair):
                outs = []
                for hh in range(pair):
                    h = slab * pair + hh
                    sls = [slice(rr * ATT_WIDTH + h * HEAD_DIM, rr * ATT_WIDTH + (h + 1) * HEAD_DIM)
                           for rr in res]
                    q = jnp.concatenate([q_ref[:, sl] for sl in sls], axis=0)
                    k_cat = jnp.concatenate([k_ref[:, sl] for sl in sls], axis=0)
                    v_cat = jnp.concatenate([v_ref[:, sl] for sl in sls], axis=0)
                    s = jnp.where(keep2, _dot_nt(k_cat, q), NEG_INF)
                    m = jnp.max(s, axis=0, keepdims=True)
                    p = jnp.exp(s - m)
                    ls.append(jnp.sum(p, axis=0, keepdims=True))
                    ms.append(m)
                    outs.append(_dot_tn(v_cat, p.astype(BF16)))
                slab_out = jnp.concatenate(outs, axis=0).T
                for j, nat in enumerate(nats):
                    acc_out[slab, nat, :] = slab_out[j * tqu:(j + 1) * tqu, :]
            pad = jnp.zeros((LANES - 2 * N_HEADS, n2), F32)
            st_t = jnp.concatenate(ms + ls + [pad], axis=0).T
            for j, nat in enumerate(nats):
                st_out[nat, :] = st_t[j * tqu:(j + 1) * tqu, :]
        return

    for a in range(sub):
        rows = slice(a * tqu, (a + 1) * tqu)
        for rr in range(rg):
            if d == 1:
                nat = rows
            else:
                nat = pl.ds(a * tqu * d + pl.program_id(2) * rg + rr, tqu, stride=d)
            if not first:
                st_old = st_in[nat, :].T
            ms, ls = [], []
            for slab in range(N_HEADS // pair):
                slab_lanes = slice(rr * ATT_WIDTH + slab * LANES, rr * ATT_WIDTH + (slab + 1) * LANES)
                if not first:
                    acc_old = acc_in[slab, nat, :].T
                outs = []
                for hh in range(pair):
                    h = slab * pair + hh
                    lo = rr * ATT_WIDTH + h * HEAD_DIM
                    sl = slice(lo, lo + HEAD_DIM)
                    q = q_ref[rows, sl]
                    if not has_prev:
                        k_cat, v_cat, mask = k_ref[rows, sl], v_ref[rows, sl], keep
                    elif a == 0:
                        k_cat = jnp.concatenate([kp_ref[:, sl], k_ref[rows, sl]], axis=0)
                        v_cat = jnp.concatenate([vp_ref[:, sl], v_ref[rows, sl]], axis=0)
                    else:
                        krows = slice(a * tqu - BAND, (a + 1) * tqu)
                        k_cat, v_cat = k_ref[krows, sl], v_ref[krows, sl]
                    s_all = _dot_nt(k_cat, q)
                    n_qs = tqu // BAND if has_prev else 1
                    wq_s = tqu // n_qs
                    m_parts, l_parts, acc_parts = [], [], []
                    for qh in range(n_qs):
                        qs = slice(qh * wq_s, (qh + 1) * wq_s)
                        if has_prev:
                            ks = slice(qh * BAND, qh * BAND + 2 * BAND)
                            part_mask = band_edge if (a == 0 and qh == 0) else band
                            s = jnp.where(part_mask, s_all[ks, qs], NEG_INF)
                            v_part = v_cat[ks, :]
                        else:
                            s = jnp.where(mask, s_all, NEG_INF)
                            v_part = v_cat
                        m = jnp.max(s, axis=0, keepdims=True)
                        if not first:
                            m_old = st_old[h:h + 1, qs]
                            l_old = st_old[N_HEADS + h:N_HEADS + h + 1, qs]
                            m_new = jnp.maximum(m, m_old)
                            alpha = jnp.exp(m_old - m_new)
                            m = m_new
                        p = jnp.exp(s - m)
                        l = jnp.sum(p, axis=0, keepdims=True)
                        acc = _dot_tn(v_part, p.astype(BF16))
                        if not first:
                            l = l + alpha * l_old
                            acc = acc + alpha * acc_old[hh * HEAD_DIM:(hh + 1) * HEAD_DIM, qs]
                        if last:
                            acc = acc / l
                        m_parts.append(m)
                        l_parts.append(l)
                        acc_parts.append(acc)
                    outs.append(jnp.concatenate(acc_parts, axis=1))
                    ms.append(jnp.concatenate(m_parts, axis=1))
                    ls.append(jnp.concatenate(l_parts, axis=1))
                slab_out = jnp.concatenate(outs, axis=0).T
                if last:
                    o_ref[rows, slab_lanes] = slab_out.astype(BF16)
                else:
                    acc_out[slab, nat, :] = slab_out
            if not last:
                pad = jnp.zeros((LANES - 2 * N_HEADS, tqu), F32)
                st_out[nat, :] = jnp.concatenate(ms + ls + [pad], axis=0).T


def _dilated_branch(q, k, v, state, dilation, last):
    B, L, _ = q.shape
    d = dilation
    S = L * d
    first = state is None
    assert not (last and d != 1)
    rg = min(4, d)
    tqu = min(2 * BAND, L)
    tqb = min(L, 4 * tqu // rg)
    sub = tqb // tqu
    n_blk = L // tqb
    n_grp = d // rg
    has_prev = L > BAND
    wq = rg * ATT_WIDTH
    n_slab = ATT_WIDTH // LANES

    main = lambda b, i, g: (b, i, g)
    prev = lambda b, i, g: (b, jnp.maximum(i * (tqb // BAND) - 1, 0), g)
    qkv_spec = pl.BlockSpec((None, tqb, wq), main)
    acc_spec = pl.BlockSpec((None, n_slab, tqb * d, LANES), lambda b, i, g: (b, 0, i, 0))
    st_spec = pl.BlockSpec((None, tqb * d, LANES), lambda b, i, g: (b, i, 0))
    in_specs = [qkv_spec] * 3
    args = [q, k, v]
    if has_prev:
        in_specs += [pl.BlockSpec((None, BAND, wq), prev)] * 2
        args += [k, v]
    if not first:
        in_specs += [acc_spec, st_spec]
        args += list(state)
    if last:
        out_specs = pl.BlockSpec((None, tqb, wq), main)
        out_shape = jax.ShapeDtypeStruct((B, S, ATT_WIDTH), BF16)
    else:
        out_specs = [acc_spec, st_spec]
        out_shape = [jax.ShapeDtypeStruct((B, n_slab, S, LANES), F32),
                     jax.ShapeDtypeStruct((B, S, LANES), F32)]
    return pl.pallas_call(
        functools.partial(_dil_kernel, d=d, rg=rg, sub=sub, tqu=tqu, has_prev=has_prev, first=first,
                          last=last),
        grid=(B, n_blk, n_grp),
        in_specs=in_specs,
        out_specs=out_specs,
        out_shape=out_shape,
        compiler_params=_cparams(("parallel", "parallel", "arbitrary"), 48),
        name=f"dilated_d{d}",
    )(*args)


def _dilated_attention(qkv_by_dilation):
    state = None
    order = sorted(DIL_CONFIGS, key=lambda wd: -wd[1])
    for n, (window, d) in enumerate(order):
        assert window // d == BAND
        state = _dilated_branch(*qkv_by_dilation[d], state, d, last=(n == len(order) - 1))
    return state


def _outproj_ln_rows(a_refs, w_refs, x_ref, g_ref, b_ref):
    y = _dot(a_refs[0][...], w_refs[0][...])
    for a_ref, w_ref in zip(a_refs[1:], w_refs[1:]):
        y = y + _dot(a_ref[...], w_ref[...])
    return _layer_norm(DEEPNORM_ALPHA * x_ref[...] + y, g_ref[...], b_ref[...])


def _route_top2(h, wt_ref, sel_ref, gate_ref):
    logits = lax.dot_general(wt_ref[...], h, (((1,), (1,)), ((), ())),
                             preferred_element_type=F32, precision=lax.Precision.HIGHEST)
    idx = lax.broadcasted_iota(jnp.int32, logits.shape, 0)
    m1 = jnp.max(logits, axis=0, keepdims=True)
    i1 = jnp.min(jnp.where(logits == m1, idx, N_EXPERTS), axis=0, keepdims=True)
    pick1 = idx == i1
    rest = jnp.where(pick1, NEG_INF, logits)
    m2 = jnp.max(rest, axis=0, keepdims=True)
    i2 = jnp.min(jnp.where(rest == m2, idx, N_EXPERTS), axis=0, keepdims=True)
    pick2 = idx == i2
    e2 = jnp.exp(m2 - m1)
    w1 = 1.0 / (1.0 + e2)
    w2 = e2 / (1.0 + e2)
    sel_ref[...] = jnp.where(jnp.logical_or(pick1, pick2), 1.0, 0.0)
    gate_ref[...] = jnp.where(pick1, w1, 0.0) + jnp.where(pick2, w2, 0.0)


def _outproj_router_kernel(a_ref, w_ref, x_ref, g_ref, b_ref, wt_ref, o_ref, sel_ref, gate_ref):
    h = _outproj_ln_rows([a_ref], [w_ref], x_ref, g_ref, b_ref)
    o_ref[...] = h
    _route_top2(h, wt_ref, sel_ref, gate_ref)


def _outproj_router(a, w, x, g, b, w_router_t, tm):
    T = x.shape[0]
    row = pl.BlockSpec((tm, D_MODEL), lambda i: (i, 0))
    vec = pl.BlockSpec((1, D_MODEL), lambda i: (0, 0))
    route = pl.BlockSpec((N_EXPERTS, tm), lambda i: (0, i))
    route_shape = jax.ShapeDtypeStruct((N_EXPERTS, T), F32)
    return pl.pallas_call(
        _outproj_router_kernel,
        grid=(T // tm,),
        in_specs=[pl.BlockSpec((tm, a.shape[1]), lambda i: (i, 0)),
                  pl.BlockSpec(w.shape, lambda i: (0, 0)),
                  row, vec, vec,
                  pl.BlockSpec((N_EXPERTS, D_MODEL), lambda i: (0, 0))],
        out_specs=[row, route, route],
        out_shape=[jax.ShapeDtypeStruct((T, D_MODEL), F32), route_shape, route_shape],
        compiler_params=_cparams(("parallel",), 48),
        name="outproj_router",
    )(a, w, x, g, b, w_router_t)


def _ffn_chunks(width):
    chunks, lo = [], 0
    while lo < width:
        size = min(2 * MXU_DIM, width - lo)
        chunks.append((lo, size))
        lo += size
    return chunks


def _dense_ffn_kernel(a0_ref, a1_ref, w0_ref, w1_ref, r_ref, g0_ref, b0_ref,
                      wg_ref, wu_ref, wd_ref, g_ref, b_ref, o_ref):
    x = _outproj_ln_rows([a0_ref, a1_ref], [w0_ref, w1_ref], r_ref, g0_ref, b0_ref)
    xb = x.astype(BF16)
    y = None
    for lo, size in _ffn_chunks(wg_ref.shape[1]):
        gate = _dot(xb, wg_ref[:, lo:lo + size])
        up = _dot(xb, wu_ref[:, lo:lo + size])
        act = (gate * jax.nn.sigmoid(gate) * up).astype(BF16)
        part = _dot(act, wd_ref[lo:lo + size, :])
        y = part if y is None else y + part
    o_ref[...] = _layer_norm(DEEPNORM_ALPHA * x + y, g_ref[...], b_ref[...])


def _mixer_out_dense_ffn(acts, ws, resid, g_mix, b_mix, wg, wu, wd, g, b, tm):
    T = resid.shape[0]
    F = wg.shape[1]
    row = pl.BlockSpec((tm, D_MODEL), lambda i: (i, 0))
    vec = pl.BlockSpec((1, D_MODEL), lambda i: (0, 0))
    once = pl.Buffered(1)
    const = lambda i: (0, 0)
    return pl.pallas_call(
        _dense_ffn_kernel,
        grid=(T // tm,),
        in_specs=[pl.BlockSpec((tm, a.shape[1]), lambda i: (i, 0)) for a in acts]
        + [pl.BlockSpec(w.shape, const, pipeline_mode=once) for w in ws]
        + [row, vec, vec,
           pl.BlockSpec((D_MODEL, F), const, pipeline_mode=once),
           pl.BlockSpec((D_MODEL, F), const, pipeline_mode=once),
           pl.BlockSpec((F, D_MODEL), const, pipeline_mode=once),
           vec, vec],
        out_specs=row,
        out_shape=jax.ShapeDtypeStruct((T, D_MODEL), F32),
        compiler_params=_cparams(("parallel",), 56),
        name="mixer_out_dense_ffn",
    )(*acts, *ws, resid, g_mix, b_mix, wg, wu, wd, g, b)


CONV_PAD = 8


def _proj1_kernel(x_ref, wqk_ref, wv_ref, wgt_ref, wconv_ref,
                  q_ref, k_ref, v_ref, gt_ref, buf):
    i = pl.program_id(1)
    tm = x_ref.shape[0]
    xb = x_ref[...].astype(BF16)
    kscale = MLSTM_HEAD_DIM ** -0.5

    @pl.when(i == 0)
    def _():
        buf[0:CONV_PAD, :] = jnp.zeros((CONV_PAD, buf.shape[1]), F32)

    wide = 2 * MXU_DIM
    for c in range(2 * D_MODEL // wide):
        lanes = slice(c * wide, (c + 1) * wide)
        buf[CONV_PAD:CONV_PAD + tm, lanes] = _dot(xb, wqk_ref[:, lanes])
    for c in range(2 * D_MODEL // LANES):
        lanes = slice(c * LANES, (c + 1) * LANES)
        ext = buf[0:CONV_PAD + tm, lanes]
        y = ext[CONV_PAD:, :] * wconv_ref[CONV_WIDTH - 1:CONV_WIDTH, lanes]
        for back in range(1, CONV_WIDTH):
            tap = CONV_WIDTH - 1 - back
            y = y + pltpu.roll(ext, back, axis=0)[CONV_PAD:, :] * wconv_ref[tap:tap + 1, lanes]
        y = y * jax.nn.sigmoid(y)
        buf[0:CONV_PAD, lanes] = buf[tm:tm + CONV_PAD, lanes]
        if c < D_MODEL // LANES:
            q_ref[:, lanes] = y.astype(BF16)
        else:
            k_ref[:, c * LANES - D_MODEL:(c + 1) * LANES - D_MODEL] = (y * kscale).astype(BF16)
    for c in range(D_MODEL // wide):
        lanes = slice(c * wide, (c + 1) * wide)
        v_ref[:, lanes] = _dot(xb, wv_ref[:, lanes]).astype(BF16)
    gt_ref[...] = _dot(xb, wgt_ref[...])


def _proj1(x, wqk, wv, wgt, wconv, tm):
    B, S, _ = x.shape
    row = lambda b, i: (b, i, 0)
    const = lambda b, i: (0, 0)
    act_spec = pl.BlockSpec((None, tm, D_MODEL), row)
    act = jax.ShapeDtypeStruct((B, S, D_MODEL), BF16)
    return pl.pallas_call(
        _proj1_kernel,
        grid=(B, S // tm),
        in_specs=[act_spec,
                  pl.BlockSpec((D_MODEL, 2 * D_MODEL), const),
                  pl.BlockSpec((D_MODEL, D_MODEL), const),
                  pl.BlockSpec((D_MODEL, 2 * N_HEADS), const),
                  pl.BlockSpec((CONV_WIDTH, 2 * D_MODEL), const)],
        out_specs=[act_spec, act_spec, act_spec,
                   pl.BlockSpec((None, tm, 2 * N_HEADS), row)],
        out_shape=[act, act, act, jax.ShapeDtypeStruct((B, S, 2 * N_HEADS), F32)],
        scratch_shapes=[pltpu.VMEM((tm + CONV_PAD, 2 * D_MODEL), F32)],
        compiler_params=_cparams(("parallel", "arbitrary"), 56),
        name="proj1",
    )(x, wqk, wv, wgt, wconv)


def _mlstm_kernel(q_ref, k_ref, v_ref, x_ref, wog_ref, gn_ref, gt_ref, bi_row, bf_row, bi_col,
                  bf_col, ng_ref, o_ref, c_sc, n_sc, m_sc, og_sc, *, L):
    ci = pl.program_id(1)

    @pl.when(ci == 0)
    def _():
        c_sc[...] = jnp.zeros(c_sc.shape, F32)
        n_sc[...] = jnp.zeros(n_sc.shape, F32)
        m_sc[...] = jnp.full(m_sc.shape, NEG_INF, F32)

    xb = x_ref[...].astype(BF16)
    wide = 2 * MXU_DIM
    for c in range(D_MODEL // wide):
        og_sc[:, c * wide:(c + 1) * wide] = _dot(xb, wog_ref[:, c * wide:(c + 1) * wide])

    row = lax.broadcasted_iota(jnp.int32, (L, L), 0)
    col = lax.broadcasted_iota(jnp.int32, (L, L), 1)
    causal = col <= row
    lower = jnp.where(causal, 1.0, 0.0).astype(BF16)
    upper = jnp.where(row <= col, 1.0, 0.0).astype(BF16)

    gn = gn_ref[...]
    gt = gt_ref[...]
    i_col = gn[:, :N_HEADS] + bi_row[...]
    lf_col = _log_sigmoid(gn[:, N_HEADS:] + bf_row[...])
    i_row = gt[:N_HEADS, :] + bi_col[...]
    lf_row = _log_sigmoid(gt[N_HEADS:, :] + bf_col[...])
    b_col = sum(_dot(lower, part) for part in _split3(lf_col))
    b_row = sum(_dot(part, upper) for part in _split3(lf_row))

    keep = row <= col
    n_pad = jnp.zeros((2 * N_HEADS - 3, MLSTM_HEAD_DIM), F32)
    for h in range(N_HEADS):
        lanes = slice(h * MLSTM_HEAD_DIM, (h + 1) * MLSTM_HEAD_DIM)
        q = q_ref[:, lanes]
        k = k_ref[:, lanes]
        v = v_ref[:, lanes]
        bt = b_row[h:h + 1, :]
        key_term = i_col[:, h:h + 1] - b_col[:, h:h + 1]
        m_prev = m_sc[h]
        ct_prev = c_sc[h]
        n_prev = n_sc[h]

        dlog = jnp.where(keep, bt + key_term, NEG_INF)
        inter = bt + m_prev
        m_t = jnp.maximum(inter, jnp.max(dlog, axis=0, keepdims=True))
        s = _dot_nt(k, q) * jnp.exp(dlog - m_t)
        inter_w = jnp.exp(inter - m_t)
        num = _dot_tn(v, s.astype(BF16)) + inter_w * _dot_nt(ct_prev.astype(BF16), q)
        n_parts = jnp.concatenate([p.astype(F32) for p in _split3(n_prev)] + [n_pad], axis=0)
        qn = jnp.sum(_dot_nt(n_parts.astype(BF16), q), axis=0, keepdims=True)
        den = jnp.sum(s, axis=0, keepdims=True) + inter_w * qn
        hh = num / jnp.maximum(jnp.abs(den), jnp.exp(-m_t))

        b_last = bt[:, L - 1:L]
        g = b_last + key_term
        m_new = jnp.maximum(b_last + m_prev, jnp.max(g, axis=0, keepdims=True))
        w = jnp.exp(g - m_new)
        decay = jnp.exp(b_last + m_prev - m_new)
        kw = k.astype(F32) * w
        c_sc[h] = decay * ct_prev + _dot_tn(v, kw.astype(BF16))
        n_sc[h] = decay * n_prev + jnp.sum(kw, axis=0, keepdims=True)
        m_sc[h] = m_new

        mu = jnp.mean(hh, axis=0, keepdims=True)
        hc = hh - mu
        var = jnp.mean(hc * hc, axis=0, keepdims=True)
        hn = (hc * lax.rsqrt(var + LN_EPS)).T * ng_ref[:, lanes]
        o_ref[:, lanes] = (hn * jax.nn.sigmoid(og_sc[:, lanes])).astype(BF16)


def _mlstm(q, k, v, x, wog, gates, gates_t, b_i, b_f, norm_g, L):
    B, S, _ = q.shape
    row = lambda b, c: (b, c, 0)
    const = lambda b, c: (0, 0)
    act_spec = pl.BlockSpec((None, L, D_MODEL), row)
    return pl.pallas_call(
        functools.partial(_mlstm_kernel, L=L),
        grid=(B, S // L),
        in_specs=[act_spec, act_spec, act_spec, act_spec,
                  pl.BlockSpec((D_MODEL, D_MODEL), const),
                  pl.BlockSpec((None, L, 2 * N_HEADS), row),
                  pl.BlockSpec((None, 2 * N_HEADS, L), lambda b, c: (b, 0, c)),
                  pl.BlockSpec((1, N_HEADS), const), pl.BlockSpec((1, N_HEADS), const),
                  pl.BlockSpec((N_HEADS, 1), const), pl.BlockSpec((N_HEADS, 1), const),
                  pl.BlockSpec((1, D_MODEL), const)],
        out_specs=act_spec,
        out_shape=jax.ShapeDtypeStruct((B, S, D_MODEL), BF16),
        scratch_shapes=[pltpu.VMEM((N_HEADS, MLSTM_HEAD_DIM, MLSTM_HEAD_DIM), F32),
                        pltpu.VMEM((N_HEADS, 1, MLSTM_HEAD_DIM), F32),
                        pltpu.VMEM((N_HEADS, 1, 1), F32),
                        pltpu.VMEM((L, D_MODEL), F32)],
        compiler_params=_cparams(("parallel", "arbitrary"), 48),
        name="mlstm",
    )(q, k, v, x, wog, gates, gates_t, b_i[None, :], b_f[None, :], b_i[:, None], b_f[:, None],
      norm_g[None, :])


MOE_TOP_K = 2
MOE_TILE = 512
RANK_BLOCK = 1024


def _rank_kernel(sel_ref, rank_ref, count_ref, upper_sc, carry_sc):
    n = sel_ref.shape[1]

    @pl.when(pl.program_id(0) == 0)
    def _():
        row = lax.broadcasted_iota(jnp.int32, (n, n), 0)
        col = lax.broadcasted_iota(jnp.int32, (n, n), 1)
        upper_sc[...] = jnp.where(row <= col, 1.0, 0.0).astype(BF16)
        carry_sc[...] = jnp.zeros(carry_sc.shape, F32)

    sel = sel_ref[...]
    incl = _dot(sel.astype(BF16), upper_sc[...])
    rank_ref[...] = carry_sc[...] + incl - sel
    carry_sc[...] = carry_sc[...] + incl[:, n - 1:n]
    count_ref[...] = carry_sc[...]


def _rank(sel_t):
    E, T = sel_t.shape
    n = min(RANK_BLOCK, T)
    return pl.pallas_call(
        _rank_kernel,
        grid=(T // n,),
        in_specs=[pl.BlockSpec((E, n), lambda i: (0, i))],
        out_specs=[pl.BlockSpec((E, n), lambda i: (0, i)), pl.BlockSpec((E, 1), lambda i: (0, 0))],
        out_shape=[jax.ShapeDtypeStruct((E, T), F32), jax.ShapeDtypeStruct((E, 1), F32)],
        scratch_shapes=[pltpu.VMEM((n, n), BF16), pltpu.VMEM((E, 1), F32)],
        compiler_params=_cparams(("arbitrary",), 32),
        name="moe_rank",
    )(sel_t)


def _slot_kernel(sel_ref, gate_ref, rank_ref, off_ref, pos_ref, w_ref):
    sel = sel_ref[...] > 0.0
    idx = lax.broadcasted_iota(jnp.int32, sel.shape, 0)
    first = jnp.min(jnp.where(sel, idx, N_EXPERTS), axis=0, keepdims=True)
    second = jnp.max(jnp.where(sel, idx, -1), axis=0, keepdims=True)
    slot = off_ref[...] + rank_ref[...]
    gate = gate_ref[...]
    rows_p, rows_w = [], []
    for which in (first, second):
        hit = idx == which
        rows_p.append(jnp.sum(jnp.where(hit, slot, 0.0), axis=0, keepdims=True))
        rows_w.append(jnp.sum(jnp.where(hit, gate, 0.0), axis=0, keepdims=True))
    pos_ref[...] = jnp.concatenate(rows_p, axis=0).astype(jnp.int32)
    w_ref[...] = jnp.concatenate(rows_w, axis=0)


def _slots(sel_t, gate_t, rank_t, offsets_col, tm):
    E, T = sel_t.shape
    spec = pl.BlockSpec((E, tm), lambda i: (0, i))
    out_spec = pl.BlockSpec((MOE_TOP_K, tm), lambda i: (0, i))
    return pl.pallas_call(
        _slot_kernel,
        grid=(T // tm,),
        in_specs=[spec, spec, spec, pl.BlockSpec((E, 1), lambda i: (0, 0))],
        out_specs=[out_spec, out_spec],
        out_shape=[jax.ShapeDtypeStruct((MOE_TOP_K, T), jnp.int32),
                   jax.ShapeDtypeStruct((MOE_TOP_K, T), F32)],
        compiler_params=_cparams(("parallel",), 32),
        name="moe_slots",
    )(sel_t, gate_t, rank_t, offsets_col)


def _dispatch_kernel(tail_ref, used_ref, pos_ref, x_ref, xs_hbm, zero_sc, sem, zsem, *, tm):
    i = pl.program_id(0)
    n_tiles = xs_hbm.shape[0] // MOE_TILE

    def fill(tile):
        return pltpu.make_async_copy(zero_sc, xs_hbm.at[pl.ds(tile * MOE_TILE, MOE_TILE)], zsem)

    @pl.when(i == 0)
    def _():
        zero_sc[...] = jnp.zeros(zero_sc.shape, F32)
        for e in range(N_EXPERTS):
            fill(tail_ref[e]).start()
        for e in range(N_EXPERTS):
            fill(tail_ref[e]).wait()
        for j in range(N_EXPERTS):
            @pl.when(n_tiles - 1 - j >= used_ref[0])
            def _():
                c = fill(n_tiles - 1 - j)
                c.start()
                c.wait()

    def issue(t, carry):
        for k in range(MOE_TOP_K):
            pltpu.make_async_copy(x_ref.at[pl.ds(t, 1)], xs_hbm.at[pl.ds(pos_ref[k, t], 1)],
                                  sem).start(priority=k)
        return carry

    lax.fori_loop(0, tm, issue, 0, unroll=8)
    for _ in range(MOE_TOP_K):
        pltpu.make_async_copy(x_ref, xs_hbm.at[pl.ds(0, tm)], sem).wait()


def _dispatch(x, pos_t, tail_tiles, n_used, n_rows, tm):
    T = x.shape[0]
    n_steps = T // tm
    pos3 = pos_t.reshape(MOE_TOP_K, n_steps, tm).transpose(1, 0, 2)
    return pl.pallas_call(
        functools.partial(_dispatch_kernel, tm=tm),
        grid_spec=pltpu.PrefetchScalarGridSpec(
            num_scalar_prefetch=2,
            grid=(n_steps,),
            in_specs=[pl.BlockSpec((None, MOE_TOP_K, tm), lambda i, tail, used: (i, 0, 0),
                                   memory_space=pltpu.SMEM),
                      pl.BlockSpec((tm, D_MODEL), lambda i, tail, used: (i, 0))],
            out_specs=pl.BlockSpec(memory_space=pl.ANY),
            scratch_shapes=[pltpu.VMEM((MOE_TILE, D_MODEL), F32),
                            pltpu.SemaphoreType.DMA, pltpu.SemaphoreType.DMA],
        ),
        out_shape=jax.ShapeDtypeStruct((n_rows, D_MODEL), F32),
        compiler_params=_cparams(("arbitrary",), 32),
        name="moe_dispatch",
    )(tail_tiles, n_used, pos3, x)


def _expert_ffn_kernel(te_ref, used_ref, x_ref, wg_ref, wu_ref, wd_ref, o_ref):
    live = pl.program_id(0) < used_ref[0]

    @pl.when(live)
    def _():
        xb = x_ref[...].astype(BF16)
        y = None
        for lo, size in _ffn_chunks(wg_ref.shape[1]):
            gate = _dot(xb, wg_ref[:, lo:lo + size])
            up = _dot(xb, wu_ref[:, lo:lo + size])
            act = (gate * jax.nn.sigmoid(gate) * up).astype(BF16)
            part = _dot(act, wd_ref[lo:lo + size, :])
            y = part if y is None else y + part
        o_ref[...] = y

    @pl.when(jnp.logical_not(live))
    def _():
        o_ref[...] = jnp.zeros(o_ref.shape, F32)


def _expert_ffn(xs, tile_expert, n_used, wg, wu, wd):
    n_rows = xs.shape[0]
    E, _, F = wg.shape
    tm = MOE_TILE
    once = pl.Buffered(1)

    def row(i, te, used):
        return (jnp.minimum(i, used[0] - 1), 0)

    return pl.pallas_call(
        _expert_ffn_kernel,
        grid_spec=pltpu.PrefetchScalarGridSpec(
            num_scalar_prefetch=2,
            grid=(n_rows // tm,),
            in_specs=[pl.BlockSpec((tm, D_MODEL), row),
                      pl.BlockSpec((None, D_MODEL, F), lambda i, te, used: (te[i], 0, 0),
                                   pipeline_mode=once),
                      pl.BlockSpec((None, D_MODEL, F), lambda i, te, used: (te[i], 0, 0),
                                   pipeline_mode=once),
                      pl.BlockSpec((None, F, D_MODEL), lambda i, te, used: (te[i], 0, 0),
                                   pipeline_mode=once)],
            out_specs=pl.BlockSpec((tm, D_MODEL), lambda i, te, used: (i, 0)),
        ),
        out_shape=jax.ShapeDtypeStruct((n_rows, D_MODEL), F32),
        compiler_params=_cparams(("arbitrary",), 56),
        name="moe_expert_ffn",
    )(tile_expert, n_used, xs, wg, wu, wd)


def _combine_kernel(pos_ref, w_ref, x_ref, g_ref, b_ref, ys_hbm, o_ref, rows_sc, sem, *, tm):
    def issue(t, carry):
        for k in range(MOE_TOP_K):
            pltpu.make_async_copy(ys_hbm.at[pl.ds(pos_ref[k, t], 1)],
                                  rows_sc.at[k, pl.ds(t, 1)], sem).start(priority=k)
        return carry

    lax.fori_loop(0, tm, issue, 0, unroll=8)
    for k in range(MOE_TOP_K):
        pltpu.make_async_copy(ys_hbm.at[pl.ds(0, tm)], rows_sc.at[k], sem).wait()
    w = w_ref[...]
    y = w[:, 0:1] * rows_sc[0] + w[:, 1:2] * rows_sc[1]
    o_ref[...] = _layer_norm(DEEPNORM_ALPHA * x_ref[...] + y, g_ref[...], b_ref[...])


def _combine_ln(ys, pos_t, w_nat, x, g, b, tm):
    T = x.shape[0]
    n_steps = T // tm
    pos3 = pos_t.reshape(MOE_TOP_K, n_steps, tm).transpose(1, 0, 2)
    row = pl.BlockSpec((tm, D_MODEL), lambda i: (i, 0))
    vec = pl.BlockSpec((1, D_MODEL), lambda i: (0, 0))
    return pl.pallas_call(
        functools.partial(_combine_kernel, tm=tm),
        grid=(n_steps,),
        in_specs=[pl.BlockSpec((None, MOE_TOP_K, tm), lambda i: (i, 0, 0), memory_space=pltpu.SMEM),
                  pl.BlockSpec((tm, MOE_TOP_K), lambda i: (i, 0)),
                  row, vec, vec,
                  pl.BlockSpec(memory_space=pl.ANY)],
        out_specs=row,
        out_shape=jax.ShapeDtypeStruct((T, D_MODEL), F32),
        scratch_shapes=[pltpu.VMEM((MOE_TOP_K, tm, D_MODEL), F32), pltpu.SemaphoreType.DMA],
        compiler_params=_cparams(("arbitrary",), 32),
        name="moe_combine_ln",
    )(pos3, w_nat, x, g, b, ys)


def _moe_sparse(x, sel_t, gate_t, wg, wu, wd, g, b):
    T = x.shape[0]
    tile = MOE_TILE
    n_rows = MOE_TOP_K * T + N_EXPERTS * tile
    n_tiles = n_rows // tile
    rank_t, counts = _rank(sel_t)
    tiles_per = jnp.ceil(counts[:, 0] / tile).astype(jnp.int32)
    tile_end = jnp.cumsum(tiles_per)
    tile_start = tile_end - tiles_per
    n_used = tile_end[-1:]
    tile_expert = jnp.minimum(
        jnp.sum(jnp.arange(n_tiles, dtype=jnp.int32)[:, None] >= tile_end[None, :], axis=1),
        N_EXPERTS - 1).astype(jnp.int32)
    tail_tiles = jnp.maximum(tile_end - 1, 0).astype(jnp.int32)
    offsets_col = (tile_start * tile).astype(F32)[:, None]
    pos_t, w_t = _slots(sel_t, gate_t, rank_t, offsets_col, tm=1024)
    xs = _dispatch(x, pos_t, tail_tiles, n_used, n_rows, tm=512)
    ys = _expert_ffn(xs, tile_expert, n_used, wg, wu, wd)
    return _combine_ln(ys, pos_t, w_t.T, x, g, b, tm=512)


def _rotary_tables(S):
    pos = jnp.arange(S, dtype=jnp.int32)
    inv = ROPE_THETA ** (-jnp.arange(ROPE_HALF, dtype=F32) / ROPE_HALF)
    ang = pos.astype(F32)[:, None] * inv[None, :]
    cos, sin = jnp.cos(ang), jnp.sin(ang)
    ones = jnp.ones((S, HEAD_DIM - ROPE_DIMS), F32)
    zeros = jnp.zeros((S, HEAD_DIM - ROPE_DIMS), F32)
    z8 = jnp.zeros((S, ROPE_HALF), F32)
    cos_h = jnp.concatenate([cos, cos, ones], axis=1)
    sup_h = jnp.concatenate([-sin, z8, zeros], axis=1)
    sdn_h = jnp.concatenate([z8, sin, zeros], axis=1)
    rep = LANES // HEAD_DIM
    return (jnp.tile(cos_h, (1, rep)), jnp.tile(sup_h, (1, rep)), jnp.tile(sdn_h, (1, rep)))


def _even_layer(h, w_in, b_forget, w_out, ln_mix_g, ln_mix_b, w_gate, w_up, w_down,
                ln_ffn_g, ln_ffn_b):
    B, S, _ = h.shape
    T = B * S
    W = ATT_WIDTH
    w_main = jnp.concatenate([w_in[:, :3 * W], w_in[:, 3 * W + N_HEADS:]], axis=1).astype(BF16)
    wf_t = w_in[:, 3 * W:3 * W + N_HEADS].T.astype(BF16)
    cos_t, sup_t, sdn_t = _rotary_tables(S)
    outs = _proj0(h, w_main, wf_t, cos_t, sup_t, sdn_t, tm=min(512, S))
    qa, ka, va = outs[:3]
    qkv_by_dilation = {d: tuple(outs[3 + 3 * n:6 + 3 * n]) for n, d in enumerate(DILATIONS)}
    f_t = outs[-1]

    bias_col = jnp.tile(b_forget, B)[:, None]
    c = _fox_cumsum(f_t.reshape(B * N_HEADS, S), bias_col).reshape(B, N_HEADS, S)
    o_fox = _fox_attention(qa, ka, va, jnp.swapaxes(c, 1, 2), t=min(512, S))
    o_dil = _dilated_attention(qkv_by_dilation)

    w_out_b = w_out.astype(BF16)
    tm = 512
    h2 = _mixer_out_dense_ffn(
        [o_fox.reshape(T, W), o_dil.reshape(T, W)], [w_out_b[:W], w_out_b[W:]],
        h.reshape(T, D_MODEL), ln_mix_g[None, :], ln_mix_b[None, :],
        w_gate.astype(BF16), w_up.astype(BF16), w_down.astype(BF16),
        ln_ffn_g[None, :], ln_ffn_b[None, :], tm)
    return h2.reshape(B, S, D_MODEL)


def _odd_layer(h, w_in, b_igate, b_fgate, w_conv, norm_g, w_out, ln_mix_g, ln_mix_b, w_router,
               w_gate, w_up, w_down, ln_ffn_g, ln_ffn_b):
    B, S, _ = h.shape
    T = B * S
    D = D_MODEL
    wqk = w_in[:, :2 * D].astype(BF16)
    wv = w_in[:, 2 * D:3 * D].astype(BF16)
    wgt = w_in[:, 3 * D:3 * D + 2 * N_HEADS].astype(BF16)
    wog = w_in[:, 3 * D + 2 * N_HEADS:].astype(BF16)
    q, k, v, gates = _proj1(h, wqk, wv, wgt, w_conv, tm=min(512, S))
    gates_t = jnp.swapaxes(gates, 1, 2)
    hm = _mlstm(q, k, v, h, wog, gates, gates_t, b_igate, b_fgate, norm_g, L=min(256, S))

    tm = 512
    h1, sel_t, gate_t = _outproj_router(hm.reshape(T, D), w_out.astype(BF16), h.reshape(T, D),
                                        ln_mix_g[None, :], ln_mix_b[None, :], w_router.T, tm)
    h2 = _moe_sparse(h1, sel_t, gate_t, w_gate.astype(BF16), w_up.astype(BF16),
                     w_down.astype(BF16), ln_ffn_g[None, :], ln_ffn_b[None, :])
    return h2.reshape(B, S, D)


def kernel(x, w_in_e, b_forget_e, w_out_e, ln_mix_g_e, ln_mix_b_e, ffn_w_gate_e, ffn_w_up_e,
           ffn_w_down_e, ln_ffn_g_e, ln_ffn_b_e, w_in_o, b_igate_o, b_fgate_o, w_conv_o,
           mlstm_norm_g_o, w_out_o, ln_mix_g_o, ln_mix_b_o, w_router_o, moe_w_gate_o,
           moe_w_up_o, moe_w_down_o, ln_ffn_g_o, ln_ffn_b_o):
    h = x
    for layer in range(DEPTH):
        i = layer // 2
        if layer % 2 == 0:
            h = _even_layer(h, w_in_e[i], b_forget_e[i], w_out_e[i], ln_mix_g_e[i], ln_mix_b_e[i],
                            ffn_w_gate_e[i], ffn_w_up_e[i], ffn_w_down_e[i], ln_ffn_g_e[i],
                            ln_ffn_b_e[i])
        else:
            h = _odd_layer(h, w_in_o[i], b_igate_o[i], b_fgate_o[i], w_conv_o[i],
                           mlstm_norm_g_o[i], w_out_o[i], ln_mix_g_o[i], ln_mix_b_o[i],
                           w_router_o[i], moe_w_gate_o[i], moe_w_up_o[i], moe_w_down_o[i],
                           ln_ffn_g_o[i], ln_ffn_b_o[i])
    return h
```

```python
import functools
import math

import jax
import jax.numpy as jnp
from jax import lax
from jax.experimental import pallas as pl
from jax.experimental.pallas import tpu as pltpu

F32 = jnp.float32
BF16 = jnp.bfloat16

D_MODEL = 1024
HEAD_DIM = 64
N_HEADS = 8
ATT_WIDTH = N_HEADS * HEAD_DIM
DIL_CONFIGS = ((128, 1), (512, 4), (2048, 16))
ROPE_THETA = 500000.0
ROPE_DIMS = HEAD_DIM // 4
ROPE_HALF = ROPE_DIMS // 2
MLSTM_HEAD_DIM = D_MODEL // N_HEADS
CONV_WIDTH = 4
D_FF_DENSE = 2816
N_EXPERTS = 8
D_FF_EXPERT = 3584
DEPTH = 2
DEEPNORM_ALPHA = (2 * DEPTH) ** 0.25
LN_EPS = 1e-5

LANES = 128
MXU_DIM = 256
BAND = 128
MIB = 1024 * 1024

NEG_INF = float("-inf")


def _cparams(semantics, vmem_mib):
    return pltpu.CompilerParams(dimension_semantics=semantics, vmem_limit_bytes=vmem_mib * MIB)


def _dot(a, b):
    return jnp.dot(a, b, preferred_element_type=F32)


def _dot_nt(a, b):
    return lax.dot_general(a, b, (((1,), (1,)), ((), ())), preferred_element_type=F32)


def _dot_tn(a, b):
    return lax.dot_general(a, b, (((0,), (0,)), ((), ())), preferred_element_type=F32)


def _split3(x):
    hi = x.astype(BF16)
    r = x - hi.astype(F32)
    mid = r.astype(BF16)
    lo = (r - mid.astype(F32)).astype(BF16)
    return hi, mid, lo


def _log_sigmoid(z):
    return -(jnp.maximum(-z, 0.0) + jnp.log1p(jnp.exp(-jnp.abs(z))))


def _layer_norm(z, g, b):
    mu = jnp.mean(z, axis=-1, keepdims=True)
    zc = z - mu
    var = jnp.mean(zc * zc, axis=-1, keepdims=True)
    return zc * lax.rsqrt(var + LN_EPS) * g + b


DILATIONS = tuple(sorted(d for _, d in DIL_CONFIGS))
N_PROJ0 = 6


def _proj0_kernel(x_ref, w_ref, wf_ref, cos_ref, sup_ref, sdn_ref, *refs):
    tm = x_ref.shape[0]
    outs = refs[:N_PROJ0]
    strided = refs[N_PROJ0:-2]
    ft_ref, row_sc = refs[-2:]
    xb = x_ref[...].astype(BF16)
    scale = HEAD_DIM ** -0.5
    for j, o_ref in enumerate(outs):
        full = _dot(xb, w_ref[:, j * ATT_WIDTH:(j + 1) * ATT_WIDTH])
        for c in range(ATT_WIDTH // LANES):
            r = full[:, c * LANES:(c + 1) * LANES]
            if j in (3, 4):
                r = (r * cos_ref[...]
                     + pltpu.roll(r, LANES - ROPE_HALF, axis=1) * sup_ref[...]
                     + pltpu.roll(r, ROPE_HALF, axis=1) * sdn_ref[...])
            if j in (0, 3):
                r = r * scale
            o_ref[:, c * LANES:(c + 1) * LANES] = r.astype(BF16)
            if j >= 3:
                row_sc[0:tm, :] = r
                pieces = [(0, 1)]
                for n, d in enumerate(DILATIONS[1:]):
                    s_ref = strided[3 * n + (j - 3)]
                    step = d // pieces[0][1]
                    rows_prev = tm // pieces[0][1]
                    new_pieces = []
                    vals = []
                    for idx, (res_prev, d_prev) in enumerate(pieces):
                        for b in range(step):
                            src = pl.ds(idx * rows_prev + b, rows_prev // step, stride=step)
                            vals.append(row_sc[src, :])
                            new_pieces.append((res_prev + d_prev * b, d))
                    for idx, ((res, _), val) in enumerate(zip(new_pieces, vals)):
                        lo = res * ATT_WIDTH + c * LANES
                        s_ref[:, lo:lo + LANES] = val.astype(BF16)
                        if d != DILATIONS[-1]:
                            row_sc[idx * (tm // d):(idx + 1) * (tm // d), :] = val
                    pieces = new_pieces
    ft_ref[...] = _dot_nt(wf_ref[...], xb)


def _proj0(x, w_main, wf_t, cos_t, sup_t, sdn_t, tm):
    B, S, _ = x.shape
    n_i = S // tm
    act = jax.ShapeDtypeStruct((B, S, ATT_WIDTH), BF16)
    act_spec = pl.BlockSpec((None, tm, ATT_WIDTH), lambda b, i: (b, i, 0))
    tab_spec = pl.BlockSpec((tm, LANES), lambda b, i: (i, 0))
    out_specs = [act_spec] * N_PROJ0
    out_shape = [act] * N_PROJ0
    for d in DILATIONS[1:]:
        out_specs += [pl.BlockSpec((None, tm // d, d * ATT_WIDTH), lambda b, i: (b, i, 0))] * 3
        out_shape += [jax.ShapeDtypeStruct((B, S // d, d * ATT_WIDTH), BF16)] * 3
    out_specs.append(pl.BlockSpec((None, N_HEADS, tm), lambda b, i: (b, 0, i)))
    out_shape.append(jax.ShapeDtypeStruct((B, N_HEADS, S), F32))
    return pl.pallas_call(
        _proj0_kernel,
        grid=(B, n_i),
        in_specs=[
            pl.BlockSpec((None, tm, D_MODEL), lambda b, i: (b, i, 0)),
            pl.BlockSpec((D_MODEL, N_PROJ0 * ATT_WIDTH), lambda b, i: (0, 0)),
            pl.BlockSpec((N_HEADS, D_MODEL), lambda b, i: (0, 0)),
            tab_spec, tab_spec, tab_spec,
        ],
        out_specs=out_specs,
        out_shape=out_shape,
        scratch_shapes=[pltpu.VMEM((tm, LANES), F32)],
        compiler_params=_cparams(("parallel", "parallel"), 48),
        name="proj0",
    )(x, w_main, wf_t, cos_t, sup_t, sdn_t)


def _fox_cumsum_kernel(f_ref, bias_ref, c_ref):
    S = f_ref.shape[1]
    lf = _log_sigmoid(f_ref[...] + bias_ref[...])
    row = lax.broadcasted_iota(jnp.int32, (S, S), 0)
    col = lax.broadcasted_iota(jnp.int32, (S, S), 1)
    upper = jnp.where(row <= col, 1.0, 0.0).astype(BF16)
    hi, mid, lo = _split3(lf)
    c_ref[...] = _dot(hi, upper) + _dot(mid, upper) + _dot(lo, upper)


def _fox_cumsum(f_t, bias_col):
    R, S = f_t.shape
    return pl.pallas_call(
        _fox_cumsum_kernel,
        grid=(1,),
        in_specs=[pl.BlockSpec((R, S), lambda i: (0, 0)), pl.BlockSpec((R, 1), lambda i: (0, 0))],
        out_specs=pl.BlockSpec((R, S), lambda i: (0, 0)),
        out_shape=jax.ShapeDtypeStruct((R, S), F32),
        compiler_params=_cparams(("arbitrary",), 48),
        name="fox_cumsum",
    )(f_t, bias_col)


def _fox_kernel(q_ref, k_ref, v_ref, c_ref, o_ref, m_sc, l_sc, acc_sc, *, t):
    i = pl.program_id(1)
    j = pl.program_id(2)

    @pl.when(j == 0)
    def _():
        m_sc[...] = jnp.full(m_sc.shape, NEG_INF, F32)
        l_sc[...] = jnp.zeros(l_sc.shape, F32)
        acc_sc[...] = jnp.zeros(acc_sc.shape, F32)

    def step(masked):
        if masked:
            key = lax.broadcasted_iota(jnp.int32, (t, t), 0)
            qry = lax.broadcasted_iota(jnp.int32, (t, t), 1)
            keep = key <= qry
        for h in range(N_HEADS):
            sl = slice(h * HEAD_DIM, (h + 1) * HEAD_DIM)
            s = _dot_nt(k_ref[:, sl], q_ref[:, sl]) - c_ref[:, h:h + 1]
            if masked:
                s = jnp.where(keep, s, NEG_INF)
            m_prev = m_sc[h:h + 1, :]
            m_new = jnp.maximum(m_prev, jnp.max(s, axis=0, keepdims=True))
            alpha = jnp.exp(m_prev - m_new)
            p = jnp.exp(s - m_new)
            l_sc[h:h + 1, :] = alpha * l_sc[h:h + 1, :] + jnp.sum(p, axis=0, keepdims=True)
            acc_sc[sl, :] = alpha * acc_sc[sl, :] + _dot_tn(v_ref[:, sl], p.astype(BF16))
            m_sc[h:h + 1, :] = m_new

    @pl.when(j < i)
    def _():
        step(False)

    @pl.when(j == i)
    def _():
        step(True)
        out_t = jnp.concatenate(
            [acc_sc[h * HEAD_DIM:(h + 1) * HEAD_DIM, :] / l_sc[h:h + 1, :] for h in range(N_HEADS)],
            axis=0)
        o_ref[...] = out_t.T.astype(BF16)


def _fox_attention(q, k, v, c, t):
    B, S, _ = q.shape
    n = S // t
    q_spec = pl.BlockSpec((None, t, ATT_WIDTH), lambda b, i, j: (b, i, 0))
    kv_spec = pl.BlockSpec((None, t, ATT_WIDTH), lambda b, i, j: (b, jnp.minimum(j, i), 0))
    return pl.pallas_call(
        functools.partial(_fox_kernel, t=t),
        grid=(B, n, n),
        in_specs=[q_spec, kv_spec, kv_spec,
                  pl.BlockSpec((None, t, N_HEADS), lambda b, i, j: (b, jnp.minimum(j, i), 0))],
        out_specs=q_spec,
        out_shape=jax.ShapeDtypeStruct((B, S, ATT_WIDTH), BF16),
        scratch_shapes=[pltpu.VMEM((N_HEADS, t), F32), pltpu.VMEM((N_HEADS, t), F32),
                        pltpu.VMEM((ATT_WIDTH, t), F32)],
        compiler_params=_cparams(("parallel", "parallel", "arbitrary"), 48),
        name="fox_attention",
    )(q, k, v, c)


def _dil_kernel(*refs, d, rg, sub, tqu, has_prev, first, last):
    refs = list(refs)
    q_ref, k_ref, v_ref = refs[:3]
    pos = 3
    if has_prev:
        kp_ref, vp_ref = refs[pos:pos + 2]
        pos += 2
    if not first:
        acc_in, st_in = refs[pos:pos + 2]
        pos += 2
    if last:
        o_ref = refs[pos]
    else:
        acc_out, st_out = refs[pos:pos + 2]

    blk = pl.program_id(1)
    if has_prev:
        key = lax.broadcasted_iota(jnp.int32, (2 * BAND, BAND), 0)
        qry = lax.broadcasted_iota(jnp.int32, (2 * BAND, BAND), 1)
        band = jnp.logical_and(key >= qry, key <= qry + BAND)
        band_edge = jnp.logical_and(band, jnp.logical_or(key >= BAND, blk > 0))
    else:
        key = lax.broadcasted_iota(jnp.int32, (tqu, tqu), 0)
        qry = lax.broadcasted_iota(jnp.int32, (tqu, tqu), 1)
        keep = key <= qry
    pair = LANES // HEAD_DIM

    if first and not has_prev and not last and rg % 2 == 0:
        n2 = 2 * tqu
        for r2 in range(rg // 2):
            res = (2 * r2, 2 * r2 + 1)
            nats = [pl.ds(pl.program_id(2) * rg + rr, tqu, stride=d) for rr in res]
            ms, ls = [], []
            for slab in range(N_HEADS // pair):
                outs = []
                for hh in range(pair):
                    h = slab * pair + hh
                    sls = [slice(rr * ATT_WIDTH + h * HEAD_DIM, rr * ATT_WIDTH + (h + 1) * HEAD_DIM)
                           for rr in res]
                    q = jnp.concatenate([q_ref[:, sl] for sl in sls], axis=0)
                    k_cat = jnp.concatenate([k_ref[:, sl] for sl in sls], axis=0)
                    v_cat = jnp.concatenate([v_ref[:, sl] for sl in sls], axis=0)
                    s_all = _dot_nt(k_cat, q)
                    m_parts, l_parts, acc_parts = [], [], []
                    for j in range(2):
                        blk_j = slice(j * tqu, (j + 1) * tqu)
                        s = jnp.where(keep, s_all[blk_j, blk_j], NEG_INF)
                        m = jnp.max(s, axis=0, keepdims=True)
                        p = jnp.exp(s - m)
                        l_parts.append(jnp.sum(p, axis=0, keepdims=True))
                        m_parts.append(m)
                        acc_parts.append(_dot_tn(v_cat[blk_j, :], p.astype(BF16)))
                    ls.append(jnp.concatenate(l_parts, axis=1))
                    ms.append(jnp.concatenate(m_parts, axis=1))
                    outs.append(jnp.concatenate(acc_parts, axis=1))
                slab_out = jnp.concatenate(outs, axis=0).T
                for j, nat in enumerate(nats):
                    acc_out[slab, nat, :] = slab_out[j * tqu:(j + 1) * tqu, :]
            pad = jnp.zeros((LANES - 2 * N_HEADS, n2), F32)
            st_t = jnp.concatenate(ms + ls + [pad], axis=0).T
            for j, nat in enumerate(nats):
                st_out[nat, :] = st_t[j * tqu:(j + 1) * tqu, :]
        return

    for a in range(sub):
        rows = slice(a * tqu, (a + 1) * tqu)
        for rr in range(rg):
            if d == 1:
                nat = rows
            else:
                nat = pl.ds(a * tqu * d + pl.program_id(2) * rg + rr, tqu, stride=d)
            if not first:
                st_old = st_in[nat, :].T
            ms, ls = [], []
            for slab in range(N_HEADS // pair):
                slab_lanes = slice(rr * ATT_WIDTH + slab * LANES, rr * ATT_WIDTH + (slab + 1) * LANES)
                if not first:
                    acc_old = acc_in[slab, nat, :].T
                outs = []
                for hh in range(pair):
                    h = slab * pair + hh
                    lo = rr * ATT_WIDTH + h * HEAD_DIM
                    sl = slice(lo, lo + HEAD_DIM)
                    q = q_ref[rows, sl]
                    if not has_prev:
                        k_cat, v_cat, mask = k_ref[rows, sl], v_ref[rows, sl], keep
                    elif a == 0:
                        k_cat = jnp.concatenate([kp_ref[:, sl], k_ref[rows, sl]], axis=0)
                        v_cat = jnp.concatenate([vp_ref[:, sl], v_ref[rows, sl]], axis=0)
                    else:
                        krows = slice(a * tqu - BAND, (a + 1) * tqu)
                        k_cat, v_cat = k_ref[krows, sl], v_ref[krows, sl]
                    s_all = _dot_nt(k_cat, q)
                    n_qs = tqu // BAND if has_prev else 1
                    wq_s = tqu // n_qs
                    m_parts, l_parts, acc_parts = [], [], []
                    for qh in range(n_qs):
                        qs = slice(qh * wq_s, (qh + 1) * wq_s)
                        if has_prev:
                            ks = slice(qh * BAND, qh * BAND + 2 * BAND)
                            part_mask = band_edge if (a == 0 and qh == 0) else band
                            s = jnp.where(part_mask, s_all[ks, qs], NEG_INF)
                            v_part = v_cat[ks, :]
                        else:
                            s = jnp.where(mask, s_all, NEG_INF)
                            v_part = v_cat
                        m = jnp.max(s, axis=0, keepdims=True)
                        if not first:
                            m_old = st_old[h:h + 1, qs]
                            l_old = st_old[N_HEADS + h:N_HEADS + h + 1, qs]
                            m_new = jnp.maximum(m, m_old)
                            alpha = jnp.exp(m_old - m_new)
                            m = m_new
                        p = jnp.exp(s - m)
                        l = jnp.sum(p, axis=0, keepdims=True)
                        acc = _dot_tn(v_part, p.astype(BF16))
                        if not first:
                            l = l + alpha * l_old
                            acc = acc + alpha * acc_old[hh * HEAD_DIM:(hh + 1) * HEAD_DIM, qs]
                        if last:
                            acc = acc / l
                        m_parts.append(m)
                        l_parts.append(l)
                        acc_parts.append(acc)
                    outs.append(jnp.concatenate(acc_parts, axis=1))
                    ms.append(jnp.concatenate(m_parts, axis=1))
                    ls.append(jnp.concatenate(l_parts, axis=1))
                slab_out = jnp.concatenate(outs, axis=0).T
                if last:
                    o_ref[rows, slab_lanes] = slab_out.astype(BF16)
                else:
                    acc_out[slab, nat, :] = slab_out
            if not last:
                pad = jnp.zeros((LANES - 2 * N_HEADS, tqu), F32)
                st_out[nat, :] = jnp.concatenate(ms + ls + [pad], axis=0).T


def _dilated_branch(q, k, v, state, dilation, last):
    B, L, _ = q.shape
    d = dilation
    S = L * d
    first = state is None
    assert not (last and d != 1)
    rg = min(4, d)
    tqu = min(2 * BAND, L)
    tqb = min(L, 4 * tqu // rg)
    sub = tqb // tqu
    n_blk = L // tqb
    n_grp = d // rg
    has_prev = L > BAND
    wq = rg * ATT_WIDTH
    n_slab = ATT_WIDTH // LANES

    main = lambda b, i, g: (b, i, g)
    prev = lambda b, i, g: (b, jnp.maximum(i * (tqb // BAND) - 1, 0), g)
    qkv_spec = pl.BlockSpec((None, tqb, wq), main)
    acc_spec = pl.BlockSpec((None, n_slab, tqb * d, LANES), lambda b, i, g: (b, 0, i, 0))
    st_spec = pl.BlockSpec((None, tqb * d, LANES), lambda b, i, g: (b, i, 0))
    in_specs = [qkv_spec] * 3
    args = [q, k, v]
    if has_prev:
        in_specs += [pl.BlockSpec((None, BAND, wq), prev)] * 2
        args += [k, v]
    if not first:
        in_specs += [acc_spec, st_spec]
        args += list(state)
    if last:
        out_specs = pl.BlockSpec((None, tqb, wq), main)
        out_shape = jax.ShapeDtypeStruct((B, S, ATT_WIDTH), BF16)
    else:
        out_specs = [acc_spec, st_spec]
        out_shape = [jax.ShapeDtypeStruct((B, n_slab, S, LANES), F32),
                     jax.ShapeDtypeStruct((B, S, LANES), F32)]
    return pl.pallas_call(
        functools.partial(_dil_kernel, d=d, rg=rg, sub=sub, tqu=tqu, has_prev=has_prev, first=first,
                          last=last),
        grid=(B, n_blk, n_grp),
        in_specs=in_specs,
        out_specs=out_specs,
        out_shape=out_shape,
        compiler_params=_cparams(("parallel", "parallel", "arbitrary"), 48),
        name=f"dilated_d{d}",
    )(*args)


def _dilated_attention(qkv_by_dilation):
    state = None
    order = sorted(DIL_CONFIGS, key=lambda wd: -wd[1])
    for n, (window, d) in enumerate(order):
        assert window // d == BAND
        state = _dilated_branch(*qkv_by_dilation[d], state, d, last=(n == len(order) - 1))
    return state


def _outproj_ln_rows(a_refs, w_refs, x_ref, g_ref, b_ref):
    y = _dot(a_refs[0][...], w_refs[0][...])
    for a_ref, w_ref in zip(a_refs[1:], w_refs[1:]):
        y = y + _dot(a_ref[...], w_ref[...])
    return _layer_norm(DEEPNORM_ALPHA * x_ref[...] + y, g_ref[...], b_ref[...])


def _route_top2(h, wt_ref, sel_ref, gate_ref):
    logits = lax.dot_general(wt_ref[...], h, (((1,), (1,)), ((), ())),
                             preferred_element_type=F32, precision=lax.Precision.HIGHEST)
    idx = lax.broadcasted_iota(jnp.int32, logits.shape, 0)
    m1 = jnp.max(logits, axis=0, keepdims=True)
    i1 = jnp.min(jnp.where(logits == m1, idx, N_EXPERTS), axis=0, keepdims=True)
    pick1 = idx == i1
    rest = jnp.where(pick1, NEG_INF, logits)
    m2 = jnp.max(rest, axis=0, keepdims=True)
    i2 = jnp.min(jnp.where(rest == m2, idx, N_EXPERTS), axis=0, keepdims=True)
    pick2 = idx == i2
    e2 = jnp.exp(m2 - m1)
    w1 = 1.0 / (1.0 + e2)
    w2 = e2 / (1.0 + e2)
    sel_ref[...] = jnp.where(jnp.logical_or(pick1, pick2), 1.0, 0.0)
    gate_ref[...] = jnp.where(pick1, w1, 0.0) + jnp.where(pick2, w2, 0.0)


def _outproj_router_kernel(a_ref, w_ref, x_ref, g_ref, b_ref, wt_ref, o_ref, sel_ref, gate_ref):
    h = _outproj_ln_rows([a_ref], [w_ref], x_ref, g_ref, b_ref)
    o_ref[...] = h
    _route_top2(h, wt_ref, sel_ref, gate_ref)


def _outproj_router(a, w, x, g, b, w_router_t, tm):
    T = x.shape[0]
    row = pl.BlockSpec((tm, D_MODEL), lambda i: (i, 0))
    vec = pl.BlockSpec((1, D_MODEL), lambda i: (0, 0))
    route = pl.BlockSpec((N_EXPERTS, tm), lambda i: (0, i))
    route_shape = jax.ShapeDtypeStruct((N_EXPERTS, T), F32)
    return pl.pallas_call(
        _outproj_router_kernel,
        grid=(T // tm,),
        in_specs=[pl.BlockSpec((tm, a.shape[1]), lambda i: (i, 0)),
                  pl.BlockSpec(w.shape, lambda i: (0, 0)),
                  row, vec, vec,
                  pl.BlockSpec((N_EXPERTS, D_MODEL), lambda i: (0, 0))],
        out_specs=[row, route, route],
        out_shape=[jax.ShapeDtypeStruct((T, D_MODEL), F32), route_shape, route_shape],
        compiler_params=_cparams(("parallel",), 48),
        name="outproj_router",
    )(a, w, x, g, b, w_router_t)


def _ffn_chunks(width):
    chunks, lo = [], 0
    while lo < width:
        size = min(2 * MXU_DIM, width - lo)
        chunks.append((lo, size))
        lo += size
    return chunks


def _dense_ffn_kernel(a0_ref, a1_ref, w0_ref, w1_ref, r_ref, g0_ref, b0_ref,
                      wg_ref, wu_ref, wd_ref, g_ref, b_ref, o_ref):
    x = _outproj_ln_rows([a0_ref, a1_ref], [w0_ref, w1_ref], r_ref, g0_ref, b0_ref)
    xb = x.astype(BF16)
    y = None
    for lo, size in _ffn_chunks(wg_ref.shape[1]):
        gate = _dot(xb, wg_ref[:, lo:lo + size])
        up = _dot(xb, wu_ref[:, lo:lo + size])
        act = (gate * jax.nn.sigmoid(gate) * up).astype(BF16)
        part = _dot(act, wd_ref[lo:lo + size, :])
        y = part if y is None else y + part
    o_ref[...] = _layer_norm(DEEPNORM_ALPHA * x + y, g_ref[...], b_ref[...])


def _mixer_out_dense_ffn(acts, ws, resid, g_mix, b_mix, wg, wu, wd, g, b, tm):
    T = resid.shape[0]
    F = wg.shape[1]
    row = pl.BlockSpec((tm, D_MODEL), lambda i: (i, 0))
    vec = pl.BlockSpec((1, D_MODEL), lambda i: (0, 0))
    once = pl.Buffered(1)
    const = lambda i: (0, 0)
    return pl.pallas_call(
        _dense_ffn_kernel,
        grid=(T // tm,),
        in_specs=[pl.BlockSpec((tm, a.shape[1]), lambda i: (i, 0)) for a in acts]
        + [pl.BlockSpec(w.shape, const, pipeline_mode=once) for w in ws]
        + [row, vec, vec,
           pl.BlockSpec((D_MODEL, F), const, pipeline_mode=once),
           pl.BlockSpec((D_MODEL, F), const, pipeline_mode=once),
           pl.BlockSpec((F, D_MODEL), const, pipeline_mode=once),
           vec, vec],
        out_specs=row,
        out_shape=jax.ShapeDtypeStruct((T, D_MODEL), F32),
        compiler_params=_cparams(("parallel",), 56),
        name="mixer_out_dense_ffn",
    )(*acts, *ws, resid, g_mix, b_mix, wg, wu, wd, g, b)


CONV_PAD = 8


def _proj1_kernel(x_ref, wqk_ref, wv_ref, wgt_ref, wconv_ref,
                  q_ref, k_ref, v_ref, gt_ref, buf):
    i = pl.program_id(1)
    tm = x_ref.shape[0]
    xb = x_ref[...].astype(BF16)
    kscale = MLSTM_HEAD_DIM ** -0.5

    @pl.when(i == 0)
    def _():
        buf[...] = jnp.zeros(buf.shape, F32)

    wide = 2 * MXU_DIM
    per_wide = wide // LANES
    for c in range(2 * D_MODEL // LANES):
        if c % per_wide == 0:
            pre = _dot(xb, wqk_ref[:, c * LANES:c * LANES + wide])
        lanes = slice(c * LANES, (c + 1) * LANES)
        cur = pre[:, (c % per_wide) * LANES:(c % per_wide + 1) * LANES]
        ext = jnp.concatenate([buf[:, lanes], cur], axis=0)
        y = cur * wconv_ref[CONV_WIDTH - 1:CONV_WIDTH, lanes]
        for back in range(1, CONV_WIDTH):
            tap = CONV_WIDTH - 1 - back
            y = y + pltpu.roll(ext, back, axis=0)[CONV_PAD:, :] * wconv_ref[tap:tap + 1, lanes]
        y = y * jax.nn.sigmoid(y)
        buf[:, lanes] = cur[tm - CONV_PAD:tm, :]
        if c < D_MODEL // LANES:
            q_ref[:, lanes] = y.astype(BF16)
        else:
            k_ref[:, c * LANES - D_MODEL:(c + 1) * LANES - D_MODEL] = (y * kscale).astype(BF16)
    for c in range(D_MODEL // wide):
        lanes = slice(c * wide, (c + 1) * wide)
        v_ref[:, lanes] = _dot(xb, wv_ref[:, lanes]).astype(BF16)
    gt_ref[...] = _dot(xb, wgt_ref[...])


def _proj1(x, wqk, wv, wgt, wconv, tm):
    B, S, _ = x.shape
    row = lambda b, i: (b, i, 0)
    const = lambda b, i: (0, 0)
    act_spec = pl.BlockSpec((None, tm, D_MODEL), row)
    act = jax.ShapeDtypeStruct((B, S, D_MODEL), BF16)
    return pl.pallas_call(
        _proj1_kernel,
        grid=(B, S // tm),
        in_specs=[act_spec,
                  pl.BlockSpec((D_MODEL, 2 * D_MODEL), const),
                  pl.BlockSpec((D_MODEL, D_MODEL), const),
                  pl.BlockSpec((D_MODEL, 2 * N_HEADS), const),
                  pl.BlockSpec((CONV_WIDTH, 2 * D_MODEL), const)],
        out_specs=[act_spec, act_spec, act_spec,
                   pl.BlockSpec((None, tm, 2 * N_HEADS), row)],
        out_shape=[act, act, act, jax.ShapeDtypeStruct((B, S, 2 * N_HEADS), F32)],
        scratch_shapes=[pltpu.VMEM((CONV_PAD, 2 * D_MODEL), F32)],
        compiler_params=_cparams(("parallel", "arbitrary"), 56),
        name="proj1",
    )(x, wqk, wv, wgt, wconv)


def _mlstm_kernel(q_ref, k_ref, v_ref, x_ref, wog_ref, gn_ref, gt_ref, bi_row, bf_row, bi_col,
                  bf_col, ng_ref, o_ref, c_sc, n_sc, m_sc, og_sc, *, L):
    ci = pl.program_id(1)

    @pl.when(ci == 0)
    def _():
        c_sc[...] = jnp.zeros(c_sc.shape, F32)
        n_sc[...] = jnp.zeros(n_sc.shape, F32)
        m_sc[...] = jnp.full(m_sc.shape, NEG_INF, F32)

    xb = x_ref[...].astype(BF16)
    wide = 2 * MXU_DIM
    for c in range(D_MODEL // wide):
        og_sc[:, c * wide:(c + 1) * wide] = _dot(xb, wog_ref[:, c * wide:(c + 1) * wide])

    row = lax.broadcasted_iota(jnp.int32, (L, L), 0)
    col = lax.broadcasted_iota(jnp.int32, (L, L), 1)
    causal = col <= row
    lower = jnp.where(causal, 1.0, 0.0).astype(BF16)
    upper = jnp.where(row <= col, 1.0, 0.0).astype(BF16)

    gn = gn_ref[...]
    gt = gt_ref[...]
    i_col = gn[:, :N_HEADS] + bi_row[...]
    lf_col = _log_sigmoid(gn[:, N_HEADS:] + bf_row[...])
    i_row = gt[:N_HEADS, :] + bi_col[...]
    lf_row = _log_sigmoid(gt[N_HEADS:, :] + bf_col[...])
    b_col = sum(_dot(lower, part) for part in _split3(lf_col))
    b_row = sum(_dot(part, upper) for part in _split3(lf_row))

    keep = row <= col
    n_pad = jnp.zeros((2 * N_HEADS - 3, MLSTM_HEAD_DIM), F32)
    for h in range(N_HEADS):
        lanes = slice(h * MLSTM_HEAD_DIM, (h + 1) * MLSTM_HEAD_DIM)
        q = q_ref[:, lanes]
        k = k_ref[:, lanes]
        v = v_ref[:, lanes]
        bt = b_row[h:h + 1, :]
        key_term = i_col[:, h:h + 1] - b_col[:, h:h + 1]
        m_prev = m_sc[h]
        ct_prev = c_sc[h]
        n_prev = n_sc[h]

        dlog = jnp.where(keep, bt + key_term, NEG_INF)
        inter = bt + m_prev
        m_t = jnp.maximum(inter, jnp.max(dlog, axis=0, keepdims=True))
        s = _dot_nt(k, q) * jnp.exp(dlog - m_t)
        inter_w = jnp.exp(inter - m_t)
        num = _dot_tn(v, s.astype(BF16)) + inter_w * _dot_nt(ct_prev.astype(BF16), q)
        n_parts = jnp.concatenate([p.astype(F32) for p in _split3(n_prev)] + [n_pad], axis=0)
        qn = jnp.sum(_dot_nt(n_parts.astype(BF16), q), axis=0, keepdims=True)
        den = jnp.sum(s, axis=0, keepdims=True) + inter_w * qn
        hh = num / jnp.maximum(jnp.abs(den), jnp.exp(-m_t))

        b_last = bt[:, L - 1:L]
        g = b_last + key_term
        m_new = jnp.maximum(b_last + m_prev, jnp.max(g, axis=0, keepdims=True))
        w = jnp.exp(g - m_new)
        decay = jnp.exp(b_last + m_prev - m_new)
        kw = k.astype(F32) * w
        c_sc[h] = decay * ct_prev + _dot_tn(v, kw.astype(BF16))
        n_sc[h] = decay * n_prev + jnp.sum(kw, axis=0, keepdims=True)
        m_sc[h] = m_new

        mu = jnp.mean(hh, axis=0, keepdims=True)
        hc = hh - mu
        var = jnp.mean(hc * hc, axis=0, keepdims=True)
        hn = (hc * lax.rsqrt(var + LN_EPS)).T * ng_ref[:, lanes]
        o_ref[:, lanes] = (hn * jax.nn.sigmoid(og_sc[:, lanes])).astype(BF16)


def _mlstm(q, k, v, x, wog, gates, gates_t, b_i, b_f, norm_g, L):
    B, S, _ = q.shape
    row = lambda b, c: (b, c, 0)
    const = lambda b, c: (0, 0)
    act_spec = pl.BlockSpec((None, L, D_MODEL), row)
    return pl.pallas_call(
        functools.partial(_mlstm_kernel, L=L),
        grid=(B, S // L),
        in_specs=[act_spec, act_spec, act_spec, act_spec,
                  pl.BlockSpec((D_MODEL, D_MODEL), const),
                  pl.BlockSpec((None, L, 2 * N_HEADS), row),
                  pl.BlockSpec((None, 2 * N_HEADS, L), lambda b, c: (b, 0, c)),
                  pl.BlockSpec((1, N_HEADS), const), pl.BlockSpec((1, N_HEADS), const),
                  pl.BlockSpec((N_HEADS, 1), const), pl.BlockSpec((N_HEADS, 1), const),
                  pl.BlockSpec((1, D_MODEL), const)],
        out_specs=act_spec,
        out_shape=jax.ShapeDtypeStruct((B, S, D_MODEL), BF16),
        scratch_shapes=[pltpu.VMEM((N_HEADS, MLSTM_HEAD_DIM, MLSTM_HEAD_DIM), F32),
                        pltpu.VMEM((N_HEADS, 1, MLSTM_HEAD_DIM), F32),
                        pltpu.VMEM((N_HEADS, 1, 1), F32),
                        pltpu.VMEM((L, D_MODEL), F32)],
        compiler_params=_cparams(("parallel", "arbitrary"), 48),
        name="mlstm",
    )(q, k, v, x, wog, gates, gates_t, b_i[None, :], b_f[None, :], b_i[:, None], b_f[:, None],
      norm_g[None, :])


MOE_TOP_K = 2
MOE_TILE = 512
RANK_BLOCK = 1024


def _rank_kernel(sel_ref, rank_ref, count_ref, upper_sc, carry_sc):
    n = sel_ref.shape[1]

    @pl.when(pl.program_id(0) == 0)
    def _():
        row = lax.broadcasted_iota(jnp.int32, (n, n), 0)
        col = lax.broadcasted_iota(jnp.int32, (n, n), 1)
        upper_sc[...] = jnp.where(row <= col, 1.0, 0.0).astype(BF16)
        carry_sc[...] = jnp.zeros(carry_sc.shape, F32)

    sel = sel_ref[...]
    incl = _dot(sel.astype(BF16), upper_sc[...])
    rank_ref[...] = carry_sc[...] + incl - sel
    carry_sc[...] = carry_sc[...] + incl[:, n - 1:n]
    count_ref[...] = carry_sc[...]


def _rank(sel_t):
    E, T = sel_t.shape
    n = min(RANK_BLOCK, T)
    return pl.pallas_call(
        _rank_kernel,
        grid=(T // n,),
        in_specs=[pl.BlockSpec((E, n), lambda i: (0, i))],
        out_specs=[pl.BlockSpec((E, n), lambda i: (0, i)), pl.BlockSpec((E, 1), lambda i: (0, 0))],
        out_shape=[jax.ShapeDtypeStruct((E, T), F32), jax.ShapeDtypeStruct((E, 1), F32)],
        scratch_shapes=[pltpu.VMEM((n, n), BF16), pltpu.VMEM((E, 1), F32)],
        compiler_params=_cparams(("arbitrary",), 32),
        name="moe_rank",
    )(sel_t)


def _slot_kernel(sel_ref, gate_ref, rank_ref, off_ref, pos_ref, w_ref):
    sel = sel_ref[...] > 0.0
    idx = lax.broadcasted_iota(jnp.int32, sel.shape, 0)
    first = jnp.min(jnp.where(sel, idx, N_EXPERTS), axis=0, keepdims=True)
    second = jnp.max(jnp.where(sel, idx, -1), axis=0, keepdims=True)
    slot = off_ref[...] + rank_ref[...]
    gate = gate_ref[...]
    rows_p, rows_w = [], []
    for which in (first, second):
        hit = idx == which
        rows_p.append(jnp.sum(jnp.where(hit, slot, 0.0), axis=0, keepdims=True))
        rows_w.append(jnp.sum(jnp.where(hit, gate, 0.0), axis=0, keepdims=True))
    pos_ref[...] = jnp.concatenate(rows_p, axis=0).astype(jnp.int32)
    w_ref[...] = jnp.concatenate(rows_w, axis=0)


def _slots(sel_t, gate_t, rank_t, offsets_col, tm):
    E, T = sel_t.shape
    spec = pl.BlockSpec((E, tm), lambda i: (0, i))
    out_spec = pl.BlockSpec((MOE_TOP_K, tm), lambda i: (0, i))
    return pl.pallas_call(
        _slot_kernel,
        grid=(T // tm,),
        in_specs=[spec, spec, spec, pl.BlockSpec((E, 1), lambda i: (0, 0))],
        out_specs=[out_spec, out_spec],
        out_shape=[jax.ShapeDtypeStruct((MOE_TOP_K, T), jnp.int32),
                   jax.ShapeDtypeStruct((MOE_TOP_K, T), F32)],
        compiler_params=_cparams(("parallel",), 32),
        name="moe_slots",
    )(sel_t, gate_t, rank_t, offsets_col)


def _dispatch_kernel(tail_ref, used_ref, pos_ref, x_ref, xs_hbm, zero_sc, sem, zsem, *, tm):
    i = pl.program_id(0)
    n_tiles = xs_hbm.shape[0] // MOE_TILE

    def fill(tile):
        return pltpu.make_async_copy(zero_sc, xs_hbm.at[pl.ds(tile * MOE_TILE, MOE_TILE)], zsem)

    @pl.when(i == 0)
    def _():
        zero_sc[...] = jnp.zeros(zero_sc.shape, F32)
        for e in range(N_EXPERTS):
            fill(tail_ref[e]).start()
        for e in range(N_EXPERTS):
            fill(tail_ref[e]).wait()
        for j in range(N_EXPERTS):
            @pl.when(n_tiles - 1 - j >= used_ref[0])
            def _():
                c = fill(n_tiles - 1 - j)
                c.start()
                c.wait()

    def issue(t, carry):
        for k in range(MOE_TOP_K):
            pltpu.make_async_copy(x_ref.at[pl.ds(t, 1)], xs_hbm.at[pl.ds(pos_ref[k, t], 1)],
                                  sem).start(priority=k)
        return carry

    lax.fori_loop(0, tm, issue, 0, unroll=8)
    for _ in range(MOE_TOP_K):
        pltpu.make_async_copy(x_ref, xs_hbm.at[pl.ds(0, tm)], sem).wait()


def _dispatch(x, pos_t, tail_tiles, n_used, n_rows, tm):
    T = x.shape[0]
    n_steps = T // tm
    pos3 = pos_t.reshape(MOE_TOP_K, n_steps, tm).transpose(1, 0, 2)
    return pl.pallas_call(
        functools.partial(_dispatch_kernel, tm=tm),
        grid_spec=pltpu.PrefetchScalarGridSpec(
            num_scalar_prefetch=2,
            grid=(n_steps,),
            in_specs=[pl.BlockSpec((None, MOE_TOP_K, tm), lambda i, tail, used: (i, 0, 0),
                                   memory_space=pltpu.SMEM),
                      pl.BlockSpec((tm, D_MODEL), lambda i, tail, used: (i, 0))],
            out_specs=pl.BlockSpec(memory_space=pl.ANY),
            scratch_shapes=[pltpu.VMEM((MOE_TILE, D_MODEL), F32),
                            pltpu.SemaphoreType.DMA, pltpu.SemaphoreType.DMA],
        ),
        out_shape=jax.ShapeDtypeStruct((n_rows, D_MODEL), F32),
        compiler_params=_cparams(("arbitrary",), 32),
        name="moe_dispatch",
    )(tail_tiles, n_used, pos3, x)


def _expert_ffn_kernel(te_ref, used_ref, x_ref, wg_ref, wu_ref, wd_ref, o_ref):
    live = pl.program_id(0) < used_ref[0]

    @pl.when(live)
    def _():
        xb = x_ref[...].astype(BF16)
        y = None
        for lo, size in _ffn_chunks(wg_ref.shape[1]):
            gate = _dot(xb, wg_ref[:, lo:lo + size])
            up = _dot(xb, wu_ref[:, lo:lo + size])
            act = (gate * jax.nn.sigmoid(gate) * up).astype(BF16)
            part = _dot(act, wd_ref[lo:lo + size, :])
            y = part if y is None else y + part
        o_ref[...] = y

    @pl.when(jnp.logical_not(live))
    def _():
        o_ref[...] = jnp.zeros(o_ref.shape, F32)


def _expert_ffn(xs, tile_expert, n_used, wg, wu, wd):
    n_rows = xs.shape[0]
    E, _, F = wg.shape
    tm = MOE_TILE
    once = pl.Buffered(1)

    def row(i, te, used):
        return (jnp.minimum(i, used[0] - 1), 0)

    return pl.pallas_call(
        _expert_ffn_kernel,
        grid_spec=pltpu.PrefetchScalarGridSpec(
            num_scalar_prefetch=2,
            grid=(n_rows // tm,),
            in_specs=[pl.BlockSpec((tm, D_MODEL), row),
                      pl.BlockSpec((None, D_MODEL, F), lambda i, te, used: (te[i], 0, 0),
                                   pipeline_mode=once),
                      pl.BlockSpec((None, D_MODEL, F), lambda i, te, used: (te[i], 0, 0),
                                   pipeline_mode=once),
                      pl.BlockSpec((None, F, D_MODEL), lambda i, te, used: (te[i], 0, 0),
                                   pipeline_mode=once)],
            out_specs=pl.BlockSpec((tm, D_MODEL), lambda i, te, used: (i, 0)),
        ),
        out_shape=jax.ShapeDtypeStruct((n_rows, D_MODEL), F32),
        compiler_params=_cparams(("arbitrary",), 56),
        name="moe_expert_ffn",
    )(tile_expert, n_used, xs, wg, wu, wd)


def _combine_kernel(pos_ref, w_ref, x_ref, g_ref, b_ref, ys_hbm, o_ref, rows_sc, sem, *, tm):
    def issue(t, carry):
        for k in range(MOE_TOP_K):
            pltpu.make_async_copy(ys_hbm.at[pl.ds(pos_ref[k, t], 1)],
                                  rows_sc.at[k, pl.ds(t, 1)], sem).start(priority=k)
        return carry

    lax.fori_loop(0, tm, issue, 0, unroll=8)
    for k in range(MOE_TOP_K):
        pltpu.make_async_copy(ys_hbm.at[pl.ds(0, tm)], rows_sc.at[k], sem).wait()
    w = w_ref[...]
    y = w[:, 0:1] * rows_sc[0] + w[:, 1:2] * rows_sc[1]
    o_ref[...] = _layer_norm(DEEPNORM_ALPHA * x_ref[...] + y, g_ref[...], b_ref[...])


def _combine_ln(ys, pos_t, w_nat, x, g, b, tm):
    T = x.shape[0]
    n_steps = T // tm
    pos3 = pos_t.reshape(MOE_TOP_K, n_steps, tm).transpose(1, 0, 2)
    row = pl.BlockSpec((tm, D_MODEL), lambda i: (i, 0))
    vec = pl.BlockSpec((1, D_MODEL), lambda i: (0, 0))
    return pl.pallas_call(
        functools.partial(_combine_kernel, tm=tm),
        grid=(n_steps,),
        in_specs=[pl.BlockSpec((None, MOE_TOP_K, tm), lambda i: (i, 0, 0), memory_space=pltpu.SMEM),
                  pl.BlockSpec((tm, MOE_TOP_K), lambda i: (i, 0)),
                  row, vec, vec,
                  pl.BlockSpec(memory_space=pl.ANY)],
        out_specs=row,
        out_shape=jax.ShapeDtypeStruct((T, D_MODEL), F32),
        scratch_shapes=[pltpu.VMEM((MOE_TOP_K, tm, D_MODEL), F32), pltpu.SemaphoreType.DMA],
        compiler_params=_cparams(("arbitrary",), 32),
        name="moe_combine_ln",
    )(pos3, w_nat, x, g, b, ys)


def _moe_sparse(x, sel_t, gate_t, wg, wu, wd, g, b):
    T = x.shape[0]
    tile = MOE_TILE
    n_rows = MOE_TOP_K * T + N_EXPERTS * tile
    n_tiles = n_rows // tile
    rank_t, counts = _rank(sel_t)
    tiles_per = jnp.ceil(counts[:, 0] / tile).astype(jnp.int32)
    tile_end = jnp.cumsum(tiles_per)
    tile_start = tile_end - tiles_per
    n_used = tile_end[-1:]
    tile_expert = jnp.minimum(
        jnp.sum(jnp.arange(n_tiles, dtype=jnp.int32)[:, None] >= tile_end[None, :], axis=1),
        N_EXPERTS - 1).astype(jnp.int32)
    tail_tiles = jnp.maximum(tile_end - 1, 0).astype(jnp.int32)
    offsets_col = (tile_start * tile).astype(F32)[:, None]
    pos_t, w_t = _slots(sel_t, gate_t, rank_t, offsets_col, tm=1024)
    xs = _dispatch(x, pos_t, tail_tiles, n_used, n_rows, tm=1024)
    ys = _expert_ffn(xs, tile_expert, n_used, wg, wu, wd)
    return _combine_ln(ys, pos_t, w_t.T, x, g, b, tm=512)


def _rotary_tables(S):
    pos = jnp.arange(S, dtype=jnp.int32)
    inv = ROPE_THETA ** (-jnp.arange(ROPE_HALF, dtype=F32) / ROPE_HALF)
    ang = pos.astype(F32)[:, None] * inv[None, :]
    cos, sin = jnp.cos(ang), jnp.sin(ang)
    ones = jnp.ones((S, HEAD_DIM - ROPE_DIMS), F32)
    zeros = jnp.zeros((S, HEAD_DIM - ROPE_DIMS), F32)
    z8 = jnp.zeros((S, ROPE_HALF), F32)
    cos_h = jnp.concatenate([cos, cos, ones], axis=1)
    sup_h = jnp.concatenate([-sin, z8, zeros], axis=1)
    sdn_h = jnp.concatenate([z8, sin, zeros], axis=1)
    rep = LANES // HEAD_DIM
    return (jnp.tile(cos_h, (1, rep)), jnp.tile(sup_h, (1, rep)), jnp.tile(sdn_h, (1, rep)))


def _even_layer(h, w_in, b_forget, w_out, ln_mix_g, ln_mix_b, w_gate, w_up, w_down,
                ln_ffn_g, ln_ffn_b):
    B, S, _ = h.shape
    T = B * S
    W = ATT_WIDTH
    w_main = jnp.concatenate([w_in[:, :3 * W], w_in[:, 3 * W + N_HEADS:]], axis=1).astype(BF16)
    wf_t = w_in[:, 3 * W:3 * W + N_HEADS].T.astype(BF16)
    cos_t, sup_t, sdn_t = _rotary_tables(S)
    outs = _proj0(h, w_main, wf_t, cos_t, sup_t, sdn_t, tm=min(512, S))
    qa, ka, va = outs[:3]
    qkv_by_dilation = {d: tuple(outs[3 + 3 * n:6 + 3 * n]) for n, d in enumerate(DILATIONS)}
    f_t = outs[-1]

    bias_col = jnp.tile(b_forget, B)[:, None]
    c = _fox_cumsum(f_t.reshape(B * N_HEADS, S), bias_col).reshape(B, N_HEADS, S)
    o_fox = _fox_attention(qa, ka, va, jnp.swapaxes(c, 1, 2), t=min(512, S))
    o_dil = _dilated_attention(qkv_by_dilation)

    w_out_b = w_out.astype(BF16)
    tm = 512
    h2 = _mixer_out_dense_ffn(
        [o_fox.reshape(T, W), o_dil.reshape(T, W)], [w_out_b[:W], w_out_b[W:]],
        h.reshape(T, D_MODEL), ln_mix_g[None, :], ln_mix_b[None, :],
        w_gate.astype(BF16), w_up.astype(BF16), w_down.astype(BF16),
        ln_ffn_g[None, :], ln_ffn_b[None, :], tm)
    return h2.reshape(B, S, D_MODEL)


def _odd_layer(h, w_in, b_igate, b_fgate, w_conv, norm_g, w_out, ln_mix_g, ln_mix_b, w_router,
               w_gate, w_up, w_down, ln_ffn_g, ln_ffn_b):
    B, S, _ = h.shape
    T = B * S
    D = D_MODEL
    wqk = w_in[:, :2 * D].astype(BF16)
    wv = w_in[:, 2 * D:3 * D].astype(BF16)
    wgt = w_in[:, 3 * D:3 * D + 2 * N_HEADS].astype(BF16)
    wog = w_in[:, 3 * D + 2 * N_HEADS:].astype(BF16)
    q, k, v, gates = _proj1(h, wqk, wv, wgt, w_conv, tm=min(512, S))
    gates_t = jnp.swapaxes(gates, 1, 2)
    hm = _mlstm(q, k, v, h, wog, gates, gates_t, b_igate, b_fgate, norm_g, L=min(256, S))

    tm = 512
    h1, sel_t, gate_t = _outproj_router(hm.reshape(T, D), w_out.astype(BF16), h.reshape(T, D),
                                        ln_mix_g[None, :], ln_mix_b[None, :], w_router.T, tm)
    h2 = _moe_sparse(h1, sel_t, gate_t, w_gate.astype(BF16), w_up.astype(BF16),
                     w_down.astype(BF16), ln_ffn_g[None, :], ln_ffn_b[None, :])
    return h2.reshape(B, S, D)


def kernel(x, w_in_e, b_forget_e, w_out_e, ln_mix_g_e, ln_mix_b_e, ffn_w_gate_e, ffn_w_up_e,
           ffn_w_down_e, ln_ffn_g_e, ln_ffn_b_e, w_in_o, b_igate_o, b_fgate_o, w_conv_o,
           mlstm_norm_g_o, w_out_o, ln_mix_g_o, ln_mix_b_o, w_router_o, moe_w_gate_o,
           moe_w_up_o, moe_w_down_o, ln_ffn_g_o, ln_ffn_b_o):
    h = x
    for layer in range(DEPTH):
        i = layer // 2
        if layer % 2 == 0:
            h = _even_layer(h, w_in_e[i], b_forget_e[i], w_out_e[i], ln_mix_g_e[i], ln_mix_b_e[i],
                            ffn_w_gate_e[i], ffn_w_up_e[i], ffn_w_down_e[i], ln_ffn_g_e[i],
                            ln_ffn_b_e[i])
        else:
            h = _odd_layer(h, w_in_o[i], b_igate_o[i], b_fgate_o[i], w_conv_o[i],
                           mlstm_norm_g_o[i], w_out_o[i], ln_mix_g_o[i], ln_mix_b_o[i],
                           w_router_o[i], moe_w_gate_o[i], moe_w_up_o[i], moe_w_down_o[i],
                           ln_ffn_g_o[i], ln_ffn_b_o[i])
    return h
```

```python
import functools
import math

import jax
import jax.numpy as jnp
from jax import lax
from jax.experimental import pallas as pl
from jax.experimental.pallas import tpu as pltpu

F32 = jnp.float32
BF16 = jnp.bfloat16

D_MODEL = 1024
HEAD_DIM = 64
N_HEADS = 8
ATT_WIDTH = N_HEADS * HEAD_DIM
DIL_CONFIGS = ((128, 1), (512, 4), (2048, 16))
ROPE_THETA = 500000.0
ROPE_DIMS = HEAD_DIM // 4
ROPE_HALF = ROPE_DIMS // 2
MLSTM_HEAD_DIM = D_MODEL // N_HEADS
CONV_WIDTH = 4
D_FF_DENSE = 2816
N_EXPERTS = 8
D_FF_EXPERT = 3584
DEPTH = 2
DEEPNORM_ALPHA = (2 * DEPTH) ** 0.25
LN_EPS = 1e-5

LANES = 128
MXU_DIM = 256
BAND = 128
MIB = 1024 * 1024

NEG_INF = float("-inf")


def _cparams(semantics, vmem_mib):
    return pltpu.CompilerParams(dimension_semantics=semantics, vmem_limit_bytes=vmem_mib * MIB)


def _dot(a, b):
    return jnp.dot(a, b, preferred_element_type=F32)


def _dot_nt(a, b):
    return lax.dot_general(a, b, (((1,), (1,)), ((), ())), preferred_element_type=F32)


def _dot_tn(a, b):
    return lax.dot_general(a, b, (((0,), (0,)), ((), ())), preferred_element_type=F32)


def _split3(x):
    hi = x.astype(BF16)
    r = x - hi.astype(F32)
    mid = r.astype(BF16)
    lo = (r - mid.astype(F32)).astype(BF16)
    return hi, mid, lo


def _log_sigmoid(z):
    return -(jnp.maximum(-z, 0.0) + jnp.log1p(jnp.exp(-jnp.abs(z))))


def _layer_norm(z, g, b):
    mu = jnp.mean(z, axis=-1, keepdims=True)
    zc = z - mu
    var = jnp.mean(zc * zc, axis=-1, keepdims=True)
    return zc * lax.rsqrt(var + LN_EPS) * g + b


DILATIONS = tuple(sorted(d for _, d in DIL_CONFIGS))
N_PROJ0 = 6


def _proj0_kernel(x_ref, w_ref, wf_ref, cos_ref, sup_ref, sdn_ref, *refs):
    tm = x_ref.shape[0]
    outs = refs[:N_PROJ0]
    strided = refs[N_PROJ0:-2]
    ft_ref, row_sc = refs[-2:]
    xb = x_ref[...].astype(BF16)
    scale = HEAD_DIM ** -0.5
    for j, o_ref in enumerate(outs):
        full = _dot(xb, w_ref[:, j * ATT_WIDTH:(j + 1) * ATT_WIDTH])
        for c in range(ATT_WIDTH // LANES):
            r = full[:, c * LANES:(c + 1) * LANES]
            if j in (3, 4):
                r = (r * cos_ref[...]
                     + pltpu.roll(r, LANES - ROPE_HALF, axis=1) * sup_ref[...]
                     + pltpu.roll(r, ROPE_HALF, axis=1) * sdn_ref[...])
            if j in (0, 3):
                r = r * scale
            o_ref[:, c * LANES:(c + 1) * LANES] = r.astype(BF16)
            if j >= 3:
                row_sc[0:tm, :] = r
                pieces = [(0, 1)]
                for n, d in enumerate(DILATIONS[1:]):
                    s_ref = strided[3 * n + (j - 3)]
                    step = d // pieces[0][1]
                    rows_prev = tm // pieces[0][1]
                    new_pieces = []
                    vals = []
                    for idx, (res_prev, d_prev) in enumerate(pieces):
                        for b in range(step):
                            src = pl.ds(idx * rows_prev + b, rows_prev // step, stride=step)
                            vals.append(row_sc[src, :])
                            new_pieces.append((res_prev + d_prev * b, d))
                    for idx, ((res, _), val) in enumerate(zip(new_pieces, vals)):
                        lo = res * ATT_WIDTH + c * LANES
                        s_ref[:, lo:lo + LANES] = val.astype(BF16)
                        if d != DILATIONS[-1]:
                            row_sc[idx * (tm // d):(idx + 1) * (tm // d), :] = val
                    pieces = new_pieces
    ft_ref[...] = _dot_nt(wf_ref[...], xb)


def _proj0(x, w_main, wf_t, cos_t, sup_t, sdn_t, tm):
    B, S, _ = x.shape
    n_i = S // tm
    act = jax.ShapeDtypeStruct((B, S, ATT_WIDTH), BF16)
    act_spec = pl.BlockSpec((None, tm, ATT_WIDTH), lambda b, i: (b, i, 0))
    tab_spec = pl.BlockSpec((tm, LANES), lambda b, i: (i, 0))
    out_specs = [act_spec] * N_PROJ0
    out_shape = [act] * N_PROJ0
    for d in DILATIONS[1:]:
        out_specs += [pl.BlockSpec((None, tm // d, d * ATT_WIDTH), lambda b, i: (b, i, 0))] * 3
        out_shape += [jax.ShapeDtypeStruct((B, S // d, d * ATT_WIDTH), BF16)] * 3
    out_specs.append(pl.BlockSpec((None, N_HEADS, tm), lambda b, i: (b, 0, i)))
    out_shape.append(jax.ShapeDtypeStruct((B, N_HEADS, S), F32))
    return pl.pallas_call(
        _proj0_kernel,
        grid=(B, n_i),
        in_specs=[
            pl.BlockSpec((None, tm, D_MODEL), lambda b, i: (b, i, 0)),
            pl.BlockSpec((D_MODEL, N_PROJ0 * ATT_WIDTH), lambda b, i: (0, 0)),
            pl.BlockSpec((N_HEADS, D_MODEL), lambda b, i: (0, 0)),
            tab_spec, tab_spec, tab_spec,
        ],
        out_specs=out_specs,
        out_shape=out_shape,
        scratch_shapes=[pltpu.VMEM((tm, LANES), F32)],
        compiler_params=_cparams(("parallel", "parallel"), 48),
        name="proj0",
    )(x, w_main, wf_t, cos_t, sup_t, sdn_t)


def _fox_cumsum_kernel(f_ref, bias_ref, c_ref):
    S = f_ref.shape[1]
    lf = _log_sigmoid(f_ref[...] + bias_ref[...])
    row = lax.broadcasted_iota(jnp.int32, (S, S), 0)
    col = lax.broadcasted_iota(jnp.int32, (S, S), 1)
    upper = jnp.where(row <= col, 1.0, 0.0).astype(BF16)
    hi, mid, lo = _split3(lf)
    c_ref[...] = _dot(hi, upper) + _dot(mid, upper) + _dot(lo, upper)


def _fox_cumsum(f_t, bias_col):
    R, S = f_t.shape
    return pl.pallas_call(
        _fox_cumsum_kernel,
        grid=(1,),
        in_specs=[pl.BlockSpec((R, S), lambda i: (0, 0)), pl.BlockSpec((R, 1), lambda i: (0, 0))],
        out_specs=pl.BlockSpec((R, S), lambda i: (0, 0)),
        out_shape=jax.ShapeDtypeStruct((R, S), F32),
        compiler_params=_cparams(("arbitrary",), 48),
        name="fox_cumsum",
    )(f_t, bias_col)


def _fox_kernel(q_ref, k_ref, v_ref, c_ref, o_ref, m_sc, l_sc, acc_sc, *, t, n):
    i = pl.program_id(1)
    m_sc[...] = jnp.full(m_sc.shape, NEG_INF, F32)
    l_sc[...] = jnp.zeros(l_sc.shape, F32)
    acc_sc[...] = jnp.zeros(acc_sc.shape, F32)

    def step(j, masked):
        kr = slice(j * t, (j + 1) * t)
        if masked:
            key = lax.broadcasted_iota(jnp.int32, (t, t), 0)
            qry = lax.broadcasted_iota(jnp.int32, (t, t), 1)
            keep = key <= qry
        for h in range(N_HEADS):
            sl = slice(h * HEAD_DIM, (h + 1) * HEAD_DIM)
            s = _dot_nt(k_ref[kr, sl], q_ref[:, sl]) - c_ref[kr, h:h + 1]
            if masked:
                s = jnp.where(keep, s, NEG_INF)
            m_prev = m_sc[h:h + 1, :]
            m_new = jnp.maximum(m_prev, jnp.max(s, axis=0, keepdims=True))
            alpha = jnp.exp(m_prev - m_new)
            p = jnp.exp(s - m_new)
            l_sc[h:h + 1, :] = alpha * l_sc[h:h + 1, :] + jnp.sum(p, axis=0, keepdims=True)
            acc_sc[sl, :] = alpha * acc_sc[sl, :] + _dot_tn(v_ref[kr, sl], p.astype(BF16))
            m_sc[h:h + 1, :] = m_new

    for j in range(n):
        if j < n - 1:
            pl.when(j < i)(functools.partial(step, j, False))
        pl.when(j == i)(functools.partial(step, j, True))

    out_t = jnp.concatenate(
        [acc_sc[h * HEAD_DIM:(h + 1) * HEAD_DIM, :] / l_sc[h:h + 1, :] for h in range(N_HEADS)],
        axis=0)
    o_ref[...] = out_t.T.astype(BF16)


def _fox_attention(q, k, v, c, t):
    B, S, _ = q.shape
    n = S // t
    q_spec = pl.BlockSpec((None, t, ATT_WIDTH), lambda b, i: (b, i, 0))
    kv_spec = pl.BlockSpec((None, S, ATT_WIDTH), lambda b, i: (b, 0, 0))
    return pl.pallas_call(
        functools.partial(_fox_kernel, t=t, n=n),
        grid=(B, n),
        in_specs=[q_spec, kv_spec, kv_spec,
                  pl.BlockSpec((None, S, N_HEADS), lambda b, i: (b, 0, 0))],
        out_specs=q_spec,
        out_shape=jax.ShapeDtypeStruct((B, S, ATT_WIDTH), BF16),
        scratch_shapes=[pltpu.VMEM((N_HEADS, t), F32), pltpu.VMEM((N_HEADS, t), F32),
                        pltpu.VMEM((ATT_WIDTH, t), F32)],
        compiler_params=_cparams(("parallel", "parallel"), 48),
        name="fox_attention",
    )(q, k, v, c)


def _dil_kernel(*refs, d, rg, sub, tqu, has_prev, first, last):
    refs = list(refs)
    q_ref, k_ref, v_ref = refs[:3]
    pos = 3
    if has_prev:
        kp_ref, vp_ref = refs[pos:pos + 2]
        pos += 2
    if not first:
        acc_in, st_in = refs[pos:pos + 2]
        pos += 2
    if last:
        o_ref = refs[pos]
    else:
        acc_out, st_out = refs[pos:pos + 2]

    blk = pl.program_id(1)
    if has_prev:
        key = lax.broadcasted_iota(jnp.int32, (2 * BAND, BAND), 0)
        qry = lax.broadcasted_iota(jnp.int32, (2 * BAND, BAND), 1)
        band = jnp.logical_and(key >= qry, key <= qry + BAND)
        band_edge = jnp.logical_and(band, jnp.logical_or(key >= BAND, blk > 0))
    else:
        key = lax.broadcasted_iota(jnp.int32, (tqu, tqu), 0)
        qry = lax.broadcasted_iota(jnp.int32, (tqu, tqu), 1)
        keep = key <= qry
    pair = LANES // HEAD_DIM

    if first and not has_prev and not last and rg % 2 == 0:
        n2 = 2 * tqu
        for r2 in range(rg // 2):
            res = (2 * r2, 2 * r2 + 1)
            nats = [pl.ds(pl.program_id(2) * rg + rr, tqu, stride=d) for rr in res]
            ms, ls = [], []
            for slab in range(N_HEADS // pair):
                outs = []
                for hh in range(pair):
                    h = slab * pair + hh
                    sls = [slice(rr * ATT_WIDTH + h * HEAD_DIM, rr * ATT_WIDTH + (h + 1) * HEAD_DIM)
                           for rr in res]
                    q = jnp.concatenate([q_ref[:, sl] for sl in sls], axis=0)
                    k_cat = jnp.concatenate([k_ref[:, sl] for sl in sls], axis=0)
                    v_cat = jnp.concatenate([v_ref[:, sl] for sl in sls], axis=0)
                    s_all = _dot_nt(k_cat, q)
                    m_parts, l_parts, acc_parts = [], [], []
                    for j in range(2):
                        blk_j = slice(j * tqu, (j + 1) * tqu)
                        s = jnp.where(keep, s_all[blk_j, blk_j], NEG_INF)
                        m = jnp.max(s, axis=0, keepdims=True)
                        p = jnp.exp(s - m)
                        l_parts.append(jnp.sum(p, axis=0, keepdims=True))
                        m_parts.append(m)
                        acc_parts.append(_dot_tn(v_cat[blk_j, :], p.astype(BF16)))
                    ls.append(jnp.concatenate(l_parts, axis=1))
                    ms.append(jnp.concatenate(m_parts, axis=1))
                    outs.append(jnp.concatenate(acc_parts, axis=1))
                slab_out = jnp.concatenate(outs, axis=0).T
                for j, nat in enumerate(nats):
                    acc_out[slab, nat, :] = slab_out[j * tqu:(j + 1) * tqu, :]
            pad = jnp.zeros((LANES - 2 * N_HEADS, n2), F32)
            st_t = jnp.concatenate(ms + ls + [pad], axis=0).T
            for j, nat in enumerate(nats):
                st_out[nat, :] = st_t[j * tqu:(j + 1) * tqu, :]
        return

    for a in range(sub):
        rows = slice(a * tqu, (a + 1) * tqu)
        for rr in range(rg):
            if d == 1:
                nat = rows
            else:
                nat = pl.ds(a * tqu * d + pl.program_id(2) * rg + rr, tqu, stride=d)
            if not first:
                st_old = st_in[nat, :].T
            ms, ls = [], []
            for slab in range(N_HEADS // pair):
                slab_lanes = slice(rr * ATT_WIDTH + slab * LANES, rr * ATT_WIDTH + (slab + 1) * LANES)
                if not first:
                    acc_old = acc_in[slab, nat, :].T
                outs = []
                for hh in range(pair):
                    h = slab * pair + hh
                    lo = rr * ATT_WIDTH + h * HEAD_DIM
                    sl = slice(lo, lo + HEAD_DIM)
                    q = q_ref[rows, sl]
                    if not has_prev:
                        k_cat, v_cat, mask = k_ref[rows, sl], v_ref[rows, sl], keep
                    elif a == 0:
                        k_cat = jnp.concatenate([kp_ref[:, sl], k_ref[rows, sl]], axis=0)
                        v_cat = jnp.concatenate([vp_ref[:, sl], v_ref[rows, sl]], axis=0)
                    else:
                        krows = slice(a * tqu - BAND, (a + 1) * tqu)
                        k_cat, v_cat = k_ref[krows, sl], v_ref[krows, sl]
                    s_all = _dot_nt(k_cat, q)
                    n_qs = tqu // BAND if has_prev else 1
                    wq_s = tqu // n_qs
                    m_parts, l_parts, acc_parts = [], [], []
                    for qh in range(n_qs):
                        qs = slice(qh * wq_s, (qh + 1) * wq_s)
                        if has_prev:
                            ks = slice(qh * BAND, qh * BAND + 2 * BAND)
                            part_mask = band_edge if (a == 0 and qh == 0) else band
                            s = jnp.where(part_mask, s_all[ks, qs], NEG_INF)
                            v_part = v_cat[ks, :]
                        else:
                            s = jnp.where(mask, s_all, NEG_INF)
                            v_part = v_cat
                        m = jnp.max(s, axis=0, keepdims=True)
                        if not first:
                            m_old = st_old[h:h + 1, qs]
                            l_old = st_old[N_HEADS + h:N_HEADS + h + 1, qs]
                            m_new = jnp.maximum(m, m_old)
                            alpha = jnp.exp(m_old - m_new)
                            m = m_new
                        p = jnp.exp(s - m)
                        l = jnp.sum(p, axis=0, keepdims=True)
                        acc = _dot_tn(v_part, p.astype(BF16))
                        if not first:
                            l = l + alpha * l_old
                            acc = acc + alpha * acc_old[hh * HEAD_DIM:(hh + 1) * HEAD_DIM, qs]
                        if last:
                            acc = acc / l
                        m_parts.append(m)
                        l_parts.append(l)
                        acc_parts.append(acc)
                    outs.append(jnp.concatenate(acc_parts, axis=1))
                    ms.append(jnp.concatenate(m_parts, axis=1))
                    ls.append(jnp.concatenate(l_parts, axis=1))
                slab_out = jnp.concatenate(outs, axis=0).T
                if last:
                    o_ref[rows, slab_lanes] = slab_out.astype(BF16)
                else:
                    acc_out[slab, nat, :] = slab_out
            if not last:
                pad = jnp.zeros((LANES - 2 * N_HEADS, tqu), F32)
                st_out[nat, :] = jnp.concatenate(ms + ls + [pad], axis=0).T


def _dilated_branch(q, k, v, state, dilation, last):
    B, L, _ = q.shape
    d = dilation
    S = L * d
    first = state is None
    assert not (last and d != 1)
    rg = min(4, d)
    tqu = min(2 * BAND, L)
    tqb = min(L, 4 * tqu // rg)
    sub = tqb // tqu
    n_blk = L // tqb
    n_grp = d // rg
    has_prev = L > BAND
    wq = rg * ATT_WIDTH
    n_slab = ATT_WIDTH // LANES

    main = lambda b, i, g: (b, i, g)
    prev = lambda b, i, g: (b, jnp.maximum(i * (tqb // BAND) - 1, 0), g)
    qkv_spec = pl.BlockSpec((None, tqb, wq), main)
    acc_spec = pl.BlockSpec((None, n_slab, tqb * d, LANES), lambda b, i, g: (b, 0, i, 0))
    st_spec = pl.BlockSpec((None, tqb * d, LANES), lambda b, i, g: (b, i, 0))
    in_specs = [qkv_spec] * 3
    args = [q, k, v]
    if has_prev:
        in_specs += [pl.BlockSpec((None, BAND, wq), prev)] * 2
        args += [k, v]
    if not first:
        in_specs += [acc_spec, st_spec]
        args += list(state)
    if last:
        out_specs = pl.BlockSpec((None, tqb, wq), main)
        out_shape = jax.ShapeDtypeStruct((B, S, ATT_WIDTH), BF16)
    else:
        out_specs = [acc_spec, st_spec]
        out_shape = [jax.ShapeDtypeStruct((B, n_slab, S, LANES), F32),
                     jax.ShapeDtypeStruct((B, S, LANES), F32)]
    return pl.pallas_call(
        functools.partial(_dil_kernel, d=d, rg=rg, sub=sub, tqu=tqu, has_prev=has_prev, first=first,
                          last=last),
        grid=(B, n_blk, n_grp),
        in_specs=in_specs,
        out_specs=out_specs,
        out_shape=out_shape,
        compiler_params=_cparams(("parallel", "parallel", "arbitrary"), 48),
        name=f"dilated_d{d}",
    )(*args)


def _dilated_attention(qkv_by_dilation):
    state = None
    order = sorted(DIL_CONFIGS, key=lambda wd: -wd[1])
    for n, (window, d) in enumerate(order):
        assert window // d == BAND
        state = _dilated_branch(*qkv_by_dilation[d], state, d, last=(n == len(order) - 1))
    return state


def _outproj_ln_rows(a_refs, w_refs, x_ref, g_ref, b_ref):
    y = _dot(a_refs[0][...], w_refs[0][...])
    for a_ref, w_ref in zip(a_refs[1:], w_refs[1:]):
        y = y + _dot(a_ref[...], w_ref[...])
    return _layer_norm(DEEPNORM_ALPHA * x_ref[...] + y, g_ref[...], b_ref[...])


def _route_top2(h, wt_ref, sel_ref, gate_ref):
    logits = lax.dot_general(wt_ref[...], h, (((1,), (1,)), ((), ())),
                             preferred_element_type=F32, precision=lax.Precision.HIGHEST)
    idx = lax.broadcasted_iota(jnp.int32, logits.shape, 0)
    m1 = jnp.max(logits, axis=0, keepdims=True)
    i1 = jnp.min(jnp.where(logits == m1, idx, N_EXPERTS), axis=0, keepdims=True)
    pick1 = idx == i1
    rest = jnp.where(pick1, NEG_INF, logits)
    m2 = jnp.max(rest, axis=0, keepdims=True)
    i2 = jnp.min(jnp.where(rest == m2, idx, N_EXPERTS), axis=0, keepdims=True)
    pick2 = idx == i2
    e2 = jnp.exp(m2 - m1)
    w1 = 1.0 / (1.0 + e2)
    w2 = e2 / (1.0 + e2)
    sel_ref[...] = jnp.where(jnp.logical_or(pick1, pick2), 1.0, 0.0)
    gate_ref[...] = jnp.where(pick1, w1, 0.0) + jnp.where(pick2, w2, 0.0)


def _outproj_router_kernel(a_ref, w_ref, x_ref, g_ref, b_ref, wt_ref, o_ref, sel_ref, gate_ref):
    h = _outproj_ln_rows([a_ref], [w_ref], x_ref, g_ref, b_ref)
    o_ref[...] = h
    _route_top2(h, wt_ref, sel_ref, gate_ref)


def _outproj_router(a, w, x, g, b, w_router_t, tm):
    T = x.shape[0]
    row = pl.BlockSpec((tm, D_MODEL), lambda i: (i, 0))
    vec = pl.BlockSpec((1, D_MODEL), lambda i: (0, 0))
    route = pl.BlockSpec((N_EXPERTS, tm), lambda i: (0, i))
    route_shape = jax.ShapeDtypeStruct((N_EXPERTS, T), F32)
    return pl.pallas_call(
        _outproj_router_kernel,
        grid=(T // tm,),
        in_specs=[pl.BlockSpec((tm, a.shape[1]), lambda i: (i, 0)),
                  pl.BlockSpec(w.shape, lambda i: (0, 0)),
                  row, vec, vec,
                  pl.BlockSpec((N_EXPERTS, D_MODEL), lambda i: (0, 0))],
        out_specs=[row, route, route],
        out_shape=[jax.ShapeDtypeStruct((T, D_MODEL), F32), route_shape, route_shape],
        compiler_params=_cparams(("parallel",), 48),
        name="outproj_router",
    )(a, w, x, g, b, w_router_t)


def _ffn_chunks(width):
    chunks, lo = [], 0
    while lo < width:
        size = min(2 * MXU_DIM, width - lo)
        chunks.append((lo, size))
        lo += size
    return chunks


def _dense_ffn_kernel(a0_ref, a1_ref, w0_ref, w1_ref, r_ref, g0_ref, b0_ref,
                      wg_ref, wu_ref, wd_ref, g_ref, b_ref, o_ref):
    x = _outproj_ln_rows([a0_ref, a1_ref], [w0_ref, w1_ref], r_ref, g0_ref, b0_ref)
    xb = x.astype(BF16)
    y = None
    for lo, size in _ffn_chunks(wg_ref.shape[1]):
        gate = _dot(xb, wg_ref[:, lo:lo + size])
        up = _dot(xb, wu_ref[:, lo:lo + size])
        act = (gate * jax.nn.sigmoid(gate) * up).astype(BF16)
        part = _dot(act, wd_ref[lo:lo + size, :])
        y = part if y is None else y + part
    o_ref[...] = _layer_norm(DEEPNORM_ALPHA * x + y, g_ref[...], b_ref[...])


def _mixer_out_dense_ffn(acts, ws, resid, g_mix, b_mix, wg, wu, wd, g, b, tm):
    T = resid.shape[0]
    F = wg.shape[1]
    row = pl.BlockSpec((tm, D_MODEL), lambda i: (i, 0))
    vec = pl.BlockSpec((1, D_MODEL), lambda i: (0, 0))
    once = pl.Buffered(1)
    const = lambda i: (0, 0)
    return pl.pallas_call(
        _dense_ffn_kernel,
        grid=(T // tm,),
        in_specs=[pl.BlockSpec((tm, a.shape[1]), lambda i: (i, 0)) for a in acts]
        + [pl.BlockSpec(w.shape, const, pipeline_mode=once) for w in ws]
        + [row, vec, vec,
           pl.BlockSpec((D_MODEL, F), const, pipeline_mode=once),
           pl.BlockSpec((D_MODEL, F), const, pipeline_mode=once),
           pl.BlockSpec((F, D_MODEL), const, pipeline_mode=once),
           vec, vec],
        out_specs=row,
        out_shape=jax.ShapeDtypeStruct((T, D_MODEL), F32),
        compiler_params=_cparams(("parallel",), 56),
        name="mixer_out_dense_ffn",
    )(*acts, *ws, resid, g_mix, b_mix, wg, wu, wd, g, b)


CONV_PAD = 8


def _proj1_kernel(x_ref, wqk_ref, wv_ref, wgt_ref, wconv_ref,
                  q_ref, k_ref, v_ref, gt_ref, buf):
    i = pl.program_id(1)
    tm = x_ref.shape[0]
    xb = x_ref[...].astype(BF16)
    kscale = MLSTM_HEAD_DIM ** -0.5

    @pl.when(i == 0)
    def _():
        buf[...] = jnp.zeros(buf.shape, F32)

    wide = 2 * MXU_DIM
    per_wide = wide // LANES
    for c in range(2 * D_MODEL // LANES):
        if c % per_wide == 0:
            pre = _dot(xb, wqk_ref[:, c * LANES:c * LANES + wide])
        lanes = slice(c * LANES, (c + 1) * LANES)
        cur = pre[:, (c % per_wide) * LANES:(c % per_wide + 1) * LANES]
        ext = jnp.concatenate([buf[:, lanes], cur], axis=0)
        y = cur * wconv_ref[CONV_WIDTH - 1:CONV_WIDTH, lanes]
        for back in range(1, CONV_WIDTH):
            tap = CONV_WIDTH - 1 - back
            y = y + pltpu.roll(ext, back, axis=0)[CONV_PAD:, :] * wconv_ref[tap:tap + 1, lanes]
        y = y * jax.nn.sigmoid(y)
        buf[:, lanes] = cur[tm - CONV_PAD:tm, :]
        if c < D_MODEL // LANES:
            q_ref[:, lanes] = y.astype(BF16)
        else:
            k_ref[:, c * LANES - D_MODEL:(c + 1) * LANES - D_MODEL] = (y * kscale).astype(BF16)
    for c in range(D_MODEL // wide):
        lanes = slice(c * wide, (c + 1) * wide)
        v_ref[:, lanes] = _dot(xb, wv_ref[:, lanes]).astype(BF16)
    gt_ref[...] = _dot(xb, wgt_ref[...])


def _proj1(x, wqk, wv, wgt, wconv, tm):
    B, S, _ = x.shape
    row = lambda b, i: (b, i, 0)
    const = lambda b, i: (0, 0)
    act_spec = pl.BlockSpec((None, tm, D_MODEL), row)
    act = jax.ShapeDtypeStruct((B, S, D_MODEL), BF16)
    return pl.pallas_call(
        _proj1_kernel,
        grid=(B, S // tm),
        in_specs=[act_spec,
                  pl.BlockSpec((D_MODEL, 2 * D_MODEL), const),
                  pl.BlockSpec((D_MODEL, D_MODEL), const),
                  pl.BlockSpec((D_MODEL, 2 * N_HEADS), const),
                  pl.BlockSpec((CONV_WIDTH, 2 * D_MODEL), const)],
        out_specs=[act_spec, act_spec, act_spec,
                   pl.BlockSpec((None, tm, 2 * N_HEADS), row)],
        out_shape=[act, act, act, jax.ShapeDtypeStruct((B, S, 2 * N_HEADS), F32)],
        scratch_shapes=[pltpu.VMEM((CONV_PAD, 2 * D_MODEL), F32)],
        compiler_params=_cparams(("parallel", "arbitrary"), 56),
        name="proj1",
    )(x, wqk, wv, wgt, wconv)


def _mlstm_kernel(q_ref, k_ref, v_ref, x_ref, wog_ref, gn_ref, gt_ref, bi_row, bf_row, bi_col,
                  bf_col, ng_ref, o_ref, c_sc, n_sc, m_sc, og_sc, *, L):
    ci = pl.program_id(1)

    @pl.when(ci == 0)
    def _():
        c_sc[...] = jnp.zeros(c_sc.shape, F32)
        n_sc[...] = jnp.zeros(n_sc.shape, F32)
        m_sc[...] = jnp.full(m_sc.shape, NEG_INF, F32)

    xb = x_ref[...].astype(BF16)
    wide = 2 * MXU_DIM
    for c in range(D_MODEL // wide):
        og_sc[:, c * wide:(c + 1) * wide] = _dot(xb, wog_ref[:, c * wide:(c + 1) * wide])

    row = lax.broadcasted_iota(jnp.int32, (L, L), 0)
    col = lax.broadcasted_iota(jnp.int32, (L, L), 1)
    causal = col <= row
    lower = jnp.where(causal, 1.0, 0.0).astype(BF16)
    upper = jnp.where(row <= col, 1.0, 0.0).astype(BF16)

    gn = gn_ref[...]
    gt = gt_ref[...]
    i_col = gn[:, :N_HEADS] + bi_row[...]
    lf_col = _log_sigmoid(gn[:, N_HEADS:] + bf_row[...])
    i_row = gt[:N_HEADS, :] + bi_col[...]
    lf_row = _log_sigmoid(gt[N_HEADS:, :] + bf_col[...])
    b_col = sum(_dot(lower, part) for part in _split3(lf_col))
    b_row = sum(_dot(part, upper) for part in _split3(lf_row))

    keep = row <= col
    n_pad = jnp.zeros((2 * N_HEADS - 3, MLSTM_HEAD_DIM), F32)
    for h in range(N_HEADS):
        lanes = slice(h * MLSTM_HEAD_DIM, (h + 1) * MLSTM_HEAD_DIM)
        q = q_ref[:, lanes]
        k = k_ref[:, lanes]
        v = v_ref[:, lanes]
        bt = b_row[h:h + 1, :]
        key_term = i_col[:, h:h + 1] - b_col[:, h:h + 1]
        m_prev = m_sc[h]
        ct_prev = c_sc[h]
        n_prev = n_sc[h]

        dlog = jnp.where(keep, bt + key_term, NEG_INF)
        inter = bt + m_prev
        m_t = jnp.maximum(inter, jnp.max(dlog, axis=0, keepdims=True))
        s = _dot_nt(k, q) * jnp.exp(dlog - m_t)
        inter_w = jnp.exp(inter - m_t)
        num = _dot_tn(v, s.astype(BF16)) + inter_w * _dot_nt(ct_prev.astype(BF16), q)
        n_parts = jnp.concatenate([p.astype(F32) for p in _split3(n_prev)] + [n_pad], axis=0)
        qn = jnp.sum(_dot_nt(n_parts.astype(BF16), q), axis=0, keepdims=True)
        den = jnp.sum(s, axis=0, keepdims=True) + inter_w * qn
        hh = num / jnp.maximum(jnp.abs(den), jnp.exp(-m_t))

        b_last = bt[:, L - 1:L]
        g = b_last + key_term
        m_new = jnp.maximum(b_last + m_prev, jnp.max(g, axis=0, keepdims=True))
        w = jnp.exp(g - m_new)
        decay = jnp.exp(b_last + m_prev - m_new)
        kw = k.astype(F32) * w
        c_sc[h] = decay * ct_prev + _dot_tn(v, kw.astype(BF16))
        n_sc[h] = decay * n_prev + jnp.sum(kw, axis=0, keepdims=True)
        m_sc[h] = m_new

        mu = jnp.mean(hh, axis=0, keepdims=True)
        hc = hh - mu
        var = jnp.mean(hc * hc, axis=0, keepdims=True)
        hn = (hc * lax.rsqrt(var + LN_EPS)).T * ng_ref[:, lanes]
        o_ref[:, lanes] = (hn * jax.nn.sigmoid(og_sc[:, lanes])).astype(BF16)


def _mlstm(q, k, v, x, wog, gates, gates_t, b_i, b_f, norm_g, L):
    B, S, _ = q.shape
    row = lambda b, c: (b, c, 0)
    const = lambda b, c: (0, 0)
    act_spec = pl.BlockSpec((None, L, D_MODEL), row)
    return pl.pallas_call(
        functools.partial(_mlstm_kernel, L=L),
        grid=(B, S // L),
        in_specs=[act_spec, act_spec, act_spec, act_spec,
                  pl.BlockSpec((D_MODEL, D_MODEL), const),
                  pl.BlockSpec((None, L, 2 * N_HEADS), row),
                  pl.BlockSpec((None, 2 * N_HEADS, L), lambda b, c: (b, 0, c)),
                  pl.BlockSpec((1, N_HEADS), const), pl.BlockSpec((1, N_HEADS), const),
                  pl.BlockSpec((N_HEADS, 1), const), pl.BlockSpec((N_HEADS, 1), const),
                  pl.BlockSpec((1, D_MODEL), const)],
        out_specs=act_spec,
        out_shape=jax.ShapeDtypeStruct((B, S, D_MODEL), BF16),
        scratch_shapes=[pltpu.VMEM((N_HEADS, MLSTM_HEAD_DIM, MLSTM_HEAD_DIM), F32),
                        pltpu.VMEM((N_HEADS, 1, MLSTM_HEAD_DIM), F32),
                        pltpu.VMEM((N_HEADS, 1, 1), F32),
                        pltpu.VMEM((L, D_MODEL), F32)],
        compiler_params=_cparams(("parallel", "arbitrary"), 48),
        name="mlstm",
    )(q, k, v, x, wog, gates, gates_t, b_i[None, :], b_f[None, :], b_i[:, None], b_f[:, None],
      norm_g[None, :])


MOE_TOP_K = 2
MOE_TILE = 512
RANK_BLOCK = 1024


def _rank_kernel(sel_ref, rank_ref, count_ref, upper_sc, carry_sc):
    n = sel_ref.shape[1]

    @pl.when(pl.program_id(0) == 0)
    def _():
        row = lax.broadcasted_iota(jnp.int32, (n, n), 0)
        col = lax.broadcasted_iota(jnp.int32, (n, n), 1)
        upper_sc[...] = jnp.where(row <= col, 1.0, 0.0).astype(BF16)
        carry_sc[...] = jnp.zeros(carry_sc.shape, F32)

    sel = sel_ref[...]
    incl = _dot(sel.astype(BF16), upper_sc[...])
    rank_ref[...] = carry_sc[...] + incl - sel
    carry_sc[...] = carry_sc[...] + incl[:, n - 1:n]
    count_ref[...] = carry_sc[...]


def _rank(sel_t):
    E, T = sel_t.shape
    n = min(RANK_BLOCK, T)
    return pl.pallas_call(
        _rank_kernel,
        grid=(T // n,),
        in_specs=[pl.BlockSpec((E, n), lambda i: (0, i))],
        out_specs=[pl.BlockSpec((E, n), lambda i: (0, i)), pl.BlockSpec((E, 1), lambda i: (0, 0))],
        out_shape=[jax.ShapeDtypeStruct((E, T), F32), jax.ShapeDtypeStruct((E, 1), F32)],
        scratch_shapes=[pltpu.VMEM((n, n), BF16), pltpu.VMEM((E, 1), F32)],
        compiler_params=_cparams(("arbitrary",), 32),
        name="moe_rank",
    )(sel_t)


def _slot_kernel(sel_ref, gate_ref, rank_ref, off_ref, pos_ref, w_ref):
    sel = sel_ref[...] > 0.0
    idx = lax.broadcasted_iota(jnp.int32, sel.shape, 0)
    first = jnp.min(jnp.where(sel, idx, N_EXPERTS), axis=0, keepdims=True)
    second = jnp.max(jnp.where(sel, idx, -1), axis=0, keepdims=True)
    slot = off_ref[...] + rank_ref[...]
    gate = gate_ref[...]
    rows_p, rows_w = [], []
    for which in (first, second):
        hit = idx == which
        rows_p.append(jnp.sum(jnp.where(hit, slot, 0.0), axis=0, keepdims=True))
        rows_w.append(jnp.sum(jnp.where(hit, gate, 0.0), axis=0, keepdims=True))
    pos_ref[...] = jnp.concatenate(rows_p, axis=0).astype(jnp.int32)
    w_ref[...] = jnp.concatenate(rows_w, axis=0)


def _slots(sel_t, gate_t, rank_t, offsets_col, tm):
    E, T = sel_t.shape
    spec = pl.BlockSpec((E, tm), lambda i: (0, i))
    out_spec = pl.BlockSpec((MOE_TOP_K, tm), lambda i: (0, i))
    return pl.pallas_call(
        _slot_kernel,
        grid=(T // tm,),
        in_specs=[spec, spec, spec, pl.BlockSpec((E, 1), lambda i: (0, 0))],
        out_specs=[out_spec, out_spec],
        out_shape=[jax.ShapeDtypeStruct((MOE_TOP_K, T), jnp.int32),
                   jax.ShapeDtypeStruct((MOE_TOP_K, T), F32)],
        compiler_params=_cparams(("parallel",), 32),
        name="moe_slots",
    )(sel_t, gate_t, rank_t, offsets_col)


def _dispatch_kernel(tail_ref, used_ref, pos_ref, x_ref, xs_hbm, zero_sc, sem, zsem, *, tm):
    i = pl.program_id(0)
    n_tiles = xs_hbm.shape[0] // MOE_TILE

    def fill(tile):
        return pltpu.make_async_copy(zero_sc, xs_hbm.at[pl.ds(tile * MOE_TILE, MOE_TILE)], zsem)

    @pl.when(i == 0)
    def _():
        zero_sc[...] = jnp.zeros(zero_sc.shape, F32)
        for e in range(N_EXPERTS):
            fill(tail_ref[e]).start()
        for e in range(N_EXPERTS):
            fill(tail_ref[e]).wait()
        for j in range(N_EXPERTS):
            @pl.when(n_tiles - 1 - j >= used_ref[0])
            def _():
                c = fill(n_tiles - 1 - j)
                c.start()
                c.wait()

    def issue(t, carry):
        for k in range(MOE_TOP_K):
            pltpu.make_async_copy(x_ref.at[pl.ds(t, 1)], xs_hbm.at[pl.ds(pos_ref[k, t], 1)],
                                  sem).start(priority=k)
        return carry

    lax.fori_loop(0, tm, issue, 0, unroll=8)
    for _ in range(MOE_TOP_K):
        pltpu.make_async_copy(x_ref, xs_hbm.at[pl.ds(0, tm)], sem).wait()


def _dispatch(x, pos_t, tail_tiles, n_used, n_rows, tm):
    T = x.shape[0]
    n_steps = T // tm
    pos3 = pos_t.reshape(MOE_TOP_K, n_steps, tm).transpose(1, 0, 2)
    return pl.pallas_call(
        functools.partial(_dispatch_kernel, tm=tm),
        grid_spec=pltpu.PrefetchScalarGridSpec(
            num_scalar_prefetch=2,
            grid=(n_steps,),
            in_specs=[pl.BlockSpec((None, MOE_TOP_K, tm), lambda i, tail, used: (i, 0, 0),
                                   memory_space=pltpu.SMEM),
                      pl.BlockSpec((tm, D_MODEL), lambda i, tail, used: (i, 0))],
            out_specs=pl.BlockSpec(memory_space=pl.ANY),
            scratch_shapes=[pltpu.VMEM((MOE_TILE, D_MODEL), F32),
                            pltpu.SemaphoreType.DMA, pltpu.SemaphoreType.DMA],
        ),
        out_shape=jax.ShapeDtypeStruct((n_rows, D_MODEL), F32),
        compiler_params=_cparams(("arbitrary",), 32),
        name="moe_dispatch",
    )(tail_tiles, n_used, pos3, x)


def _expert_ffn_kernel(te_ref, used_ref, x_ref, wg_ref, wu_ref, wd_ref, o_ref):
    live = pl.program_id(0) < used_ref[0]

    @pl.when(live)
    def _():
        xb = x_ref[...].astype(BF16)
        y = None
        for lo, size in _ffn_chunks(wg_ref.shape[1]):
            gate = _dot(xb, wg_ref[:, lo:lo + size])
            up = _dot(xb, wu_ref[:, lo:lo + size])
            act = (gate * jax.nn.sigmoid(gate) * up).astype(BF16)
            part = _dot(act, wd_ref[lo:lo + size, :])
            y = part if y is None else y + part
        o_ref[...] = y

    @pl.when(jnp.logical_not(live))
    def _():
        o_ref[...] = jnp.zeros(o_ref.shape, F32)


def _expert_ffn(xs, tile_expert, n_used, wg, wu, wd):
    n_rows = xs.shape[0]
    E, _, F = wg.shape
    tm = MOE_TILE
    once = pl.Buffered(1)

    def row(i, te, used):
        return (jnp.minimum(i, used[0] - 1), 0)

    return pl.pallas_call(
        _expert_ffn_kernel,
        grid_spec=pltpu.PrefetchScalarGridSpec(
            num_scalar_prefetch=2,
            grid=(n_rows // tm,),
            in_specs=[pl.BlockSpec((tm, D_MODEL), row),
                      pl.BlockSpec((None, D_MODEL, F), lambda i, te, used: (te[i], 0, 0),
                                   pipeline_mode=once),
                      pl.BlockSpec((None, D_MODEL, F), lambda i, te, used: (te[i], 0, 0),
                                   pipeline_mode=once),
                      pl.BlockSpec((None, F, D_MODEL), lambda i, te, used: (te[i], 0, 0),
                                   pipeline_mode=once)],
            out_specs=pl.BlockSpec((tm, D_MODEL), lambda i, te, used: (i, 0)),
        ),
        out_shape=jax.ShapeDtypeStruct((n_rows, D_MODEL), F32),
        compiler_params=_cparams(("arbitrary",), 56),
        name="moe_expert_ffn",
    )(tile_expert, n_used, xs, wg, wu, wd)


def _combine_kernel(pos_ref, w_ref, x_ref, g_ref, b_ref, ys_hbm, o_ref, rows_sc, sem, *, tm):
    def issue(t, carry):
        for k in range(MOE_TOP_K):
            pltpu.make_async_copy(ys_hbm.at[pl.ds(pos_ref[k, t], 1)],
                                  rows_sc.at[k, pl.ds(t, 1)], sem).start(priority=k)
        return carry

    lax.fori_loop(0, tm, issue, 0, unroll=8)
    for k in range(MOE_TOP_K):
        pltpu.make_async_copy(ys_hbm.at[pl.ds(0, tm)], rows_sc.at[k], sem).wait()
    w = w_ref[...]
    y = w[:, 0:1] * rows_sc[0] + w[:, 1:2] * rows_sc[1]
    o_ref[...] = _layer_norm(DEEPNORM_ALPHA * x_ref[...] + y, g_ref[...], b_ref[...])


def _combine_ln(ys, pos_t, w_nat, x, g, b, tm):
    T = x.shape[0]
    n_steps = T // tm
    pos3 = pos_t.reshape(MOE_TOP_K, n_steps, tm).transpose(1, 0, 2)
    row = pl.BlockSpec((tm, D_MODEL), lambda i: (i, 0))
    vec = pl.BlockSpec((1, D_MODEL), lambda i: (0, 0))
    return pl.pallas_call(
        functools.partial(_combine_kernel, tm=tm),
        grid=(n_steps,),
        in_specs=[pl.BlockSpec((None, MOE_TOP_K, tm), lambda i: (i, 0, 0), memory_space=pltpu.SMEM),
                  pl.BlockSpec((tm, MOE_TOP_K), lambda i: (i, 0)),
                  row, vec, vec,
                  pl.BlockSpec(memory_space=pl.ANY)],
        out_specs=row,
        out_shape=jax.ShapeDtypeStruct((T, D_MODEL), F32),
        scratch_shapes=[pltpu.VMEM((MOE_TOP_K, tm, D_MODEL), F32), pltpu.SemaphoreType.DMA],
        compiler_params=_cparams(("arbitrary",), 32),
        name="moe_combine_ln",
    )(pos3, w_nat, x, g, b, ys)


def _moe_sparse(x, sel_t, gate_t, wg, wu, wd, g, b):
    T = x.shape[0]
    tile = MOE_TILE
    n_rows = MOE_TOP_K * T + N_EXPERTS * tile
    n_tiles = n_rows // tile
    rank_t, counts = _rank(sel_t)
    tiles_per = jnp.ceil(counts[:, 0] / tile).astype(jnp.int32)
    tile_end = jnp.cumsum(tiles_per)
    tile_start = tile_end - tiles_per
    n_used = tile_end[-1:]
    tile_expert = jnp.minimum(
        jnp.sum(jnp.arange(n_tiles, dtype=jnp.int32)[:, None] >= tile_end[None, :], axis=1),
        N_EXPERTS - 1).astype(jnp.int32)
    tail_tiles = jnp.maximum(tile_end - 1, 0).astype(jnp.int32)
    offsets_col = (tile_start * tile).astype(F32)[:, None]
    pos_t, w_t = _slots(sel_t, gate_t, rank_t, offsets_col, tm=1024)
    xs = _dispatch(x, pos_t, tail_tiles, n_used, n_rows, tm=1024)
    ys = _expert_ffn(xs, tile_expert, n_used, wg, wu, wd)
    return _combine_ln(ys, pos_t, w_t.T, x, g, b, tm=512)


def _rotary_tables(S):
    pos = jnp.arange(S, dtype=jnp.int32)
    inv = ROPE_THETA ** (-jnp.arange(ROPE_HALF, dtype=F32) / ROPE_HALF)
    ang = pos.astype(F32)[:, None] * inv[None, :]
    cos, sin = jnp.cos(ang), jnp.sin(ang)
    ones = jnp.ones((S, HEAD_DIM - ROPE_DIMS), F32)
    zeros = jnp.zeros((S, HEAD_DIM - ROPE_DIMS), F32)
    z8 = jnp.zeros((S, ROPE_HALF), F32)
    cos_h = jnp.concatenate([cos, cos, ones], axis=1)
    sup_h = jnp.concatenate([-sin, z8, zeros], axis=1)
    sdn_h = jnp.concatenate([z8, sin, zeros], axis=1)
    rep = LANES // HEAD_DIM
    return (jnp.tile(cos_h, (1, rep)), jnp.tile(sup_h, (1, rep)), jnp.tile(sdn_h, (1, rep)))


def _even_layer(h, w_in, b_forget, w_out, ln_mix_g, ln_mix_b, w_gate, w_up, w_down,
                ln_ffn_g, ln_ffn_b):
    B, S, _ = h.shape
    T = B * S
    W = ATT_WIDTH
    w_main = jnp.concatenate([w_in[:, :3 * W], w_in[:, 3 * W + N_HEADS:]], axis=1).astype(BF16)
    wf_t = w_in[:, 3 * W:3 * W + N_HEADS].T.astype(BF16)
    cos_t, sup_t, sdn_t = _rotary_tables(S)
    outs = _proj0(h, w_main, wf_t, cos_t, sup_t, sdn_t, tm=min(512, S))
    qa, ka, va = outs[:3]
    qkv_by_dilation = {d: tuple(outs[3 + 3 * n:6 + 3 * n]) for n, d in enumerate(DILATIONS)}
    f_t = outs[-1]

    bias_col = jnp.tile(b_forget, B)[:, None]
    c = _fox_cumsum(f_t.reshape(B * N_HEADS, S), bias_col).reshape(B, N_HEADS, S)
    o_fox = _fox_attention(qa, ka, va, jnp.swapaxes(c, 1, 2), t=min(512, S))
    o_dil = _dilated_attention(qkv_by_dilation)

    w_out_b = w_out.astype(BF16)
    tm = 512
    h2 = _mixer_out_dense_ffn(
        [o_fox.reshape(T, W), o_dil.reshape(T, W)], [w_out_b[:W], w_out_b[W:]],
        h.reshape(T, D_MODEL), ln_mix_g[None, :], ln_mix_b[None, :],
        w_gate.astype(BF16), w_up.astype(BF16), w_down.astype(BF16),
        ln_ffn_g[None, :], ln_ffn_b[None, :], tm)
    return h2.reshape(B, S, D_MODEL)


def _odd_layer(h, w_in, b_igate, b_fgate, w_conv, norm_g, w_out, ln_mix_g, ln_mix_b, w_router,
               w_gate, w_up, w_down, ln_ffn_g, ln_ffn_b):
    B, S, _ = h.shape
    T = B * S
    D = D_MODEL
    wqk = w_in[:, :2 * D].astype(BF16)
    wv = w_in[:, 2 * D:3 * D].astype(BF16)
    wgt = w_in[:, 3 * D:3 * D + 2 * N_HEADS].astype(BF16)
    wog = w_in[:, 3 * D + 2 * N_HEADS:].astype(BF16)
    q, k, v, gates = _proj1(h, wqk, wv, wgt, w_conv, tm=min(512, S))
    gates_t = jnp.swapaxes(gates, 1, 2)
    hm = _mlstm(q, k, v, h, wog, gates, gates_t, b_igate, b_fgate, norm_g, L=min(256, S))

    tm = 512
    h1, sel_t, gate_t = _outproj_router(hm.reshape(T, D), w_out.astype(BF16), h.reshape(T, D),
                                        ln_mix_g[None, :], ln_mix_b[None, :], w_router.T, tm)
    h2 = _moe_sparse(h1, sel_t, gate_t, w_gate.astype(BF16), w_up.astype(BF16),
                     w_down.astype(BF16), ln_ffn_g[None, :], ln_ffn_b[None, :])
    return h2.reshape(B, S, D)


def kernel(x, w_in_e, b_forget_e, w_out_e, ln_mix_g_e, ln_mix_b_e, ffn_w_gate_e, ffn_w_up_e,
           ffn_w_down_e, ln_ffn_g_e, ln_ffn_b_e, w_in_o, b_igate_o, b_fgate_o, w_conv_o,
           mlstm_norm_g_o, w_out_o, ln_mix_g_o, ln_mix_b_o, w_router_o, moe_w_gate_o,
           moe_w_up_o, moe_w_down_o, ln_ffn_g_o, ln_ffn_b_o):
    h = x
    for layer in range(DEPTH):
        i = layer // 2
        if layer % 2 == 0:
            h = _even_layer(h, w_in_e[i], b_forget_e[i], w_out_e[i], ln_mix_g_e[i], ln_mix_b_e[i],
                            ffn_w_gate_e[i], ffn_w_up_e[i], ffn_w_down_e[i], ln_ffn_g_e[i],
                            ln_ffn_b_e[i])
        else:
            h = _odd_layer(h, w_in_o[i], b_igate_o[i], b_fgate_o[i], w_conv_o[i],
                           mlstm_norm_g_o[i], w_out_o[i], ln_mix_g_o[i], ln_mix_b_o[i],
                           w_router_o[i], moe_w_gate_o[i], moe_w_up_o[i], moe_w_down_o[i],
                           ln_ffn_g_o[i], ln_ffn_b_o[i])
    return h
```

```python
import functools
import math

import jax
import jax.numpy as jnp
from jax import lax
from jax.experimental import pallas as pl
from jax.experimental.pallas import tpu as pltpu

F32 = jnp.float32
BF16 = jnp.bfloat16

D_MODEL = 1024
HEAD_DIM = 64
N_HEADS = 8
ATT_WIDTH = N_HEADS * HEAD_DIM
DIL_CONFIGS = ((128, 1), (512, 4), (2048, 16))
ROPE_THETA = 500000.0
ROPE_DIMS = HEAD_DIM // 4
ROPE_HALF = ROPE_DIMS // 2
MLSTM_HEAD_DIM = D_MODEL // N_HEADS
CONV_WIDTH = 4
D_FF_DENSE = 2816
N_EXPERTS = 8
D_FF_EXPERT = 3584
DEPTH = 2
DEEPNORM_ALPHA = (2 * DEPTH) ** 0.25
LN_EPS = 1e-5

LANES = 128
MXU_DIM = 256
BAND = 128
MIB = 1024 * 1024

ROW_TILE = 512
MLSTM_CHUNK = 256
ROUTE_TILE = 1024
COMBINE_TILE = 512
VMEM_LARGE_MIB = 56
VMEM_MID_MIB = 48
VMEM_SMALL_MIB = 32

NEG_INF = float("-inf")


def _cparams(semantics, vmem_mib):
    return pltpu.CompilerParams(dimension_semantics=semantics, vmem_limit_bytes=vmem_mib * MIB)


def _dot(a, b):
    return jnp.dot(a, b, preferred_element_type=F32)


def _dot_nt(a, b):
    return lax.dot_general(a, b, (((1,), (1,)), ((), ())), preferred_element_type=F32)


def _dot_tn(a, b):
    return lax.dot_general(a, b, (((0,), (0,)), ((), ())), preferred_element_type=F32)


def _split3(x):
    hi = x.astype(BF16)
    r = x - hi.astype(F32)
    mid = r.astype(BF16)
    lo = (r - mid.astype(F32)).astype(BF16)
    return hi, mid, lo


def _log_sigmoid(z):
    return -(jnp.maximum(-z, 0.0) + jnp.log1p(jnp.exp(-jnp.abs(z))))


def _layer_norm(z, g, b):
    mu = jnp.mean(z, axis=-1, keepdims=True)
    zc = z - mu
    var = jnp.mean(zc * zc, axis=-1, keepdims=True)
    return zc * lax.rsqrt(var + LN_EPS) * g + b


DILATIONS = tuple(sorted(d for _, d in DIL_CONFIGS))
N_PROJ0 = 6


def _proj0_kernel(x_ref, w_ref, wf_ref, cos_ref, sup_ref, sdn_ref, *refs):
    tm = x_ref.shape[0]
    outs = refs[:N_PROJ0]
    strided = refs[N_PROJ0:-2]
    ft_ref, row_sc = refs[-2:]
    xb = x_ref[...].astype(BF16)
    scale = HEAD_DIM ** -0.5
    for j, o_ref in enumerate(outs):
        full = _dot(xb, w_ref[:, j * ATT_WIDTH:(j + 1) * ATT_WIDTH])
        for c in range(ATT_WIDTH // LANES):
            r = full[:, c * LANES:(c + 1) * LANES]
            if j in (3, 4):
                r = (r * cos_ref[...]
                     + pltpu.roll(r, LANES - ROPE_HALF, axis=1) * sup_ref[...]
                     + pltpu.roll(r, ROPE_HALF, axis=1) * sdn_ref[...])
            if j in (0, 3):
                r = r * scale
            o_ref[:, c * LANES:(c + 1) * LANES] = r.astype(BF16)
            if j >= 3:
                row_sc[0:tm, :] = r
                pieces = [(0, 1)]
                for n, d in enumerate(DILATIONS[1:]):
                    s_ref = strided[3 * n + (j - 3)]
                    step = d // pieces[0][1]
                    rows_prev = tm // pieces[0][1]
                    new_pieces = []
                    vals = []
                    for idx, (res_prev, d_prev) in enumerate(pieces):
                        for b in range(step):
                            src = pl.ds(idx * rows_prev + b, rows_prev // step, stride=step)
                            vals.append(row_sc[src, :])
                            new_pieces.append((res_prev + d_prev * b, d))
                    for idx, ((res, _), val) in enumerate(zip(new_pieces, vals)):
                        lo = res * ATT_WIDTH + c * LANES
                        s_ref[:, lo:lo + LANES] = val.astype(BF16)
                        if d != DILATIONS[-1]:
                            row_sc[idx * (tm // d):(idx + 1) * (tm // d), :] = val
                    pieces = new_pieces
    ft_ref[...] = _dot_nt(wf_ref[...], xb)


def _proj0(x, w_main, wf_t, cos_t, sup_t, sdn_t, tm):
    B, S, _ = x.shape
    n_i = S // tm
    act = jax.ShapeDtypeStruct((B, S, ATT_WIDTH), BF16)
    act_spec = pl.BlockSpec((None, tm, ATT_WIDTH), lambda b, i: (b, i, 0))
    tab_spec = pl.BlockSpec((tm, LANES), lambda b, i: (i, 0))
    out_specs = [act_spec] * N_PROJ0
    out_shape = [act] * N_PROJ0
    for d in DILATIONS[1:]:
        out_specs += [pl.BlockSpec((None, tm // d, d * ATT_WIDTH), lambda b, i: (b, i, 0))] * 3
        out_shape += [jax.ShapeDtypeStruct((B, S // d, d * ATT_WIDTH), BF16)] * 3
    out_specs.append(pl.BlockSpec((None, N_HEADS, tm), lambda b, i: (b, 0, i)))
    out_shape.append(jax.ShapeDtypeStruct((B, N_HEADS, S), F32))
    return pl.pallas_call(
        _proj0_kernel,
        grid=(B, n_i),
        in_specs=[
            pl.BlockSpec((None, tm, D_MODEL), lambda b, i: (b, i, 0)),
            pl.BlockSpec((D_MODEL, N_PROJ0 * ATT_WIDTH), lambda b, i: (0, 0)),
            pl.BlockSpec((N_HEADS, D_MODEL), lambda b, i: (0, 0)),
            tab_spec, tab_spec, tab_spec,
        ],
        out_specs=out_specs,
        out_shape=out_shape,
        scratch_shapes=[pltpu.VMEM((tm, LANES), F32)],
        compiler_params=_cparams(("parallel", "parallel"), VMEM_MID_MIB),
        name="proj0",
    )(x, w_main, wf_t, cos_t, sup_t, sdn_t)


def _fox_cumsum_kernel(f_ref, bias_ref, c_ref):
    S = f_ref.shape[1]
    lf = _log_sigmoid(f_ref[...] + bias_ref[...])
    row = lax.broadcasted_iota(jnp.int32, (S, S), 0)
    col = lax.broadcasted_iota(jnp.int32, (S, S), 1)
    upper = jnp.where(row <= col, 1.0, 0.0).astype(BF16)
    hi, mid, lo = _split3(lf)
    c_ref[...] = _dot(hi, upper) + _dot(mid, upper) + _dot(lo, upper)


def _fox_cumsum(f_t, bias_col):
    R, S = f_t.shape
    return pl.pallas_call(
        _fox_cumsum_kernel,
        grid=(1,),
        in_specs=[pl.BlockSpec((R, S), lambda i: (0, 0)), pl.BlockSpec((R, 1), lambda i: (0, 0))],
        out_specs=pl.BlockSpec((R, S), lambda i: (0, 0)),
        out_shape=jax.ShapeDtypeStruct((R, S), F32),
        compiler_params=_cparams(("arbitrary",), VMEM_MID_MIB),
        name="fox_cumsum",
    )(f_t, bias_col)


def _fox_kernel(q_ref, k_ref, v_ref, c_ref, o_ref, m_sc, l_sc, acc_sc, *, t, n):
    i = pl.program_id(1)

    def step(j, masked):
        kr = slice(j * t, (j + 1) * t)
        if masked:
            key = lax.broadcasted_iota(jnp.int32, (t, t), 0)
            qry = lax.broadcasted_iota(jnp.int32, (t, t), 1)
            keep = key <= qry
        for h in range(N_HEADS):
            sl = slice(h * HEAD_DIM, (h + 1) * HEAD_DIM)
            s = _dot_nt(k_ref[kr, sl], q_ref[:, sl]) - c_ref[kr, h:h + 1]
            if masked:
                s = jnp.where(keep, s, NEG_INF)
            m_blk = jnp.max(s, axis=0, keepdims=True)
            if j == 0:
                p = jnp.exp(s - m_blk)
                l_sc[h:h + 1, :] = jnp.sum(p, axis=0, keepdims=True)
                acc_sc[sl, :] = _dot_tn(v_ref[kr, sl], p.astype(BF16))
                m_sc[h:h + 1, :] = m_blk
                continue
            m_prev = m_sc[h:h + 1, :]
            m_new = jnp.maximum(m_prev, m_blk)
            alpha = jnp.exp(m_prev - m_new)
            p = jnp.exp(s - m_new)
            l_sc[h:h + 1, :] = alpha * l_sc[h:h + 1, :] + jnp.sum(p, axis=0, keepdims=True)
            acc_sc[sl, :] = alpha * acc_sc[sl, :] + _dot_tn(v_ref[kr, sl], p.astype(BF16))
            m_sc[h:h + 1, :] = m_new

    for j in range(n):
        if j < n - 1:
            pl.when(j < i)(functools.partial(step, j, False))
        pl.when(j == i)(functools.partial(step, j, True))

    out_t = jnp.concatenate(
        [acc_sc[h * HEAD_DIM:(h + 1) * HEAD_DIM, :] / l_sc[h:h + 1, :] for h in range(N_HEADS)],
        axis=0)
    o_ref[...] = out_t.T.astype(BF16)


def _fox_attention(q, k, v, c, t):
    B, S, _ = q.shape
    n = S // t
    q_spec = pl.BlockSpec((None, t, ATT_WIDTH), lambda b, i: (b, i, 0))
    kv_spec = pl.BlockSpec((None, S, ATT_WIDTH), lambda b, i: (b, 0, 0))
    return pl.pallas_call(
        functools.partial(_fox_kernel, t=t, n=n),
        grid=(B, n),
        in_specs=[q_spec, kv_spec, kv_spec,
                  pl.BlockSpec((None, S, N_HEADS), lambda b, i: (b, 0, 0))],
        out_specs=q_spec,
        out_shape=jax.ShapeDtypeStruct((B, S, ATT_WIDTH), BF16),
        scratch_shapes=[pltpu.VMEM((N_HEADS, t), F32), pltpu.VMEM((N_HEADS, t), F32),
                        pltpu.VMEM((ATT_WIDTH, t), F32)],
        compiler_params=_cparams(("parallel", "parallel"), VMEM_MID_MIB),
        name="fox_attention",
    )(q, k, v, c)


def _dil_kernel(*refs, d, rg, sub, tqu, has_prev, first, last):
    refs = list(refs)
    q_ref, k_ref, v_ref = refs[:3]
    pos = 3
    if has_prev:
        kp_ref, vp_ref = refs[pos:pos + 2]
        pos += 2
    if not first:
        acc_in, st_in = refs[pos:pos + 2]
        pos += 2
    if last:
        o_ref = refs[pos]
    else:
        acc_out, st_out = refs[pos:pos + 2]

    blk = pl.program_id(1)
    if has_prev:
        key = lax.broadcasted_iota(jnp.int32, (2 * BAND, BAND), 0)
        qry = lax.broadcasted_iota(jnp.int32, (2 * BAND, BAND), 1)
        band = jnp.logical_and(key >= qry, key <= qry + BAND)
        band_edge = jnp.logical_and(band, jnp.logical_or(key >= BAND, blk > 0))
    else:
        key = lax.broadcasted_iota(jnp.int32, (tqu, tqu), 0)
        qry = lax.broadcasted_iota(jnp.int32, (tqu, tqu), 1)
        keep = key <= qry
    pair = LANES // HEAD_DIM

    if first and not has_prev and not last and rg % 2 == 0:
        n2 = 2 * tqu
        for r2 in range(rg // 2):
            res = (2 * r2, 2 * r2 + 1)
            nats = [pl.ds(pl.program_id(2) * rg + rr, tqu, stride=d) for rr in res]
            ms, ls = [], []
            for slab in range(N_HEADS // pair):
                outs = []
                for hh in range(pair):
                    h = slab * pair + hh
                    sls = [slice(rr * ATT_WIDTH + h * HEAD_DIM, rr * ATT_WIDTH + (h + 1) * HEAD_DIM)
                           for rr in res]
                    q = jnp.concatenate([q_ref[:, sl] for sl in sls], axis=0)
                    k_cat = jnp.concatenate([k_ref[:, sl] for sl in sls], axis=0)
                    v_cat = jnp.concatenate([v_ref[:, sl] for sl in sls], axis=0)
                    s_all = _dot_nt(k_cat, q)
                    m_parts, l_parts, acc_parts = [], [], []
                    for j in range(2):
                        blk_j = slice(j * tqu, (j + 1) * tqu)
                        s = jnp.where(keep, s_all[blk_j, blk_j], NEG_INF)
                        m = jnp.max(s, axis=0, keepdims=True)
                        p = jnp.exp(s - m)
                        l_parts.append(jnp.sum(p, axis=0, keepdims=True))
                        m_parts.append(m)
                        acc_parts.append(_dot_tn(v_cat[blk_j, :], p.astype(BF16)))
                    ls.append(jnp.concatenate(l_parts, axis=1))
                    ms.append(jnp.concatenate(m_parts, axis=1))
                    outs.append(jnp.concatenate(acc_parts, axis=1))
                slab_out = jnp.concatenate(outs, axis=0).T
                for j, nat in enumerate(nats):
                    acc_out[slab, nat, :] = slab_out[j * tqu:(j + 1) * tqu, :]
            pad = jnp.zeros((LANES - 2 * N_HEADS, n2), F32)
            st_t = jnp.concatenate(ms + ls + [pad], axis=0).T
            for j, nat in enumerate(nats):
                st_out[nat, :] = st_t[j * tqu:(j + 1) * tqu, :]
        return

    for a in range(sub):
        rows = slice(a * tqu, (a + 1) * tqu)
        for rr in range(rg):
            if d == 1:
                nat = rows
            else:
                nat = pl.ds(a * tqu * d + pl.program_id(2) * rg + rr, tqu, stride=d)
            if not first:
                st_old = st_in[nat, :].T
            ms, ls = [], []
            for slab in range(N_HEADS // pair):
                slab_lanes = slice(rr * ATT_WIDTH + slab * LANES, rr * ATT_WIDTH + (slab + 1) * LANES)
                if not first:
                    acc_old = acc_in[slab, nat, :].T
                outs = []
                for hh in range(pair):
                    h = slab * pair + hh
                    lo = rr * ATT_WIDTH + h * HEAD_DIM
                    sl = slice(lo, lo + HEAD_DIM)
                    q = q_ref[rows, sl]
                    if not has_prev:
                        k_cat, v_cat, mask = k_ref[rows, sl], v_ref[rows, sl], keep
                    elif a == 0:
                        k_cat = jnp.concatenate([kp_ref[:, sl], k_ref[rows, sl]], axis=0)
                        v_cat = jnp.concatenate([vp_ref[:, sl], v_ref[rows, sl]], axis=0)
                    else:
                        krows = slice(a * tqu - BAND, (a + 1) * tqu)
                        k_cat, v_cat = k_ref[krows, sl], v_ref[krows, sl]
                    s_all = _dot_nt(k_cat, q)
                    n_qs = tqu // BAND if has_prev else 1
                    wq_s = tqu // n_qs
                    m_parts, l_parts, acc_parts = [], [], []
                    for qh in range(n_qs):
                        qs = slice(qh * wq_s, (qh + 1) * wq_s)
                        if has_prev:
                            ks = slice(qh * BAND, qh * BAND + 2 * BAND)
                            part_mask = band_edge if (a == 0 and qh == 0) else band
                            s = jnp.where(part_mask, s_all[ks, qs], NEG_INF)
                            v_part = v_cat[ks, :]
                        else:
                            s = jnp.where(mask, s_all, NEG_INF)
                            v_part = v_cat
                        m = jnp.max(s, axis=0, keepdims=True)
                        if not first:
                            m_old = st_old[h:h + 1, qs]
                            l_old = st_old[N_HEADS + h:N_HEADS + h + 1, qs]
                            m_new = jnp.maximum(m, m_old)
                            alpha = jnp.exp(m_old - m_new)
                            m = m_new
                        p = jnp.exp(s - m)
                        l = jnp.sum(p, axis=0, keepdims=True)
                        acc = _dot_tn(v_part, p.astype(BF16))
                        if not first:
                            l = l + alpha * l_old
                            acc = acc + alpha * acc_old[hh * HEAD_DIM:(hh + 1) * HEAD_DIM, qs]
                        if last:
                            acc = acc / l
                        m_parts.append(m)
                        l_parts.append(l)
                        acc_parts.append(acc)
                    outs.append(jnp.concatenate(acc_parts, axis=1))
                    ms.append(jnp.concatenate(m_parts, axis=1))
                    ls.append(jnp.concatenate(l_parts, axis=1))
                slab_out = jnp.concatenate(outs, axis=0).T
                if last:
                    o_ref[rows, slab_lanes] = slab_out.astype(BF16)
                else:
                    acc_out[slab, nat, :] = slab_out
            if not last:
                pad = jnp.zeros((LANES - 2 * N_HEADS, tqu), F32)
                st_out[nat, :] = jnp.concatenate(ms + ls + [pad], axis=0).T


def _dilated_branch(q, k, v, state, dilation, last):
    B, L, _ = q.shape
    d = dilation
    S = L * d
    first = state is None
    assert not (last and d != 1)
    rg = min(4, d)
    tqu = min(2 * BAND, L)
    tqb = min(L, 4 * tqu // rg)
    sub = tqb // tqu
    n_blk = L // tqb
    n_grp = d // rg
    has_prev = L > BAND
    wq = rg * ATT_WIDTH
    n_slab = ATT_WIDTH // LANES

    main = lambda b, i, g: (b, i, g)
    prev = lambda b, i, g: (b, jnp.maximum(i * (tqb // BAND) - 1, 0), g)
    qkv_spec = pl.BlockSpec((None, tqb, wq), main)
    acc_spec = pl.BlockSpec((None, n_slab, tqb * d, LANES), lambda b, i, g: (b, 0, i, 0))
    st_spec = pl.BlockSpec((None, tqb * d, LANES), lambda b, i, g: (b, i, 0))
    in_specs = [qkv_spec] * 3
    args = [q, k, v]
    if has_prev:
        in_specs += [pl.BlockSpec((None, BAND, wq), prev)] * 2
        args += [k, v]
    if not first:
        in_specs += [acc_spec, st_spec]
        args += list(state)
    if last:
        out_specs = pl.BlockSpec((None, tqb, wq), main)
        out_shape = jax.ShapeDtypeStruct((B, S, ATT_WIDTH), BF16)
    else:
        out_specs = [acc_spec, st_spec]
        out_shape = [jax.ShapeDtypeStruct((B, n_slab, S, LANES), F32),
                     jax.ShapeDtypeStruct((B, S, LANES), F32)]
    return pl.pallas_call(
        functools.partial(_dil_kernel, d=d, rg=rg, sub=sub, tqu=tqu, has_prev=has_prev, first=first,
                          last=last),
        grid=(B, n_blk, n_grp),
        in_specs=in_specs,
        out_specs=out_specs,
        out_shape=out_shape,
        compiler_params=_cparams(("parallel", "parallel", "arbitrary"), VMEM_MID_MIB),
        name=f"dilated_d{d}",
    )(*args)


def _dilated_attention(qkv_by_dilation):
    state = None
    order = sorted(DIL_CONFIGS, key=lambda wd: -wd[1])
    for n, (window, d) in enumerate(order):
        assert window // d == BAND
        state = _dilated_branch(*qkv_by_dilation[d], state, d, last=(n == len(order) - 1))
    return state


def _outproj_ln_rows(a_refs, w_refs, x_ref, g_ref, b_ref):
    y = _dot(a_refs[0][...], w_refs[0][...])
    for a_ref, w_ref in zip(a_refs[1:], w_refs[1:]):
        y = y + _dot(a_ref[...], w_ref[...])
    return _layer_norm(DEEPNORM_ALPHA * x_ref[...] + y, g_ref[...], b_ref[...])


def _route_top2(h, wt_ref, sel_ref, gate_ref):
    logits = lax.dot_general(wt_ref[...], h, (((1,), (1,)), ((), ())),
                             preferred_element_type=F32, precision=lax.Precision.HIGHEST)
    idx = lax.broadcasted_iota(jnp.int32, logits.shape, 0)
    m1 = jnp.max(logits, axis=0, keepdims=True)
    i1 = jnp.min(jnp.where(logits == m1, idx, N_EXPERTS), axis=0, keepdims=True)
    pick1 = idx == i1
    rest = jnp.where(pick1, NEG_INF, logits)
    m2 = jnp.max(rest, axis=0, keepdims=True)
    i2 = jnp.min(jnp.where(rest == m2, idx, N_EXPERTS), axis=0, keepdims=True)
    pick2 = idx == i2
    e2 = jnp.exp(m2 - m1)
    w1 = 1.0 / (1.0 + e2)
    w2 = e2 / (1.0 + e2)
    sel_ref[...] = jnp.where(jnp.logical_or(pick1, pick2), 1.0, 0.0)
    gate_ref[...] = jnp.where(pick1, w1, 0.0) + jnp.where(pick2, w2, 0.0)


def _outproj_router_kernel(a_ref, w_ref, x_ref, g_ref, b_ref, wt_ref, o_ref, sel_ref, gate_ref):
    h = _outproj_ln_rows([a_ref], [w_ref], x_ref, g_ref, b_ref)
    o_ref[...] = h
    _route_top2(h, wt_ref, sel_ref, gate_ref)


def _outproj_router(a, w, x, g, b, w_router_t, tm):
    T = x.shape[0]
    row = pl.BlockSpec((tm, D_MODEL), lambda i: (i, 0))
    vec = pl.BlockSpec((1, D_MODEL), lambda i: (0, 0))
    route = pl.BlockSpec((N_EXPERTS, tm), lambda i: (0, i))
    route_shape = jax.ShapeDtypeStruct((N_EXPERTS, T), F32)
    return pl.pallas_call(
        _outproj_router_kernel,
        grid=(T // tm,),
        in_specs=[pl.BlockSpec((tm, a.shape[1]), lambda i: (i, 0)),
                  pl.BlockSpec(w.shape, lambda i: (0, 0)),
                  row, vec, vec,
                  pl.BlockSpec((N_EXPERTS, D_MODEL), lambda i: (0, 0))],
        out_specs=[row, route, route],
        out_shape=[jax.ShapeDtypeStruct((T, D_MODEL), F32), route_shape, route_shape],
        compiler_params=_cparams(("parallel",), VMEM_MID_MIB),
        name="outproj_router",
    )(a, w, x, g, b, w_router_t)


def _ffn_chunks(width):
    chunks, lo = [], 0
    while lo < width:
        size = min(2 * MXU_DIM, width - lo)
        chunks.append((lo, size))
        lo += size
    return chunks


def _dense_ffn_kernel(a0_ref, a1_ref, w0_ref, w1_ref, r_ref, g0_ref, b0_ref,
                      wg_ref, wu_ref, wd_ref, g_ref, b_ref, o_ref):
    x = _outproj_ln_rows([a0_ref, a1_ref], [w0_ref, w1_ref], r_ref, g0_ref, b0_ref)
    xb = x.astype(BF16)
    y = None
    for lo, size in _ffn_chunks(wg_ref.shape[1]):
        gate = _dot(xb, wg_ref[:, lo:lo + size])
        up = _dot(xb, wu_ref[:, lo:lo + size])
        act = (gate * jax.nn.sigmoid(gate) * up).astype(BF16)
        part = _dot(act, wd_ref[lo:lo + size, :])
        y = part if y is None else y + part
    o_ref[...] = _layer_norm(DEEPNORM_ALPHA * x + y, g_ref[...], b_ref[...])


def _mixer_out_dense_ffn(acts, ws, resid, g_mix, b_mix, wg, wu, wd, g, b, tm):
    T = resid.shape[0]
    F = wg.shape[1]
    row = pl.BlockSpec((tm, D_MODEL), lambda i: (i, 0))
    vec = pl.BlockSpec((1, D_MODEL), lambda i: (0, 0))
    once = pl.Buffered(1)
    const = lambda i: (0, 0)
    return pl.pallas_call(
        _dense_ffn_kernel,
        grid=(T // tm,),
        in_specs=[pl.BlockSpec((tm, a.shape[1]), lambda i: (i, 0)) for a in acts]
        + [pl.BlockSpec(w.shape, const, pipeline_mode=once) for w in ws]
        + [row, vec, vec,
           pl.BlockSpec((D_MODEL, F), const, pipeline_mode=once),
           pl.BlockSpec((D_MODEL, F), const, pipeline_mode=once),
           pl.BlockSpec((F, D_MODEL), const, pipeline_mode=once),
           vec, vec],
        out_specs=row,
        out_shape=jax.ShapeDtypeStruct((T, D_MODEL), F32),
        compiler_params=_cparams(("parallel",), VMEM_LARGE_MIB),
        name="mixer_out_dense_ffn",
    )(*acts, *ws, resid, g_mix, b_mix, wg, wu, wd, g, b)


CONV_PAD = 8


def _proj1_kernel(x_ref, wqk_ref, wv_ref, wgt_ref, wconv_ref,
                  q_ref, k_ref, v_ref, gt_ref, buf):
    i = pl.program_id(1)
    tm = x_ref.shape[0]
    xb = x_ref[...].astype(BF16)
    kscale = MLSTM_HEAD_DIM ** -0.5

    @pl.when(i == 0)
    def _():
        buf[...] = jnp.zeros(buf.shape, F32)

    wide = 2 * MXU_DIM
    per_wide = wide // LANES
    for c in range(2 * D_MODEL // LANES):
        if c % per_wide == 0:
            pre = _dot(xb, wqk_ref[:, c * LANES:c * LANES + wide])
        lanes = slice(c * LANES, (c + 1) * LANES)
        cur = pre[:, (c % per_wide) * LANES:(c % per_wide + 1) * LANES]
        ext = jnp.concatenate([buf[:, lanes], cur], axis=0)
        y = cur * wconv_ref[CONV_WIDTH - 1:CONV_WIDTH, lanes]
        for back in range(1, CONV_WIDTH):
            tap = CONV_WIDTH - 1 - back
            y = y + pltpu.roll(ext, back, axis=0)[CONV_PAD:, :] * wconv_ref[tap:tap + 1, lanes]
        y = y * jax.nn.sigmoid(y)
        buf[:, lanes] = cur[tm - CONV_PAD:tm, :]
        if c < D_MODEL // LANES:
            q_ref[:, lanes] = y.astype(BF16)
        else:
            k_ref[:, c * LANES - D_MODEL:(c + 1) * LANES - D_MODEL] = (y * kscale).astype(BF16)
    for c in range(D_MODEL // wide):
        lanes = slice(c * wide, (c + 1) * wide)
        v_ref[:, lanes] = _dot(xb, wv_ref[:, lanes]).astype(BF16)
    gt_ref[...] = _dot(xb, wgt_ref[...])


def _proj1(x, wqk, wv, wgt, wconv, tm):
    B, S, _ = x.shape
    row = lambda b, i: (b, i, 0)
    const = lambda b, i: (0, 0)
    act_spec = pl.BlockSpec((None, tm, D_MODEL), row)
    act = jax.ShapeDtypeStruct((B, S, D_MODEL), BF16)
    return pl.pallas_call(
        _proj1_kernel,
        grid=(B, S // tm),
        in_specs=[act_spec,
                  pl.BlockSpec((D_MODEL, 2 * D_MODEL), const),
                  pl.BlockSpec((D_MODEL, D_MODEL), const),
                  pl.BlockSpec((D_MODEL, 2 * N_HEADS), const),
                  pl.BlockSpec((CONV_WIDTH, 2 * D_MODEL), const)],
        out_specs=[act_spec, act_spec, act_spec,
                   pl.BlockSpec((None, tm, 2 * N_HEADS), row)],
        out_shape=[act, act, act, jax.ShapeDtypeStruct((B, S, 2 * N_HEADS), F32)],
        scratch_shapes=[pltpu.VMEM((CONV_PAD, 2 * D_MODEL), F32)],
        compiler_params=_cparams(("parallel", "arbitrary"), VMEM_LARGE_MIB),
        name="proj1",
    )(x, wqk, wv, wgt, wconv)


def _mlstm_kernel(q_ref, k_ref, v_ref, x_ref, wog_ref, gn_ref, gt_ref, bi_row, bf_row, bi_col,
                  bf_col, ng_ref, o_ref, c_sc, n_sc, m_sc, og_sc, *, L):
    ci = pl.program_id(1)

    @pl.when(ci == 0)
    def _():
        c_sc[...] = jnp.zeros(c_sc.shape, F32)
        n_sc[...] = jnp.zeros(n_sc.shape, F32)
        m_sc[...] = jnp.full(m_sc.shape, NEG_INF, F32)

    xb = x_ref[...].astype(BF16)
    wide = 2 * MXU_DIM
    for c in range(D_MODEL // wide):
        og_sc[:, c * wide:(c + 1) * wide] = _dot(xb, wog_ref[:, c * wide:(c + 1) * wide])

    row = lax.broadcasted_iota(jnp.int32, (L, L), 0)
    col = lax.broadcasted_iota(jnp.int32, (L, L), 1)
    causal = col <= row
    lower = jnp.where(causal, 1.0, 0.0).astype(BF16)
    upper = jnp.where(row <= col, 1.0, 0.0).astype(BF16)

    gn = gn_ref[...]
    gt = gt_ref[...]
    i_col = gn[:, :N_HEADS] + bi_row[...]
    lf_col = _log_sigmoid(gn[:, N_HEADS:] + bf_row[...])
    i_row = gt[:N_HEADS, :] + bi_col[...]
    lf_row = _log_sigmoid(gt[N_HEADS:, :] + bf_col[...])
    b_col = sum(_dot(lower, part) for part in _split3(lf_col))
    b_row = sum(_dot(part, upper) for part in _split3(lf_row))

    keep = row <= col
    n_pad = jnp.zeros((2 * N_HEADS - 3, MLSTM_HEAD_DIM), F32)
    for h in range(N_HEADS):
        lanes = slice(h * MLSTM_HEAD_DIM, (h + 1) * MLSTM_HEAD_DIM)
        q = q_ref[:, lanes]
        k = k_ref[:, lanes]
        v = v_ref[:, lanes]
        bt = b_row[h:h + 1, :]
        key_term = i_col[:, h:h + 1] - b_col[:, h:h + 1]
        m_prev = m_sc[h]
        ct_prev = c_sc[h]
        n_prev = n_sc[h]

        dlog = jnp.where(keep, bt + key_term, NEG_INF)
        inter = bt + m_prev
        m_t = jnp.maximum(inter, jnp.max(dlog, axis=0, keepdims=True))
        s = _dot_nt(k, q) * jnp.exp(dlog - m_t)
        inter_w = jnp.exp(inter - m_t)
        num = _dot_tn(v, s.astype(BF16)) + inter_w * _dot_nt(ct_prev.astype(BF16), q)
        n_parts = jnp.concatenate([p.astype(F32) for p in _split3(n_prev)] + [n_pad], axis=0)
        qn = jnp.sum(_dot_nt(n_parts.astype(BF16), q), axis=0, keepdims=True)
        den = jnp.sum(s, axis=0, keepdims=True) + inter_w * qn
        hh = num / jnp.maximum(jnp.abs(den), jnp.exp(-m_t))

        b_last = bt[:, L - 1:L]
        g = b_last + key_term
        m_new = jnp.maximum(b_last + m_prev, jnp.max(g, axis=0, keepdims=True))
        w = jnp.exp(g - m_new)
        decay = jnp.exp(b_last + m_prev - m_new)
        kw = k.astype(F32) * w
        c_sc[h] = decay * ct_prev + _dot_tn(v, kw.astype(BF16))
        n_sc[h] = decay * n_prev + jnp.sum(kw, axis=0, keepdims=True)
        m_sc[h] = m_new

        mu = jnp.mean(hh, axis=0, keepdims=True)
        hc = hh - mu
        var = jnp.mean(hc * hc, axis=0, keepdims=True)
        hn = (hc * lax.rsqrt(var + LN_EPS)).T * ng_ref[:, lanes]
        o_ref[:, lanes] = (hn * jax.nn.sigmoid(og_sc[:, lanes])).astype(BF16)


def _mlstm(q, k, v, x, wog, gates, gates_t, b_i, b_f, norm_g, L):
    B, S, _ = q.shape
    row = lambda b, c: (b, c, 0)
    const = lambda b, c: (0, 0)
    act_spec = pl.BlockSpec((None, L, D_MODEL), row)
    return pl.pallas_call(
        functools.partial(_mlstm_kernel, L=L),
        grid=(B, S // L),
        in_specs=[act_spec, act_spec, act_spec, act_spec,
                  pl.BlockSpec((D_MODEL, D_MODEL), const),
                  pl.BlockSpec((None, L, 2 * N_HEADS), row),
                  pl.BlockSpec((None, 2 * N_HEADS, L), lambda b, c: (b, 0, c)),
                  pl.BlockSpec((1, N_HEADS), const), pl.BlockSpec((1, N_HEADS), const),
                  pl.BlockSpec((N_HEADS, 1), const), pl.BlockSpec((N_HEADS, 1), const),
                  pl.BlockSpec((1, D_MODEL), const)],
        out_specs=act_spec,
        out_shape=jax.ShapeDtypeStruct((B, S, D_MODEL), BF16),
        scratch_shapes=[pltpu.VMEM((N_HEADS, MLSTM_HEAD_DIM, MLSTM_HEAD_DIM), F32),
                        pltpu.VMEM((N_HEADS, 1, MLSTM_HEAD_DIM), F32),
                        pltpu.VMEM((N_HEADS, 1, 1), F32),
                        pltpu.VMEM((L, D_MODEL), F32)],
        compiler_params=_cparams(("parallel", "arbitrary"), VMEM_MID_MIB),
        name="mlstm",
    )(q, k, v, x, wog, gates, gates_t, b_i[None, :], b_f[None, :], b_i[:, None], b_f[:, None],
      norm_g[None, :])


MOE_TOP_K = 2
MOE_TILE = 512
RANK_BLOCK = 1024


def _rank_kernel(sel_ref, rank_ref, count_ref, upper_sc, carry_sc):
    n = sel_ref.shape[1]

    @pl.when(pl.program_id(0) == 0)
    def _():
        row = lax.broadcasted_iota(jnp.int32, (n, n), 0)
        col = lax.broadcasted_iota(jnp.int32, (n, n), 1)
        upper_sc[...] = jnp.where(row <= col, 1.0, 0.0).astype(BF16)
        carry_sc[...] = jnp.zeros(carry_sc.shape, F32)

    sel = sel_ref[...]
    incl = _dot(sel.astype(BF16), upper_sc[...])
    rank_ref[...] = carry_sc[...] + incl - sel
    carry_sc[...] = carry_sc[...] + incl[:, n - 1:n]
    count_ref[...] = carry_sc[...]


def _rank(sel_t):
    E, T = sel_t.shape
    n = min(RANK_BLOCK, T)
    return pl.pallas_call(
        _rank_kernel,
        grid=(T // n,),
        in_specs=[pl.BlockSpec((E, n), lambda i: (0, i))],
        out_specs=[pl.BlockSpec((E, n), lambda i: (0, i)), pl.BlockSpec((E, 1), lambda i: (0, 0))],
        out_shape=[jax.ShapeDtypeStruct((E, T), F32), jax.ShapeDtypeStruct((E, 1), F32)],
        scratch_shapes=[pltpu.VMEM((n, n), BF16), pltpu.VMEM((E, 1), F32)],
        compiler_params=_cparams(("arbitrary",), VMEM_SMALL_MIB),
        name="moe_rank",
    )(sel_t)


def _slot_kernel(sel_ref, gate_ref, rank_ref, off_ref, pos_ref, w_ref):
    sel = sel_ref[...] > 0.0
    idx = lax.broadcasted_iota(jnp.int32, sel.shape, 0)
    first = jnp.min(jnp.where(sel, idx, N_EXPERTS), axis=0, keepdims=True)
    second = jnp.max(jnp.where(sel, idx, -1), axis=0, keepdims=True)
    slot = off_ref[...] + rank_ref[...]
    gate = gate_ref[...]
    rows_p, rows_w = [], []
    for which in (first, second):
        hit = idx == which
        rows_p.append(jnp.sum(jnp.where(hit, slot, 0.0), axis=0, keepdims=True))
        rows_w.append(jnp.sum(jnp.where(hit, gate, 0.0), axis=0, keepdims=True))
    pos_ref[...] = jnp.concatenate(rows_p, axis=0).astype(jnp.int32)
    w_ref[...] = jnp.concatenate(rows_w, axis=0)


def _slots(sel_t, gate_t, rank_t, offsets_col, tm):
    E, T = sel_t.shape
    spec = pl.BlockSpec((E, tm), lambda i: (0, i))
    out_spec = pl.BlockSpec((MOE_TOP_K, tm), lambda i: (0, i))
    return pl.pallas_call(
        _slot_kernel,
        grid=(T // tm,),
        in_specs=[spec, spec, spec, pl.BlockSpec((E, 1), lambda i: (0, 0))],
        out_specs=[out_spec, out_spec],
        out_shape=[jax.ShapeDtypeStruct((MOE_TOP_K, T), jnp.int32),
                   jax.ShapeDtypeStruct((MOE_TOP_K, T), F32)],
        compiler_params=_cparams(("parallel",), VMEM_SMALL_MIB),
        name="moe_slots",
    )(sel_t, gate_t, rank_t, offsets_col)


def _dispatch_kernel(tail_ref, used_ref, pos_ref, x_ref, xs_hbm, zero_sc, sem, zsem, *, tm):
    i = pl.program_id(0)
    n_tiles = xs_hbm.shape[0] // MOE_TILE

    def fill(tile):
        return pltpu.make_async_copy(zero_sc, xs_hbm.at[pl.ds(tile * MOE_TILE, MOE_TILE)], zsem)

    @pl.when(i == 0)
    def _():
        zero_sc[...] = jnp.zeros(zero_sc.shape, F32)
        for e in range(N_EXPERTS):
            fill(tail_ref[e]).start()
        for e in range(N_EXPERTS):
            fill(tail_ref[e]).wait()
        for j in range(N_EXPERTS):
            @pl.when(n_tiles - 1 - j >= used_ref[0])
            def _():
                c = fill(n_tiles - 1 - j)
                c.start()
                c.wait()

    def issue(t, carry):
        for k in range(MOE_TOP_K):
            pltpu.make_async_copy(x_ref.at[pl.ds(t, 1)], xs_hbm.at[pl.ds(pos_ref[k, t], 1)],
                                  sem).start(priority=k)
        return carry

    lax.fori_loop(0, tm, issue, 0, unroll=8)
    for _ in range(MOE_TOP_K):
        pltpu.make_async_copy(x_ref, xs_hbm.at[pl.ds(0, tm)], sem).wait()


def _dispatch(x, pos_t, tail_tiles, n_used, n_rows, tm):
    T = x.shape[0]
    n_steps = T // tm
    pos3 = pos_t.reshape(MOE_TOP_K, n_steps, tm).transpose(1, 0, 2)
    return pl.pallas_call(
        functools.partial(_dispatch_kernel, tm=tm),
        grid_spec=pltpu.PrefetchScalarGridSpec(
            num_scalar_prefetch=2,
            grid=(n_steps,),
            in_specs=[pl.BlockSpec((None, MOE_TOP_K, tm), lambda i, tail, used: (i, 0, 0),
                                   memory_space=pltpu.SMEM),
                      pl.BlockSpec((tm, D_MODEL), lambda i, tail, used: (i, 0))],
            out_specs=pl.BlockSpec(memory_space=pl.ANY),
            scratch_shapes=[pltpu.VMEM((MOE_TILE, D_MODEL), F32),
                            pltpu.SemaphoreType.DMA, pltpu.SemaphoreType.DMA],
        ),
        out_shape=jax.ShapeDtypeStruct((n_rows, D_MODEL), F32),
        compiler_params=_cparams(("arbitrary",), VMEM_SMALL_MIB),
        name="moe_dispatch",
    )(tail_tiles, n_used, pos3, x)


def _expert_ffn_kernel(te_ref, used_ref, x_ref, wg_ref, wu_ref, wd_ref, o_ref):
    live = pl.program_id(0) < used_ref[0]

    @pl.when(live)
    def _():
        xb = x_ref[...].astype(BF16)
        y = None
        for lo, size in _ffn_chunks(wg_ref.shape[1]):
            gate = _dot(xb, wg_ref[:, lo:lo + size])
            up = _dot(xb, wu_ref[:, lo:lo + size])
            act = (gate * jax.nn.sigmoid(gate) * up).astype(BF16)
            part = _dot(act, wd_ref[lo:lo + size, :])
            y = part if y is None else y + part
        o_ref[...] = y

    @pl.when(jnp.logical_not(live))
    def _():
        o_ref[...] = jnp.zeros(o_ref.shape, F32)


def _expert_ffn(xs, tile_expert, n_used, wg, wu, wd):
    n_rows = xs.shape[0]
    E, _, F = wg.shape
    tm = MOE_TILE
    once = pl.Buffered(1)

    def row(i, te, used):
        return (jnp.minimum(i, used[0] - 1), 0)

    return pl.pallas_call(
        _expert_ffn_kernel,
        grid_spec=pltpu.PrefetchScalarGridSpec(
            num_scalar_prefetch=2,
            grid=(n_rows // tm,),
            in_specs=[pl.BlockSpec((tm, D_MODEL), row),
                      pl.BlockSpec((None, D_MODEL, F), lambda i, te, used: (te[i], 0, 0),
                                   pipeline_mode=once),
                      pl.BlockSpec((None, D_MODEL, F), lambda i, te, used: (te[i], 0, 0),
                                   pipeline_mode=once),
                      pl.BlockSpec((None, F, D_MODEL), lambda i, te, used: (te[i], 0, 0),
                                   pipeline_mode=once)],
            out_specs=pl.BlockSpec((tm, D_MODEL), lambda i, te, used: (i, 0)),
        ),
        out_shape=jax.ShapeDtypeStruct((n_rows, D_MODEL), F32),
        compiler_params=_cparams(("arbitrary",), VMEM_LARGE_MIB),
        name="moe_expert_ffn",
    )(tile_expert, n_used, xs, wg, wu, wd)


def _combine_kernel(pos_ref, w_ref, x_ref, g_ref, b_ref, ys_hbm, o_ref, rows_sc, sem, *, tm):
    def issue(t, carry):
        for k in range(MOE_TOP_K):
            pltpu.make_async_copy(ys_hbm.at[pl.ds(pos_ref[k, t], 1)],
                                  rows_sc.at[k, pl.ds(t, 1)], sem).start(priority=k)
        return carry

    lax.fori_loop(0, tm, issue, 0, unroll=8)
    for k in range(MOE_TOP_K):
        pltpu.make_async_copy(ys_hbm.at[pl.ds(0, tm)], rows_sc.at[k], sem).wait()
    w = w_ref[...]
    y = w[:, 0:1] * rows_sc[0] + w[:, 1:2] * rows_sc[1]
    o_ref[...] = _layer_norm(DEEPNORM_ALPHA * x_ref[...] + y, g_ref[...], b_ref[...])


def _combine_ln(ys, pos_t, w_nat, x, g, b, tm):
    T = x.shape[0]
    n_steps = T // tm
    pos3 = pos_t.reshape(MOE_TOP_K, n_steps, tm).transpose(1, 0, 2)
    row = pl.BlockSpec((tm, D_MODEL), lambda i: (i, 0))
    vec = pl.BlockSpec((1, D_MODEL), lambda i: (0, 0))
    return pl.pallas_call(
        functools.partial(_combine_kernel, tm=tm),
        grid=(n_steps,),
        in_specs=[pl.BlockSpec((None, MOE_TOP_K, tm), lambda i: (i, 0, 0), memory_space=pltpu.SMEM),
                  pl.BlockSpec((tm, MOE_TOP_K), lambda i: (i, 0)),
                  row, vec, vec,
                  pl.BlockSpec(memory_space=pl.ANY)],
        out_specs=row,
        out_shape=jax.ShapeDtypeStruct((T, D_MODEL), F32),
        scratch_shapes=[pltpu.VMEM((MOE_TOP_K, tm, D_MODEL), F32), pltpu.SemaphoreType.DMA],
        compiler_params=_cparams(("arbitrary",), VMEM_SMALL_MIB),
        name="moe_combine_ln",
    )(pos3, w_nat, x, g, b, ys)


def _moe_sparse(x, sel_t, gate_t, wg, wu, wd, g, b):
    T = x.shape[0]
    tile = MOE_TILE
    n_rows = MOE_TOP_K * T + N_EXPERTS * tile
    n_tiles = n_rows // tile
    rank_t, counts = _rank(sel_t)
    tiles_per = jnp.ceil(counts[:, 0] / tile).astype(jnp.int32)
    tile_end = jnp.cumsum(tiles_per)
    tile_start = tile_end - tiles_per
    n_used = tile_end[-1:]
    tile_expert = jnp.minimum(
        jnp.sum(jnp.arange(n_tiles, dtype=jnp.int32)[:, None] >= tile_end[None, :], axis=1),
        N_EXPERTS - 1).astype(jnp.int32)
    tail_tiles = jnp.maximum(tile_end - 1, 0).astype(jnp.int32)
    offsets_col = (tile_start * tile).astype(F32)[:, None]
    pos_t, w_t = _slots(sel_t, gate_t, rank_t, offsets_col, tm=ROUTE_TILE)
    xs = _dispatch(x, pos_t, tail_tiles, n_used, n_rows, tm=ROUTE_TILE)
    ys = _expert_ffn(xs, tile_expert, n_used, wg, wu, wd)
    return _combine_ln(ys, pos_t, w_t.T, x, g, b, tm=COMBINE_TILE)


def _rotary_tables(S):
    pos = jnp.arange(S, dtype=jnp.int32)
    inv = ROPE_THETA ** (-jnp.arange(ROPE_HALF, dtype=F32) / ROPE_HALF)
    ang = pos.astype(F32)[:, None] * inv[None, :]
    cos, sin = jnp.cos(ang), jnp.sin(ang)
    ones = jnp.ones((S, HEAD_DIM - ROPE_DIMS), F32)
    zeros = jnp.zeros((S, HEAD_DIM - ROPE_DIMS), F32)
    z8 = jnp.zeros((S, ROPE_HALF), F32)
    cos_h = jnp.concatenate([cos, cos, ones], axis=1)
    sup_h = jnp.concatenate([-sin, z8, zeros], axis=1)
    sdn_h = jnp.concatenate([z8, sin, zeros], axis=1)
    rep = LANES // HEAD_DIM
    return (jnp.tile(cos_h, (1, rep)), jnp.tile(sup_h, (1, rep)), jnp.tile(sdn_h, (1, rep)))


def _even_layer(h, w_in, b_forget, w_out, ln_mix_g, ln_mix_b, w_gate, w_up, w_down,
                ln_ffn_g, ln_ffn_b):
    B, S, _ = h.shape
    T = B * S
    W = ATT_WIDTH
    w_main = jnp.concatenate([w_in[:, :3 * W], w_in[:, 3 * W + N_HEADS:]], axis=1).astype(BF16)
    wf_t = w_in[:, 3 * W:3 * W + N_HEADS].T.astype(BF16)
    cos_t, sup_t, sdn_t = _rotary_tables(S)
    outs = _proj0(h, w_main, wf_t, cos_t, sup_t, sdn_t, tm=min(ROW_TILE, S))
    qa, ka, va = outs[:3]
    qkv_by_dilation = {d: tuple(outs[3 + 3 * n:6 + 3 * n]) for n, d in enumerate(DILATIONS)}
    f_t = outs[-1]

    bias_col = jnp.tile(b_forget, B)[:, None]
    c = _fox_cumsum(f_t.reshape(B * N_HEADS, S), bias_col).reshape(B, N_HEADS, S)
    o_fox = _fox_attention(qa, ka, va, jnp.swapaxes(c, 1, 2), t=min(ROW_TILE, S))
    o_dil = _dilated_attention(qkv_by_dilation)

    w_out_b = w_out.astype(BF16)
    tm = ROW_TILE
    h2 = _mixer_out_dense_ffn(
        [o_fox.reshape(T, W), o_dil.reshape(T, W)], [w_out_b[:W], w_out_b[W:]],
        h.reshape(T, D_MODEL), ln_mix_g[None, :], ln_mix_b[None, :],
        w_gate.astype(BF16), w_up.astype(BF16), w_down.astype(BF16),
        ln_ffn_g[None, :], ln_ffn_b[None, :], tm)
    return h2.reshape(B, S, D_MODEL)


def _odd_layer(h, w_in, b_igate, b_fgate, w_conv, norm_g, w_out, ln_mix_g, ln_mix_b, w_router,
               w_gate, w_up, w_down, ln_ffn_g, ln_ffn_b):
    B, S, _ = h.shape
    T = B * S
    D = D_MODEL
    wqk = w_in[:, :2 * D].astype(BF16)
    wv = w_in[:, 2 * D:3 * D].astype(BF16)
    wgt = w_in[:, 3 * D:3 * D + 2 * N_HEADS].astype(BF16)
    wog = w_in[:, 3 * D + 2 * N_HEADS:].astype(BF16)
    q, k, v, gates = _proj1(h, wqk, wv, wgt, w_conv, tm=min(ROW_TILE, S))
    gates_t = jnp.swapaxes(gates, 1, 2)
    hm = _mlstm(q, k, v, h, wog, gates, gates_t, b_igate, b_fgate, norm_g,
                L=min(MLSTM_CHUNK, S))

    tm = ROW_TILE
    h1, sel_t, gate_t = _outproj_router(hm.reshape(T, D), w_out.astype(BF16), h.reshape(T, D),
                                        ln_mix_g[None, :], ln_mix_b[None, :], w_router.T, tm)
    h2 = _moe_sparse(h1, sel_t, gate_t, w_gate.astype(BF16), w_up.astype(BF16),
                     w_down.astype(BF16), ln_ffn_g[None, :], ln_ffn_b[None, :])
    return h2.reshape(B, S, D)


def kernel(x, w_in_e, b_forget_e, w_out_e, ln_mix_g_e, ln_mix_b_e, ffn_w_gate_e, ffn_w_up_e,
           ffn_w_down_e, ln_ffn_g_e, ln_ffn_b_e, w_in_o, b_igate_o, b_fgate_o, w_conv_o,
           mlstm_norm_g_o, w_out_o, ln_mix_g_o, ln_mix_b_o, w_router_o, moe_w_gate_o,
           moe_w_up_o, moe_w_down_o, ln_ffn_g_o, ln_ffn_b_o):
    h = x
    for layer in range(DEPTH):
        i = layer // 2
        if layer % 2 == 0:
            h = _even_layer(h, w_in_e[i], b_forget_e[i], w_out_e[i], ln_mix_g_e[i], ln_mix_b_e[i],
                            ffn_w_gate_e[i], ffn_w_up_e[i], ffn_w_down_e[i], ln_ffn_g_e[i],
                            ln_ffn_b_e[i])
        else:
            h = _odd_layer(h, w_in_o[i], b_igate_o[i], b_fgate_o[i], w_conv_o[i],
                           mlstm_norm_g_o[i], w_out_o[i], ln_mix_g_o[i], ln_mix_b_o[i],
                           w_router_o[i], moe_w_gate_o[i], moe_w_up_o[i], moe_w_down_o[i],
                           ln_ffn_g_o[i], ln_ffn_b_o[i])
    return h
```

```python
import functools

import jax
import jax.numpy as jnp
from jax import lax
from jax.experimental import pallas as pl
from jax.experimental.pallas import tpu as pltpu

F32 = jnp.float32
BF16 = jnp.bfloat16

D_MODEL = 1024
HEAD_DIM = 64
N_HEADS = 8
ATT_WIDTH = N_HEADS * HEAD_DIM
DIL_CONFIGS = ((128, 1), (512, 4), (2048, 16))
ROPE_THETA = 500000.0
ROPE_DIMS = HEAD_DIM // 4
ROPE_HALF = ROPE_DIMS // 2
MLSTM_HEAD_DIM = D_MODEL // N_HEADS
CONV_WIDTH = 4
N_EXPERTS = 8
DEPTH = 2
DEEPNORM_ALPHA = (2 * DEPTH) ** 0.25
LN_EPS = 1e-5

LANES = 128
MXU_DIM = 256
BAND = 128
MIB = 1024 * 1024

ROW_TILE = 512
MLSTM_CHUNK = 256
ROUTE_TILE = 1024
COMBINE_TILE = 512
VMEM_LARGE_MIB = 56
VMEM_MID_MIB = 48
VMEM_SMALL_MIB = 32

NEG_INF = float("-inf")


def _cparams(semantics, vmem_mib):
    return pltpu.CompilerParams(dimension_semantics=semantics, vmem_limit_bytes=vmem_mib * MIB)


def _dot(a, b):
    return jnp.dot(a, b, preferred_element_type=F32)


def _dot_nt(a, b):
    return lax.dot_general(a, b, (((1,), (1,)), ((), ())), preferred_element_type=F32)


def _dot_tn(a, b):
    return lax.dot_general(a, b, (((0,), (0,)), ((), ())), preferred_element_type=F32)


def _split3(x):
    hi = x.astype(BF16)
    r = x - hi.astype(F32)
    mid = r.astype(BF16)
    lo = (r - mid.astype(F32)).astype(BF16)
    return hi, mid, lo


def _log_sigmoid(z):
    return -(jnp.maximum(-z, 0.0) + jnp.log1p(jnp.exp(-jnp.abs(z))))


def _layer_norm(z, g, b):
    mu = jnp.mean(z, axis=-1, keepdims=True)
    zc = z - mu
    var = jnp.mean(zc * zc, axis=-1, keepdims=True)
    return zc * lax.rsqrt(var + LN_EPS) * g + b


DILATIONS = tuple(sorted(d for _, d in DIL_CONFIGS))
N_PROJ0 = 6


def _proj0_kernel(x_ref, w_ref, wf_ref, cos_ref, sup_ref, sdn_ref, *refs):
    tm = x_ref.shape[0]
    outs = refs[:N_PROJ0]
    strided = refs[N_PROJ0:-2]
    ft_ref, row_sc = refs[-2:]
    xb = x_ref[...].astype(BF16)
    scale = HEAD_DIM ** -0.5
    for j, o_ref in enumerate(outs):
        full = _dot(xb, w_ref[:, j * ATT_WIDTH:(j + 1) * ATT_WIDTH])
        for c in range(ATT_WIDTH // LANES):
            r = full[:, c * LANES:(c + 1) * LANES]
            if j in (3, 4):
                r = (r * cos_ref[...]
                     + pltpu.roll(r, LANES - ROPE_HALF, axis=1) * sup_ref[...]
                     + pltpu.roll(r, ROPE_HALF, axis=1) * sdn_ref[...])
            if j in (0, 3):
                r = r * scale
            o_ref[:, c * LANES:(c + 1) * LANES] = r.astype(BF16)
            if j >= 3:
                row_sc[0:tm, :] = r
                pieces = [(0, 1)]
                for n, d in enumerate(DILATIONS[1:]):
                    s_ref = strided[3 * n + (j - 3)]
                    step = d // pieces[0][1]
                    rows_prev = tm // pieces[0][1]
                    new_pieces = []
                    vals = []
                    for idx, (res_prev, d_prev) in enumerate(pieces):
                        for b in range(step):
                            src = pl.ds(idx * rows_prev + b, rows_prev // step, stride=step)
                            vals.append(row_sc[src, :])
                            new_pieces.append((res_prev + d_prev * b, d))
                    for idx, ((res, _), val) in enumerate(zip(new_pieces, vals)):
                        lo = res * ATT_WIDTH + c * LANES
                        s_ref[:, lo:lo + LANES] = val.astype(BF16)
                        if d != DILATIONS[-1]:
                            row_sc[idx * (tm // d):(idx + 1) * (tm // d), :] = val
                    pieces = new_pieces
    ft_ref[...] = _dot_nt(wf_ref[...], xb)


def _proj0(x, w_main, wf_t, cos_t, sup_t, sdn_t, tm):
    B, S, _ = x.shape
    n_i = S // tm
    act = jax.ShapeDtypeStruct((B, S, ATT_WIDTH), BF16)
    act_spec = pl.BlockSpec((None, tm, ATT_WIDTH), lambda b, i: (b, i, 0))
    tab_spec = pl.BlockSpec((tm, LANES), lambda b, i: (i, 0))
    out_specs = [act_spec] * N_PROJ0
    out_shape = [act] * N_PROJ0
    for d in DILATIONS[1:]:
        out_specs += [pl.BlockSpec((None, tm // d, d * ATT_WIDTH), lambda b, i: (b, i, 0))] * 3
        out_shape += [jax.ShapeDtypeStruct((B, S // d, d * ATT_WIDTH), BF16)] * 3
    out_specs.append(pl.BlockSpec((None, N_HEADS, tm), lambda b, i: (b, 0, i)))
    out_shape.append(jax.ShapeDtypeStruct((B, N_HEADS, S), F32))
    return pl.pallas_call(
        _proj0_kernel,
        grid=(B, n_i),
        in_specs=[
            pl.BlockSpec((None, tm, D_MODEL), lambda b, i: (b, i, 0)),
            pl.BlockSpec((D_MODEL, N_PROJ0 * ATT_WIDTH), lambda b, i: (0, 0)),
            pl.BlockSpec((N_HEADS, D_MODEL), lambda b, i: (0, 0)),
            tab_spec, tab_spec, tab_spec,
        ],
        out_specs=out_specs,
        out_shape=out_shape,
        scratch_shapes=[pltpu.VMEM((tm, LANES), F32)],
        compiler_params=_cparams(("parallel", "parallel"), VMEM_MID_MIB),
        name="proj0",
    )(x, w_main, wf_t, cos_t, sup_t, sdn_t)


def _fox_cumsum_kernel(f_ref, bias_ref, c_ref):
    S = f_ref.shape[1]
    lf = _log_sigmoid(f_ref[...] + bias_ref[...])
    row = lax.broadcasted_iota(jnp.int32, (S, S), 0)
    col = lax.broadcasted_iota(jnp.int32, (S, S), 1)
    upper = jnp.where(row <= col, 1.0, 0.0).astype(BF16)
    hi, mid, lo = _split3(lf)
    c_ref[...] = _dot(hi, upper) + _dot(mid, upper) + _dot(lo, upper)


def _fox_cumsum(f_t, bias_col):
    R, S = f_t.shape
    return pl.pallas_call(
        _fox_cumsum_kernel,
        grid=(1,),
        in_specs=[pl.BlockSpec((R, S), lambda i: (0, 0)), pl.BlockSpec((R, 1), lambda i: (0, 0))],
        out_specs=pl.BlockSpec((R, S), lambda i: (0, 0)),
        out_shape=jax.ShapeDtypeStruct((R, S), F32),
        compiler_params=_cparams(("arbitrary",), VMEM_MID_MIB),
        name="fox_cumsum",
    )(f_t, bias_col)


def _fox_kernel(q_ref, k_ref, v_ref, c_ref, o_ref, m_sc, l_sc, acc_sc, *, t, n):
    i = pl.program_id(1)

    def step(j, masked):
        kr = slice(j * t, (j + 1) * t)
        if masked:
            key = lax.broadcasted_iota(jnp.int32, (t, t), 0)
            qry = lax.broadcasted_iota(jnp.int32, (t, t), 1)
            keep = key <= qry
        for h in range(N_HEADS):
            sl = slice(h * HEAD_DIM, (h + 1) * HEAD_DIM)
            s = _dot_nt(k_ref[kr, sl], q_ref[:, sl]) - c_ref[kr, h:h + 1]
            if masked:
                s = jnp.where(keep, s, NEG_INF)
            m_blk = jnp.max(s, axis=0, keepdims=True)
            if j == 0:
                p = jnp.exp(s - m_blk)
                l_sc[h:h + 1, :] = jnp.sum(p, axis=0, keepdims=True)
                acc_sc[sl, :] = _dot_tn(v_ref[kr, sl], p.astype(BF16))
                m_sc[h:h + 1, :] = m_blk
                continue
            m_prev = m_sc[h:h + 1, :]
            m_new = jnp.maximum(m_prev, m_blk)
            alpha = jnp.exp(m_prev - m_new)
            p = jnp.exp(s - m_new)
            l_sc[h:h + 1, :] = alpha * l_sc[h:h + 1, :] + jnp.sum(p, axis=0, keepdims=True)
            acc_sc[sl, :] = alpha * acc_sc[sl, :] + _dot_tn(v_ref[kr, sl], p.astype(BF16))
            m_sc[h:h + 1, :] = m_new

    for j in range(n):
        if j < n - 1:
            pl.when(j < i)(functools.partial(step, j, False))
        pl.when(j == i)(functools.partial(step, j, True))

    out_t = jnp.concatenate(
        [acc_sc[h * HEAD_DIM:(h + 1) * HEAD_DIM, :] / l_sc[h:h + 1, :] for h in range(N_HEADS)],
        axis=0)
    o_ref[...] = out_t.T.astype(BF16)


def _fox_attention(q, k, v, c, t):
    B, S, _ = q.shape
    n = S // t
    q_spec = pl.BlockSpec((None, t, ATT_WIDTH), lambda b, i: (b, i, 0))
    kv_spec = pl.BlockSpec((None, S, ATT_WIDTH), lambda b, i: (b, 0, 0))
    return pl.pallas_call(
        functools.partial(_fox_kernel, t=t, n=n),
        grid=(B, n),
        in_specs=[q_spec, kv_spec, kv_spec,
                  pl.BlockSpec((None, S, N_HEADS), lambda b, i: (b, 0, 0))],
        out_specs=q_spec,
        out_shape=jax.ShapeDtypeStruct((B, S, ATT_WIDTH), BF16),
        scratch_shapes=[pltpu.VMEM((N_HEADS, t), F32), pltpu.VMEM((N_HEADS, t), F32),
                        pltpu.VMEM((ATT_WIDTH, t), F32)],
        compiler_params=_cparams(("parallel", "parallel"), VMEM_MID_MIB),
        name="fox_attention",
    )(q, k, v, c)


def _dil_kernel(*refs, d, rg, sub, tqu, has_prev, first, last):
    refs = list(refs)
    q_ref, k_ref, v_ref = refs[:3]
    pos = 3
    if has_prev:
        kp_ref, vp_ref = refs[pos:pos + 2]
        pos += 2
    if not first:
        acc_in, st_in = refs[pos:pos + 2]
        pos += 2
    if last:
        o_ref = refs[pos]
    else:
        acc_out, st_out = refs[pos:pos + 2]

    blk = pl.program_id(1)
    if has_prev:
        key = lax.broadcasted_iota(jnp.int32, (2 * BAND, BAND), 0)
        qry = lax.broadcasted_iota(jnp.int32, (2 * BAND, BAND), 1)
        band = jnp.logical_and(key >= qry, key <= qry + BAND)
        band_edge = jnp.logical_and(band, jnp.logical_or(key >= BAND, blk > 0))
    else:
        key = lax.broadcasted_iota(jnp.int32, (tqu, tqu), 0)
        qry = lax.broadcasted_iota(jnp.int32, (tqu, tqu), 1)
        keep = key <= qry
    pair = LANES // HEAD_DIM

    if first and not has_prev and not last and rg % 2 == 0:
        n2 = 2 * tqu
        for r2 in range(rg // 2):
            res = (2 * r2, 2 * r2 + 1)
            nats = [pl.ds(pl.program_id(2) * rg + rr, tqu, stride=d) for rr in res]
            ms, ls = [], []
            for slab in range(N_HEADS // pair):
                outs = []
                for hh in range(pair):
                    h = slab * pair + hh
                    sls = [slice(rr * ATT_WIDTH + h * HEAD_DIM, rr * ATT_WIDTH + (h + 1) * HEAD_DIM)
                           for rr in res]
                    q = jnp.concatenate([q_ref[:, sl] for sl in sls], axis=0)
                    k_cat = jnp.concatenate([k_ref[:, sl] for sl in sls], axis=0)
                    v_cat = jnp.concatenate([v_ref[:, sl] for sl in sls], axis=0)
                    s_all = _dot_nt(k_cat, q)
                    m_parts, l_parts, acc_parts = [], [], []
                    for j in range(2):
                        blk_j = slice(j * tqu, (j + 1) * tqu)
                        s = jnp.where(keep, s_all[blk_j, blk_j], NEG_INF)
                        m = jnp.max(s, axis=0, keepdims=True)
                        p = jnp.exp(s - m)
                        l_parts.append(jnp.sum(p, axis=0, keepdims=True))
                        m_parts.append(m)
                        acc_parts.append(_dot_tn(v_cat[blk_j, :], p.astype(BF16)))
                    ls.append(jnp.concatenate(l_parts, axis=1))
                    ms.append(jnp.concatenate(m_parts, axis=1))
                    outs.append(jnp.concatenate(acc_parts, axis=1))
                slab_out = jnp.concatenate(outs, axis=0).T
                for j, nat in enumerate(nats):
                    acc_out[slab, nat, :] = slab_out[j * tqu:(j + 1) * tqu, :]
            pad = jnp.zeros((LANES - 2 * N_HEADS, n2), F32)
            st_t = jnp.concatenate(ms + ls + [pad], axis=0).T
            for j, nat in enumerate(nats):
                st_out[nat, :] = st_t[j * tqu:(j + 1) * tqu, :]
        return

    for a in range(sub):
        rows = slice(a * tqu, (a + 1) * tqu)
        for rr in range(rg):
            if d == 1:
                nat = rows
            else:
                nat = pl.ds(a * tqu * d + pl.program_id(2) * rg + rr, tqu, stride=d)
            if not first:
                st_old = st_in[nat, :].T
            ms, ls = [], []
            for slab in range(N_HEADS // pair):
                slab_lanes = slice(rr * ATT_WIDTH + slab * LANES, rr * ATT_WIDTH + (slab + 1) * LANES)
                if not first:
                    acc_old = acc_in[slab, nat, :].T
                outs = []
                for hh in range(pair):
                    h = slab * pair + hh
                    lo = rr * ATT_WIDTH + h * HEAD_DIM
                    sl = slice(lo, lo + HEAD_DIM)
                    q = q_ref[rows, sl]
                    if not has_prev:
                        k_cat, v_cat, mask = k_ref[rows, sl], v_ref[rows, sl], keep
                    elif a == 0:
                        k_cat = jnp.concatenate([kp_ref[:, sl], k_ref[rows, sl]], axis=0)
                        v_cat = jnp.concatenate([vp_ref[:, sl], v_ref[rows, sl]], axis=0)
                    else:
                        krows = slice(a * tqu - BAND, (a + 1) * tqu)
                        k_cat, v_cat = k_ref[krows, sl], v_ref[krows, sl]
                    s_all = _dot_nt(k_cat, q)
                    n_qs = tqu // BAND if has_prev else 1
                    wq_s = tqu // n_qs
                    m_parts, l_parts, acc_parts = [], [], []
                    for qh in range(n_qs):
                        qs = slice(qh * wq_s, (qh + 1) * wq_s)
                        if has_prev:
                            ks = slice(qh * BAND, qh * BAND + 2 * BAND)
                            part_mask = band_edge if (a == 0 and qh == 0) else band
                            s = jnp.where(part_mask, s_all[ks, qs], NEG_INF)
                            v_part = v_cat[ks, :]
                        else:
                            s = jnp.where(mask, s_all, NEG_INF)
                            v_part = v_cat
                        m = jnp.max(s, axis=0, keepdims=True)
                        if not first:
                            m_old = st_old[h:h + 1, qs]
                            l_old = st_old[N_HEADS + h:N_HEADS + h + 1, qs]
                            m_new = jnp.maximum(m, m_old)
                            alpha = jnp.exp(m_old - m_new)
                            m = m_new
                        p = jnp.exp(s - m)
                        l = jnp.sum(p, axis=0, keepdims=True)
                        acc = _dot_tn(v_part, p.astype(BF16))
                        if not first:
                            l = l + alpha * l_old
                            acc = acc + alpha * acc_old[hh * HEAD_DIM:(hh + 1) * HEAD_DIM, qs]
                        if last:
                            acc = acc / l
                        m_parts.append(m)
                        l_parts.append(l)
                        acc_parts.append(acc)
                    outs.append(jnp.concatenate(acc_parts, axis=1))
                    ms.append(jnp.concatenate(m_parts, axis=1))
                    ls.append(jnp.concatenate(l_parts, axis=1))
                slab_out = jnp.concatenate(outs, axis=0).T
                if last:
                    o_ref[rows, slab_lanes] = slab_out.astype(BF16)
                else:
                    acc_out[slab, nat, :] = slab_out
            if not last:
                pad = jnp.zeros((LANES - 2 * N_HEADS, tqu), F32)
                st_out[nat, :] = jnp.concatenate(ms + ls + [pad], axis=0).T


def _dilated_branch(q, k, v, state, dilation, last):
    B, L, _ = q.shape
    d = dilation
    S = L * d
    first = state is None
    assert not (last and d != 1)
    rg = min(4, d)
    tqu = min(2 * BAND, L)
    tqb = min(L, 4 * tqu // rg)
    sub = tqb // tqu
    n_blk = L // tqb
    n_grp = d // rg
    has_prev = L > BAND
    wq = rg * ATT_WIDTH
    n_slab = ATT_WIDTH // LANES

    main = lambda b, i, g: (b, i, g)
    prev = lambda b, i, g: (b, jnp.maximum(i * (tqb // BAND) - 1, 0), g)
    qkv_spec = pl.BlockSpec((None, tqb, wq), main)
    acc_spec = pl.BlockSpec((None, n_slab, tqb * d, LANES), lambda b, i, g: (b, 0, i, 0))
    st_spec = pl.BlockSpec((None, tqb * d, LANES), lambda b, i, g: (b, i, 0))
    in_specs = [qkv_spec] * 3
    args = [q, k, v]
    if has_prev:
        in_specs += [pl.BlockSpec((None, BAND, wq), prev)] * 2
        args += [k, v]
    if not first:
        in_specs += [acc_spec, st_spec]
        args += list(state)
    if last:
        out_specs = pl.BlockSpec((None, tqb, wq), main)
        out_shape = jax.ShapeDtypeStruct((B, S, ATT_WIDTH), BF16)
    else:
        out_specs = [acc_spec, st_spec]
        out_shape = [jax.ShapeDtypeStruct((B, n_slab, S, LANES), F32),
                     jax.ShapeDtypeStruct((B, S, LANES), F32)]
    return pl.pallas_call(
        functools.partial(_dil_kernel, d=d, rg=rg, sub=sub, tqu=tqu, has_prev=has_prev, first=first,
                          last=last),
        grid=(B, n_blk, n_grp),
        in_specs=in_specs,
        out_specs=out_specs,
        out_shape=out_shape,
        compiler_params=_cparams(("parallel", "parallel", "arbitrary"), VMEM_MID_MIB),
        name=f"dilated_d{d}",
    )(*args)


def _dilated_attention(qkv_by_dilation):
    state = None
    order = sorted(DIL_CONFIGS, key=lambda wd: -wd[1])
    for n, (window, d) in enumerate(order):
        assert window // d == BAND
        state = _dilated_branch(*qkv_by_dilation[d], state, d, last=(n == len(order) - 1))
    return state


def _outproj_ln_rows(a_refs, w_refs, x_ref, g_ref, b_ref):
    y = _dot(a_refs[0][...], w_refs[0][...])
    for a_ref, w_ref in zip(a_refs[1:], w_refs[1:]):
        y = y + _dot(a_ref[...], w_ref[...])
    return _layer_norm(DEEPNORM_ALPHA * x_ref[...] + y, g_ref[...], b_ref[...])


def _route_top2(h, wt_ref, sel_ref, gate_ref):
    w_hi, w_mid, _ = _split3(wt_ref[...])
    h_hi, h_mid, _ = _split3(h)
    logits = _dot_nt(w_hi, h_hi) + (_dot_nt(w_hi, h_mid) + _dot_nt(w_mid, h_hi))
    idx = lax.broadcasted_iota(jnp.int32, logits.shape, 0)
    m1 = jnp.max(logits, axis=0, keepdims=True)
    i1 = jnp.min(jnp.where(logits == m1, idx, N_EXPERTS), axis=0, keepdims=True)
    pick1 = idx == i1
    rest = jnp.where(pick1, NEG_INF, logits)
    m2 = jnp.max(rest, axis=0, keepdims=True)
    i2 = jnp.min(jnp.where(rest == m2, idx, N_EXPERTS), axis=0, keepdims=True)
    pick2 = idx == i2
    e2 = jnp.exp(m2 - m1)
    w1 = 1.0 / (1.0 + e2)
    w2 = e2 / (1.0 + e2)
    sel_ref[...] = jnp.where(jnp.logical_or(pick1, pick2), 1.0, 0.0)
    gate_ref[...] = jnp.where(pick1, w1, 0.0) + jnp.where(pick2, w2, 0.0)


def _outproj_router_kernel(a_ref, w_ref, x_ref, g_ref, b_ref, wt_ref, o_ref, sel_ref, gate_ref):
    h = _outproj_ln_rows([a_ref], [w_ref], x_ref, g_ref, b_ref)
    o_ref[...] = h
    _route_top2(h, wt_ref, sel_ref, gate_ref)


def _outproj_router(a, w, x, g, b, w_router_t, tm):
    T = x.shape[0]
    row = pl.BlockSpec((tm, D_MODEL), lambda i: (i, 0))
    vec = pl.BlockSpec((1, D_MODEL), lambda i: (0, 0))
    route = pl.BlockSpec((N_EXPERTS, tm), lambda i: (0, i))
    route_shape = jax.ShapeDtypeStruct((N_EXPERTS, T), F32)
    return pl.pallas_call(
        _outproj_router_kernel,
        grid=(T // tm,),
        in_specs=[pl.BlockSpec((tm, a.shape[1]), lambda i: (i, 0)),
                  pl.BlockSpec(w.shape, lambda i: (0, 0)),
                  row, vec, vec,
                  pl.BlockSpec((N_EXPERTS, D_MODEL), lambda i: (0, 0))],
        out_specs=[row, route, route],
        out_shape=[jax.ShapeDtypeStruct((T, D_MODEL), F32), route_shape, route_shape],
        compiler_params=_cparams(("parallel",), VMEM_MID_MIB),
        name="outproj_router",
    )(a, w, x, g, b, w_router_t)


def _ffn_chunks(width):
    chunks, lo = [], 0
    while lo < width:
        size = min(2 * MXU_DIM, width - lo)
        chunks.append((lo, size))
        lo += size
    return chunks


def _dense_ffn_kernel(a0_ref, a1_ref, w0_ref, w1_ref, r_ref, g0_ref, b0_ref,
                      wg_ref, wu_ref, wd_ref, g_ref, b_ref, o_ref):
    x = _outproj_ln_rows([a0_ref, a1_ref], [w0_ref, w1_ref], r_ref, g0_ref, b0_ref)
    xb = x.astype(BF16)
    y = None
    for lo, size in _ffn_chunks(wg_ref.shape[1]):
        gate = _dot(xb, wg_ref[:, lo:lo + size])
        up = _dot(xb, wu_ref[:, lo:lo + size])
        act = (gate * jax.nn.sigmoid(gate) * up).astype(BF16)
        part = _dot(act, wd_ref[lo:lo + size, :])
        y = part if y is None else y + part
    o_ref[...] = _layer_norm(DEEPNORM_ALPHA * x + y, g_ref[...], b_ref[...])


def _mixer_out_dense_ffn(acts, ws, resid, g_mix, b_mix, wg, wu, wd, g, b, tm):
    T = resid.shape[0]
    F = wg.shape[1]
    row = pl.BlockSpec((tm, D_MODEL), lambda i: (i, 0))
    vec = pl.BlockSpec((1, D_MODEL), lambda i: (0, 0))
    once = pl.Buffered(1)
    const = lambda i: (0, 0)
    return pl.pallas_call(
        _dense_ffn_kernel,
        grid=(T // tm,),
        in_specs=[pl.BlockSpec((tm, a.shape[1]), lambda i: (i, 0)) for a in acts]
        + [pl.BlockSpec(w.shape, const, pipeline_mode=once) for w in ws]
        + [row, vec, vec,
           pl.BlockSpec((D_MODEL, F), const, pipeline_mode=once),
           pl.BlockSpec((D_MODEL, F), const, pipeline_mode=once),
           pl.BlockSpec((F, D_MODEL), const, pipeline_mode=once),
           vec, vec],
        out_specs=row,
        out_shape=jax.ShapeDtypeStruct((T, D_MODEL), F32),
        compiler_params=_cparams(("parallel",), VMEM_LARGE_MIB),
        name="mixer_out_dense_ffn",
    )(*acts, *ws, resid, g_mix, b_mix, wg, wu, wd, g, b)


CONV_PAD = 8


def _proj1_kernel(x_ref, wqk_ref, wv_ref, wgt_ref, wconv_ref,
                  q_ref, k_ref, v_ref, gt_ref, buf):
    i = pl.program_id(1)
    tm = x_ref.shape[0]
    xb = x_ref[...].astype(BF16)
    kscale = MLSTM_HEAD_DIM ** -0.5

    @pl.when(i == 0)
    def _():
        buf[...] = jnp.zeros(buf.shape, F32)

    wide = 2 * MXU_DIM
    per_wide = wide // LANES
    for c in range(2 * D_MODEL // LANES):
        if c % per_wide == 0:
            pre = _dot(xb, wqk_ref[:, c * LANES:c * LANES + wide])
        lanes = slice(c * LANES, (c + 1) * LANES)
        cur = pre[:, (c % per_wide) * LANES:(c % per_wide + 1) * LANES]
        ext = jnp.concatenate([buf[:, lanes], cur], axis=0)
        y = cur * wconv_ref[CONV_WIDTH - 1:CONV_WIDTH, lanes]
        for back in range(1, CONV_WIDTH):
            tap = CONV_WIDTH - 1 - back
            y = y + pltpu.roll(ext, back, axis=0)[CONV_PAD:, :] * wconv_ref[tap:tap + 1, lanes]
        y = y * jax.nn.sigmoid(y)
        buf[:, lanes] = cur[tm - CONV_PAD:tm, :]
        if c < D_MODEL // LANES:
            q_ref[:, lanes] = y.astype(BF16)
        else:
            k_ref[:, c * LANES - D_MODEL:(c + 1) * LANES - D_MODEL] = (y * kscale).astype(BF16)
    for c in range(D_MODEL // wide):
        lanes = slice(c * wide, (c + 1) * wide)
        v_ref[:, lanes] = _dot(xb, wv_ref[:, lanes]).astype(BF16)
    gt_ref[...] = _dot(xb, wgt_ref[...])


def _proj1(x, wqk, wv, wgt, wconv, tm):
    B, S, _ = x.shape
    row = lambda b, i: (b, i, 0)
    const = lambda b, i: (0, 0)
    act_spec = pl.BlockSpec((None, tm, D_MODEL), row)
    act = jax.ShapeDtypeStruct((B, S, D_MODEL), BF16)
    return pl.pallas_call(
        _proj1_kernel,
        grid=(B, S // tm),
        in_specs=[act_spec,
                  pl.BlockSpec((D_MODEL, 2 * D_MODEL), const),
                  pl.BlockSpec((D_MODEL, D_MODEL), const),
                  pl.BlockSpec((D_MODEL, 2 * N_HEADS), const),
                  pl.BlockSpec((CONV_WIDTH, 2 * D_MODEL), const)],
        out_specs=[act_spec, act_spec, act_spec,
                   pl.BlockSpec((None, tm, 2 * N_HEADS), row)],
        out_shape=[act, act, act, jax.ShapeDtypeStruct((B, S, 2 * N_HEADS), F32)],
        scratch_shapes=[pltpu.VMEM((CONV_PAD, 2 * D_MODEL), F32)],
        compiler_params=_cparams(("parallel", "arbitrary"), VMEM_LARGE_MIB),
        name="proj1",
    )(x, wqk, wv, wgt, wconv)


def _mlstm_kernel(q_ref, k_ref, v_ref, x_ref, wog_ref, gn_ref, gt_ref, bi_row, bf_row, bi_col,
                  bf_col, ng_ref, o_ref, c_sc, n_sc, m_sc, og_sc, *, L):
    ci = pl.program_id(1)

    @pl.when(ci == 0)
    def _():
        c_sc[...] = jnp.zeros(c_sc.shape, F32)
        n_sc[...] = jnp.zeros(n_sc.shape, F32)
        m_sc[...] = jnp.full(m_sc.shape, NEG_INF, F32)

    xb = x_ref[...].astype(BF16)
    wide = 2 * MXU_DIM
    for c in range(D_MODEL // wide):
        og_sc[:, c * wide:(c + 1) * wide] = _dot(xb, wog_ref[:, c * wide:(c + 1) * wide])

    row = lax.broadcasted_iota(jnp.int32, (L, L), 0)
    col = lax.broadcasted_iota(jnp.int32, (L, L), 1)
    causal = col <= row
    lower = jnp.where(causal, 1.0, 0.0).astype(BF16)
    upper = jnp.where(row <= col, 1.0, 0.0).astype(BF16)

    gn = gn_ref[...]
    gt = gt_ref[...]
    i_col = gn[:, :N_HEADS] + bi_row[...]
    lf_col = _log_sigmoid(gn[:, N_HEADS:] + bf_row[...])
    i_row = gt[:N_HEADS, :] + bi_col[...]
    lf_row = _log_sigmoid(gt[N_HEADS:, :] + bf_col[...])
    b_col = sum(_dot(lower, part) for part in _split3(lf_col))
    b_row = sum(_dot(part, upper) for part in _split3(lf_row))

    keep = row <= col
    n_pad = jnp.zeros((2 * N_HEADS - 3, MLSTM_HEAD_DIM), F32)
    for h in range(N_HEADS):
        lanes = slice(h * MLSTM_HEAD_DIM, (h + 1) * MLSTM_HEAD_DIM)
        q = q_ref[:, lanes]
        k = k_ref[:, lanes]
        v = v_ref[:, lanes]
        bt = b_row[h:h + 1, :]
        key_term = i_col[:, h:h + 1] - b_col[:, h:h + 1]
        m_prev = m_sc[h]
        ct_prev = c_sc[h]
        n_prev = n_sc[h]

        dlog = jnp.where(keep, bt + key_term, NEG_INF)
        inter = bt + m_prev
        m_t = jnp.maximum(inter, jnp.max(dlog, axis=0, keepdims=True))
        s = _dot_nt(k, q) * jnp.exp(dlog - m_t)
        inter_w = jnp.exp(inter - m_t)
        num = _dot_tn(v, s.astype(BF16)) + inter_w * _dot_nt(ct_prev.astype(BF16), q)
        n_parts = jnp.concatenate([p.astype(F32) for p in _split3(n_prev)] + [n_pad], axis=0)
        qn = jnp.sum(_dot_nt(n_parts.astype(BF16), q), axis=0, keepdims=True)
        den = jnp.sum(s, axis=0, keepdims=True) + inter_w * qn
        hh = num / jnp.maximum(jnp.abs(den), jnp.exp(-m_t))

        b_last = bt[:, L - 1:L]
        g = b_last + key_term
        m_new = jnp.maximum(b_last + m_prev, jnp.max(g, axis=0, keepdims=True))
        w = jnp.exp(g - m_new)
        decay = jnp.exp(b_last + m_prev - m_new)
        kw = k.astype(F32) * w
        c_sc[h] = decay * ct_prev + _dot_tn(v, kw.astype(BF16))
        n_sc[h] = decay * n_prev + jnp.sum(kw, axis=0, keepdims=True)
        m_sc[h] = m_new

        mu = jnp.mean(hh, axis=0, keepdims=True)
        hc = hh - mu
        var = jnp.mean(hc * hc, axis=0, keepdims=True)
        hn = (hc * lax.rsqrt(var + LN_EPS)).T * ng_ref[:, lanes]
        o_ref[:, lanes] = (hn * jax.nn.sigmoid(og_sc[:, lanes])).astype(BF16)


def _mlstm(q, k, v, x, wog, gates, gates_t, b_i, b_f, norm_g, L):
    B, S, _ = q.shape
    row = lambda b, c: (b, c, 0)
    const = lambda b, c: (0, 0)
    act_spec = pl.BlockSpec((None, L, D_MODEL), row)
    return pl.pallas_call(
        functools.partial(_mlstm_kernel, L=L),
        grid=(B, S // L),
        in_specs=[act_spec, act_spec, act_spec, act_spec,
                  pl.BlockSpec((D_MODEL, D_MODEL), const),
                  pl.BlockSpec((None, L, 2 * N_HEADS), row),
                  pl.BlockSpec((None, 2 * N_HEADS, L), lambda b, c: (b, 0, c)),
                  pl.BlockSpec((1, N_HEADS), const), pl.BlockSpec((1, N_HEADS), const),
                  pl.BlockSpec((N_HEADS, 1), const), pl.BlockSpec((N_HEADS, 1), const),
                  pl.BlockSpec((1, D_MODEL), const)],
        out_specs=act_spec,
        out_shape=jax.ShapeDtypeStruct((B, S, D_MODEL), BF16),
        scratch_shapes=[pltpu.VMEM((N_HEADS, MLSTM_HEAD_DIM, MLSTM_HEAD_DIM), F32),
                        pltpu.VMEM((N_HEADS, 1, MLSTM_HEAD_DIM), F32),
                        pltpu.VMEM((N_HEADS, 1, 1), F32),
                        pltpu.VMEM((L, D_MODEL), F32)],
        compiler_params=_cparams(("parallel", "arbitrary"), VMEM_MID_MIB),
        name="mlstm",
    )(q, k, v, x, wog, gates, gates_t, b_i[None, :], b_f[None, :], b_i[:, None], b_f[:, None],
      norm_g[None, :])


MOE_TOP_K = 2
MOE_TILE = 512
RANK_BLOCK = 1024


def _rank_kernel(sel_ref, rank_ref, count_ref, upper_sc, carry_sc):
    n = sel_ref.shape[1]

    @pl.when(pl.program_id(0) == 0)
    def _():
        row = lax.broadcasted_iota(jnp.int32, (n, n), 0)
        col = lax.broadcasted_iota(jnp.int32, (n, n), 1)
        upper_sc[...] = jnp.where(row <= col, 1.0, 0.0).astype(BF16)
        carry_sc[...] = jnp.zeros(carry_sc.shape, F32)

    sel = sel_ref[...]
    incl = _dot(sel.astype(BF16), upper_sc[...])
    rank_ref[...] = carry_sc[...] + incl - sel
    carry_sc[...] = carry_sc[...] + incl[:, n - 1:n]
    count_ref[...] = carry_sc[...]


def _rank(sel_t):
    E, T = sel_t.shape
    n = min(RANK_BLOCK, T)
    return pl.pallas_call(
        _rank_kernel,
        grid=(T // n,),
        in_specs=[pl.BlockSpec((E, n), lambda i: (0, i))],
        out_specs=[pl.BlockSpec((E, n), lambda i: (0, i)), pl.BlockSpec((E, 1), lambda i: (0, 0))],
        out_shape=[jax.ShapeDtypeStruct((E, T), F32), jax.ShapeDtypeStruct((E, 1), F32)],
        scratch_shapes=[pltpu.VMEM((n, n), BF16), pltpu.VMEM((E, 1), F32)],
        compiler_params=_cparams(("arbitrary",), VMEM_SMALL_MIB),
        name="moe_rank",
    )(sel_t)


def _slot_kernel(sel_ref, gate_ref, rank_ref, off_ref, pos_ref, w_ref):
    sel = sel_ref[...] > 0.0
    idx = lax.broadcasted_iota(jnp.int32, sel.shape, 0)
    first = jnp.min(jnp.where(sel, idx, N_EXPERTS), axis=0, keepdims=True)
    second = jnp.max(jnp.where(sel, idx, -1), axis=0, keepdims=True)
    slot = off_ref[...] + rank_ref[...]
    gate = gate_ref[...]
    rows_p, rows_w = [], []
    for which in (first, second):
        hit = idx == which
        rows_p.append(jnp.sum(jnp.where(hit, slot, 0.0), axis=0, keepdims=True))
        rows_w.append(jnp.sum(jnp.where(hit, gate, 0.0), axis=0, keepdims=True))
    pos_ref[...] = jnp.concatenate(rows_p, axis=0).astype(jnp.int32)
    w_ref[...] = jnp.concatenate(rows_w, axis=0)


def _slots(sel_t, gate_t, rank_t, offsets_col, tm):
    E, T = sel_t.shape
    spec = pl.BlockSpec((E, tm), lambda i: (0, i))
    out_spec = pl.BlockSpec((MOE_TOP_K, tm), lambda i: (0, i))
    return pl.pallas_call(
        _slot_kernel,
        grid=(T // tm,),
        in_specs=[spec, spec, spec, pl.BlockSpec((E, 1), lambda i: (0, 0))],
        out_specs=[out_spec, out_spec],
        out_shape=[jax.ShapeDtypeStruct((MOE_TOP_K, T), jnp.int32),
                   jax.ShapeDtypeStruct((MOE_TOP_K, T), F32)],
        compiler_params=_cparams(("parallel",), VMEM_SMALL_MIB),
        name="moe_slots",
    )(sel_t, gate_t, rank_t, offsets_col)


def _dispatch_kernel(tail_ref, used_ref, pos_ref, x_ref, xs_hbm, zero_sc, sem, zsem, *, tm):
    i = pl.program_id(0)
    n_tiles = xs_hbm.shape[0] // MOE_TILE

    def fill(tile):
        return pltpu.make_async_copy(zero_sc, xs_hbm.at[pl.ds(tile * MOE_TILE, MOE_TILE)], zsem)

    @pl.when(i == 0)
    def _():
        zero_sc[...] = jnp.zeros(zero_sc.shape, F32)
        for e in range(N_EXPERTS):
            fill(tail_ref[e]).start()
        for e in range(N_EXPERTS):
            fill(tail_ref[e]).wait()
        for j in range(N_EXPERTS):
            @pl.when(n_tiles - 1 - j >= used_ref[0])
            def _():
                c = fill(n_tiles - 1 - j)
                c.start()
                c.wait()

    def issue(t, carry):
        for k in range(MOE_TOP_K):
            pltpu.make_async_copy(x_ref.at[pl.ds(t, 1)], xs_hbm.at[pl.ds(pos_ref[k, t], 1)],
                                  sem).start(priority=k)
        return carry

    lax.fori_loop(0, tm, issue, 0, unroll=8)
    for _ in range(MOE_TOP_K):
        pltpu.make_async_copy(x_ref, xs_hbm.at[pl.ds(0, tm)], sem).wait()


def _dispatch(x, pos_t, tail_tiles, n_used, n_rows, tm):
    T = x.shape[0]
    n_steps = T // tm
    pos3 = pos_t.reshape(MOE_TOP_K, n_steps, tm).transpose(1, 0, 2)
    return pl.pallas_call(
        functools.partial(_dispatch_kernel, tm=tm),
        grid_spec=pltpu.PrefetchScalarGridSpec(
            num_scalar_prefetch=2,
            grid=(n_steps,),
            in_specs=[pl.BlockSpec((None, MOE_TOP_K, tm), lambda i, tail, used: (i, 0, 0),
                                   memory_space=pltpu.SMEM),
                      pl.BlockSpec((tm, D_MODEL), lambda i, tail, used: (i, 0))],
            out_specs=pl.BlockSpec(memory_space=pl.ANY),
            scratch_shapes=[pltpu.VMEM((MOE_TILE, D_MODEL), F32),
                            pltpu.SemaphoreType.DMA, pltpu.SemaphoreType.DMA],
        ),
        out_shape=jax.ShapeDtypeStruct((n_rows, D_MODEL), F32),
        compiler_params=_cparams(("arbitrary",), VMEM_SMALL_MIB),
        name="moe_dispatch",
    )(tail_tiles, n_used, pos3, x)


def _expert_ffn_kernel(te_ref, used_ref, x_ref, wg_ref, wu_ref, wd_ref, o_ref):
    live = pl.program_id(0) < used_ref[0]

    @pl.when(live)
    def _():
        xb = x_ref[...].astype(BF16)
        y = None
        for lo, size in _ffn_chunks(wg_ref.shape[1]):
            gate = _dot(xb, wg_ref[:, lo:lo + size])
            up = _dot(xb, wu_ref[:, lo:lo + size])
            act = (gate * jax.nn.sigmoid(gate) * up).astype(BF16)
            part = _dot(act, wd_ref[lo:lo + size, :])
            y = part if y is None else y + part
        o_ref[...] = y

    @pl.when(jnp.logical_not(live))
    def _():
        o_ref[...] = jnp.zeros(o_ref.shape, F32)


def _expert_ffn(xs, tile_expert, n_used, wg, wu, wd):
    n_rows = xs.shape[0]
    E, _, F = wg.shape
    tm = MOE_TILE
    once = pl.Buffered(1)

    def row(i, te, used):
        return (jnp.minimum(i, used[0] - 1), 0)

    return pl.pallas_call(
        _expert_ffn_kernel,
        grid_spec=pltpu.PrefetchScalarGridSpec(
            num_scalar_prefetch=2,
            grid=(n_rows // tm,),
            in_specs=[pl.BlockSpec((tm, D_MODEL), row),
                      pl.BlockSpec((None, D_MODEL, F), lambda i, te, used: (te[i], 0, 0),
                                   pipeline_mode=once),
                      pl.BlockSpec((None, D_MODEL, F), lambda i, te, used: (te[i], 0, 0),
                                   pipeline_mode=once),
                      pl.BlockSpec((None, F, D_MODEL), lambda i, te, used: (te[i], 0, 0),
                                   pipeline_mode=once)],
            out_specs=pl.BlockSpec((tm, D_MODEL), lambda i, te, used: (i, 0)),
        ),
        out_shape=jax.ShapeDtypeStruct((n_rows, D_MODEL), F32),
        compiler_params=_cparams(("arbitrary",), VMEM_LARGE_MIB),
        name="moe_expert_ffn",
    )(tile_expert, n_used, xs, wg, wu, wd)


def _combine_kernel(pos_ref, w_ref, x_ref, g_ref, b_ref, ys_hbm, o_ref, rows_sc, sem, *, tm):
    half = tm // 2
    for part in range(2):
        def issue(t, carry, part=part):
            for k in range(MOE_TOP_K):
                pltpu.make_async_copy(ys_hbm.at[pl.ds(pos_ref[k, t], 1)],
                                      rows_sc.at[k, pl.ds(t, 1)], sem.at[part]).start(priority=k)
            return carry

        lax.fori_loop(part * half, (part + 1) * half, issue, 0, unroll=8)
    for part in range(2):
        rows = slice(part * half, (part + 1) * half)
        for k in range(MOE_TOP_K):
            pltpu.make_async_copy(ys_hbm.at[pl.ds(0, half)], rows_sc.at[k, rows], sem.at[part]).wait()
        w = w_ref[rows, :]
        y = w[:, 0:1] * rows_sc[0, rows, :] + w[:, 1:2] * rows_sc[1, rows, :]
        o_ref[rows, :] = _layer_norm(DEEPNORM_ALPHA * x_ref[rows, :] + y, g_ref[...], b_ref[...])


def _combine_ln(ys, pos_t, w_nat, x, g, b, tm):
    T = x.shape[0]
    n_steps = T // tm
    pos3 = pos_t.reshape(MOE_TOP_K, n_steps, tm).transpose(1, 0, 2)
    row = pl.BlockSpec((tm, D_MODEL), lambda i: (i, 0))
    vec = pl.BlockSpec((1, D_MODEL), lambda i: (0, 0))
    return pl.pallas_call(
        functools.partial(_combine_kernel, tm=tm),
        grid=(n_steps,),
        in_specs=[pl.BlockSpec((None, MOE_TOP_K, tm), lambda i: (i, 0, 0), memory_space=pltpu.SMEM),
                  pl.BlockSpec((tm, MOE_TOP_K), lambda i: (i, 0)),
                  row, vec, vec,
                  pl.BlockSpec(memory_space=pl.ANY)],
        out_specs=row,
        out_shape=jax.ShapeDtypeStruct((T, D_MODEL), F32),
        scratch_shapes=[pltpu.VMEM((MOE_TOP_K, tm, D_MODEL), F32), pltpu.SemaphoreType.DMA((2,))],
        compiler_params=_cparams(("arbitrary",), VMEM_SMALL_MIB),
        name="moe_combine_ln",
    )(pos3, w_nat, x, g, b, ys)


def _moe_sparse(x, sel_t, gate_t, wg, wu, wd, g, b):
    T = x.shape[0]
    tile = MOE_TILE
    n_rows = MOE_TOP_K * T + N_EXPERTS * tile
    n_tiles = n_rows // tile
    rank_t, counts = _rank(sel_t)
    tiles_per = jnp.ceil(counts[:, 0] / tile).astype(jnp.int32)
    tile_end = jnp.cumsum(tiles_per)
    tile_start = tile_end - tiles_per
    n_used = tile_end[-1:]
    tile_expert = jnp.minimum(
        jnp.sum(jnp.arange(n_tiles, dtype=jnp.int32)[:, None] >= tile_end[None, :], axis=1),
        N_EXPERTS - 1).astype(jnp.int32)
    tail_tiles = jnp.maximum(tile_end - 1, 0).astype(jnp.int32)
    offsets_col = (tile_start * tile).astype(F32)[:, None]
    pos_t, w_t = _slots(sel_t, gate_t, rank_t, offsets_col, tm=ROUTE_TILE)
    xs = _dispatch(x, pos_t, tail_tiles, n_used, n_rows, tm=ROUTE_TILE)
    ys = _expert_ffn(xs, tile_expert, n_used, wg, wu, wd)
    return _combine_ln(ys, pos_t, w_t.T, x, g, b, tm=COMBINE_TILE)


def _rotary_tables(S):
    pos = jnp.arange(S, dtype=jnp.int32)
    inv = ROPE_THETA ** (-jnp.arange(ROPE_HALF, dtype=F32) / ROPE_HALF)
    ang = pos.astype(F32)[:, None] * inv[None, :]
    cos, sin = jnp.cos(ang), jnp.sin(ang)
    ones = jnp.ones((S, HEAD_DIM - ROPE_DIMS), F32)
    zeros = jnp.zeros((S, HEAD_DIM - ROPE_DIMS), F32)
    z8 = jnp.zeros((S, ROPE_HALF), F32)
    cos_h = jnp.concatenate([cos, cos, ones], axis=1)
    sup_h = jnp.concatenate([-sin, z8, zeros], axis=1)
    sdn_h = jnp.concatenate([z8, sin, zeros], axis=1)
    rep = LANES // HEAD_DIM
    return (jnp.tile(cos_h, (1, rep)), jnp.tile(sup_h, (1, rep)), jnp.tile(sdn_h, (1, rep)))


def _even_layer(h, w_in, b_forget, w_out, ln_mix_g, ln_mix_b, w_gate, w_up, w_down,
                ln_ffn_g, ln_ffn_b):
    B, S, _ = h.shape
    T = B * S
    W = ATT_WIDTH
    w_main = jnp.concatenate([w_in[:, :3 * W], w_in[:, 3 * W + N_HEADS:]], axis=1).astype(BF16)
    wf_t = w_in[:, 3 * W:3 * W + N_HEADS].T.astype(BF16)
    cos_t, sup_t, sdn_t = _rotary_tables(S)
    outs = _proj0(h, w_main, wf_t, cos_t, sup_t, sdn_t, tm=min(ROW_TILE, S))
    qa, ka, va = outs[:3]
    qkv_by_dilation = {d: tuple(outs[3 + 3 * n:6 + 3 * n]) for n, d in enumerate(DILATIONS)}
    f_t = outs[-1]

    bias_col = jnp.tile(b_forget, B)[:, None]
    c = _fox_cumsum(f_t.reshape(B * N_HEADS, S), bias_col).reshape(B, N_HEADS, S)
    o_fox = _fox_attention(qa, ka, va, jnp.swapaxes(c, 1, 2), t=min(ROW_TILE, S))
    o_dil = _dilated_attention(qkv_by_dilation)

    w_out_b = w_out.astype(BF16)
    tm = ROW_TILE
    h2 = _mixer_out_dense_ffn(
        [o_fox.reshape(T, W), o_dil.reshape(T, W)], [w_out_b[:W], w_out_b[W:]],
        h.reshape(T, D_MODEL), ln_mix_g[None, :], ln_mix_b[None, :],
        w_gate.astype(BF16), w_up.astype(BF16), w_down.astype(BF16),
        ln_ffn_g[None, :], ln_ffn_b[None, :], tm)
    return h2.reshape(B, S, D_MODEL)


def _odd_layer(h, w_in, b_igate, b_fgate, w_conv, norm_g, w_out, ln_mix_g, ln_mix_b, w_router,
               w_gate, w_up, w_down, ln_ffn_g, ln_ffn_b):
    B, S, _ = h.shape
    T = B * S
    D = D_MODEL
    wqk = w_in[:, :2 * D].astype(BF16)
    wv = w_in[:, 2 * D:3 * D].astype(BF16)
    wgt = w_in[:, 3 * D:3 * D + 2 * N_HEADS].astype(BF16)
    wog = w_in[:, 3 * D + 2 * N_HEADS:].astype(BF16)
    q, k, v, gates = _proj1(h, wqk, wv, wgt, w_conv, tm=min(ROW_TILE, S))
    gates_t = jnp.swapaxes(gates, 1, 2)
    hm = _mlstm(q, k, v, h, wog, gates, gates_t, b_igate, b_fgate, norm_g,
                L=min(MLSTM_CHUNK, S))

    tm = ROW_TILE
    h1, sel_t, gate_t = _outproj_router(hm.reshape(T, D), w_out.astype(BF16), h.reshape(T, D),
                                        ln_mix_g[None, :], ln_mix_b[None, :], w_router.T, tm)
    h2 = _moe_sparse(h1, sel_t, gate_t, w_gate.astype(BF16), w_up.astype(BF16),
                     w_down.astype(BF16), ln_ffn_g[None, :], ln_ffn_b[None, :])
    return h2.reshape(B, S, D)


def kernel(x, w_in_e, b_forget_e, w_out_e, ln_mix_g_e, ln_mix_b_e, ffn_w_gate_e, ffn_w_up_e,
           ffn_w_down_e, ln_ffn_g_e, ln_ffn_b_e, w_in_o, b_igate_o, b_fgate_o, w_conv_o,
           mlstm_norm_g_o, w_out_o, ln_mix_g_o, ln_mix_b_o, w_router_o, moe_w_gate_o,
           moe_w_up_o, moe_w_down_o, ln_ffn_g_o, ln_ffn_b_o):
    h = x
    for layer in range(DEPTH):
        i = layer // 2
        if layer % 2 == 0:
            h = _even_layer(h, w_in_e[i], b_forget_e[i], w_out_e[i], ln_mix_g_e[i], ln_mix_b_e[i],
                            ffn_w_gate_e[i], ffn_w_up_e[i], ffn_w_down_e[i], ln_ffn_g_e[i],
                            ln_ffn_b_e[i])
        else:
            h = _odd_layer(h, w_in_o[i], b_igate_o[i], b_fgate_o[i], w_conv_o[i],
                           mlstm_norm_g_o[i], w_out_o[i], ln_mix_g_o[i], ln_mix_b_o[i],
                           w_router_o[i], moe_w_gate_o[i], moe_w_up_o[i], moe_w_down_o[i],
                           ln_ffn_g_o[i], ln_ffn_b_o[i])
    return h
```
